```python
import jax, jax.numpy as jnp
from jax import lax
import numpy as np

D_MODEL = 2048
BATCH = 4
SEQ = 4096
DEPTH = 2

RET_HEADS = 4
RET_HEAD_DIM = 256
RET_WIDTH = RET_HEADS * RET_HEAD_DIM
RET_CHUNK = 128
RET_ROPE_BASE = 10000.0
MOBA_HEADS = 8
MOBA_HEAD_DIM = 128
MOBA_WIDTH = MOBA_HEADS * MOBA_HEAD_DIM
MOBA_BLOCK = 256
MOBA_TOPK = 3
MOBA_QCHUNK = 64
ROPE_THETA = 500000.0
ROPE_DIM = MOBA_HEAD_DIM // 4
HGRN_HEADS = 8
HGRN_HEAD_DIM = 128
HGRN_WIDTH = HGRN_HEADS * HGRN_HEAD_DIM
HGRN_CHUNK = 64
FFN_HIDDEN = ((8 * D_MODEL + 3 * 256 - 1) // (3 * 256)) * 256
NORM_EPS = 1e-6
IN_SIZES = (RET_WIDTH,) * 4 + (MOBA_WIDTH,) * 3 + (HGRN_WIDTH,) * 4 + (D_MODEL,) * 3
IN_COLS = sum(IN_SIZES)
IN_SPLITS = tuple(sum(IN_SIZES[:i + 1]) for i in range(len(IN_SIZES) - 1))

kernel_name = "hybrid_retention_moba_hgrn2_block"


def rmsnorm(x, g):
    xf = x.astype(jnp.float32)
    y = xf * lax.rsqrt(jnp.mean(xf * xf, axis=-1, keepdims=True) + NORM_EPS)
    return (y * g.astype(jnp.float32)).astype(x.dtype)


def head_layer_norm(o):
    mu = jnp.mean(o, axis=-1, keepdims=True)
    c = o - mu
    return c * lax.rsqrt(jnp.mean(c * c, axis=-1, keepdims=True) + NORM_EPS)


def head_rms_norm(o):
    return o * lax.rsqrt(jnp.mean(o * o, axis=-1, keepdims=True) + NORM_EPS)


def split_heads(t, n_heads):
    B, S, W = t.shape
    return t.reshape(B, S, n_heads, W // n_heads).transpose(0, 2, 1, 3)


def merge_heads(t):
    B, H, S, d = t.shape
    return t.transpose(0, 2, 1, 3).reshape(B, S, H * d)


def rotary(x, positions, rot_dim, base):
    half = rot_dim // 2
    inv_freq = base ** (-jnp.arange(half, dtype=jnp.float32) * 2.0 / rot_dim)
    ang = positions.astype(jnp.float32)[:, None] * inv_freq[None, :]
    cos = jnp.cos(ang).astype(x.dtype)
    sin = jnp.sin(ang).astype(x.dtype)
    x1 = x[..., :half]
    x2 = x[..., half:rot_dim]
    return jnp.concatenate([x1 * cos - x2 * sin, x2 * cos + x1 * sin, x[..., rot_dim:]], axis=-1)


def retention(q, k, v):
    B, H, S, dk = q.shape
    C = RET_CHUNK
    NC = S // C
    log_gamma = jnp.log1p(-jnp.exp2(-5.0 - jnp.arange(H, dtype=jnp.float32)))
    k = k * (dk ** -0.5)
    qc = q.reshape(B, H, NC, C, dk)
    kc = k.reshape(B, H, NC, C, dk)
    vc = v.reshape(B, H, NC, C, -1)
    t = jnp.arange(C, dtype=jnp.float32)
    rel = t[:, None] - t[None, :]
    decay = jnp.where(rel >= 0, jnp.exp(log_gamma[:, None, None] * jnp.maximum(rel, 0.0)), 0.0)
    scores = jnp.einsum('bhnid,bhnjd->bhnij', qc, kc) * decay[None, :, None]
    intra = jnp.einsum('bhnij,bhnje->bhnie', scores, vc)
    k_w = kc * jnp.exp(log_gamma[:, None] * (C - 1.0 - t))[None, :, None, :, None]
    kv = jnp.einsum('bhncd,bhnce->nbhde', k_w, vc)
    chunk_decay = jnp.exp(log_gamma * C)[None, :, None, None]

    def step(state, kv_n):
        return chunk_decay * state + kv_n, state

    _, r_prev = lax.scan(step, jnp.zeros_like(kv[0]), kv)
    q_w = qc * jnp.exp(log_gamma[:, None] * (t + 1.0))[None, :, None, :, None]
    cross = jnp.einsum('bhncd,nbhde->bhnce', q_w, r_prev)
    return (intra + cross).reshape(B, H, S, -1)


def moba_attention(q, k, v):
    B, H, S, d = q.shape
    L = MOBA_BLOCK
    QC = MOBA_QCHUNK
    NQ = S // QC
    pad = (-S) % L
    k_p = jnp.pad(k, ((0, 0), (0, 0), (0, pad), (0, 0)))
    v_p = jnp.pad(v, ((0, 0), (0, 0), (0, pad), (0, 0)))
    NB = (S + pad) // L
    K = min(MOBA_TOPK, NB)
    scale = d ** -0.5
    kb = k_p.reshape(B, H, NB, L, d)
    vb = v_p.reshape(B, H, NB, L, d)
    k_mean = jnp.mean(kb.astype(jnp.float32), axis=3)
    q_block = jnp.arange(S) // L
    gate = jnp.einsum('bhsd,bhnd->bhsn', q.astype(jnp.float32), k_mean)
    fully_past = jnp.arange(NB)[None, :] < q_block[:, None]
    gate = jnp.where(fully_past, gate, -jnp.inf)
    _, sel = lax.top_k(gate, K)
    valid = sel < q_block[:, None]

    def to_qchunks(t):
        t = t.reshape((B, H, NQ, QC) + t.shape[3:])
        t = jnp.swapaxes(t, 1, 2)
        return t.reshape((B * NQ, H, QC) + t.shape[4:])

    b_idx = jnp.repeat(jnp.arange(B), NQ)
    c_idx = jnp.tile(jnp.arange(NQ), B)
    h_idx = jnp.arange(H)[:, None, None]

    def attend(xs):
        qn, seln, validn, bi, ci = xs
        kbb = kb[bi]
        vbb = vb[bi]
        k_sel = kbb[h_idx, seln]
        v_sel = vbb[h_idx, seln]
        s_sel = jnp.einsum('hqd,hqkld->hqkl', qn, k_sel).astype(jnp.float32) * scale
        s_sel = jnp.where(validn[..., None], s_sel, -jnp.inf).reshape(H, QC, K * L)
        j = (ci * QC) // L
        k_own = kbb[:, j]
        v_own = vbb[:, j]
        q_pos = ci * QC + jnp.arange(QC)
        key_pos = j * L + jnp.arange(L)
        s_own = jnp.einsum('hqd,hld->hql', qn, k_own).astype(jnp.float32) * scale
        s_own = jnp.where(key_pos[None, None, :] <= q_pos[None, :, None], s_own, -jnp.inf)
        p = jax.nn.softmax(jnp.concatenate([s_sel, s_own], axis=-1), axis=-1).astype(qn.dtype)
        p_sel = p[..., :K * L].reshape(H, QC, K, L)
        p_own = p[..., K * L:]
        return (jnp.einsum('hqkl,hqkld->hqd', p_sel, v_sel)
                + jnp.einsum('hql,hld->hqd', p_own, v_own))

    out = lax.map(attend, (to_qchunks(q), to_qchunks(sel), to_qchunks(valid), b_idx, c_idx))
    out = jnp.swapaxes(out.reshape(B, NQ, H, QC, d), 1, 2)
    return out.reshape(B, H, S, d)


def hgrn2_recurrence(q, k, v, log_f):
    B, H, S, dk = q.shape
    dv = v.shape[-1]
    C = HGRN_CHUNK
    NC = S // C

    def to_chunks(t):
        return jnp.moveaxis(t.reshape(B, H, NC, C, t.shape[-1]), 2, 0)

    causal = jnp.tril(jnp.ones((C, C), dtype=bool))[:, :, None]

    def step(state, xs):
        qn, kn, vn, gn = xs
        b = jnp.cumsum(gn, axis=2)
        expo = jnp.where(causal, b[:, :, :, None, :] - b[:, :, None, :, :], -jnp.inf)
        attn = jnp.einsum('bhid,bhjd,bhijd->bhij', qn, kn, jnp.exp(expo))
        intra = jnp.einsum('bhij,bhje->bhie', attn, vn)
        cross = jnp.einsum('bhid,bhde->bhie', qn * jnp.exp(b), state)
        b_last = b[:, :, -1:, :]
        k_dec = kn * jnp.exp(b_last - b)
        new_state = (jnp.exp(b_last[:, :, 0, :])[..., None] * state
                     + jnp.einsum('bhjd,bhje->bhde', k_dec, vn))
        return new_state, intra + cross

    init = jnp.zeros((B, H, dk, dv), jnp.float32)
    _, out = lax.scan(step, init, (to_chunks(q), to_chunks(k), to_chunks(v), to_chunks(log_f)))
    return jnp.moveaxis(out, 0, 2).reshape(B, H, S, dv)


def hybrid_layer(x, lower_bound, norm_mix, w_in, ret_norm, hgrn_norm, w_branch_ret, w_branch_moba,
                 w_branch_hgrn, w_out, norm_ffn, w_ffn_gate, w_ffn_up, w_ffn_down):
    B, S, _ = x.shape
    dt = x.dtype
    f32 = jnp.float32
    pos = jnp.arange(S)
    h = rmsnorm(x, norm_mix)
    proj = h @ w_in
    (rq, rk, rv, rg, mq, mk, mv, hq, hf, hi, hg,
     gate_ret, gate_moba, gate_hgrn) = jnp.split(proj, IN_SPLITS, axis=-1)

    rq_h = rotary(split_heads(rq, RET_HEADS), pos, RET_HEAD_DIM, RET_ROPE_BASE).astype(f32)
    rk_h = rotary(split_heads(rk, RET_HEADS), pos, RET_HEAD_DIM, RET_ROPE_BASE).astype(f32)
    ret = retention(rq_h, rk_h, split_heads(rv, RET_HEADS).astype(f32))
    ret = merge_heads(head_layer_norm(ret)) * ret_norm.astype(f32)
    ret = ret.astype(dt) * jax.nn.silu(rg)

    mq_h = rotary(split_heads(mq, MOBA_HEADS), pos, ROPE_DIM, ROPE_THETA)
    mk_h = rotary(split_heads(mk, MOBA_HEADS), pos, ROPE_DIM, ROPE_THETA)
    moba = merge_heads(moba_attention(mq_h, mk_h, split_heads(mv, MOBA_HEADS)))

    f = lower_bound + (1.0 - lower_bound) * jax.nn.sigmoid(hf.astype(f32))
    hgrn = hgrn2_recurrence(split_heads(jax.nn.silu(hq.astype(f32)), HGRN_HEADS),
                            split_heads(1.0 - f, HGRN_HEADS),
                            split_heads(hi.astype(f32), HGRN_HEADS),
                            split_heads(jnp.log(f), HGRN_HEADS))
    hgrn = merge_heads(head_rms_norm(hgrn)) * hgrn_norm.astype(f32)
    hgrn = hgrn.astype(dt) * jax.nn.silu(hg)

    mixed = (jax.nn.sigmoid(gate_ret) * (ret @ w_branch_ret)
             + jax.nn.sigmoid(gate_moba) * (moba @ w_branch_moba)
             + jax.nn.sigmoid(gate_hgrn) * (hgrn @ w_branch_hgrn))
    x = x + mixed @ w_out

    h = rmsnorm(x, norm_ffn)
    x = x + (jax.nn.silu(h @ w_ffn_gate) * (h @ w_ffn_up)) @ w_ffn_down
    return x


def setup_inputs(seed: int = 0) -> dict:
    key = jax.random.key(seed)
    ks = jax.random.split(key, 16)
    f32 = jnp.float32

    def nrm(k, shape, fan_in):
        return jax.random.normal(k, shape, f32) * (fan_in ** -0.5)

    def gain(k, shape):
        return 1.0 + 0.02 * jax.random.normal(k, shape, f32)

    return {
        "x": jax.random.normal(ks[0], (BATCH, SEQ, D_MODEL), f32),
        "hgrn_lower_bounds": 0.5 * jax.random.normal(ks[1], (DEPTH, HGRN_WIDTH), f32),
        "norm_mix": gain(ks[2], (DEPTH, D_MODEL)),
        "w_in": nrm(ks[3], (DEPTH, D_MODEL, IN_COLS), D_MODEL),
        "ret_norm": gain(ks[4], (DEPTH, RET_WIDTH)),
        "hgrn_norm": gain(ks[5], (DEPTH, HGRN_WIDTH)),
        "w_branch_ret": nrm(ks[6], (DEPTH, RET_WIDTH, D_MODEL), RET_WIDTH),
        "w_branch_moba": nrm(ks[7], (DEPTH, MOBA_WIDTH, D_MODEL), MOBA_WIDTH),
        "w_branch_hgrn": nrm(ks[8], (DEPTH, HGRN_WIDTH, D_MODEL), HGRN_WIDTH),
        "w_out": nrm(ks[9], (DEPTH, D_MODEL, D_MODEL), D_MODEL),
        "norm_ffn": gain(ks[10], (DEPTH, D_MODEL)),
        "w_ffn_gate": nrm(ks[11], (DEPTH, D_MODEL, FFN_HIDDEN), D_MODEL),
        "w_ffn_up": nrm(ks[12], (DEPTH, D_MODEL, FFN_HIDDEN), D_MODEL),
        "w_ffn_down": nrm(ks[13], (DEPTH, FFN_HIDDEN, D_MODEL), FFN_HIDDEN),
        "final_norm": gain(ks[14], (D_MODEL,)),
    }


def reference(x, hgrn_lower_bounds, norm_mix, w_in, ret_norm, hgrn_norm, w_branch_ret, w_branch_moba,
              w_branch_hgrn, w_out, norm_ffn, w_ffn_gate, w_ffn_up, w_ffn_down, final_norm):
    lb_soft = jax.nn.softmax(hgrn_lower_bounds.astype(jnp.float32), axis=0)
    lower_bounds = jnp.cumsum(lb_soft, axis=0) - lb_soft[0]
    h = x
    for layer in range(DEPTH):
        h = hybrid_layer(h, lower_bounds[layer], norm_mix[layer], w_in[layer], ret_norm[layer],
                         hgrn_norm[layer], w_branch_ret[layer], w_branch_moba[layer], w_branch_hgrn[layer],
                         w_out[layer], norm_ffn[layer], w_ffn_gate[layer], w_ffn_up[layer], w_ffn_down[layer])
    return rmsnorm(h, final_norm)
```

```python
import functools

import jax
import jax.numpy as jnp
from jax import lax
from jax.experimental import pallas as pl
from jax.experimental.pallas import tpu as pltpu

F32 = jnp.float32
BF16 = jnp.bfloat16

D_MODEL = 2048
RET_HEADS = 4
RET_HEAD_DIM = 256
RET_WIDTH = RET_HEADS * RET_HEAD_DIM
RET_CHUNK = 128
RET_ROPE_BASE = 10000.0
MOBA_HEADS = 8
MOBA_HEAD_DIM = 128
MOBA_WIDTH = MOBA_HEADS * MOBA_HEAD_DIM
MOBA_BLOCK = 256
MOBA_TOPK = 3
ROPE_THETA = 500000.0
ROPE_DIM = MOBA_HEAD_DIM // 4
HGRN_HEADS = 8
HGRN_HEAD_DIM = 128
HGRN_WIDTH = HGRN_HEADS * HGRN_HEAD_DIM
HGRN_CHUNK = 64
HGRN_SUB = 16
NORM_EPS = 1e-6
IN_SIZES = (RET_WIDTH,) * 4 + (MOBA_WIDTH,) * 3 + (HGRN_WIDTH,) * 4 + (D_MODEL,) * 3
IN_COLS = sum(IN_SIZES)
IN_OFFS = tuple(sum(IN_SIZES[:i]) for i in range(len(IN_SIZES)))
(OFF_RQ, OFF_RK, OFF_RV, OFF_RG, OFF_MQ, OFF_MK, OFF_MV,
 OFF_HQ, OFF_HF, OFF_HI, OFF_HG, OFF_GR, OFF_GM, OFF_GH) = IN_OFFS

V7X_VMEM_BYTES = 64 * 1024 * 1024
LANES = 128
SUBLANES = 8
MASK_VALUE = -1e30

NT_DIMS = (((1,), (1,)), ((), ()))
TN_DIMS = (((0,), (0,)), ((), ()))


def _compiler_params(n_grid_dims, vmem_mib):
    assert vmem_mib * 1024 * 1024 < V7X_VMEM_BYTES
    return pltpu.CompilerParams(
        dimension_semantics=("arbitrary",) * n_grid_dims,
        vmem_limit_bytes=vmem_mib * 1024 * 1024)


def _silu(x):
    return x * jax.nn.sigmoid(x)


def _rms_normalize(x, g):
    r = lax.rsqrt(jnp.mean(x * x, axis=-1, keepdims=True) + NORM_EPS)
    return x * r * g


def _norm_matmul_kernel(x_ref, g_ref, w_ref, o_ref, h_ref):
    @pl.when(pl.program_id(1) == 0)
    def _():
        h_ref[...] = _rms_normalize(x_ref[...], g_ref[...]).astype(BF16)

    o_ref[...] = jnp.dot(h_ref[...], w_ref[...], preferred_element_type=F32)


def norm_matmul(x, g, w, *, tm=1024, tn=1024):
    n, d = x.shape
    cols = w.shape[1]
    assert n % tm == 0 and cols % tn == 0
    return pl.pallas_call(
        _norm_matmul_kernel,
        grid=(n // tm, cols // tn),
        in_specs=[pl.BlockSpec((tm, d), lambda i, j: (i, 0)),
                  pl.BlockSpec((1, d), lambda i, j: (0, 0)),
                  pl.BlockSpec((d, tn), lambda i, j: (0, j))],
        out_specs=pl.BlockSpec((tm, tn), lambda i, j: (i, j)),
        out_shape=jax.ShapeDtypeStruct((n, cols), F32),
        scratch_shapes=[pltpu.VMEM((tm, d), BF16)],
        compiler_params=_compiler_params(2, 48),
        name="norm_inproj",
    )(x, g.reshape(1, d), w)


def _ffn_up_kernel(x_ref, g_ref, wg_ref, wu_ref, o_ref, h_ref):
    @pl.when(pl.program_id(1) == 0)
    def _():
        h_ref[...] = _rms_normalize(x_ref[...], g_ref[...]).astype(BF16)

    h = h_ref[...]
    gate = jnp.dot(h, wg_ref[...], preferred_element_type=F32)
    up = jnp.dot(h, wu_ref[...], preferred_element_type=F32)
    o_ref[...] = (_silu(gate) * up).astype(BF16)


def ffn_up(x, g, wg, wu, *, tm=1024, tn=512):
    n, d = x.shape
    hidden = wg.shape[1]
    assert n % tm == 0 and hidden % tn == 0
    return pl.pallas_call(
        _ffn_up_kernel,
        grid=(n // tm, hidden // tn),
        in_specs=[pl.BlockSpec((tm, d), lambda i, j: (i, 0)),
                  pl.BlockSpec((1, d), lambda i, j: (0, 0)),
                  pl.BlockSpec((d, tn), lambda i, j: (0, j)),
                  pl.BlockSpec((d, tn), lambda i, j: (0, j))],
        out_specs=pl.BlockSpec((tm, tn), lambda i, j: (i, j)),
        out_shape=jax.ShapeDtypeStruct((n, hidden), BF16),
        scratch_shapes=[pltpu.VMEM((tm, d), BF16)],
        compiler_params=_compiler_params(2, 48),
        name="ffn_up",
    )(x, g.reshape(1, d), wg, wu)


def _matmul_residual_kernel(a_ref, w_ref, r_ref, o_ref):
    o_ref[...] = r_ref[...] + jnp.dot(a_ref[...], w_ref[...], preferred_element_type=F32)


def matmul_residual(a, w, res, *, tm=1024, tn=512, name):
    n, k = a.shape
    cols = w.shape[1]
    assert n % tm == 0 and cols % tn == 0
    return pl.pallas_call(
        _matmul_residual_kernel,
        grid=(n // tm, cols // tn),
        in_specs=[pl.BlockSpec((tm, k), lambda i, j: (i, 0)),
                  pl.BlockSpec((k, tn), lambda i, j: (0, j)),
                  pl.BlockSpec((tm, tn), lambda i, j: (i, j))],
        out_specs=pl.BlockSpec((tm, tn), lambda i, j: (i, j)),
        out_shape=jax.ShapeDtypeStruct((n, cols), F32),
        compiler_params=_compiler_params(2, 52),
        name=name,
    )(a, w, res)


def _merge_kernel(ret_ref, moba_ref, hgrn_ref, gr_ref, gm_ref, gh_ref,
                  wr_ref, wm_ref, wh_ref, o_ref):
    def branch(gate_ref, a_ref, w_ref):
        return jax.nn.sigmoid(gate_ref[...]) * jnp.dot(
            a_ref[...], w_ref[...], preferred_element_type=F32)

    mixed = (branch(gr_ref, ret_ref, wr_ref) + branch(gm_ref, moba_ref, wm_ref)
             + branch(gh_ref, hgrn_ref, wh_ref))
    o_ref[...] = mixed.astype(BF16)


def merge_branches(ret, moba, hgrn, proj, wr, wm, wh, *, tm=512, tn=1024):
    n = ret.shape[0]
    assert n % tm == 0 and D_MODEL % tn == 0
    assert OFF_GR % tn == 0 and OFF_GM % tn == 0 and OFF_GH % tn == 0

    def branch_spec(width):
        return pl.BlockSpec((tm, width), lambda j, i: (i, 0))

    def gate_spec(off):
        return pl.BlockSpec((tm, tn), lambda j, i: (i, off // tn + j))

    def weight_spec(width):
        return pl.BlockSpec((width, tn), lambda j, i: (0, j))

    return pl.pallas_call(
        _merge_kernel,
        grid=(D_MODEL // tn, n // tm),
        in_specs=[branch_spec(RET_WIDTH), branch_spec(MOBA_WIDTH), branch_spec(HGRN_WIDTH),
                  gate_spec(OFF_GR), gate_spec(OFF_GM), gate_spec(OFF_GH),
                  weight_spec(RET_WIDTH), weight_spec(MOBA_WIDTH), weight_spec(HGRN_WIDTH)],
        out_specs=pl.BlockSpec((tm, tn), lambda j, i: (i, j)),
        out_shape=jax.ShapeDtypeStruct((n, D_MODEL), BF16),
        compiler_params=_compiler_params(2, 48),
        name="merge_branches",
    )(ret, moba, hgrn, proj, proj, proj, wr, wm, wh)


def _rmsnorm_kernel(x_ref, g_ref, o_ref):
    o_ref[...] = _rms_normalize(x_ref[...], g_ref[...])


def rmsnorm(x, g, *, tm=1024):
    n, d = x.shape
    assert n % tm == 0
    return pl.pallas_call(
        _rmsnorm_kernel,
        grid=(n // tm,),
        in_specs=[pl.BlockSpec((tm, d), lambda i: (i, 0)),
                  pl.BlockSpec((1, d), lambda i: (0, 0))],
        out_specs=pl.BlockSpec((tm, d), lambda i: (i, 0)),
        out_shape=jax.ShapeDtypeStruct((n, d), F32),
        compiler_params=_compiler_params(1, 40),
        name="final_rmsnorm",
    )(x, g.reshape(1, d))


def _retention_kernel(lg_ref, q_ref, k_ref, v_ref, g_ref, cos_ref, sin_ref, nrm_ref,
                      o_ref, state_ref, *, chunks):
    c_len = RET_CHUNK
    half = RET_HEAD_DIM // 2

    @pl.when(pl.program_id(2) == 0)
    def _():
        state_ref[...] = jnp.zeros_like(state_ref)

    lg = lg_ref[pl.program_id(1)]
    t_col = lax.broadcasted_iota(jnp.int32, (c_len, 1), 0).astype(F32)
    rel = (lax.broadcasted_iota(jnp.int32, (c_len, c_len), 0)
           - lax.broadcasted_iota(jnp.int32, (c_len, c_len), 1)).astype(F32)
    decay = jnp.where(rel >= 0, jnp.exp(lg * jnp.maximum(rel, 0.0)), 0.0)
    q_weight = jnp.exp(lg * (t_col + 1.0))
    k_weight = jnp.exp(lg * (c_len - 1.0 - t_col))
    chunk_decay = jnp.exp(jnp.full((1, RET_HEAD_DIM), lg * c_len, F32))

    def rotate(x, cos, sin):
        x1 = x[:, :half]
        x2 = x[:, half:]
        return jnp.concatenate([x1 * cos - x2 * sin, x2 * cos + x1 * sin], axis=1)

    for c in range(chunks):
        rows = slice(c * c_len, (c + 1) * c_len)
        cos = cos_ref[rows, :]
        sin = sin_ref[rows, :]
        q = rotate(q_ref[rows, :], cos, sin)
        k = rotate(k_ref[rows, :], cos, sin) * (RET_HEAD_DIM ** -0.5)
        v = v_ref[rows, :].astype(BF16)
        state = state_ref[...]

        scores = lax.dot_general(q.astype(BF16), k.astype(BF16), NT_DIMS,
                                 preferred_element_type=F32) * decay
        intra = jnp.dot(scores.astype(BF16), v, preferred_element_type=F32)
        cross = jnp.dot((q * q_weight).astype(BF16), state.astype(BF16),
                        preferred_element_type=F32)
        kv = lax.dot_general((k * k_weight).astype(BF16), v, TN_DIMS,
                             preferred_element_type=F32)
        state_ref[...] = chunk_decay * state + kv

        out = intra + cross
        centered = out - jnp.mean(out, axis=-1, keepdims=True)
        normed = centered * lax.rsqrt(
            jnp.mean(centered * centered, axis=-1, keepdims=True) + NORM_EPS)
        gated = (normed * nrm_ref[...]) * _silu(g_ref[rows, :])
        o_ref[rows, :] = gated.astype(BF16)


def retention_mixer(proj, ret_norm, batch, seq, *, rows_per_step=512):
    n = proj.shape[0]
    d = RET_HEAD_DIM
    assert seq % rows_per_step == 0 and rows_per_step % RET_CHUNK == 0
    steps = seq // rows_per_step
    log_gamma = jnp.log1p(-jnp.exp2(-5.0 - jnp.arange(RET_HEADS, dtype=F32)))
    half = d // 2
    inv_freq = RET_ROPE_BASE ** (-jnp.arange(half, dtype=F32) * 2.0 / d)
    ang = jnp.arange(seq).astype(F32)[:, None] * inv_freq[None, :]
    cos, sin = jnp.cos(ang), jnp.sin(ang)

    def col_spec(off):
        return pl.BlockSpec((rows_per_step, d), lambda b, h, s: (b * steps + s, off // d + h))

    table_spec = pl.BlockSpec((rows_per_step, half), lambda b, h, s: (s, 0))
    return pl.pallas_call(
        functools.partial(_retention_kernel, chunks=rows_per_step // RET_CHUNK),
        grid=(batch, RET_HEADS, steps),
        in_specs=[pl.BlockSpec(memory_space=pltpu.SMEM),
                  col_spec(OFF_RQ), col_spec(OFF_RK), col_spec(OFF_RV), col_spec(OFF_RG),
                  table_spec, table_spec,
                  pl.BlockSpec((1, d), lambda b, h, s: (0, h))],
        out_specs=pl.BlockSpec((rows_per_step, d), lambda b, h, s: (b * steps + s, h)),
        out_shape=jax.ShapeDtypeStruct((n, RET_WIDTH), BF16),
        scratch_shapes=[pltpu.VMEM((d, d), F32)],
        compiler_params=_compiler_params(3, 32),
        name="retention",
    )(log_gamma, proj, proj, proj, proj, cos, sin, ret_norm.reshape(1, RET_WIDTH))


def _moba_rotate(x, cos, sin_lo, sin_hi):
    half = ROPE_DIM // 2
    return (x * cos + pltpu.roll(x, MOBA_HEAD_DIM - half, 1) * sin_lo
            + pltpu.roll(x, half, 1) * sin_hi)


def _moba_kernel(q_ref, k_ref, v_ref, cq_ref, slq_ref, shq_ref, ck_ref, slk_ref, shk_ref,
                 o_ref, kbf_ref, vbf_ref, kmean_ref, *, n_blocks):
    blk = MOBA_BLOCK
    i = pl.program_id(2)
    scale = MOBA_HEAD_DIM ** -0.5

    @pl.when(i == 0)
    def _():
        kmean_ref[...] = jnp.zeros_like(kmean_ref)
        for j in range(n_blocks):
            rows = slice(j * blk, (j + 1) * blk)
            kr = _moba_rotate(k_ref[rows, :], ck_ref[rows, :], slk_ref[rows, :], shk_ref[rows, :])
            kbf_ref[rows, :] = kr.astype(BF16)
            vbf_ref[rows, :] = v_ref[rows, :].astype(BF16)
            kmean_ref[j:j + 1, :] = jnp.mean(kr, axis=0, keepdims=True)

    q = _moba_rotate(q_ref[...], cq_ref[...], slq_ref[...], shq_ref[...])
    q_bf = q.astype(BF16)

    gate = lax.dot_general(q, kmean_ref[...], NT_DIMS, preferred_element_type=F32,
                           precision=lax.Precision.HIGHEST)
    lane = lax.broadcasted_iota(jnp.int32, (blk, LANES), 1)
    lane_f = lane.astype(F32)
    g = jnp.where(lane < i, gate, -jnp.inf)
    bias = jnp.full((blk, LANES), MASK_VALUE, F32)
    for _ in range(min(MOBA_TOPK, n_blocks)):
        best = jnp.max(g, axis=-1, keepdims=True)
        hit = (g == best) & (g > -jnp.inf)
        first = jnp.min(jnp.where(hit, lane_f, float(LANES)), axis=-1, keepdims=True)
        pick = lane_f == first
        bias = jnp.where(pick, 0.0, bias)
        g = jnp.where(pick, -jnp.inf, g)

    def block_scores(j):
        start = pl.multiple_of(j * blk, blk)
        s = lax.dot_general(q_bf, kbf_ref[pl.ds(start, blk), :], NT_DIMS,
                            preferred_element_type=F32) * scale
        return s, vbf_ref[pl.ds(start, blk), :]

    s, v_blk = block_scores(i)
    row = lax.broadcasted_iota(jnp.int32, (blk, blk), 0)
    col = lax.broadcasted_iota(jnp.int32, (blk, blk), 1)
    s = jnp.where(col <= row, s, MASK_VALUE)
    m0 = jnp.max(s, axis=-1, keepdims=True)
    p = jnp.exp(s - m0)
    l0 = jnp.sum(p, axis=-1, keepdims=True)
    acc0 = jnp.dot(p.astype(BF16), v_blk, preferred_element_type=F32)

    def past_block(j, carry):
        m, l, acc = carry
        s, v_blk = block_scores(j)
        s = s + jnp.sum(jnp.where(lane == j, bias, 0.0), axis=-1, keepdims=True)
        m_new = jnp.maximum(m, jnp.max(s, axis=-1, keepdims=True))
        alpha = jnp.exp(m - m_new)
        p = jnp.exp(s - m_new)
        l = alpha * l + jnp.sum(p, axis=-1, keepdims=True)
        acc = alpha * acc + jnp.dot(p.astype(BF16), v_blk, preferred_element_type=F32)
        return m_new, l, acc

    _, l, acc = lax.fori_loop(0, i, past_block, (m0, l0, acc0))
    o_ref[...] = (acc / l).astype(BF16)


def moba_mixer(proj, batch, seq):
    n = proj.shape[0]
    d = MOBA_HEAD_DIM
    blk = MOBA_BLOCK
    assert seq % blk == 0
    n_blocks = seq // blk
    assert n_blocks <= LANES
    half = ROPE_DIM // 2
    inv_freq = ROPE_THETA ** (-jnp.arange(half, dtype=F32) * 2.0 / ROPE_DIM)
    ang = jnp.arange(seq).astype(F32)[:, None] * inv_freq[None, :]
    cos, sin = jnp.cos(ang), jnp.sin(ang)
    zeros = jnp.zeros((seq, d - ROPE_DIM), F32)
    zero_half = jnp.zeros((seq, half), F32)
    cos_t = jnp.concatenate([cos, cos, jnp.ones_like(zeros)], axis=1)
    sin_lo = jnp.concatenate([-sin, zero_half, zeros], axis=1)
    sin_hi = jnp.concatenate([zero_half, sin, zeros], axis=1)

    q_spec = pl.BlockSpec((blk, d), lambda b, h, i: (b * n_blocks + i, OFF_MQ // d + h))
    k_spec = pl.BlockSpec((seq, d), lambda b, h, i: (b, OFF_MK // d + h))
    v_spec = pl.BlockSpec((seq, d), lambda b, h, i: (b, OFF_MV // d + h))
    tq_spec = pl.BlockSpec((blk, d), lambda b, h, i: (i, 0))
    tk_spec = pl.BlockSpec((seq, d), lambda b, h, i: (0, 0))
    return pl.pallas_call(
        functools.partial(_moba_kernel, n_blocks=n_blocks),
        grid=(batch, MOBA_HEADS, n_blocks),
        in_specs=[q_spec, k_spec, v_spec, tq_spec, tq_spec, tq_spec, tk_spec, tk_spec, tk_spec],
        out_specs=pl.BlockSpec((blk, d), lambda b, h, i: (b * n_blocks + i, h)),
        out_shape=jax.ShapeDtypeStruct((n, MOBA_WIDTH), BF16),
        scratch_shapes=[pltpu.VMEM((seq, d), BF16), pltpu.VMEM((seq, d), BF16),
                        pltpu.VMEM((LANES, d), F32)],
        compiler_params=_compiler_params(3, 40),
        name="moba",
    )(proj, proj, proj, cos_t, sin_lo, sin_hi, cos_t, sin_lo, sin_hi)


def _hgrn_kernel(lbp_ref, tril_ref, q_ref, f_ref, v_ref, g_ref, nrm_ref, o_ref,
                 state_ref, b_ref, k_ref, *, layer, rows_per_step):
    c_len = HGRN_CHUNK
    sub = HGRN_SUB
    n_sub = c_len // sub
    d = HGRN_HEAD_DIM

    @pl.when(pl.program_id(2) == 0)
    def _():
        state_ref[...] = jnp.zeros_like(state_ref)

    params = lbp_ref[...]
    e = jnp.exp(params - jnp.max(params, axis=0, keepdims=True))
    soft = e / jnp.sum(e, axis=0, keepdims=True)
    lower = jnp.zeros((1, d), F32)
    for r in range(1, layer + 1):
        lower = lower + soft[r:r + 1, :]

    forget = lower + (1.0 - lower) * jax.nn.sigmoid(f_ref[...])
    k_ref[...] = 1.0 - forget
    log_f = jnp.log(forget)
    tril = tril_ref[...]
    hi = log_f.astype(BF16)
    rest = log_f - hi.astype(F32)
    mid = rest.astype(BF16)
    lo = (rest - mid.astype(F32)).astype(BF16)
    b_ref[...] = (jnp.dot(tril, hi, preferred_element_type=F32)
                  + jnp.dot(tril, mid, preferred_element_type=F32)
                  + jnp.dot(tril, lo, preferred_element_type=F32))

    row8 = lax.broadcasted_iota(jnp.int32, (SUBLANES, d), 0)
    assert sub & (sub - 1) == 0
    grp_row = jnp.right_shift(lax.broadcasted_iota(jnp.int32, (c_len, LANES), 0),
                              sub.bit_length() - 1)
    col = lax.broadcasted_iota(jnp.int32, (c_len, LANES), 1)
    grp_col = jnp.full((c_len, LANES), -1, jnp.int32)
    bound = 0
    for grp in range(1, n_sub):
        grp_col = jnp.where((col >= bound) & (col < bound + grp * sub), grp, grp_col)
        bound += grp * sub
    assert bound <= LANES
    off_mask = grp_row == grp_col

    def bcast_row(ref, r, rows):
        return jnp.broadcast_to(ref[r:r + 1, :], (rows, d))

    for c in range(rows_per_step // c_len):
        r0 = c * c_len
        rows = slice(r0, r0 + c_len)
        b = b_ref[rows, :]
        q = _silu(q_ref[rows, :])
        k = k_ref[rows, :]
        v = v_ref[rows, :]

        diag_parts = []
        for s_idx in range(n_sub):
            s0 = r0 + s_idx * sub
            for half_idx in range(sub // SUBLANES):
                lo_row = half_idx * SUBLANES
                bi = b_ref[s0 + lo_row:s0 + lo_row + SUBLANES, :]
                qi = q[s_idx * sub + lo_row:s_idx * sub + lo_row + SUBLANES, :]
                acc = jnp.zeros((SUBLANES, d), F32)
                for j in range(lo_row + SUBLANES):
                    bj = bcast_row(b_ref, s0 + j, SUBLANES)
                    kj = bcast_row(k_ref, s0 + j, SUBLANES)
                    vj = bcast_row(v_ref, s0 + j, SUBLANES)
                    expo = bi - bj
                    if j > lo_row:
                        expo = jnp.where(row8 + lo_row >= j, expo, -jnp.inf)
                    a = jnp.sum(qi * kj * jnp.exp(expo), axis=-1, keepdims=True)
                    acc = acc + a * vj
                diag_parts.append(acc)
        diag = jnp.concatenate(diag_parts, axis=0)

        q_parts = [q[:sub, :] * jnp.exp(b[:sub, :])]
        k_parts = []
        v_parts = []
        for s_idx in range(1, n_sub):
            b_anchor = b_ref[r0 + s_idx * sub - 1:r0 + s_idx * sub, :]
            lo_r, hi_r = s_idx * sub, (s_idx + 1) * sub
            q_parts.append(q[lo_r:hi_r, :] * jnp.exp(b[lo_r:hi_r, :] - b_anchor))
            k_parts.append(k[:lo_r, :] * jnp.exp(b_anchor - b[:lo_r, :]))
            v_parts.append(v[:lo_r, :])
        pad = jnp.zeros((LANES - bound, d), F32)
        q_tilde = jnp.concatenate(q_parts, axis=0).astype(BF16)
        k_hat = jnp.concatenate(k_parts + [pad], axis=0).astype(BF16)
        v_hat = jnp.concatenate(v_parts + [pad], axis=0).astype(BF16)
        pair = lax.dot_general(q_tilde, k_hat, NT_DIMS, preferred_element_type=F32)
        off = jnp.dot(jnp.where(off_mask, pair, 0.0).astype(BF16), v_hat,
                      preferred_element_type=F32)

        state_t = state_ref[...]
        cross = lax.dot_general((q * jnp.exp(b)).astype(BF16), state_t.astype(BF16), NT_DIMS,
                                preferred_element_type=F32)
        b_last = b_ref[r0 + c_len - 1:r0 + c_len, :]
        k_dec = k * jnp.exp(b_last - b)
        state_ref[...] = state_t * jnp.exp(b_last) + lax.dot_general(
            v.astype(BF16), k_dec.astype(BF16), TN_DIMS, preferred_element_type=F32)

        out = diag + off + cross
        normed = out * lax.rsqrt(jnp.mean(out * out, axis=-1, keepdims=True) + NORM_EPS)
        gated = (normed * nrm_ref[...]) * _silu(g_ref[rows, :])
        o_ref[rows, :] = gated.astype(BF16)


def hgrn_mixer(proj, lower_bound_params, hgrn_norm, layer, batch, seq, *, rows_per_step=256):
    n = proj.shape[0]
    d = HGRN_HEAD_DIM
    depth = lower_bound_params.shape[0]
    assert seq % rows_per_step == 0 and rows_per_step % HGRN_CHUNK == 0
    steps = seq // rows_per_step
    r = jnp.arange(rows_per_step)
    tril = ((r[None, :] <= r[:, None])
            & (r[None, :] // HGRN_CHUNK == r[:, None] // HGRN_CHUNK)).astype(BF16)

    def col_spec(off):
        return pl.BlockSpec((rows_per_step, d), lambda b, h, s: (b * steps + s, off // d + h))

    return pl.pallas_call(
        functools.partial(_hgrn_kernel, layer=layer, rows_per_step=rows_per_step),
        grid=(batch, HGRN_HEADS, steps),
        in_specs=[pl.BlockSpec((depth, d), lambda b, h, s: (0, h)),
                  pl.BlockSpec((rows_per_step, rows_per_step), lambda b, h, s: (0, 0)),
                  col_spec(OFF_HQ), col_spec(OFF_HF), col_spec(OFF_HI), col_spec(OFF_HG),
                  pl.BlockSpec((1, d), lambda b, h, s: (0, h))],
        out_specs=pl.BlockSpec((rows_per_step, d), lambda b, h, s: (b * steps + s, h)),
        out_shape=jax.ShapeDtypeStruct((n, HGRN_WIDTH), BF16),
        scratch_shapes=[pltpu.VMEM((d, d), F32),
                        pltpu.VMEM((rows_per_step, d), F32),
                        pltpu.VMEM((rows_per_step, d), F32)],
        compiler_params=_compiler_params(3, 32),
        name="hgrn2",
    )(lower_bound_params.astype(F32), tril, proj, proj, proj, proj,
      hgrn_norm.reshape(1, HGRN_WIDTH))


def kernel(x, hgrn_lower_bounds, norm_mix, w_in, ret_norm, hgrn_norm, w_branch_ret, w_branch_moba,
           w_branch_hgrn, w_out, norm_ffn, w_ffn_gate, w_ffn_up, w_ffn_down, final_norm):
    batch, seq, d_model = x.shape
    assert d_model == D_MODEL
    depth = w_in.shape[0]
    h = x.reshape(batch * seq, d_model)
    for layer in range(depth):
        proj = norm_matmul(h, norm_mix[layer], w_in[layer].astype(BF16))
        ret = retention_mixer(proj, ret_norm[layer], batch, seq)
        moba = moba_mixer(proj, batch, seq)
        hgrn = hgrn_mixer(proj, hgrn_lower_bounds, hgrn_norm[layer], layer, batch, seq)
        mixed = merge_branches(ret, moba, hgrn, proj,
                               w_branch_ret[layer].astype(BF16),
                               w_branch_moba[layer].astype(BF16),
                               w_branch_hgrn[layer].astype(BF16))
        h = matmul_residual(mixed, w_out[layer].astype(BF16), h, tn=1024, name="out_proj")
        act = ffn_up(h, norm_ffn[layer], w_ffn_gate[layer].astype(BF16),
                     w_ffn_up[layer].astype(BF16))
        h = matmul_residual(act, w_ffn_down[layer].astype(BF16), h, tn=512, name="ffn_down")
    return rmsnorm(h, final_norm).reshape(batch, seq, d_model)
```

```python
import functools

import jax
import jax.numpy as jnp
from jax import lax
from jax.experimental import pallas as pl
from jax.experimental.pallas import tpu as pltpu

F32 = jnp.float32
BF16 = jnp.bfloat16

D_MODEL = 2048
RET_HEADS = 4
RET_HEAD_DIM = 256
RET_WIDTH = RET_HEADS * RET_HEAD_DIM
RET_CHUNK = 128
RET_ROPE_BASE = 10000.0
MOBA_HEADS = 8
MOBA_HEAD_DIM = 128
MOBA_WIDTH = MOBA_HEADS * MOBA_HEAD_DIM
MOBA_BLOCK = 256
MOBA_TOPK = 3
ROPE_THETA = 500000.0
ROPE_DIM = MOBA_HEAD_DIM // 4
HGRN_HEADS = 8
HGRN_HEAD_DIM = 128
HGRN_WIDTH = HGRN_HEADS * HGRN_HEAD_DIM
HGRN_CHUNK = 64
HGRN_SUB = 16
NORM_EPS = 1e-6
IN_SIZES = (RET_WIDTH,) * 4 + (MOBA_WIDTH,) * 3 + (HGRN_WIDTH,) * 4 + (D_MODEL,) * 3
IN_COLS = sum(IN_SIZES)
IN_OFFS = tuple(sum(IN_SIZES[:i]) for i in range(len(IN_SIZES)))
(OFF_RQ, OFF_RK, OFF_RV, OFF_RG, OFF_MQ, OFF_MK, OFF_MV,
 OFF_HQ, OFF_HF, OFF_HI, OFF_HG, OFF_GR, OFF_GM, OFF_GH) = IN_OFFS

V7X_VMEM_BYTES = 64 * 1024 * 1024
LANES = 128
SUBLANES = 8
MASK_VALUE = -1e30

NT_DIMS = (((1,), (1,)), ((), ()))
TN_DIMS = (((0,), (0,)), ((), ()))


def _compiler_params(n_grid_dims, vmem_mib):
    assert vmem_mib * 1024 * 1024 < V7X_VMEM_BYTES
    return pltpu.CompilerParams(
        dimension_semantics=("arbitrary",) * n_grid_dims,
        vmem_limit_bytes=vmem_mib * 1024 * 1024)


def _silu(x):
    return x * jax.nn.sigmoid(x)


def _rms_normalize(x, g):
    r = lax.rsqrt(jnp.mean(x * x, axis=-1, keepdims=True) + NORM_EPS)
    return x * r * g


def _norm_matmul_kernel(x_ref, g_ref, w_ref, o_ref, h_ref):
    @pl.when(pl.program_id(1) == 0)
    def _():
        h_ref[...] = _rms_normalize(x_ref[...], g_ref[...]).astype(BF16)

    o_ref[...] = jnp.dot(h_ref[...], w_ref[...], preferred_element_type=F32)


def norm_matmul(x, g, w, *, tm=1024, tn=1024):
    n, d = x.shape
    cols = w.shape[1]
    assert n % tm == 0 and cols % tn == 0
    return pl.pallas_call(
        _norm_matmul_kernel,
        grid=(n // tm, cols // tn),
        in_specs=[pl.BlockSpec((tm, d), lambda i, j: (i, 0)),
                  pl.BlockSpec((1, d), lambda i, j: (0, 0)),
                  pl.BlockSpec((d, tn), lambda i, j: (0, j))],
        out_specs=pl.BlockSpec((tm, tn), lambda i, j: (i, j)),
        out_shape=jax.ShapeDtypeStruct((n, cols), F32),
        scratch_shapes=[pltpu.VMEM((tm, d), BF16)],
        compiler_params=_compiler_params(2, 48),
        name="norm_inproj",
    )(x, g.reshape(1, d), w)


def _ffn_up_kernel(x_ref, g_ref, wg_ref, wu_ref, o_ref, h_ref):
    @pl.when(pl.program_id(1) == 0)
    def _():
        h_ref[...] = _rms_normalize(x_ref[...], g_ref[...]).astype(BF16)

    h = h_ref[...]
    gate = jnp.dot(h, wg_ref[...], preferred_element_type=F32)
    up = jnp.dot(h, wu_ref[...], preferred_element_type=F32)
    o_ref[...] = (_silu(gate) * up).astype(BF16)


def ffn_up(x, g, wg, wu, *, tm=1024, tn=512):
    n, d = x.shape
    hidden = wg.shape[1]
    assert n % tm == 0 and hidden % tn == 0
    return pl.pallas_call(
        _ffn_up_kernel,
        grid=(n // tm, hidden // tn),
        in_specs=[pl.BlockSpec((tm, d), lambda i, j: (i, 0)),
                  pl.BlockSpec((1, d), lambda i, j: (0, 0)),
                  pl.BlockSpec((d, tn), lambda i, j: (0, j)),
                  pl.BlockSpec((d, tn), lambda i, j: (0, j))],
        out_specs=pl.BlockSpec((tm, tn), lambda i, j: (i, j)),
        out_shape=jax.ShapeDtypeStruct((n, hidden), BF16),
        scratch_shapes=[pltpu.VMEM((tm, d), BF16)],
        compiler_params=_compiler_params(2, 48),
        name="ffn_up",
    )(x, g.reshape(1, d), wg, wu)


def _matmul_residual_kernel(a_ref, w_ref, r_ref, o_ref):
    o_ref[...] = r_ref[...] + jnp.dot(a_ref[...], w_ref[...], preferred_element_type=F32)


def matmul_residual(a, w, res, *, tm=1024, tn=512, name):
    n, k = a.shape
    cols = w.shape[1]
    assert n % tm == 0 and cols % tn == 0
    return pl.pallas_call(
        _matmul_residual_kernel,
        grid=(n // tm, cols // tn),
        in_specs=[pl.BlockSpec((tm, k), lambda i, j: (i, 0)),
                  pl.BlockSpec((k, tn), lambda i, j: (0, j)),
                  pl.BlockSpec((tm, tn), lambda i, j: (i, j))],
        out_specs=pl.BlockSpec((tm, tn), lambda i, j: (i, j)),
        out_shape=jax.ShapeDtypeStruct((n, cols), F32),
        compiler_params=_compiler_params(2, 52),
        name=name,
    )(a, w, res)


def _merge_kernel(ret_ref, moba_ref, hgrn_ref, gr_ref, gm_ref, gh_ref,
                  wr_ref, wm_ref, wh_ref, o_ref):
    def branch(gate_ref, a_ref, w_ref):
        return jax.nn.sigmoid(gate_ref[...]) * jnp.dot(
            a_ref[...], w_ref[...], preferred_element_type=F32)

    mixed = (branch(gr_ref, ret_ref, wr_ref) + branch(gm_ref, moba_ref, wm_ref)
             + branch(gh_ref, hgrn_ref, wh_ref))
    o_ref[...] = mixed.astype(BF16)


def merge_branches(ret, moba, hgrn, proj, wr, wm, wh, *, tm=512, tn=1024):
    n = ret.shape[0]
    assert n % tm == 0 and D_MODEL % tn == 0
    assert OFF_GR % tn == 0 and OFF_GM % tn == 0 and OFF_GH % tn == 0

    def branch_spec(width):
        return pl.BlockSpec((tm, width), lambda j, i: (i, 0))

    def gate_spec(off):
        return pl.BlockSpec((tm, tn), lambda j, i: (i, off // tn + j))

    def weight_spec(width):
        return pl.BlockSpec((width, tn), lambda j, i: (0, j))

    return pl.pallas_call(
        _merge_kernel,
        grid=(D_MODEL // tn, n // tm),
        in_specs=[branch_spec(RET_WIDTH), branch_spec(MOBA_WIDTH), branch_spec(HGRN_WIDTH),
                  gate_spec(OFF_GR), gate_spec(OFF_GM), gate_spec(OFF_GH),
                  weight_spec(RET_WIDTH), weight_spec(MOBA_WIDTH), weight_spec(HGRN_WIDTH)],
        out_specs=pl.BlockSpec((tm, tn), lambda j, i: (i, j)),
        out_shape=jax.ShapeDtypeStruct((n, D_MODEL), BF16),
        compiler_params=_compiler_params(2, 48),
        name="merge_branches",
    )(ret, moba, hgrn, proj, proj, proj, wr, wm, wh)


def _rmsnorm_kernel(x_ref, g_ref, o_ref):
    o_ref[...] = _rms_normalize(x_ref[...], g_ref[...])


def rmsnorm(x, g, *, tm=1024):
    n, d = x.shape
    assert n % tm == 0
    return pl.pallas_call(
        _rmsnorm_kernel,
        grid=(n // tm,),
        in_specs=[pl.BlockSpec((tm, d), lambda i: (i, 0)),
                  pl.BlockSpec((1, d), lambda i: (0, 0))],
        out_specs=pl.BlockSpec((tm, d), lambda i: (i, 0)),
        out_shape=jax.ShapeDtypeStruct((n, d), F32),
        compiler_params=_compiler_params(1, 40),
        name="final_rmsnorm",
    )(x, g.reshape(1, d))


def _retention_kernel(lg_ref, q_ref, k_ref, v_ref, g_ref, cos_ref, sin_ref, nrm_ref,
                      o_ref, state_ref, *, chunks):
    c_len = RET_CHUNK
    half = RET_HEAD_DIM // 2

    @pl.when(pl.program_id(2) == 0)
    def _():
        state_ref[...] = jnp.zeros_like(state_ref)

    lg = lg_ref[pl.program_id(1)]
    t_col = lax.broadcasted_iota(jnp.int32, (c_len, 1), 0).astype(F32)
    rel = (lax.broadcasted_iota(jnp.int32, (c_len, c_len), 0)
           - lax.broadcasted_iota(jnp.int32, (c_len, c_len), 1)).astype(F32)
    decay = jnp.where(rel >= 0, jnp.exp(lg * jnp.maximum(rel, 0.0)), 0.0)
    q_weight = jnp.exp(lg * (t_col + 1.0))
    k_weight = jnp.exp(lg * (c_len - 1.0 - t_col))
    chunk_decay = jnp.exp(jnp.full((1, RET_HEAD_DIM), lg * c_len, F32))

    def rotate(x, cos, sin):
        x1 = x[:, :half]
        x2 = x[:, half:]
        return jnp.concatenate([x1 * cos - x2 * sin, x2 * cos + x1 * sin], axis=1)

    for c in range(chunks):
        rows = slice(c * c_len, (c + 1) * c_len)
        cos = cos_ref[rows, :]
        sin = sin_ref[rows, :]
        q = rotate(q_ref[rows, :], cos, sin)
        k = rotate(k_ref[rows, :], cos, sin) * (RET_HEAD_DIM ** -0.5)
        v = v_ref[rows, :].astype(BF16)
        state = state_ref[...]

        scores = lax.dot_general(q.astype(BF16), k.astype(BF16), NT_DIMS,
                                 preferred_element_type=F32) * decay
        intra = jnp.dot(scores.astype(BF16), v, preferred_element_type=F32)
        cross = jnp.dot((q * q_weight).astype(BF16), state.astype(BF16),
                        preferred_element_type=F32)
        kv = lax.dot_general((k * k_weight).astype(BF16), v, TN_DIMS,
                             preferred_element_type=F32)
        state_ref[...] = chunk_decay * state + kv

        out = intra + cross
        centered = out - jnp.mean(out, axis=-1, keepdims=True)
        normed = centered * lax.rsqrt(
            jnp.mean(centered * centered, axis=-1, keepdims=True) + NORM_EPS)
        gated = (normed * nrm_ref[...]) * _silu(g_ref[rows, :])
        o_ref[rows, :] = gated.astype(BF16)


def retention_mixer(proj, ret_norm, batch, seq, *, rows_per_step=512):
    n = proj.shape[0]
    d = RET_HEAD_DIM
    assert seq % rows_per_step == 0 and rows_per_step % RET_CHUNK == 0
    steps = seq // rows_per_step
    log_gamma = jnp.log1p(-jnp.exp2(-5.0 - jnp.arange(RET_HEADS, dtype=F32)))
    half = d // 2
    inv_freq = RET_ROPE_BASE ** (-jnp.arange(half, dtype=F32) * 2.0 / d)
    ang = jnp.arange(seq).astype(F32)[:, None] * inv_freq[None, :]
    cos, sin = jnp.cos(ang), jnp.sin(ang)

    def col_spec(off):
        return pl.BlockSpec((rows_per_step, d), lambda b, h, s: (b * steps + s, off // d + h))

    table_spec = pl.BlockSpec((rows_per_step, half), lambda b, h, s: (s, 0))
    return pl.pallas_call(
        functools.partial(_retention_kernel, chunks=rows_per_step // RET_CHUNK),
        grid=(batch, RET_HEADS, steps),
        in_specs=[pl.BlockSpec(memory_space=pltpu.SMEM),
                  col_spec(OFF_RQ), col_spec(OFF_RK), col_spec(OFF_RV), col_spec(OFF_RG),
                  table_spec, table_spec,
                  pl.BlockSpec((1, d), lambda b, h, s: (0, h))],
        out_specs=pl.BlockSpec((rows_per_step, d), lambda b, h, s: (b * steps + s, h)),
        out_shape=jax.ShapeDtypeStruct((n, RET_WIDTH), BF16),
        scratch_shapes=[pltpu.VMEM((d, d), F32)],
        compiler_params=_compiler_params(3, 32),
        name="retention",
    )(log_gamma, proj, proj, proj, proj, cos, sin, ret_norm.reshape(1, RET_WIDTH))


MOBA_UNROLL = 15

def _moba_rotate(x, cos, sin_lo, sin_hi):
    half = ROPE_DIM // 2
    return (x * cos + pltpu.roll(x, MOBA_HEAD_DIM - half, 1) * sin_lo
            + pltpu.roll(x, half, 1) * sin_hi)


def _split_dot(a, b):
    a_hi = a.astype(BF16)
    a_lo = (a - a_hi.astype(F32)).astype(BF16)
    b_hi = b.astype(BF16)
    b_lo = (b - b_hi.astype(F32)).astype(BF16)
    return (jnp.dot(a_hi, b_hi, preferred_element_type=F32)
            + jnp.dot(a_hi, b_lo, preferred_element_type=F32)
            + jnp.dot(a_lo, b_hi, preferred_element_type=F32))


def _moba_kernel(q_ref, k_ref, v_ref, cos_ref, slo_ref, shi_ref, o_ref,
                 kaug_ref, vt_ref, kmean_ref, qt_ref, s_ref, m_ref, l_ref, acc_ref, *, n_blocks):
    blk = MOBA_BLOCK
    d = MOBA_HEAD_DIM
    p = pl.program_id(2)
    n_past = n_blocks - 1
    gate_rows = kmean_ref.shape[0]
    groups = blk // SUBLANES
    q_scale = (d ** -0.5) * 1.4426950408889634

    def rotate(ref, start):
        rows = pl.ds(start, blk)
        return _moba_rotate(ref[rows, :], cos_ref[rows, :], slo_ref[rows, :], shi_ref[rows, :])

    @pl.when(p == 0)
    def _():
        lane = lax.broadcasted_iota(jnp.int32, (blk, LANES), 1)
        kmean_ref[...] = jnp.zeros_like(kmean_ref)
        for j in range(n_blocks):
            kr = rotate(k_ref, j * blk)
            kaug_ref[j, :, :d] = kr.astype(BF16)
            kaug_ref[j, :, d:] = jnp.where(lane == j, 1.0, 0.0).astype(BF16)
            vt_ref[j] = v_ref[j * blk:(j + 1) * blk, :].T.astype(BF16)
            kmean_ref[j:j + 1, :] = jnp.mean(kr, axis=0, keepdims=True)

    def group_reduce(x, op):
        return op(x.reshape(groups, SUBLANES, blk), axis=0)

    def setup_tile(w, i):
        q_t = rotate(q_ref, pl.multiple_of(i * blk, blk)).T
        gate = _split_dot(kmean_ref[...], q_t)
        row = lax.broadcasted_iota(jnp.int32, (gate_rows, blk), 0)
        row_f = row.astype(F32)
        g = jnp.where(row < i, gate, -jnp.inf)
        bias = jnp.full((gate_rows, blk), MASK_VALUE, F32)
        for _ in range(min(MOBA_TOPK, n_blocks)):
            best = jnp.max(g, axis=0, keepdims=True)
            hit = (g == best) & (g > -jnp.inf)
            first = jnp.min(jnp.where(hit, row_f, float(LANES)), axis=0, keepdims=True)
            pick = row_f == first
            bias = jnp.where(pick, 0.0, bias)
            g = jnp.where(pick, -jnp.inf, g)
        bias = jnp.where(row == i, 0.0, bias)
        pad = jnp.zeros((LANES - gate_rows, blk), F32)
        qt_ref[w] = jnp.concatenate([q_t * q_scale, bias, pad], axis=0).astype(BF16)

        s_t = jnp.dot(kaug_ref[i], qt_ref[w], preferred_element_type=F32)
        key = lax.broadcasted_iota(jnp.int32, (blk, blk), 0)
        qry = lax.broadcasted_iota(jnp.int32, (blk, blk), 1)
        s_t = jnp.where(key <= qry, s_t, MASK_VALUE)
        s_ref[n_past + w] = s_t
        m_ref[w] = group_reduce(s_t, jnp.max)

    def tile_of(t):
        w = jnp.where(t >= p, 1, 0)
        return w, t - w * p

    def score_block(t, carry):
        w, j = tile_of(t)
        s_t = jnp.dot(kaug_ref[j], qt_ref[w], preferred_element_type=F32)
        s_ref[t] = s_t
        m_ref[w] = jnp.maximum(m_ref[w], group_reduce(s_t, jnp.max))
        return carry

    def weigh_block(slot, w, j, first):
        s_t = s_ref[slot].reshape(groups, SUBLANES, blk)
        p_t = jnp.exp2(s_t - m_ref[w][None])
        l_part = jnp.sum(p_t, axis=0)
        pv = jnp.dot(vt_ref[j], p_t.reshape(blk, blk).astype(BF16),
                     preferred_element_type=F32)
        if first:
            l_ref[w] = l_part
            acc_ref[w] = pv
        else:
            l_ref[w] += l_part
            acc_ref[w] += pv

    def weigh_past_block(t, carry):
        w, j = tile_of(t)
        weigh_block(t, w, j, first=False)
        return carry

    tiles = ((0, p), (1, n_past - p))
    for w, i in tiles:
        setup_tile(w, i)
    lax.fori_loop(0, n_past, score_block, 0, unroll=MOBA_UNROLL)
    for w, i in tiles:
        m_ref[w] = jnp.broadcast_to(jnp.max(m_ref[w], axis=0, keepdims=True), (SUBLANES, blk))
        weigh_block(n_past + w, w, i, first=True)
    lax.fori_loop(0, n_past, weigh_past_block, 0, unroll=MOBA_UNROLL)
    for w, i in tiles:
        l = jnp.sum(l_ref[w], axis=0, keepdims=True)
        o_ref[pl.ds(pl.multiple_of(i * blk, blk), blk), :] = (acc_ref[w] / l).T.astype(BF16)


def moba_mixer(proj, batch, seq):
    n = proj.shape[0]
    d = MOBA_HEAD_DIM
    blk = MOBA_BLOCK
    assert seq % blk == 0
    n_blocks = seq // blk
    assert n_blocks <= LANES and n_blocks % 2 == 0
    half = ROPE_DIM // 2
    inv_freq = ROPE_THETA ** (-jnp.arange(half, dtype=F32) * 2.0 / ROPE_DIM)
    ang = jnp.arange(seq).astype(F32)[:, None] * inv_freq[None, :]
    cos, sin = jnp.cos(ang), jnp.sin(ang)
    zeros = jnp.zeros((seq, d - ROPE_DIM), F32)
    zero_half = jnp.zeros((seq, half), F32)
    cos_t = jnp.concatenate([cos, cos, jnp.ones_like(zeros)], axis=1)
    sin_lo = jnp.concatenate([-sin, zero_half, zeros], axis=1)
    sin_hi = jnp.concatenate([zero_half, sin, zeros], axis=1)

    def col_spec(off):
        return pl.BlockSpec((seq, d), lambda b, h, p: (b, off // d + h))

    table_spec = pl.BlockSpec((seq, d), lambda b, h, p: (0, 0))
    gate_rows = -(-n_blocks // SUBLANES) * SUBLANES
    return pl.pallas_call(
        functools.partial(_moba_kernel, n_blocks=n_blocks),
        grid=(batch, MOBA_HEADS, n_blocks // 2),
        in_specs=[col_spec(OFF_MQ), col_spec(OFF_MK), col_spec(OFF_MV),
                  table_spec, table_spec, table_spec],
        out_specs=pl.BlockSpec((seq, d), lambda b, h, p: (b, h)),
        out_shape=jax.ShapeDtypeStruct((n, MOBA_WIDTH), BF16),
        scratch_shapes=[pltpu.VMEM((n_blocks, blk, 2 * d), BF16),
                        pltpu.VMEM((n_blocks, d, blk), BF16),
                        pltpu.VMEM((gate_rows, d), F32),
                        pltpu.VMEM((2, 2 * d, blk), BF16),
                        pltpu.VMEM((n_blocks + 1, blk, blk), F32),
                        pltpu.VMEM((2, SUBLANES, blk), F32),
                        pltpu.VMEM((2, SUBLANES, blk), F32),
                        pltpu.VMEM((2, d, blk), F32)],
        compiler_params=_compiler_params(3, 48),
        name="moba",
    )(proj, proj, proj, cos_t, sin_lo, sin_hi)


def _hgrn_kernel(lbp_ref, tril_ref, q_ref, f_ref, v_ref, g_ref, nrm_ref, o_ref,
                 state_ref, b_ref, k_ref, *, layer, rows_per_step):
    c_len = HGRN_CHUNK
    sub = HGRN_SUB
    n_sub = c_len // sub
    d = HGRN_HEAD_DIM

    @pl.when(pl.program_id(2) == 0)
    def _():
        state_ref[...] = jnp.zeros_like(state_ref)

    params = lbp_ref[...]
    e = jnp.exp(params - jnp.max(params, axis=0, keepdims=True))
    soft = e / jnp.sum(e, axis=0, keepdims=True)
    lower = jnp.zeros((1, d), F32)
    for r in range(1, layer + 1):
        lower = lower + soft[r:r + 1, :]

    forget = lower + (1.0 - lower) * jax.nn.sigmoid(f_ref[...])
    k_ref[...] = 1.0 - forget
    log_f = jnp.log2(forget)
    tril = tril_ref[...]
    hi = log_f.astype(BF16)
    rest = log_f - hi.astype(F32)
    mid = rest.astype(BF16)
    lo = (rest - mid.astype(F32)).astype(BF16)
    b_ref[...] = (jnp.dot(tril, hi, preferred_element_type=F32)
                  + jnp.dot(tril, mid, preferred_element_type=F32)
                  + jnp.dot(tril, lo, preferred_element_type=F32))

    row8 = lax.broadcasted_iota(jnp.int32, (SUBLANES, d), 0)
    assert sub & (sub - 1) == 0
    bound = sub * n_sub * (n_sub - 1) // 2
    width = -(-bound // LANES) * LANES
    grp_row = jnp.right_shift(lax.broadcasted_iota(jnp.int32, (c_len, width), 0),
                              sub.bit_length() - 1)
    col = lax.broadcasted_iota(jnp.int32, (c_len, width), 1)
    grp_col = jnp.full((c_len, width), -1, jnp.int32)
    for grp in range(1, n_sub):
        start = sub * grp * (grp - 1) // 2
        grp_col = jnp.where((col >= start) & (col < start + grp * sub), grp, grp_col)
    off_mask = grp_row == grp_col

    def bcast_row(ref, r, rows):
        return jnp.broadcast_to(ref[r:r + 1, :], (rows, d))

    for c in range(rows_per_step // c_len):
        r0 = c * c_len
        rows = slice(r0, r0 + c_len)
        b = b_ref[rows, :]
        q = _silu(q_ref[rows, :])
        k = k_ref[rows, :]
        v = v_ref[rows, :]

        diag_parts = []
        for s_idx in range(n_sub):
            s0 = r0 + s_idx * sub
            for half_idx in range(sub // SUBLANES):
                lo_row = half_idx * SUBLANES
                bi = b_ref[s0 + lo_row:s0 + lo_row + SUBLANES, :]
                qi = q[s_idx * sub + lo_row:s_idx * sub + lo_row + SUBLANES, :]
                acc = jnp.zeros((SUBLANES, d), F32)
                for j in range(lo_row + SUBLANES):
                    bj = bcast_row(b_ref, s0 + j, SUBLANES)
                    kj = bcast_row(k_ref, s0 + j, SUBLANES)
                    vj = bcast_row(v_ref, s0 + j, SUBLANES)
                    expo = bi - bj
                    if j > lo_row:
                        expo = jnp.where(row8 + lo_row >= j, expo, -jnp.inf)
                    a = jnp.sum(qi * kj * jnp.exp2(expo), axis=-1, keepdims=True)
                    acc = acc + a * vj
                diag_parts.append(acc)
        diag = jnp.concatenate(diag_parts, axis=0)

        q_parts = [q[:sub, :] * jnp.exp2(b[:sub, :])]
        k_parts = []
        v_parts = []
        for s_idx in range(1, n_sub):
            b_anchor = b_ref[r0 + s_idx * sub - 1:r0 + s_idx * sub, :]
            lo_r, hi_r = s_idx * sub, (s_idx + 1) * sub
            q_parts.append(q[lo_r:hi_r, :] * jnp.exp2(b[lo_r:hi_r, :] - b_anchor))
            k_parts.append(k[:lo_r, :] * jnp.exp2(b_anchor - b[:lo_r, :]))
            v_parts.append(v[:lo_r, :])
        pad = [jnp.zeros((width - bound, d), F32)] if width > bound else []
        q_tilde = jnp.concatenate(q_parts, axis=0).astype(BF16)
        k_hat = jnp.concatenate(k_parts + pad, axis=0).astype(BF16)
        v_hat = jnp.concatenate(v_parts + pad, axis=0).astype(BF16)
        pair = lax.dot_general(q_tilde, k_hat, NT_DIMS, preferred_element_type=F32)
        off = jnp.dot(jnp.where(off_mask, pair, 0.0).astype(BF16), v_hat,
                      preferred_element_type=F32)

        state_t = state_ref[...]
        cross = lax.dot_general((q * jnp.exp2(b)).astype(BF16), state_t.astype(BF16), NT_DIMS,
                                preferred_element_type=F32)
        b_last = b_ref[r0 + c_len - 1:r0 + c_len, :]
        k_dec = k * jnp.exp2(b_last - b)
        state_ref[...] = state_t * jnp.exp2(b_last) + lax.dot_general(
            v.astype(BF16), k_dec.astype(BF16), TN_DIMS, preferred_element_type=F32)

        out = diag + off + cross
        normed = out * lax.rsqrt(jnp.mean(out * out, axis=-1, keepdims=True) + NORM_EPS)
        gated = (normed * nrm_ref[...]) * _silu(g_ref[rows, :])
        o_ref[rows, :] = gated.astype(BF16)


def hgrn_mixer(proj, lower_bound_params, hgrn_norm, layer, batch, seq, *, rows_per_step=256):
    n = proj.shape[0]
    d = HGRN_HEAD_DIM
    depth = lower_bound_params.shape[0]
    assert seq % rows_per_step == 0 and rows_per_step % HGRN_CHUNK == 0
    steps = seq // rows_per_step
    r = jnp.arange(rows_per_step)
    tril = ((r[None, :] <= r[:, None])
            & (r[None, :] // HGRN_CHUNK == r[:, None] // HGRN_CHUNK)).astype(BF16)

    def col_spec(off):
        return pl.BlockSpec((rows_per_step, d), lambda b, h, s: (b * steps + s, off // d + h))

    return pl.pallas_call(
        functools.partial(_hgrn_kernel, layer=layer, rows_per_step=rows_per_step),
        grid=(batch, HGRN_HEADS, steps),
        in_specs=[pl.BlockSpec((depth, d), lambda b, h, s: (0, h)),
                  pl.BlockSpec((rows_per_step, rows_per_step), lambda b, h, s: (0, 0)),
                  col_spec(OFF_HQ), col_spec(OFF_HF), col_spec(OFF_HI), col_spec(OFF_HG),
                  pl.BlockSpec((1, d), lambda b, h, s: (0, h))],
        out_specs=pl.BlockSpec((rows_per_step, d), lambda b, h, s: (b * steps + s, h)),
        out_shape=jax.ShapeDtypeStruct((n, HGRN_WIDTH), BF16),
        scratch_shapes=[pltpu.VMEM((d, d), F32),
                        pltpu.VMEM((rows_per_step, d), F32),
                        pltpu.VMEM((rows_per_step, d), F32)],
        compiler_params=_compiler_params(3, 32),
        name="hgrn2",
    )(lower_bound_params.astype(F32), tril, proj, proj, proj, proj,
      hgrn_norm.reshape(1, HGRN_WIDTH))


def kernel(x, hgrn_lower_bounds, norm_mix, w_in, ret_norm, hgrn_norm, w_branch_ret, w_branch_moba,
           w_branch_hgrn, w_out, norm_ffn, w_ffn_gate, w_ffn_up, w_ffn_down, final_norm):
    batch, seq, d_model = x.shape
    assert d_model == D_MODEL
    depth = w_in.shape[0]
    h = x.reshape(batch * seq, d_model)
    for layer in range(depth):
        proj = norm_matmul(h, norm_mix[layer], w_in[layer].astype(BF16))
        ret = retention_mixer(proj, ret_norm[layer], batch, seq)
        moba = moba_mixer(proj, batch, seq)
        hgrn = hgrn_mixer(proj, hgrn_lower_bounds, hgrn_norm[layer], layer, batch, seq)
        mixed = merge_branches(ret, moba, hgrn, proj,
                               w_branch_ret[layer].astype(BF16),
                               w_branch_moba[layer].astype(BF16),
                               w_branch_hgrn[layer].astype(BF16))
        h = matmul_residual(mixed, w_out[layer].astype(BF16), h, tn=1024, name="out_proj")
        act = ffn_up(h, norm_ffn[layer], w_ffn_gate[layer].astype(BF16),
                     w_ffn_up[layer].astype(BF16))
        h = matmul_residual(act, w_ffn_down[layer].astype(BF16), h, tn=512, name="ffn_down")
    return rmsnorm(h, final_norm).reshape(batch, seq, d_model)
```

```python
import functools

import jax
import jax.numpy as jnp
from jax import lax
from jax.experimental import pallas as pl
from jax.experimental.pallas import tpu as pltpu

F32 = jnp.float32
BF16 = jnp.bfloat16

D_MODEL = 2048
RET_HEADS = 4
RET_HEAD_DIM = 256
RET_WIDTH = RET_HEADS * RET_HEAD_DIM
RET_CHUNK = 128
RET_ROPE_BASE = 10000.0
MOBA_HEADS = 8
MOBA_HEAD_DIM = 128
MOBA_WIDTH = MOBA_HEADS * MOBA_HEAD_DIM
MOBA_BLOCK = 256
MOBA_TOPK = 3
ROPE_THETA = 500000.0
ROPE_DIM = MOBA_HEAD_DIM // 4
HGRN_HEADS = 8
HGRN_HEAD_DIM = 128
HGRN_WIDTH = HGRN_HEADS * HGRN_HEAD_DIM
HGRN_CHUNK = 64
HGRN_SUB = 16
HGRN_CUMSUM_ROWS = 256
NORM_EPS = 1e-6
IN_SIZES = (RET_WIDTH,) * 4 + (MOBA_WIDTH,) * 3 + (HGRN_WIDTH,) * 4 + (D_MODEL,) * 3
IN_COLS = sum(IN_SIZES)
IN_OFFS = tuple(sum(IN_SIZES[:i]) for i in range(len(IN_SIZES)))
(OFF_RQ, OFF_RK, OFF_RV, OFF_RG, OFF_MQ, OFF_MK, OFF_MV,
 OFF_HQ, OFF_HF, OFF_HI, OFF_HG, OFF_GR, OFF_GM, OFF_GH) = IN_OFFS

V7X_VMEM_BYTES = 64 * 1024 * 1024
LANES = 128
SUBLANES = 8
MASK_VALUE = -1e30

NT_DIMS = (((1,), (1,)), ((), ()))
TN_DIMS = (((0,), (0,)), ((), ()))


def _compiler_params(n_grid_dims, vmem_mib):
    assert vmem_mib * 1024 * 1024 < V7X_VMEM_BYTES
    return pltpu.CompilerParams(
        dimension_semantics=("arbitrary",) * n_grid_dims,
        vmem_limit_bytes=vmem_mib * 1024 * 1024)


def _silu(x):
    return x * jax.nn.sigmoid(x)


def _layer_weight_spec(layer, rows, tn, col_of):
    return pl.BlockSpec((None, rows, tn), lambda *g: (layer, 0, col_of(*g)))


def _rms_normalize(x, g):
    r = lax.rsqrt(jnp.mean(x * x, axis=-1, keepdims=True) + NORM_EPS)
    return x * r * g


def _norm_matmul_kernel(x_ref, g_ref, w_ref, o_ref, h_ref):
    @pl.when(pl.program_id(1) == 0)
    def _():
        h_ref[...] = _rms_normalize(x_ref[...], g_ref[...]).astype(BF16)

    o_ref[...] = jnp.dot(h_ref[...], w_ref[...], preferred_element_type=F32)


def norm_matmul(x, g, w, layer, *, tm=1024, tn=1024):
    n, d = x.shape
    cols = w.shape[2]
    assert n % tm == 0 and cols % tn == 0
    return pl.pallas_call(
        _norm_matmul_kernel,
        grid=(n // tm, cols // tn),
        in_specs=[pl.BlockSpec((tm, d), lambda i, j: (i, 0)),
                  pl.BlockSpec((1, d), lambda i, j: (0, 0)),
                  _layer_weight_spec(layer, d, tn, lambda i, j: j)],
        out_specs=pl.BlockSpec((tm, tn), lambda i, j: (i, j)),
        out_shape=jax.ShapeDtypeStruct((n, cols), F32),
        scratch_shapes=[pltpu.VMEM((tm, d), BF16)],
        compiler_params=_compiler_params(2, 48),
        name="norm_inproj",
    )(x, g.reshape(1, d), w)


def _ffn_up_kernel(x_ref, g_ref, wg_ref, wu_ref, o_ref, h_ref):
    @pl.when(pl.program_id(1) == 0)
    def _():
        h_ref[...] = _rms_normalize(x_ref[...], g_ref[...]).astype(BF16)

    h = h_ref[...]
    gate = jnp.dot(h, wg_ref[...], preferred_element_type=F32)
    up = jnp.dot(h, wu_ref[...], preferred_element_type=F32)
    o_ref[...] = (_silu(gate) * up).astype(BF16)


def ffn_up(x, g, wg, wu, layer, *, tm=1024, tn=512):
    n, d = x.shape
    hidden = wg.shape[2]
    assert n % tm == 0 and hidden % tn == 0
    w_spec = _layer_weight_spec(layer, d, tn, lambda i, j: j)
    return pl.pallas_call(
        _ffn_up_kernel,
        grid=(n // tm, hidden // tn),
        in_specs=[pl.BlockSpec((tm, d), lambda i, j: (i, 0)),
                  pl.BlockSpec((1, d), lambda i, j: (0, 0)),
                  w_spec, w_spec],
        out_specs=pl.BlockSpec((tm, tn), lambda i, j: (i, j)),
        out_shape=jax.ShapeDtypeStruct((n, hidden), BF16),
        scratch_shapes=[pltpu.VMEM((tm, d), BF16)],
        compiler_params=_compiler_params(2, 48),
        name="ffn_up",
    )(x, g.reshape(1, d), wg, wu)


def _matmul_residual_kernel(a_ref, w_ref, r_ref, o_ref):
    o_ref[...] = r_ref[...] + jnp.dot(a_ref[...], w_ref[...], preferred_element_type=F32)


def matmul_residual(a, w, res, layer, *, tm=1024, tn=512, name):
    n, k = a.shape
    cols = w.shape[2]
    assert n % tm == 0 and cols % tn == 0
    return pl.pallas_call(
        _matmul_residual_kernel,
        grid=(n // tm, cols // tn),
        in_specs=[pl.BlockSpec((tm, k), lambda i, j: (i, 0)),
                  _layer_weight_spec(layer, k, tn, lambda i, j: j),
                  pl.BlockSpec((tm, tn), lambda i, j: (i, j))],
        out_specs=pl.BlockSpec((tm, tn), lambda i, j: (i, j)),
        out_shape=jax.ShapeDtypeStruct((n, cols), F32),
        compiler_params=_compiler_params(2, 52),
        name=name,
    )(a, w, res)


def _merge_kernel(ret_ref, moba_ref, hgrn_ref, gr_ref, gm_ref, gh_ref,
                  wr_ref, wm_ref, wh_ref, o_ref):
    def branch(gate_ref, a_ref, w_ref):
        return jax.nn.sigmoid(gate_ref[...]) * jnp.dot(
            a_ref[...], w_ref[...], preferred_element_type=F32)

    mixed = (branch(gr_ref, ret_ref, wr_ref) + branch(gm_ref, moba_ref, wm_ref)
             + branch(gh_ref, hgrn_ref, wh_ref))
    o_ref[...] = mixed.astype(BF16)


def merge_branches(ret, moba, hgrn, proj, wr, wm, wh, layer, *, tm=512, tn=1024):
    n = ret.shape[0]
    assert n % tm == 0 and D_MODEL % tn == 0
    assert OFF_GR % tn == 0 and OFF_GM % tn == 0 and OFF_GH % tn == 0

    def branch_spec(width):
        return pl.BlockSpec((tm, width), lambda j, i: (i, 0))

    def gate_spec(off):
        return pl.BlockSpec((tm, tn), lambda j, i: (i, off // tn + j))

    def weight_spec(width):
        return _layer_weight_spec(layer, width, tn, lambda j, i: j)

    return pl.pallas_call(
        _merge_kernel,
        grid=(D_MODEL // tn, n // tm),
        in_specs=[branch_spec(RET_WIDTH), branch_spec(MOBA_WIDTH), branch_spec(HGRN_WIDTH),
                  gate_spec(OFF_GR), gate_spec(OFF_GM), gate_spec(OFF_GH),
                  weight_spec(RET_WIDTH), weight_spec(MOBA_WIDTH), weight_spec(HGRN_WIDTH)],
        out_specs=pl.BlockSpec((tm, tn), lambda j, i: (i, j)),
        out_shape=jax.ShapeDtypeStruct((n, D_MODEL), BF16),
        compiler_params=_compiler_params(2, 48),
        name="merge_branches",
    )(ret, moba, hgrn, proj, proj, proj, wr, wm, wh)


def _rmsnorm_kernel(x_ref, g_ref, o_ref):
    o_ref[...] = _rms_normalize(x_ref[...], g_ref[...])


def rmsnorm(x, g, *, tm=1024):
    n, d = x.shape
    assert n % tm == 0
    return pl.pallas_call(
        _rmsnorm_kernel,
        grid=(n // tm,),
        in_specs=[pl.BlockSpec((tm, d), lambda i: (i, 0)),
                  pl.BlockSpec((1, d), lambda i: (0, 0))],
        out_specs=pl.BlockSpec((tm, d), lambda i: (i, 0)),
        out_shape=jax.ShapeDtypeStruct((n, d), F32),
        compiler_params=_compiler_params(1, 40),
        name="final_rmsnorm",
    )(x, g.reshape(1, d))


def _retention_kernel(lg_ref, q_ref, k_ref, v_ref, g_ref, cos_ref, sin_ref, nrm_ref,
                      o_ref, state_ref, *, chunks):
    c_len = RET_CHUNK
    half = RET_HEAD_DIM // 2

    @pl.when(pl.program_id(2) == 0)
    def _():
        state_ref[...] = jnp.zeros_like(state_ref)

    lg = lg_ref[pl.program_id(1)]
    t_col = lax.broadcasted_iota(jnp.int32, (c_len, 1), 0).astype(F32)
    rel = (lax.broadcasted_iota(jnp.int32, (c_len, c_len), 0)
           - lax.broadcasted_iota(jnp.int32, (c_len, c_len), 1)).astype(F32)
    decay = jnp.where(rel >= 0, jnp.exp(lg * jnp.maximum(rel, 0.0)), 0.0)
    q_weight = jnp.exp(lg * (t_col + 1.0))
    k_weight = jnp.exp(lg * (c_len - 1.0 - t_col))
    chunk_decay = jnp.exp(jnp.full((1, RET_HEAD_DIM), lg * c_len, F32))

    def rotate(x, cos, sin):
        x1 = x[:, :half]
        x2 = x[:, half:]
        return jnp.concatenate([x1 * cos - x2 * sin, x2 * cos + x1 * sin], axis=1)

    for c in range(chunks):
        rows = slice(c * c_len, (c + 1) * c_len)
        cos = cos_ref[rows, :]
        sin = sin_ref[rows, :]
        q = rotate(q_ref[rows, :], cos, sin)
        k = rotate(k_ref[rows, :], cos, sin) * (RET_HEAD_DIM ** -0.5)
        v = v_ref[rows, :].astype(BF16)
        state = state_ref[...]

        scores = lax.dot_general(q.astype(BF16), k.astype(BF16), NT_DIMS,
                                 preferred_element_type=F32) * decay
        intra = jnp.dot(scores.astype(BF16), v, preferred_element_type=F32)
        cross = jnp.dot((q * q_weight).astype(BF16), state.astype(BF16),
                        preferred_element_type=F32)
        kv = lax.dot_general((k * k_weight).astype(BF16), v, TN_DIMS,
                             preferred_element_type=F32)
        state_ref[...] = chunk_decay * state + kv

        out = intra + cross
        centered = out - jnp.mean(out, axis=-1, keepdims=True)
        normed = centered * lax.rsqrt(
            jnp.mean(centered * centered, axis=-1, keepdims=True) + NORM_EPS)
        gated = (normed * nrm_ref[...]) * _silu(g_ref[rows, :])
        o_ref[rows, :] = gated.astype(BF16)


def retention_mixer(proj, ret_norm, batch, seq, *, rows_per_step=512):
    n = proj.shape[0]
    d = RET_HEAD_DIM
    assert seq % rows_per_step == 0 and rows_per_step % RET_CHUNK == 0
    steps = seq // rows_per_step
    log_gamma = jnp.log1p(-jnp.exp2(-5.0 - jnp.arange(RET_HEADS, dtype=F32)))
    half = d // 2
    inv_freq = RET_ROPE_BASE ** (-jnp.arange(half, dtype=F32) * 2.0 / d)
    ang = jnp.arange(seq).astype(F32)[:, None] * inv_freq[None, :]
    cos, sin = jnp.cos(ang), jnp.sin(ang)

    def col_spec(off):
        return pl.BlockSpec((rows_per_step, d), lambda b, h, s: (b * steps + s, off // d + h))

    table_spec = pl.BlockSpec((rows_per_step, half), lambda b, h, s: (s, 0))
    return pl.pallas_call(
        functools.partial(_retention_kernel, chunks=rows_per_step // RET_CHUNK),
        grid=(batch, RET_HEADS, steps),
        in_specs=[pl.BlockSpec(memory_space=pltpu.SMEM),
                  col_spec(OFF_RQ), col_spec(OFF_RK), col_spec(OFF_RV), col_spec(OFF_RG),
                  table_spec, table_spec,
                  pl.BlockSpec((1, d), lambda b, h, s: (0, h))],
        out_specs=pl.BlockSpec((rows_per_step, d), lambda b, h, s: (b * steps + s, h)),
        out_shape=jax.ShapeDtypeStruct((n, RET_WIDTH), BF16),
        scratch_shapes=[pltpu.VMEM((d, d), F32)],
        compiler_params=_compiler_params(3, 32),
        name="retention",
    )(log_gamma, proj, proj, proj, proj, cos, sin, ret_norm.reshape(1, RET_WIDTH))


def _moba_rotate(x, cos, sin_lo, sin_hi):
    half = ROPE_DIM // 2
    return (x * cos + pltpu.roll(x, MOBA_HEAD_DIM - half, 1) * sin_lo
            + pltpu.roll(x, half, 1) * sin_hi)


def _split_dot(a, b):
    a_hi = a.astype(BF16)
    a_lo = (a - a_hi.astype(F32)).astype(BF16)
    b_hi = b.astype(BF16)
    b_lo = (b - b_hi.astype(F32)).astype(BF16)
    return (jnp.dot(a_hi, b_hi, preferred_element_type=F32)
            + jnp.dot(a_hi, b_lo, preferred_element_type=F32)
            + jnp.dot(a_lo, b_hi, preferred_element_type=F32))


def _moba_kernel(q_ref, k_ref, v_ref, cos_ref, slo_ref, shi_ref, o_ref,
                 kaug_ref, vt_ref, kmean_ref, qt_ref, s_ref, m_ref, l_ref, acc_ref, *, n_blocks):
    blk = MOBA_BLOCK
    d = MOBA_HEAD_DIM
    p = pl.program_id(2)
    n_past = n_blocks - 1
    gate_rows = kmean_ref.shape[0]
    groups = blk // SUBLANES
    q_scale = (d ** -0.5) * 1.4426950408889634

    def rotate(ref, start):
        rows = pl.ds(start, blk)
        return _moba_rotate(ref[rows, :], cos_ref[rows, :], slo_ref[rows, :], shi_ref[rows, :])

    @pl.when(p == 0)
    def _():
        lane = lax.broadcasted_iota(jnp.int32, (blk, LANES), 1)
        kmean_ref[...] = jnp.zeros_like(kmean_ref)
        for j in range(n_blocks):
            kr = rotate(k_ref, j * blk)
            kaug_ref[j, :, :d] = kr.astype(BF16)
            kaug_ref[j, :, d:] = jnp.where(lane == j, 1.0, 0.0).astype(BF16)
            vt_ref[j] = v_ref[j * blk:(j + 1) * blk, :].T.astype(BF16)
            kmean_ref[j:j + 1, :] = jnp.mean(kr, axis=0, keepdims=True)

    def group_reduce(x, op):
        return op(x.reshape(groups, SUBLANES, blk), axis=0)

    def setup_tile(w, i):
        q_t = rotate(q_ref, pl.multiple_of(i * blk, blk)).T
        gate = _split_dot(kmean_ref[...], q_t)
        row = lax.broadcasted_iota(jnp.int32, (gate_rows, blk), 0)
        row_f = row.astype(F32)
        g = jnp.where(row < i, gate, -jnp.inf)
        bias = jnp.full((gate_rows, blk), MASK_VALUE, F32)
        for _ in range(min(MOBA_TOPK, n_blocks)):
            best = jnp.max(g, axis=0, keepdims=True)
            hit = (g == best) & (g > -jnp.inf)
            first = jnp.min(jnp.where(hit, row_f, float(LANES)), axis=0, keepdims=True)
            pick = row_f == first
            bias = jnp.where(pick, 0.0, bias)
            g = jnp.where(pick, -jnp.inf, g)
        bias = jnp.where(row == i, 0.0, bias)
        pad = jnp.zeros((LANES - gate_rows, blk), F32)
        qt_ref[w] = jnp.concatenate([q_t * q_scale, bias, pad], axis=0).astype(BF16)

        s_t = jnp.dot(kaug_ref[i], qt_ref[w], preferred_element_type=F32)
        key = lax.broadcasted_iota(jnp.int32, (blk, blk), 0)
        qry = lax.broadcasted_iota(jnp.int32, (blk, blk), 1)
        s_t = jnp.where(key <= qry, s_t, MASK_VALUE)
        s_ref[n_past + w] = s_t
        m_ref[w] = group_reduce(s_t, jnp.max)

    def tile_of(t):
        w = jnp.where(t >= p, 1, 0)
        return w, t - w * p

    def score_block(t, carry):
        w, j = tile_of(t)
        s_t = jnp.dot(kaug_ref[j], qt_ref[w], preferred_element_type=F32)
        s_ref[t] = s_t
        m_ref[w] = jnp.maximum(m_ref[w], group_reduce(s_t, jnp.max))
        return carry

    def weigh_block(slot, w, j, first):
        s_t = s_ref[slot].reshape(groups, SUBLANES, blk)
        p_t = jnp.exp2(s_t - m_ref[w][None])
        l_part = jnp.sum(p_t, axis=0)
        pv = jnp.dot(vt_ref[j], p_t.reshape(blk, blk).astype(BF16),
                     preferred_element_type=F32)
        if first:
            l_ref[w] = l_part
            acc_ref[w] = pv
        else:
            l_ref[w] += l_part
            acc_ref[w] += pv

    def weigh_past_block(t, carry):
        w, j = tile_of(t)
        weigh_block(t, w, j, first=False)
        return carry

    tiles = ((0, p), (1, n_past - p))
    for w, i in tiles:
        setup_tile(w, i)
    lax.fori_loop(0, n_past, score_block, 0, unroll=True)
    for w, i in tiles:
        m_ref[w] = jnp.broadcast_to(jnp.max(m_ref[w], axis=0, keepdims=True), (SUBLANES, blk))
        weigh_block(n_past + w, w, i, first=True)
    lax.fori_loop(0, n_past, weigh_past_block, 0, unroll=True)
    for w, i in tiles:
        l = jnp.sum(l_ref[w], axis=0, keepdims=True)
        o_ref[pl.ds(pl.multiple_of(i * blk, blk), blk), :] = (acc_ref[w] / l).T.astype(BF16)


def moba_mixer(proj, batch, seq):
    n = proj.shape[0]
    d = MOBA_HEAD_DIM
    blk = MOBA_BLOCK
    assert seq % blk == 0
    n_blocks = seq // blk
    assert n_blocks <= LANES and n_blocks % 2 == 0
    half = ROPE_DIM // 2
    inv_freq = ROPE_THETA ** (-jnp.arange(half, dtype=F32) * 2.0 / ROPE_DIM)
    ang = jnp.arange(seq).astype(F32)[:, None] * inv_freq[None, :]
    cos, sin = jnp.cos(ang), jnp.sin(ang)
    zeros = jnp.zeros((seq, d - ROPE_DIM), F32)
    zero_half = jnp.zeros((seq, half), F32)
    cos_t = jnp.concatenate([cos, cos, jnp.ones_like(zeros)], axis=1)
    sin_lo = jnp.concatenate([-sin, zero_half, zeros], axis=1)
    sin_hi = jnp.concatenate([zero_half, sin, zeros], axis=1)

    def col_spec(off):
        return pl.BlockSpec((seq, d), lambda b, h, p: (b, off // d + h))

    table_spec = pl.BlockSpec((seq, d), lambda b, h, p: (0, 0))
    gate_rows = -(-n_blocks // SUBLANES) * SUBLANES
    return pl.pallas_call(
        functools.partial(_moba_kernel, n_blocks=n_blocks),
        grid=(batch, MOBA_HEADS, n_blocks // 2),
        in_specs=[col_spec(OFF_MQ), col_spec(OFF_MK), col_spec(OFF_MV),
                  table_spec, table_spec, table_spec],
        out_specs=pl.BlockSpec((seq, d), lambda b, h, p: (b, h)),
        out_shape=jax.ShapeDtypeStruct((n, MOBA_WIDTH), BF16),
        scratch_shapes=[pltpu.VMEM((n_blocks, blk, 2 * d), BF16),
                        pltpu.VMEM((n_blocks, d, blk), BF16),
                        pltpu.VMEM((gate_rows, d), F32),
                        pltpu.VMEM((2, 2 * d, blk), BF16),
                        pltpu.VMEM((n_blocks + 1, blk, blk), F32),
                        pltpu.VMEM((2, SUBLANES, blk), F32),
                        pltpu.VMEM((2, SUBLANES, blk), F32),
                        pltpu.VMEM((2, d, blk), F32)],
        compiler_params=_compiler_params(3, 48),
        name="moba",
    )(proj, proj, proj, cos_t, sin_lo, sin_hi)


def _hgrn_kernel(lbp_ref, tril_ref, q_ref, f_ref, v_ref, g_ref, nrm_ref, o_ref,
                 state_ref, b_ref, k_ref, *, layer, rows_per_step):
    c_len = HGRN_CHUNK
    sub = HGRN_SUB
    n_sub = c_len // sub
    d = HGRN_HEAD_DIM

    @pl.when(pl.program_id(2) == 0)
    def _():
        state_ref[...] = jnp.zeros_like(state_ref)

    params = lbp_ref[...]
    e = jnp.exp(params - jnp.max(params, axis=0, keepdims=True))
    soft = e / jnp.sum(e, axis=0, keepdims=True)
    lower = jnp.zeros((1, d), F32)
    for r in range(1, layer + 1):
        lower = lower + soft[r:r + 1, :]

    tril = tril_ref[...]
    seg_len = tril_ref.shape[0]
    for seg0 in range(0, rows_per_step, seg_len):
        seg = slice(seg0, seg0 + seg_len)
        forget = lower + (1.0 - lower) * jax.nn.sigmoid(f_ref[seg, :])
        k_ref[seg, :] = 1.0 - forget
        log_f = jnp.log2(forget)
        hi = log_f.astype(BF16)
        rest = log_f - hi.astype(F32)
        mid = rest.astype(BF16)
        lo = (rest - mid.astype(F32)).astype(BF16)
        b_ref[seg, :] = (jnp.dot(tril, hi, preferred_element_type=F32)
                         + jnp.dot(tril, mid, preferred_element_type=F32)
                         + jnp.dot(tril, lo, preferred_element_type=F32))

    row8 = lax.broadcasted_iota(jnp.int32, (SUBLANES, d), 0)
    assert sub & (sub - 1) == 0
    bound = sub * n_sub * (n_sub - 1) // 2
    width = -(-bound // LANES) * LANES
    grp_row = jnp.right_shift(lax.broadcasted_iota(jnp.int32, (c_len, width), 0),
                              sub.bit_length() - 1)
    col = lax.broadcasted_iota(jnp.int32, (c_len, width), 1)
    grp_col = jnp.full((c_len, width), -1, jnp.int32)
    for grp in range(1, n_sub):
        start = sub * grp * (grp - 1) // 2
        grp_col = jnp.where((col >= start) & (col < start + grp * sub), grp, grp_col)
    off_mask = grp_row == grp_col

    def bcast_row(ref, r, rows):
        return jnp.broadcast_to(ref[r:r + 1, :], (rows, d))

    for c in range(rows_per_step // c_len):
        r0 = c * c_len
        rows = slice(r0, r0 + c_len)
        b = b_ref[rows, :]
        q = _silu(q_ref[rows, :])
        k = k_ref[rows, :]
        v = v_ref[rows, :]

        diag_parts = []
        for s_idx in range(n_sub):
            s0 = r0 + s_idx * sub
            for half_idx in range(sub // SUBLANES):
                lo_row = half_idx * SUBLANES
                bi = b_ref[s0 + lo_row:s0 + lo_row + SUBLANES, :]
                qi = q[s_idx * sub + lo_row:s_idx * sub + lo_row + SUBLANES, :]
                acc = jnp.zeros((SUBLANES, d), F32)
                for j in range(lo_row + SUBLANES):
                    bj = bcast_row(b_ref, s0 + j, SUBLANES)
                    kj = bcast_row(k_ref, s0 + j, SUBLANES)
                    vj = bcast_row(v_ref, s0 + j, SUBLANES)
                    expo = bi - bj
                    if j > lo_row:
                        expo = jnp.where(row8 + lo_row >= j, expo, -jnp.inf)
                    a = jnp.sum(qi * kj * jnp.exp2(expo), axis=-1, keepdims=True)
                    acc = acc + a * vj
                diag_parts.append(acc)
        diag = jnp.concatenate(diag_parts, axis=0)

        q_parts = [q[:sub, :] * jnp.exp2(b[:sub, :])]
        k_parts = []
        v_parts = []
        for s_idx in range(1, n_sub):
            b_anchor = b_ref[r0 + s_idx * sub - 1:r0 + s_idx * sub, :]
            lo_r, hi_r = s_idx * sub, (s_idx + 1) * sub
            q_parts.append(q[lo_r:hi_r, :] * jnp.exp2(b[lo_r:hi_r, :] - b_anchor))
            k_parts.append(k[:lo_r, :] * jnp.exp2(b_anchor - b[:lo_r, :]))
            v_parts.append(v[:lo_r, :])
        pad = [jnp.zeros((width - bound, d), F32)] if width > bound else []
        q_tilde = jnp.concatenate(q_parts, axis=0).astype(BF16)
        k_hat = jnp.concatenate(k_parts + pad, axis=0).astype(BF16)
        v_hat = jnp.concatenate(v_parts + pad, axis=0).astype(BF16)
        pair = lax.dot_general(q_tilde, k_hat, NT_DIMS, preferred_element_type=F32)
        off = jnp.dot(jnp.where(off_mask, pair, 0.0).astype(BF16), v_hat,
                      preferred_element_type=F32)

        state_t = state_ref[...]
        cross = lax.dot_general((q * jnp.exp2(b)).astype(BF16), state_t.astype(BF16), NT_DIMS,
                                preferred_element_type=F32)
        b_last = b_ref[r0 + c_len - 1:r0 + c_len, :]
        k_dec = k * jnp.exp2(b_last - b)
        state_ref[...] = state_t * jnp.exp2(b_last) + lax.dot_general(
            v.astype(BF16), k_dec.astype(BF16), TN_DIMS, preferred_element_type=F32)

        out = diag + off + cross
        normed = out * lax.rsqrt(jnp.mean(out * out, axis=-1, keepdims=True) + NORM_EPS)
        gated = (normed * nrm_ref[...]) * _silu(g_ref[rows, :])
        o_ref[rows, :] = gated.astype(BF16)


def hgrn_mixer(proj, lower_bound_params, hgrn_norm, layer, batch, seq, *, rows_per_step=1024):
    n = proj.shape[0]
    d = HGRN_HEAD_DIM
    depth = lower_bound_params.shape[0]
    seg_len = min(rows_per_step, HGRN_CUMSUM_ROWS)
    assert seq % rows_per_step == 0 and rows_per_step % seg_len == 0
    assert seg_len % HGRN_CHUNK == 0
    steps = seq // rows_per_step
    r = jnp.arange(seg_len)
    tril = ((r[None, :] <= r[:, None])
            & (r[None, :] // HGRN_CHUNK == r[:, None] // HGRN_CHUNK)).astype(BF16)

    def col_spec(off):
        return pl.BlockSpec((rows_per_step, d), lambda b, h, s: (b * steps + s, off // d + h))

    return pl.pallas_call(
        functools.partial(_hgrn_kernel, layer=layer, rows_per_step=rows_per_step),
        grid=(batch, HGRN_HEADS, steps),
        in_specs=[pl.BlockSpec((depth, d), lambda b, h, s: (0, h)),
                  pl.BlockSpec((seg_len, seg_len), lambda b, h, s: (0, 0)),
                  col_spec(OFF_HQ), col_spec(OFF_HF), col_spec(OFF_HI), col_spec(OFF_HG),
                  pl.BlockSpec((1, d), lambda b, h, s: (0, h))],
        out_specs=pl.BlockSpec((rows_per_step, d), lambda b, h, s: (b * steps + s, h)),
        out_shape=jax.ShapeDtypeStruct((n, HGRN_WIDTH), BF16),
        scratch_shapes=[pltpu.VMEM((d, d), F32),
                        pltpu.VMEM((rows_per_step, d), F32),
                        pltpu.VMEM((rows_per_step, d), F32)],
        compiler_params=_compiler_params(3, 32),
        name="hgrn2",
    )(lower_bound_params.astype(F32), tril, proj, proj, proj, proj,
      hgrn_norm.reshape(1, HGRN_WIDTH))


def kernel(x, hgrn_lower_bounds, norm_mix, w_in, ret_norm, hgrn_norm, w_branch_ret, w_branch_moba,
           w_branch_hgrn, w_out, norm_ffn, w_ffn_gate, w_ffn_up, w_ffn_down, final_norm):
    batch, seq, d_model = x.shape
    assert d_model == D_MODEL
    depth = w_in.shape[0]
    (w_in, w_branch_ret, w_branch_moba, w_branch_hgrn, w_out, w_ffn_gate, w_ffn_up,
     w_ffn_down) = (w.astype(BF16) for w in (w_in, w_branch_ret, w_branch_moba, w_branch_hgrn,
                                             w_out, w_ffn_gate, w_ffn_up, w_ffn_down))
    h = x.reshape(batch * seq, d_model)
    for layer in range(depth):
        proj = norm_matmul(h, norm_mix[layer], w_in, layer)
        ret = retention_mixer(proj, ret_norm[layer], batch, seq)
        moba = moba_mixer(proj, batch, seq)
        hgrn = hgrn_mixer(proj, hgrn_lower_bounds, hgrn_norm[layer], layer, batch, seq)
        mixed = merge_branches(ret, moba, hgrn, proj, w_branch_ret, w_branch_moba,
                               w_branch_hgrn, layer)
        h = matmul_residual(mixed, w_out, h, layer, tn=1024, name="out_proj")
        act = ffn_up(h, norm_ffn[layer], w_ffn_gate, w_ffn_up, layer)
        h = matmul_residual(act, w_ffn_down, h, layer, tn=512, name="ffn_down")
    return rmsnorm(h, final_norm).reshape(batch, seq, d_model)
```

```python
import functools

import jax
import jax.numpy as jnp
from jax import lax
from jax.experimental import pallas as pl
from jax.experimental.pallas import tpu as pltpu

F32 = jnp.float32
BF16 = jnp.bfloat16

D_MODEL = 2048
RET_HEADS = 4
RET_HEAD_DIM = 256
RET_WIDTH = RET_HEADS * RET_HEAD_DIM
RET_CHUNK = 128
RET_ROPE_BASE = 10000.0
MOBA_HEADS = 8
MOBA_HEAD_DIM = 128
MOBA_WIDTH = MOBA_HEADS * MOBA_HEAD_DIM
MOBA_BLOCK = 256
MOBA_TOPK = 3
ROPE_THETA = 500000.0
ROPE_DIM = MOBA_HEAD_DIM // 4
HGRN_HEADS = 8
HGRN_HEAD_DIM = 128
HGRN_WIDTH = HGRN_HEADS * HGRN_HEAD_DIM
HGRN_CHUNK = 64
HGRN_SUB = 16
HGRN_CUMSUM_ROWS = 256
NORM_EPS = 1e-6
IN_SIZES = (RET_WIDTH,) * 4 + (MOBA_WIDTH,) * 3 + (HGRN_WIDTH,) * 4 + (D_MODEL,) * 3
IN_COLS = sum(IN_SIZES)
IN_OFFS = tuple(sum(IN_SIZES[:i]) for i in range(len(IN_SIZES)))
(OFF_RQ, OFF_RK, OFF_RV, OFF_RG, OFF_MQ, OFF_MK, OFF_MV,
 OFF_HQ, OFF_HF, OFF_HI, OFF_HG, OFF_GR, OFF_GM, OFF_GH) = IN_OFFS

V7X_VMEM_BYTES = 64 * 1024 * 1024
LANES = 128
SUBLANES = 8
MASK_VALUE = -1e30

NT_DIMS = (((1,), (1,)), ((), ()))
TN_DIMS = (((0,), (0,)), ((), ()))


def _compiler_params(n_grid_dims, vmem_mib):
    assert vmem_mib * 1024 * 1024 < V7X_VMEM_BYTES
    return pltpu.CompilerParams(
        dimension_semantics=("arbitrary",) * n_grid_dims,
        vmem_limit_bytes=vmem_mib * 1024 * 1024)


def _silu(x):
    return x * jax.nn.sigmoid(x)


def _layer_weight_spec(layer, rows, tn, col_of):
    return pl.BlockSpec((None, rows, tn), lambda *g: (layer, 0, col_of(*g)))


def _rms_normalize(x, g):
    r = lax.rsqrt(jnp.mean(x * x, axis=-1, keepdims=True) + NORM_EPS)
    return x * r * g


def _norm_matmul_kernel(x_ref, g_ref, w_ref, o_ref, h_ref):
    @pl.when(pl.program_id(1) == 0)
    def _():
        h_ref[...] = _rms_normalize(x_ref[...], g_ref[...]).astype(BF16)

    o_ref[...] = jnp.dot(h_ref[...], w_ref[...], preferred_element_type=F32)


def norm_matmul(x, g, w, layer, *, tm=1024, tn=1024):
    n, d = x.shape
    cols = w.shape[2]
    assert n % tm == 0 and cols % tn == 0
    return pl.pallas_call(
        _norm_matmul_kernel,
        grid=(n // tm, cols // tn),
        in_specs=[pl.BlockSpec((tm, d), lambda i, j: (i, 0)),
                  pl.BlockSpec((1, d), lambda i, j: (0, 0)),
                  _layer_weight_spec(layer, d, tn, lambda i, j: j)],
        out_specs=pl.BlockSpec((tm, tn), lambda i, j: (i, j)),
        out_shape=jax.ShapeDtypeStruct((n, cols), F32),
        scratch_shapes=[pltpu.VMEM((tm, d), BF16)],
        compiler_params=_compiler_params(2, 48),
        name="norm_inproj",
    )(x, g.reshape(1, d), w)


def _ffn_up_kernel(x_ref, g_ref, wg_ref, wu_ref, o_ref, h_ref):
    @pl.when(pl.program_id(1) == 0)
    def _():
        h_ref[...] = _rms_normalize(x_ref[...], g_ref[...]).astype(BF16)

    h = h_ref[...]
    gate = jnp.dot(h, wg_ref[...], preferred_element_type=F32)
    up = jnp.dot(h, wu_ref[...], preferred_element_type=F32)
    o_ref[...] = (_silu(gate) * up).astype(BF16)


def ffn_up(x, g, wg, wu, layer, *, tm=1024, tn=512):
    n, d = x.shape
    hidden = wg.shape[2]
    assert n % tm == 0 and hidden % tn == 0
    w_spec = _layer_weight_spec(layer, d, tn, lambda i, j: j)
    return pl.pallas_call(
        _ffn_up_kernel,
        grid=(n // tm, hidden // tn),
        in_specs=[pl.BlockSpec((tm, d), lambda i, j: (i, 0)),
                  pl.BlockSpec((1, d), lambda i, j: (0, 0)),
                  w_spec, w_spec],
        out_specs=pl.BlockSpec((tm, tn), lambda i, j: (i, j)),
        out_shape=jax.ShapeDtypeStruct((n, hidden), BF16),
        scratch_shapes=[pltpu.VMEM((tm, d), BF16)],
        compiler_params=_compiler_params(2, 48),
        name="ffn_up",
    )(x, g.reshape(1, d), wg, wu)


def _matmul_residual_kernel(a_ref, w_ref, r_ref, o_ref):
    o_ref[...] = r_ref[...] + jnp.dot(a_ref[...], w_ref[...], preferred_element_type=F32)


def matmul_residual(a, w, res, layer, *, tm=1024, tn=512, name):
    n, k = a.shape
    cols = w.shape[2]
    assert n % tm == 0 and cols % tn == 0
    return pl.pallas_call(
        _matmul_residual_kernel,
        grid=(n // tm, cols // tn),
        in_specs=[pl.BlockSpec((tm, k), lambda i, j: (i, 0)),
                  _layer_weight_spec(layer, k, tn, lambda i, j: j),
                  pl.BlockSpec((tm, tn), lambda i, j: (i, j))],
        out_specs=pl.BlockSpec((tm, tn), lambda i, j: (i, j)),
        out_shape=jax.ShapeDtypeStruct((n, cols), F32),
        compiler_params=_compiler_params(2, 52),
        name=name,
    )(a, w, res)


def _merge_kernel(ret_ref, moba_ref, hgrn_ref, gr_ref, gm_ref, gh_ref,
                  wr_ref, wm_ref, wh_ref, o_ref):
    def branch(gate_ref, a_ref, w_ref):
        return jax.nn.sigmoid(gate_ref[...]) * jnp.dot(
            a_ref[...], w_ref[...], preferred_element_type=F32)

    mixed = (branch(gr_ref, ret_ref, wr_ref) + branch(gm_ref, moba_ref, wm_ref)
             + branch(gh_ref, hgrn_ref, wh_ref))
    o_ref[...] = mixed.astype(BF16)


def merge_branches(ret, moba, hgrn, proj, wr, wm, wh, layer, *, tm=512, tn=1024):
    n = ret.shape[0]
    assert n % tm == 0 and D_MODEL % tn == 0
    assert OFF_GR % tn == 0 and OFF_GM % tn == 0 and OFF_GH % tn == 0

    def branch_spec(width):
        return pl.BlockSpec((tm, width), lambda j, i: (i, 0))

    def gate_spec(off):
        return pl.BlockSpec((tm, tn), lambda j, i: (i, off // tn + j))

    def weight_spec(width):
        return _layer_weight_spec(layer, width, tn, lambda j, i: j)

    return pl.pallas_call(
        _merge_kernel,
        grid=(D_MODEL // tn, n // tm),
        in_specs=[branch_spec(RET_WIDTH), branch_spec(MOBA_WIDTH), branch_spec(HGRN_WIDTH),
                  gate_spec(OFF_GR), gate_spec(OFF_GM), gate_spec(OFF_GH),
                  weight_spec(RET_WIDTH), weight_spec(MOBA_WIDTH), weight_spec(HGRN_WIDTH)],
        out_specs=pl.BlockSpec((tm, tn), lambda j, i: (i, j)),
        out_shape=jax.ShapeDtypeStruct((n, D_MODEL), BF16),
        compiler_params=_compiler_params(2, 48),
        name="merge_branches",
    )(ret, moba, hgrn, proj, proj, proj, wr, wm, wh)


def _rmsnorm_kernel(x_ref, g_ref, o_ref):
    o_ref[...] = _rms_normalize(x_ref[...], g_ref[...])


def rmsnorm(x, g, *, tm=1024):
    n, d = x.shape
    assert n % tm == 0
    return pl.pallas_call(
        _rmsnorm_kernel,
        grid=(n // tm,),
        in_specs=[pl.BlockSpec((tm, d), lambda i: (i, 0)),
                  pl.BlockSpec((1, d), lambda i: (0, 0))],
        out_specs=pl.BlockSpec((tm, d), lambda i: (i, 0)),
        out_shape=jax.ShapeDtypeStruct((n, d), F32),
        compiler_params=_compiler_params(1, 40),
        name="final_rmsnorm",
    )(x, g.reshape(1, d))


def _retention_kernel(lg_ref, q_ref, k_ref, v_ref, g_ref, cos_ref, sin_ref, nrm_ref,
                      o_ref, state_ref, *, chunks):
    c_len = RET_CHUNK
    d = RET_HEAD_DIM
    half = d // 2

    @pl.when(pl.program_id(1) == 0)
    def _():
        state_ref[...] = jnp.zeros_like(state_ref)

    t_col = lax.broadcasted_iota(jnp.int32, (c_len, 1), 0).astype(F32)
    rel = (lax.broadcasted_iota(jnp.int32, (c_len, c_len), 0)
           - lax.broadcasted_iota(jnp.int32, (c_len, c_len), 1)).astype(F32)

    def rotate(x, cos, sin):
        x1 = x[:, :half]
        x2 = x[:, half:]
        return jnp.concatenate([x1 * cos - x2 * sin, x2 * cos + x1 * sin], axis=1)

    for h in range(RET_HEADS):
        cols = slice(h * d, (h + 1) * d)
        lg = lg_ref[h]
        decay = jnp.where(rel >= 0, jnp.exp(lg * jnp.maximum(rel, 0.0)), 0.0)
        q_weight = jnp.exp(lg * (t_col + 1.0))
        k_weight = jnp.exp(lg * (c_len - 1.0 - t_col))
        chunk_decay = jnp.exp(jnp.full((1, d), lg * c_len, F32))
        for c in range(chunks):
            rows = slice(c * c_len, (c + 1) * c_len)
            cos = cos_ref[rows, :]
            sin = sin_ref[rows, :]
            q = rotate(q_ref[rows, cols], cos, sin)
            k = rotate(k_ref[rows, cols], cos, sin) * (d ** -0.5)
            v = v_ref[rows, cols].astype(BF16)
            state = state_ref[h]

            scores = lax.dot_general(q.astype(BF16), k.astype(BF16), NT_DIMS,
                                     preferred_element_type=F32) * decay
            intra = jnp.dot(scores.astype(BF16), v, preferred_element_type=F32)
            cross = jnp.dot((q * q_weight).astype(BF16), state.astype(BF16),
                            preferred_element_type=F32)
            kv = lax.dot_general((k * k_weight).astype(BF16), v, TN_DIMS,
                                 preferred_element_type=F32)
            state_ref[h] = chunk_decay * state + kv

            out = intra + cross
            centered = out - jnp.mean(out, axis=-1, keepdims=True)
            normed = centered * lax.rsqrt(
                jnp.mean(centered * centered, axis=-1, keepdims=True) + NORM_EPS)
            gated = (normed * nrm_ref[:, cols]) * _silu(g_ref[rows, cols])
            o_ref[rows, cols] = gated.astype(BF16)


def retention_mixer(proj, ret_norm, batch, seq, *, rows_per_step=512):
    n = proj.shape[0]
    d = RET_HEAD_DIM
    assert seq % rows_per_step == 0 and rows_per_step % RET_CHUNK == 0
    steps = seq // rows_per_step
    log_gamma = jnp.log1p(-jnp.exp2(-5.0 - jnp.arange(RET_HEADS, dtype=F32)))
    half = d // 2
    inv_freq = RET_ROPE_BASE ** (-jnp.arange(half, dtype=F32) * 2.0 / d)
    ang = jnp.arange(seq).astype(F32)[:, None] * inv_freq[None, :]
    cos, sin = jnp.cos(ang), jnp.sin(ang)

    def col_spec(off):
        return pl.BlockSpec((rows_per_step, RET_WIDTH),
                            lambda b, s: (b * steps + s, off // RET_WIDTH))

    table_spec = pl.BlockSpec((rows_per_step, half), lambda b, s: (s, 0))
    return pl.pallas_call(
        functools.partial(_retention_kernel, chunks=rows_per_step // RET_CHUNK),
        grid=(batch, steps),
        in_specs=[pl.BlockSpec(memory_space=pltpu.SMEM),
                  col_spec(OFF_RQ), col_spec(OFF_RK), col_spec(OFF_RV), col_spec(OFF_RG),
                  table_spec, table_spec,
                  pl.BlockSpec((1, RET_WIDTH), lambda b, s: (0, 0))],
        out_specs=pl.BlockSpec((rows_per_step, RET_WIDTH), lambda b, s: (b * steps + s, 0)),
        out_shape=jax.ShapeDtypeStruct((n, RET_WIDTH), BF16),
        scratch_shapes=[pltpu.VMEM((RET_HEADS, d, d), F32)],
        compiler_params=_compiler_params(2, 40),
        name="retention",
    )(log_gamma, proj, proj, proj, proj, cos, sin, ret_norm.reshape(1, RET_WIDTH))


def _moba_rotate(x, cos, sin_lo, sin_hi):
    half = ROPE_DIM // 2
    return (x * cos + pltpu.roll(x, MOBA_HEAD_DIM - half, 1) * sin_lo
            + pltpu.roll(x, half, 1) * sin_hi)


def _split_dot(a, b):
    a_hi = a.astype(BF16)
    a_lo = (a - a_hi.astype(F32)).astype(BF16)
    b_hi = b.astype(BF16)
    b_lo = (b - b_hi.astype(F32)).astype(BF16)
    return (jnp.dot(a_hi, b_hi, preferred_element_type=F32)
            + jnp.dot(a_hi, b_lo, preferred_element_type=F32)
            + jnp.dot(a_lo, b_hi, preferred_element_type=F32))


def _moba_kernel(q_ref, k_ref, v_ref, cos_ref, slo_ref, shi_ref, o_ref,
                 kaug_ref, vt_ref, kmean_ref, qt_ref, s_ref, acc_ref, snap_ref, *, n_blocks):
    blk = MOBA_BLOCK
    d = MOBA_HEAD_DIM
    p = pl.program_id(2)
    n_past = n_blocks - 1
    gate_rows = kmean_ref.shape[0]
    groups = blk // SUBLANES
    q_scale = (d ** -0.5) * 1.4426950408889634

    def rotate(ref, start):
        rows = pl.ds(start, blk)
        return _moba_rotate(ref[rows, :], cos_ref[rows, :], slo_ref[rows, :], shi_ref[rows, :])

    @pl.when(p == 0)
    def _():
        lane = lax.broadcasted_iota(jnp.int32, (blk, LANES), 1)
        kmean_ref[...] = jnp.zeros_like(kmean_ref)
        for j in range(n_blocks):
            kr = rotate(k_ref, j * blk)
            kaug_ref[j, :, :d] = kr.astype(BF16)
            kaug_ref[j, :, d:] = jnp.where(lane == j, 1.0, 0.0).astype(BF16)
            vt_ref[j] = v_ref[j * blk:(j + 1) * blk, :].T.astype(BF16)
            kmean_ref[j:j + 1, :] = jnp.mean(kr, axis=0, keepdims=True)

    def group_reduce(x, op):
        return op(x.reshape(groups, SUBLANES, blk), axis=0)

    def setup_tile(w, i):
        q_t = rotate(q_ref, pl.multiple_of(i * blk, blk)).T
        gate = _split_dot(kmean_ref[...], q_t)
        row = lax.broadcasted_iota(jnp.int32, (gate_rows, blk), 0)
        row_f = row.astype(F32)
        g = jnp.where(row < i, gate, -jnp.inf)
        bias = jnp.full((gate_rows, blk), MASK_VALUE, F32)
        for _ in range(min(MOBA_TOPK, n_blocks)):
            best = jnp.max(g, axis=0, keepdims=True)
            hit = (g == best) & (g > -jnp.inf)
            first = jnp.min(jnp.where(hit, row_f, float(LANES)), axis=0, keepdims=True)
            pick = row_f == first
            bias = jnp.where(pick, 0.0, bias)
            g = jnp.where(pick, -jnp.inf, g)
        bias = jnp.where(row == i, 0.0, bias)
        pad = jnp.zeros((LANES - gate_rows, blk), F32)
        qt_ref[w] = jnp.concatenate([q_t * q_scale, bias, pad], axis=0).astype(BF16)

        s_t = jnp.dot(kaug_ref[i], qt_ref[w], preferred_element_type=F32)
        key = lax.broadcasted_iota(jnp.int32, (blk, blk), 0)
        qry = lax.broadcasted_iota(jnp.int32, (blk, blk), 1)
        s_t = jnp.where(key <= qry, s_t, MASK_VALUE)
        s_ref[n_past + w] = s_t
        return group_reduce(s_t, jnp.max)

    def tile_of(t):
        in_b = t >= p
        return in_b, jnp.where(in_b, t - p, t)

    def col_max(m8):
        return jnp.broadcast_to(jnp.max(m8, axis=0, keepdims=True), (SUBLANES, blk))

    def weights(slot, m8):
        p_t = jnp.exp2(s_ref[slot].reshape(groups, SUBLANES, blk) - m8[None])
        return jnp.sum(p_t, axis=0), p_t.reshape(blk, blk).astype(BF16)

    def values(j, p_bf):
        return jnp.dot(vt_ref[j], p_bf, preferred_element_type=F32)

    i_a, i_b = p, n_past - p
    m_a = setup_tile(0, i_a)
    m_b = setup_tile(1, i_b)
    never = jnp.full((SUBLANES, blk), -jnp.inf, F32)
    for t in range(n_past):
        in_b, j = tile_of(t)
        s_t = jnp.dot(kaug_ref[j], qt_ref[jnp.where(in_b, 1, 0)], preferred_element_type=F32)
        s_ref[t] = s_t
        gm = group_reduce(s_t, jnp.max)
        m_a = jnp.maximum(m_a, jnp.where(in_b, never, gm))
        m_b = jnp.maximum(m_b, jnp.where(in_b, gm, never))
    m_a = col_max(m_a)
    m_b = col_max(m_b)

    l_a, p_own = weights(n_past, m_a)
    acc_ref[0] = values(i_a, p_own)
    l_b, p_own = weights(n_past + 1, m_b)
    acc_ref[1] = values(i_b, p_own)
    acc = jnp.zeros((d, blk), F32)
    for t in range(n_past):
        in_b, j = tile_of(t)
        l_part, p_bf = weights(t, jnp.where(in_b, m_b, m_a))
        acc = acc * jnp.where(t == p, 0.0, 1.0) + values(j, p_bf)
        l_a = l_a + jnp.where(in_b, 0.0, l_part)
        l_b = l_b + jnp.where(in_b, l_part, 0.0)
        if t < snap_ref.shape[0]:
            snap_ref[t] = acc
    past_a = snap_ref[jnp.maximum(p - 1, 0)] * jnp.where(p > 0, 1.0, 0.0)
    for w, i, l8, past in ((0, i_a, l_a, past_a), (1, i_b, l_b, acc)):
        l = jnp.sum(l8, axis=0, keepdims=True)
        out_t = (acc_ref[w] + past) / l
        o_ref[pl.ds(pl.multiple_of(i * blk, blk), blk), :] = out_t.T.astype(BF16)


def moba_mixer(proj, batch, seq):
    n = proj.shape[0]
    d = MOBA_HEAD_DIM
    blk = MOBA_BLOCK
    assert seq % blk == 0
    n_blocks = seq // blk
    assert n_blocks <= LANES and n_blocks % 2 == 0
    half = ROPE_DIM // 2
    inv_freq = ROPE_THETA ** (-jnp.arange(half, dtype=F32) * 2.0 / ROPE_DIM)
    ang = jnp.arange(seq).astype(F32)[:, None] * inv_freq[None, :]
    cos, sin = jnp.cos(ang), jnp.sin(ang)
    zeros = jnp.zeros((seq, d - ROPE_DIM), F32)
    zero_half = jnp.zeros((seq, half), F32)
    cos_t = jnp.concatenate([cos, cos, jnp.ones_like(zeros)], axis=1)
    sin_lo = jnp.concatenate([-sin, zero_half, zeros], axis=1)
    sin_hi = jnp.concatenate([zero_half, sin, zeros], axis=1)

    def col_spec(off):
        return pl.BlockSpec((seq, d), lambda b, h, p: (b, off // d + h))

    table_spec = pl.BlockSpec((seq, d), lambda b, h, p: (0, 0))
    gate_rows = -(-n_blocks // SUBLANES) * SUBLANES
    return pl.pallas_call(
        functools.partial(_moba_kernel, n_blocks=n_blocks),
        grid=(batch, MOBA_HEADS, n_blocks // 2),
        in_specs=[col_spec(OFF_MQ), col_spec(OFF_MK), col_spec(OFF_MV),
                  table_spec, table_spec, table_spec],
        out_specs=pl.BlockSpec((seq, d), lambda b, h, p: (b, h)),
        out_shape=jax.ShapeDtypeStruct((n, MOBA_WIDTH), BF16),
        scratch_shapes=[pltpu.VMEM((n_blocks, blk, 2 * d), BF16),
                        pltpu.VMEM((n_blocks, d, blk), BF16),
                        pltpu.VMEM((gate_rows, d), F32),
                        pltpu.VMEM((2, 2 * d, blk), BF16),
                        pltpu.VMEM((n_blocks + 1, blk, blk), F32),
                        pltpu.VMEM((2, d, blk), F32),
                        pltpu.VMEM((n_blocks // 2 - 1, d, blk), F32)],
        compiler_params=_compiler_params(3, 48),
        name="moba",
    )(proj, proj, proj, cos_t, sin_lo, sin_hi)


def _hgrn_kernel(lbp_ref, tril_ref, q_ref, f_ref, v_ref, g_ref, nrm_ref, o_ref,
                 state_ref, b_ref, k_ref, *, layer, rows_per_step):
    c_len = HGRN_CHUNK
    sub = HGRN_SUB
    n_sub = c_len // sub
    d = HGRN_HEAD_DIM

    @pl.when(pl.program_id(2) == 0)
    def _():
        state_ref[...] = jnp.zeros_like(state_ref)

    params = lbp_ref[...]
    e = jnp.exp(params - jnp.max(params, axis=0, keepdims=True))
    soft = e / jnp.sum(e, axis=0, keepdims=True)
    lower = jnp.zeros((1, d), F32)
    for r in range(1, layer + 1):
        lower = lower + soft[r:r + 1, :]

    tril = tril_ref[...]
    seg_len = tril_ref.shape[0]
    for seg0 in range(0, rows_per_step, seg_len):
        seg = slice(seg0, seg0 + seg_len)
        forget = lower + (1.0 - lower) * jax.nn.sigmoid(f_ref[seg, :])
        k_ref[seg, :] = 1.0 - forget
        log_f = jnp.log2(forget)
        hi = log_f.astype(BF16)
        rest = log_f - hi.astype(F32)
        mid = rest.astype(BF16)
        lo = (rest - mid.astype(F32)).astype(BF16)
        b_ref[seg, :] = (jnp.dot(tril, hi, preferred_element_type=F32)
                         + jnp.dot(tril, mid, preferred_element_type=F32)
                         + jnp.dot(tril, lo, preferred_element_type=F32))

    row8 = lax.broadcasted_iota(jnp.int32, (SUBLANES, d), 0)
    assert sub & (sub - 1) == 0
    bound = sub * n_sub * (n_sub - 1) // 2
    width = -(-bound // LANES) * LANES
    grp_row = jnp.right_shift(lax.broadcasted_iota(jnp.int32, (c_len, width), 0),
                              sub.bit_length() - 1)
    col = lax.broadcasted_iota(jnp.int32, (c_len, width), 1)
    grp_col = jnp.full((c_len, width), -1, jnp.int32)
    for grp in range(1, n_sub):
        start = sub * grp * (grp - 1) // 2
        grp_col = jnp.where((col >= start) & (col < start + grp * sub), grp, grp_col)
    off_mask = grp_row == grp_col

    def bcast_row(ref, r, rows):
        return jnp.broadcast_to(ref[r:r + 1, :], (rows, d))

    for c in range(rows_per_step // c_len):
        r0 = c * c_len
        rows = slice(r0, r0 + c_len)
        b = b_ref[rows, :]
        q = _silu(q_ref[rows, :])
        k = k_ref[rows, :]
        v = v_ref[rows, :]

        diag_parts = []
        for s_idx in range(n_sub):
            s0 = r0 + s_idx * sub
            for half_idx in range(sub // SUBLANES):
                lo_row = half_idx * SUBLANES
                bi = b_ref[s0 + lo_row:s0 + lo_row + SUBLANES, :]
                qi = q[s_idx * sub + lo_row:s_idx * sub + lo_row + SUBLANES, :]
                acc = jnp.zeros((SUBLANES, d), F32)
                for j in range(lo_row + SUBLANES):
                    bj = bcast_row(b_ref, s0 + j, SUBLANES)
                    kj = bcast_row(k_ref, s0 + j, SUBLANES)
                    vj = bcast_row(v_ref, s0 + j, SUBLANES)
                    expo = bi - bj
                    if j > lo_row:
                        expo = jnp.where(row8 + lo_row >= j, expo, -jnp.inf)
                    a = jnp.sum(qi * kj * jnp.exp2(expo), axis=-1, keepdims=True)
                    acc = acc + a * vj
                diag_parts.append(acc)
        diag = jnp.concatenate(diag_parts, axis=0)

        q_parts = [q[:sub, :] * jnp.exp2(b[:sub, :])]
        k_parts = []
        v_parts = []
        for s_idx in range(1, n_sub):
            b_anchor = b_ref[r0 + s_idx * sub - 1:r0 + s_idx * sub, :]
            lo_r, hi_r = s_idx * sub, (s_idx + 1) * sub
            q_parts.append(q[lo_r:hi_r, :] * jnp.exp2(b[lo_r:hi_r, :] - b_anchor))
            k_parts.append(k[:lo_r, :] * jnp.exp2(b_anchor - b[:lo_r, :]))
            v_parts.append(v[:lo_r, :])
        pad = [jnp.zeros((width - bound, d), F32)] if width > bound else []
        q_tilde = jnp.concatenate(q_parts, axis=0).astype(BF16)
        k_hat = jnp.concatenate(k_parts + pad, axis=0).astype(BF16)
        v_hat = jnp.concatenate(v_parts + pad, axis=0).astype(BF16)
        pair = lax.dot_general(q_tilde, k_hat, NT_DIMS, preferred_element_type=F32)
        off = jnp.dot(jnp.where(off_mask, pair, 0.0).astype(BF16), v_hat,
                      preferred_element_type=F32)

        state_t = state_ref[...]
        cross = lax.dot_general((q * jnp.exp2(b)).astype(BF16), state_t.astype(BF16), NT_DIMS,
                                preferred_element_type=F32)
        b_last = b_ref[r0 + c_len - 1:r0 + c_len, :]
        k_dec = k * jnp.exp2(b_last - b)
        state_ref[...] = state_t * jnp.exp2(b_last) + lax.dot_general(
            v.astype(BF16), k_dec.astype(BF16), TN_DIMS, preferred_element_type=F32)

        out = diag + off + cross
        normed = out * lax.rsqrt(jnp.mean(out * out, axis=-1, keepdims=True) + NORM_EPS)
        gated = (normed * nrm_ref[...]) * _silu(g_ref[rows, :])
        o_ref[rows, :] = gated.astype(BF16)


def hgrn_mixer(proj, lower_bound_params, hgrn_norm, layer, batch, seq, *, rows_per_step=1024):
    n = proj.shape[0]
    d = HGRN_HEAD_DIM
    depth = lower_bound_params.shape[0]
    seg_len = min(rows_per_step, HGRN_CUMSUM_ROWS)
    assert seq % rows_per_step == 0 and rows_per_step % seg_len == 0
    assert seg_len % HGRN_CHUNK == 0
    steps = seq // rows_per_step
    r = jnp.arange(seg_len)
    tril = ((r[None, :] <= r[:, None])
            & (r[None, :] // HGRN_CHUNK == r[:, None] // HGRN_CHUNK)).astype(BF16)

    def col_spec(off):
        return pl.BlockSpec((rows_per_step, d), lambda b, h, s: (b * steps + s, off // d + h))

    return pl.pallas_call(
        functools.partial(_hgrn_kernel, layer=layer, rows_per_step=rows_per_step),
        grid=(batch, HGRN_HEADS, steps),
        in_specs=[pl.BlockSpec((depth, d), lambda b, h, s: (0, h)),
                  pl.BlockSpec((seg_len, seg_len), lambda b, h, s: (0, 0)),
                  col_spec(OFF_HQ), col_spec(OFF_HF), col_spec(OFF_HI), col_spec(OFF_HG),
                  pl.BlockSpec((1, d), lambda b, h, s: (0, h))],
        out_specs=pl.BlockSpec((rows_per_step, d), lambda b, h, s: (b * steps + s, h)),
        out_shape=jax.ShapeDtypeStruct((n, HGRN_WIDTH), BF16),
        scratch_shapes=[pltpu.VMEM((d, d), F32),
                        pltpu.VMEM((rows_per_step, d), F32),
                        pltpu.VMEM((rows_per_step, d), F32)],
        compiler_params=_compiler_params(3, 32),
        name="hgrn2",
    )(lower_bound_params.astype(F32), tril, proj, proj, proj, proj,
      hgrn_norm.reshape(1, HGRN_WIDTH))


def kernel(x, hgrn_lower_bounds, norm_mix, w_in, ret_norm, hgrn_norm, w_branch_ret, w_branch_moba,
           w_branch_hgrn, w_out, norm_ffn, w_ffn_gate, w_ffn_up, w_ffn_down, final_norm):
    batch, seq, d_model = x.shape
    assert d_model == D_MODEL
    depth = w_in.shape[0]
    (w_in, w_branch_ret, w_branch_moba, w_branch_hgrn, w_out, w_ffn_gate, w_ffn_up,
     w_ffn_down) = (w.astype(BF16) for w in (w_in, w_branch_ret, w_branch_moba, w_branch_hgrn,
                                             w_out, w_ffn_gate, w_ffn_up, w_ffn_down))
    h = x.reshape(batch * seq, d_model)
    for layer in range(depth):
        proj = norm_matmul(h, norm_mix[layer], w_in, layer)
        ret = retention_mixer(proj, ret_norm[layer], batch, seq)
        moba = moba_mixer(proj, batch, seq)
        hgrn = hgrn_mixer(proj, hgrn_lower_bounds, hgrn_norm[layer], layer, batch, seq)
        mixed = merge_branches(ret, moba, hgrn, proj, w_branch_ret, w_branch_moba,
                               w_branch_hgrn, layer)
        h = matmul_residual(mixed, w_out, h, layer, tn=1024, name="out_proj")
        act = ffn_up(h, norm_ffn[layer], w_ffn_gate, w_ffn_up, layer)
        h = matmul_residual(act, w_ffn_down, h, layer, tn=512, name="ffn_down")
    return rmsnorm(h, final_norm).reshape(batch, seq, d_model)
```

```python
import functools

import jax
import jax.numpy as jnp
from jax import lax
from jax.experimental import pallas as pl
from jax.experimental.pallas import tpu as pltpu

F32 = jnp.float32
BF16 = jnp.bfloat16

D_MODEL = 2048
RET_HEADS = 4
RET_HEAD_DIM = 256
RET_WIDTH = RET_HEADS * RET_HEAD_DIM
RET_CHUNK = 128
RET_ROPE_BASE = 10000.0
MOBA_HEADS = 8
MOBA_HEAD_DIM = 128
MOBA_WIDTH = MOBA_HEADS * MOBA_HEAD_DIM
MOBA_BLOCK = 256
MOBA_TOPK = 3
ROPE_THETA = 500000.0
ROPE_DIM = MOBA_HEAD_DIM // 4
HGRN_HEADS = 8
HGRN_HEAD_DIM = 128
HGRN_WIDTH = HGRN_HEADS * HGRN_HEAD_DIM
HGRN_CHUNK = 64
HGRN_SUB = 16
HGRN_CUMSUM_ROWS = 256
NORM_EPS = 1e-6
IN_SIZES = (RET_WIDTH,) * 4 + (MOBA_WIDTH,) * 3 + (HGRN_WIDTH,) * 4 + (D_MODEL,) * 3
IN_COLS = sum(IN_SIZES)
IN_OFFS = tuple(sum(IN_SIZES[:i]) for i in range(len(IN_SIZES)))
(OFF_RQ, OFF_RK, OFF_RV, OFF_RG, OFF_MQ, OFF_MK, OFF_MV,
 OFF_HQ, OFF_HF, OFF_HI, OFF_HG, OFF_GR, OFF_GM, OFF_GH) = IN_OFFS

V7X_VMEM_BYTES = 64 * 1024 * 1024
LANES = 128
SUBLANES = 8
MASK_VALUE = -1e30

NT_DIMS = (((1,), (1,)), ((), ()))
TN_DIMS = (((0,), (0,)), ((), ()))


def _compiler_params(n_grid_dims, vmem_mib):
    assert vmem_mib * 1024 * 1024 < V7X_VMEM_BYTES
    return pltpu.CompilerParams(
        dimension_semantics=("arbitrary",) * n_grid_dims,
        vmem_limit_bytes=vmem_mib * 1024 * 1024)


def _silu(x):
    return x * jax.nn.sigmoid(x)


def _layer_weight_spec(layer, rows, tn, col_of):
    return pl.BlockSpec((None, rows, tn), lambda *g: (layer, 0, col_of(*g)))


def _rms_normalize(x, g):
    r = lax.rsqrt(jnp.mean(x * x, axis=-1, keepdims=True) + NORM_EPS)
    return x * r * g


def _norm_matmul_kernel(x_ref, g_ref, w_ref, o_ref, h_ref):
    @pl.when(pl.program_id(1) == 0)
    def _():
        h_ref[...] = _rms_normalize(x_ref[...], g_ref[...]).astype(BF16)

    o_ref[...] = jnp.dot(h_ref[...], w_ref[...], preferred_element_type=F32)


def norm_matmul(x, g, w, layer, *, tm=1024, tn=1024):
    n, d = x.shape
    cols = w.shape[2]
    assert n % tm == 0 and cols % tn == 0
    return pl.pallas_call(
        _norm_matmul_kernel,
        grid=(n // tm, cols // tn),
        in_specs=[pl.BlockSpec((tm, d), lambda i, j: (i, 0)),
                  pl.BlockSpec((1, d), lambda i, j: (0, 0)),
                  _layer_weight_spec(layer, d, tn, lambda i, j: j)],
        out_specs=pl.BlockSpec((tm, tn), lambda i, j: (i, j)),
        out_shape=jax.ShapeDtypeStruct((n, cols), F32),
        scratch_shapes=[pltpu.VMEM((tm, d), BF16)],
        compiler_params=_compiler_params(2, 48),
        name="norm_inproj",
    )(x, g.reshape(1, d), w)


def _ffn_up_kernel(x_ref, g_ref, wg_ref, wu_ref, o_ref, h_ref):
    @pl.when(pl.program_id(1) == 0)
    def _():
        h_ref[...] = _rms_normalize(x_ref[...], g_ref[...]).astype(BF16)

    h = h_ref[...]
    gate = jnp.dot(h, wg_ref[...], preferred_element_type=F32)
    up = jnp.dot(h, wu_ref[...], preferred_element_type=F32)
    o_ref[...] = (_silu(gate) * up).astype(BF16)


def ffn_up(x, g, wg, wu, layer, *, tm=1024, tn=512):
    n, d = x.shape
    hidden = wg.shape[2]
    assert n % tm == 0 and hidden % tn == 0
    w_spec = _layer_weight_spec(layer, d, tn, lambda i, j: j)
    return pl.pallas_call(
        _ffn_up_kernel,
        grid=(n // tm, hidden // tn),
        in_specs=[pl.BlockSpec((tm, d), lambda i, j: (i, 0)),
                  pl.BlockSpec((1, d), lambda i, j: (0, 0)),
                  w_spec, w_spec],
        out_specs=pl.BlockSpec((tm, tn), lambda i, j: (i, j)),
        out_shape=jax.ShapeDtypeStruct((n, hidden), BF16),
        scratch_shapes=[pltpu.VMEM((tm, d), BF16)],
        compiler_params=_compiler_params(2, 48),
        name="ffn_up",
    )(x, g.reshape(1, d), wg, wu)


def _matmul_residual_kernel(a_ref, w_ref, r_ref, o_ref):
    o_ref[...] = r_ref[...] + jnp.dot(a_ref[...], w_ref[...], preferred_element_type=F32)


def matmul_residual(a, w, res, layer, *, tm=1024, tn=512, name):
    n, k = a.shape
    cols = w.shape[2]
    assert n % tm == 0 and cols % tn == 0
    return pl.pallas_call(
        _matmul_residual_kernel,
        grid=(n // tm, cols // tn),
        in_specs=[pl.BlockSpec((tm, k), lambda i, j: (i, 0)),
                  _layer_weight_spec(layer, k, tn, lambda i, j: j),
                  pl.BlockSpec((tm, tn), lambda i, j: (i, j))],
        out_specs=pl.BlockSpec((tm, tn), lambda i, j: (i, j)),
        out_shape=jax.ShapeDtypeStruct((n, cols), F32),
        compiler_params=_compiler_params(2, 52),
        name=name,
    )(a, w, res)


def _merge_kernel(ret_ref, moba_ref, hgrn_ref, gr_ref, gm_ref, gh_ref,
                  wr_ref, wm_ref, wh_ref, o_ref):
    def branch(gate_ref, a_ref, w_ref):
        return jax.nn.sigmoid(gate_ref[...]) * jnp.dot(
            a_ref[...], w_ref[...], preferred_element_type=F32)

    mixed = (branch(gr_ref, ret_ref, wr_ref) + branch(gm_ref, moba_ref, wm_ref)
             + branch(gh_ref, hgrn_ref, wh_ref))
    o_ref[...] = mixed.astype(BF16)


def merge_branches(ret, moba, hgrn, proj, wr, wm, wh, layer, *, tm=512, tn=1024):
    n = ret.shape[0]
    assert n % tm == 0 and D_MODEL % tn == 0
    assert OFF_GR % tn == 0 and OFF_GM % tn == 0 and OFF_GH % tn == 0

    def branch_spec(width):
        return pl.BlockSpec((tm, width), lambda j, i: (i, 0))

    def gate_spec(off):
        return pl.BlockSpec((tm, tn), lambda j, i: (i, off // tn + j))

    def weight_spec(width):
        return _layer_weight_spec(layer, width, tn, lambda j, i: j)

    return pl.pallas_call(
        _merge_kernel,
        grid=(D_MODEL // tn, n // tm),
        in_specs=[branch_spec(RET_WIDTH), branch_spec(MOBA_WIDTH), branch_spec(HGRN_WIDTH),
                  gate_spec(OFF_GR), gate_spec(OFF_GM), gate_spec(OFF_GH),
                  weight_spec(RET_WIDTH), weight_spec(MOBA_WIDTH), weight_spec(HGRN_WIDTH)],
        out_specs=pl.BlockSpec((tm, tn), lambda j, i: (i, j)),
        out_shape=jax.ShapeDtypeStruct((n, D_MODEL), BF16),
        compiler_params=_compiler_params(2, 48),
        name="merge_branches",
    )(ret, moba, hgrn, proj, proj, proj, wr, wm, wh)


def _rmsnorm_kernel(x_ref, g_ref, o_ref):
    o_ref[...] = _rms_normalize(x_ref[...], g_ref[...])


def rmsnorm(x, g, *, tm=1024):
    n, d = x.shape
    assert n % tm == 0
    return pl.pallas_call(
        _rmsnorm_kernel,
        grid=(n // tm,),
        in_specs=[pl.BlockSpec((tm, d), lambda i: (i, 0)),
                  pl.BlockSpec((1, d), lambda i: (0, 0))],
        out_specs=pl.BlockSpec((tm, d), lambda i: (i, 0)),
        out_shape=jax.ShapeDtypeStruct((n, d), F32),
        compiler_params=_compiler_params(1, 40),
        name="final_rmsnorm",
    )(x, g.reshape(1, d))


def _retention_kernel(lg_ref, q_ref, k_ref, v_ref, g_ref, cos_ref, sin_ref, nrm_ref,
                      o_ref, state_ref, *, chunks):
    c_len = RET_CHUNK
    d = RET_HEAD_DIM
    half = d // 2

    @pl.when(pl.program_id(1) == 0)
    def _():
        state_ref[...] = jnp.zeros_like(state_ref)

    t_col = lax.broadcasted_iota(jnp.int32, (c_len, 1), 0).astype(F32)
    rel = (lax.broadcasted_iota(jnp.int32, (c_len, c_len), 0)
           - lax.broadcasted_iota(jnp.int32, (c_len, c_len), 1)).astype(F32)

    def rotate(x, cos, sin):
        x1 = x[:, :half]
        x2 = x[:, half:]
        return jnp.concatenate([x1 * cos - x2 * sin, x2 * cos + x1 * sin], axis=1)

    for h in range(RET_HEADS):
        cols = slice(h * d, (h + 1) * d)
        lg = lg_ref[h]
        decay = jnp.where(rel >= 0, jnp.exp(lg * jnp.maximum(rel, 0.0)), 0.0)
        q_weight = jnp.exp(lg * (t_col + 1.0))
        k_weight = jnp.exp(lg * (c_len - 1.0 - t_col))
        chunk_decay = jnp.exp(jnp.full((1, d), lg * c_len, F32))
        for c in range(chunks):
            rows = slice(c * c_len, (c + 1) * c_len)
            cos = cos_ref[rows, :]
            sin = sin_ref[rows, :]
            q = rotate(q_ref[rows, cols], cos, sin)
            k = rotate(k_ref[rows, cols], cos, sin) * (d ** -0.5)
            v = v_ref[rows, cols].astype(BF16)
            state = state_ref[h]

            scores = lax.dot_general(q.astype(BF16), k.astype(BF16), NT_DIMS,
                                     preferred_element_type=F32) * decay
            intra = jnp.dot(scores.astype(BF16), v, preferred_element_type=F32)
            cross = jnp.dot((q * q_weight).astype(BF16), state.astype(BF16),
                            preferred_element_type=F32)
            kv = lax.dot_general((k * k_weight).astype(BF16), v, TN_DIMS,
                                 preferred_element_type=F32)
            state_ref[h] = chunk_decay * state + kv

            out = intra + cross
            centered = out - jnp.mean(out, axis=-1, keepdims=True)
            normed = centered * lax.rsqrt(
                jnp.mean(centered * centered, axis=-1, keepdims=True) + NORM_EPS)
            gated = (normed * nrm_ref[:, cols]) * _silu(g_ref[rows, cols])
            o_ref[rows, cols] = gated.astype(BF16)


def retention_mixer(proj, ret_norm, batch, seq, *, rows_per_step=512):
    n = proj.shape[0]
    d = RET_HEAD_DIM
    assert seq % rows_per_step == 0 and rows_per_step % RET_CHUNK == 0
    steps = seq // rows_per_step
    log_gamma = jnp.log1p(-jnp.exp2(-5.0 - jnp.arange(RET_HEADS, dtype=F32)))
    half = d // 2
    inv_freq = RET_ROPE_BASE ** (-jnp.arange(half, dtype=F32) * 2.0 / d)
    ang = jnp.arange(seq).astype(F32)[:, None] * inv_freq[None, :]
    cos, sin = jnp.cos(ang), jnp.sin(ang)

    def col_spec(off):
        return pl.BlockSpec((rows_per_step, RET_WIDTH),
                            lambda b, s: (b * steps + s, off // RET_WIDTH))

    table_spec = pl.BlockSpec((rows_per_step, half), lambda b, s: (s, 0))
    return pl.pallas_call(
        functools.partial(_retention_kernel, chunks=rows_per_step // RET_CHUNK),
        grid=(batch, steps),
        in_specs=[pl.BlockSpec(memory_space=pltpu.SMEM),
                  col_spec(OFF_RQ), col_spec(OFF_RK), col_spec(OFF_RV), col_spec(OFF_RG),
                  table_spec, table_spec,
                  pl.BlockSpec((1, RET_WIDTH), lambda b, s: (0, 0))],
        out_specs=pl.BlockSpec((rows_per_step, RET_WIDTH), lambda b, s: (b * steps + s, 0)),
        out_shape=jax.ShapeDtypeStruct((n, RET_WIDTH), BF16),
        scratch_shapes=[pltpu.VMEM((RET_HEADS, d, d), F32)],
        compiler_params=_compiler_params(2, 40),
        name="retention",
    )(log_gamma, proj, proj, proj, proj, cos, sin, ret_norm.reshape(1, RET_WIDTH))


MOBA_PAIRS_PER_STEP = 2


def _moba_rotate(x, cos, sin_lo, sin_hi):
    half = ROPE_DIM // 2
    return (x * cos + pltpu.roll(x, MOBA_HEAD_DIM - half, 1) * sin_lo
            + pltpu.roll(x, half, 1) * sin_hi)


def _split_dot(a, b):
    a_hi = a.astype(BF16)
    a_lo = (a - a_hi.astype(F32)).astype(BF16)
    b_hi = b.astype(BF16)
    b_lo = (b - b_hi.astype(F32)).astype(BF16)
    return (jnp.dot(a_hi, b_hi, preferred_element_type=F32)
            + jnp.dot(a_hi, b_lo, preferred_element_type=F32)
            + jnp.dot(a_lo, b_hi, preferred_element_type=F32))


def _moba_kernel(q_ref, k_ref, v_ref, cos_ref, slo_ref, shi_ref, o_ref,
                 kaug_ref, vt_ref, kmean_ref, qt_ref, s_ref, acc_ref, snap_ref, *, n_blocks):
    blk = MOBA_BLOCK
    d = MOBA_HEAD_DIM
    step = pl.program_id(2)
    n_past = n_blocks - 1
    pairs = snap_ref.shape[0]
    gate_rows = kmean_ref.shape[0]
    groups = blk // SUBLANES
    q_scale = (d ** -0.5) * 1.4426950408889634

    def rotate(ref, start):
        rows = pl.ds(start, blk)
        return _moba_rotate(ref[rows, :], cos_ref[rows, :], slo_ref[rows, :], shi_ref[rows, :])

    @pl.when(step == 0)
    def _():
        lane = lax.broadcasted_iota(jnp.int32, (blk, LANES), 1)
        kmean_ref[...] = jnp.zeros_like(kmean_ref)
        for j in range(n_blocks):
            kr = rotate(k_ref, j * blk)
            kaug_ref[j, :, :d] = kr.astype(BF16)
            kaug_ref[j, :, d:] = jnp.where(lane == j, 1.0, 0.0).astype(BF16)
            vt_ref[j] = v_ref[j * blk:(j + 1) * blk, :].T.astype(BF16)
            kmean_ref[j:j + 1, :] = jnp.mean(kr, axis=0, keepdims=True)

    def group_reduce(x, op):
        return op(x.reshape(groups, SUBLANES, blk), axis=0)

    def setup_tile(w, i, own_slot):
        q_t = rotate(q_ref, pl.multiple_of(i * blk, blk)).T
        gate = _split_dot(kmean_ref[...], q_t)
        row = lax.broadcasted_iota(jnp.int32, (gate_rows, blk), 0)
        row_f = row.astype(F32)
        g = jnp.where(row < i, gate, -jnp.inf)
        bias = jnp.full((gate_rows, blk), MASK_VALUE, F32)
        for _ in range(min(MOBA_TOPK, n_blocks)):
            best = jnp.max(g, axis=0, keepdims=True)
            hit = (g == best) & (g > -jnp.inf)
            first = jnp.min(jnp.where(hit, row_f, float(LANES)), axis=0, keepdims=True)
            pick = row_f == first
            bias = jnp.where(pick, 0.0, bias)
            g = jnp.where(pick, -jnp.inf, g)
        bias = jnp.where(row == i, 0.0, bias)
        pad = jnp.zeros((LANES - gate_rows, blk), F32)
        qt_ref[w] = jnp.concatenate([q_t * q_scale, bias, pad], axis=0).astype(BF16)

        s_t = jnp.dot(kaug_ref[i], qt_ref[w], preferred_element_type=F32)
        key = lax.broadcasted_iota(jnp.int32, (blk, blk), 0)
        qry = lax.broadcasted_iota(jnp.int32, (blk, blk), 1)
        s_t = jnp.where(key <= qry, s_t, MASK_VALUE)
        s_ref[own_slot] = s_t
        return group_reduce(s_t, jnp.max)

    def tile_of(t, p):
        in_b = t >= p
        return in_b, jnp.where(in_b, t - p, t)

    def col_max(m8):
        return jnp.broadcast_to(jnp.max(m8, axis=0, keepdims=True), (SUBLANES, blk))

    def weights(slot, m8):
        p_t = jnp.exp2(s_ref[slot].reshape(groups, SUBLANES, blk) - m8[None])
        return jnp.sum(p_t, axis=0), p_t.reshape(blk, blk).astype(BF16)

    def values(j, p_bf):
        return jnp.dot(vt_ref[j], p_bf, preferred_element_type=F32)

    never = jnp.full((SUBLANES, blk), -jnp.inf, F32)
    scored = []
    for k in range(pairs):
        p = step * pairs + k
        base = k * (n_past + 2)
        i_a, i_b = p, n_past - p
        m_a = setup_tile(2 * k, i_a, base + n_past)
        m_b = setup_tile(2 * k + 1, i_b, base + n_past + 1)
        for t in range(n_past):
            in_b, j = tile_of(t, p)
            s_t = jnp.dot(kaug_ref[j], qt_ref[2 * k + jnp.where(in_b, 1, 0)],
                          preferred_element_type=F32)
            s_ref[base + t] = s_t
            gm = group_reduce(s_t, jnp.max)
            m_a = jnp.maximum(m_a, jnp.where(in_b, never, gm))
            m_b = jnp.maximum(m_b, jnp.where(in_b, gm, never))
        scored.append((p, base, i_a, i_b, col_max(m_a), col_max(m_b)))

    for k, (p, base, i_a, i_b, m_a, m_b) in enumerate(scored):
        l_a, p_own = weights(base + n_past, m_a)
        acc_ref[2 * k] = values(i_a, p_own)
        l_b, p_own = weights(base + n_past + 1, m_b)
        acc_ref[2 * k + 1] = values(i_b, p_own)
        acc = jnp.zeros((d, blk), F32)
        for t in range(n_past):
            in_b, j = tile_of(t, p)
            l_part, p_bf = weights(base + t, jnp.where(in_b, m_b, m_a))
            acc = acc * jnp.where(t == p, 0.0, 1.0) + values(j, p_bf)
            l_a = l_a + jnp.where(in_b, 0.0, l_part)
            l_b = l_b + jnp.where(in_b, l_part, 0.0)
            if t < snap_ref.shape[1]:
                snap_ref[k, t] = acc
        past_a = snap_ref[k, jnp.maximum(p - 1, 0)] * jnp.where(p > 0, 1.0, 0.0)
        for w, i, l8, past in ((2 * k, i_a, l_a, past_a), (2 * k + 1, i_b, l_b, acc)):
            l = jnp.sum(l8, axis=0, keepdims=True)
            out_t = (acc_ref[w] + past) / l
            o_ref[pl.ds(pl.multiple_of(i * blk, blk), blk), :] = out_t.T.astype(BF16)


def moba_mixer(proj, batch, seq):
    n = proj.shape[0]
    d = MOBA_HEAD_DIM
    blk = MOBA_BLOCK
    assert seq % blk == 0
    n_blocks = seq // blk
    assert n_blocks <= LANES and n_blocks % 2 == 0
    half = ROPE_DIM // 2
    inv_freq = ROPE_THETA ** (-jnp.arange(half, dtype=F32) * 2.0 / ROPE_DIM)
    ang = jnp.arange(seq).astype(F32)[:, None] * inv_freq[None, :]
    cos, sin = jnp.cos(ang), jnp.sin(ang)
    zeros = jnp.zeros((seq, d - ROPE_DIM), F32)
    zero_half = jnp.zeros((seq, half), F32)
    cos_t = jnp.concatenate([cos, cos, jnp.ones_like(zeros)], axis=1)
    sin_lo = jnp.concatenate([-sin, zero_half, zeros], axis=1)
    sin_hi = jnp.concatenate([zero_half, sin, zeros], axis=1)

    def col_spec(off):
        return pl.BlockSpec((seq, d), lambda b, h, p: (b, off // d + h))

    table_spec = pl.BlockSpec((seq, d), lambda b, h, p: (0, 0))
    gate_rows = -(-n_blocks // SUBLANES) * SUBLANES
    pairs = MOBA_PAIRS_PER_STEP
    assert (n_blocks // 2) % pairs == 0
    return pl.pallas_call(
        functools.partial(_moba_kernel, n_blocks=n_blocks),
        grid=(batch, MOBA_HEADS, n_blocks // 2 // pairs),
        in_specs=[col_spec(OFF_MQ), col_spec(OFF_MK), col_spec(OFF_MV),
                  table_spec, table_spec, table_spec],
        out_specs=pl.BlockSpec((seq, d), lambda b, h, p: (b, h)),
        out_shape=jax.ShapeDtypeStruct((n, MOBA_WIDTH), BF16),
        scratch_shapes=[pltpu.VMEM((n_blocks, blk, 2 * d), BF16),
                        pltpu.VMEM((n_blocks, d, blk), BF16),
                        pltpu.VMEM((gate_rows, d), F32),
                        pltpu.VMEM((2 * pairs, 2 * d, blk), BF16),
                        pltpu.VMEM((pairs * (n_blocks + 1), blk, blk), F32),
                        pltpu.VMEM((2 * pairs, d, blk), F32),
                        pltpu.VMEM((pairs, n_blocks // 2 - 1, d, blk), F32)],
        compiler_params=_compiler_params(3, 52),
        name="moba",
    )(proj, proj, proj, cos_t, sin_lo, sin_hi)


def _hgrn_kernel(lbp_ref, tril_ref, q_ref, f_ref, v_ref, g_ref, nrm_ref, o_ref,
                 state_ref, b_ref, k_ref, *, layer, rows_per_step):
    c_len = HGRN_CHUNK
    sub = HGRN_SUB
    n_sub = c_len // sub
    d = HGRN_HEAD_DIM

    @pl.when(pl.program_id(2) == 0)
    def _():
        state_ref[...] = jnp.zeros_like(state_ref)

    params = lbp_ref[...]
    e = jnp.exp(params - jnp.max(params, axis=0, keepdims=True))
    soft = e / jnp.sum(e, axis=0, keepdims=True)
    lower = jnp.zeros((1, d), F32)
    for r in range(1, layer + 1):
        lower = lower + soft[r:r + 1, :]

    tril = tril_ref[...]
    seg_len = tril_ref.shape[0]
    for seg0 in range(0, rows_per_step, seg_len):
        seg = slice(seg0, seg0 + seg_len)
        forget = lower + (1.0 - lower) * jax.nn.sigmoid(f_ref[seg, :])
        k_ref[seg, :] = 1.0 - forget
        log_f = jnp.log2(forget)
        hi = log_f.astype(BF16)
        rest = log_f - hi.astype(F32)
        mid = rest.astype(BF16)
        lo = (rest - mid.astype(F32)).astype(BF16)
        b_ref[seg, :] = (jnp.dot(tril, hi, preferred_element_type=F32)
                         + jnp.dot(tril, mid, preferred_element_type=F32)
                         + jnp.dot(tril, lo, preferred_element_type=F32))

    row8 = lax.broadcasted_iota(jnp.int32, (SUBLANES, d), 0)
    assert sub & (sub - 1) == 0
    bound = sub * n_sub * (n_sub - 1) // 2
    width = -(-bound // LANES) * LANES
    grp_row = jnp.right_shift(lax.broadcasted_iota(jnp.int32, (c_len, width), 0),
                              sub.bit_length() - 1)
    col = lax.broadcasted_iota(jnp.int32, (c_len, width), 1)
    grp_col = jnp.full((c_len, width), -1, jnp.int32)
    for grp in range(1, n_sub):
        start = sub * grp * (grp - 1) // 2
        grp_col = jnp.where((col >= start) & (col < start + grp * sub), grp, grp_col)
    off_mask = grp_row == grp_col

    def bcast_row(ref, r, rows):
        return jnp.broadcast_to(ref[r:r + 1, :], (rows, d))

    for c in range(rows_per_step // c_len):
        r0 = c * c_len
        rows = slice(r0, r0 + c_len)
        b = b_ref[rows, :]
        q = _silu(q_ref[rows, :])
        k = k_ref[rows, :]
        v = v_ref[rows, :]

        diag_parts = []
        for s_idx in range(n_sub):
            s0 = r0 + s_idx * sub
            for half_idx in range(sub // SUBLANES):
                lo_row = half_idx * SUBLANES
                bi = b_ref[s0 + lo_row:s0 + lo_row + SUBLANES, :]
                qi = q[s_idx * sub + lo_row:s_idx * sub + lo_row + SUBLANES, :]
                acc = jnp.zeros((SUBLANES, d), F32)
                for j in range(lo_row + SUBLANES):
                    bj = bcast_row(b_ref, s0 + j, SUBLANES)
                    kj = bcast_row(k_ref, s0 + j, SUBLANES)
                    vj = bcast_row(v_ref, s0 + j, SUBLANES)
                    expo = bi - bj
                    if j > lo_row:
                        expo = jnp.where(row8 + lo_row >= j, expo, -jnp.inf)
                    a = jnp.sum(qi * kj * jnp.exp2(expo), axis=-1, keepdims=True)
                    acc = acc + a * vj
                diag_parts.append(acc)
        diag = jnp.concatenate(diag_parts, axis=0)

        q_parts = [q[:sub, :] * jnp.exp2(b[:sub, :])]
        k_parts = []
        v_parts = []
        for s_idx in range(1, n_sub):
            b_anchor = b_ref[r0 + s_idx * sub - 1:r0 + s_idx * sub, :]
            lo_r, hi_r = s_idx * sub, (s_idx + 1) * sub
            q_parts.append(q[lo_r:hi_r, :] * jnp.exp2(b[lo_r:hi_r, :] - b_anchor))
            k_parts.append(k[:lo_r, :] * jnp.exp2(b_anchor - b[:lo_r, :]))
            v_parts.append(v[:lo_r, :])
        pad = [jnp.zeros((width - bound, d), F32)] if width > bound else []
        q_tilde = jnp.concatenate(q_parts, axis=0).astype(BF16)
        k_hat = jnp.concatenate(k_parts + pad, axis=0).astype(BF16)
        v_hat = jnp.concatenate(v_parts + pad, axis=0).astype(BF16)
        pair = lax.dot_general(q_tilde, k_hat, NT_DIMS, preferred_element_type=F32)
        off = jnp.dot(jnp.where(off_mask, pair, 0.0).astype(BF16), v_hat,
                      preferred_element_type=F32)

        state_t = state_ref[...]
        cross = lax.dot_general((q * jnp.exp2(b)).astype(BF16), state_t.astype(BF16), NT_DIMS,
                                preferred_element_type=F32)
        b_last = b_ref[r0 + c_len - 1:r0 + c_len, :]
        k_dec = k * jnp.exp2(b_last - b)
        state_ref[...] = state_t * jnp.exp2(b_last) + lax.dot_general(
            v.astype(BF16), k_dec.astype(BF16), TN_DIMS, preferred_element_type=F32)

        out = diag + off + cross
        normed = out * lax.rsqrt(jnp.mean(out * out, axis=-1, keepdims=True) + NORM_EPS)
        gated = (normed * nrm_ref[...]) * _silu(g_ref[rows, :])
        o_ref[rows, :] = gated.astype(BF16)


def hgrn_mixer(proj, lower_bound_params, hgrn_norm, layer, batch, seq, *, rows_per_step=1024):
    n = proj.shape[0]
    d = HGRN_HEAD_DIM
    depth = lower_bound_params.shape[0]
    seg_len = min(rows_per_step, HGRN_CUMSUM_ROWS)
    assert seq % rows_per_step == 0 and rows_per_step % seg_len == 0
    assert seg_len % HGRN_CHUNK == 0
    steps = seq // rows_per_step
    r = jnp.arange(seg_len)
    tril = ((r[None, :] <= r[:, None])
            & (r[None, :] // HGRN_CHUNK == r[:, None] // HGRN_CHUNK)).astype(BF16)

    def col_spec(off):
        return pl.BlockSpec((rows_per_step, d), lambda b, h, s: (b * steps + s, off // d + h))

    return pl.pallas_call(
        functools.partial(_hgrn_kernel, layer=layer, rows_per_step=rows_per_step),
        grid=(batch, HGRN_HEADS, steps),
        in_specs=[pl.BlockSpec((depth, d), lambda b, h, s: (0, h)),
                  pl.BlockSpec((seg_len, seg_len), lambda b, h, s: (0, 0)),
                  col_spec(OFF_HQ), col_spec(OFF_HF), col_spec(OFF_HI), col_spec(OFF_HG),
                  pl.BlockSpec((1, d), lambda b, h, s: (0, h))],
        out_specs=pl.BlockSpec((rows_per_step, d), lambda b, h, s: (b * steps + s, h)),
        out_shape=jax.ShapeDtypeStruct((n, HGRN_WIDTH), BF16),
        scratch_shapes=[pltpu.VMEM((d, d), F32),
                        pltpu.VMEM((rows_per_step, d), F32),
                        pltpu.VMEM((rows_per_step, d), F32)],
        compiler_params=_compiler_params(3, 32),
        name="hgrn2",
    )(lower_bound_params.astype(F32), tril, proj, proj, proj, proj,
      hgrn_norm.reshape(1, HGRN_WIDTH))


def kernel(x, hgrn_lower_bounds, norm_mix, w_in, ret_norm, hgrn_norm, w_branch_ret, w_branch_moba,
           w_branch_hgrn, w_out, norm_ffn, w_ffn_gate, w_ffn_up, w_ffn_down, final_norm):
    batch, seq, d_model = x.shape
    assert d_model == D_MODEL
    depth = w_in.shape[0]
    (w_in, w_branch_ret, w_branch_moba, w_branch_hgrn, w_out, w_ffn_gate, w_ffn_up,
     w_ffn_down) = (w.astype(BF16) for w in (w_in, w_branch_ret, w_branch_moba, w_branch_hgrn,
                                             w_out, w_ffn_gate, w_ffn_up, w_ffn_down))
    h = x.reshape(batch * seq, d_model)
    for layer in range(depth):
        proj = norm_matmul(h, norm_mix[layer], w_in, layer)
        ret = retention_mixer(proj, ret_norm[layer], batch, seq)
        moba = moba_mixer(proj, batch, seq)
        hgrn = hgrn_mixer(proj, hgrn_lower_bounds, hgrn_norm[layer], layer, batch, seq)
        mixed = merge_branches(ret, moba, hgrn, proj, w_branch_ret, w_branch_moba,
                               w_branch_hgrn, layer)
        h = matmul_residual(mixed, w_out, h, layer, tn=1024, name="out_proj")
        act = ffn_up(h, norm_ffn[layer], w_ffn_gate, w_ffn_up, layer)
        h = matmul_residual(act, w_ffn_down, h, layer, tn=512, name="ffn_down")
    return rmsnorm(h, final_norm).reshape(batch, seq, d_model)
```

```python
import functools

import jax
import jax.numpy as jnp
from jax import lax
from jax.experimental import pallas as pl
from jax.experimental.pallas import tpu as pltpu

F32 = jnp.float32
BF16 = jnp.bfloat16

D_MODEL = 2048
RET_HEADS = 4
RET_HEAD_DIM = 256
RET_WIDTH = RET_HEADS * RET_HEAD_DIM
RET_CHUNK = 128
RET_ROPE_BASE = 10000.0
MOBA_HEADS = 8
MOBA_HEAD_DIM = 128
MOBA_WIDTH = MOBA_HEADS * MOBA_HEAD_DIM
MOBA_BLOCK = 256
MOBA_TOPK = 3
ROPE_THETA = 500000.0
ROPE_DIM = MOBA_HEAD_DIM // 4
HGRN_HEADS = 8
HGRN_HEAD_DIM = 128
HGRN_WIDTH = HGRN_HEADS * HGRN_HEAD_DIM
HGRN_CHUNK = 64
HGRN_SUB = 16
HGRN_CUMSUM_ROWS = 256
NORM_EPS = 1e-6
IN_SIZES = (RET_WIDTH,) * 4 + (MOBA_WIDTH,) * 3 + (HGRN_WIDTH,) * 4 + (D_MODEL,) * 3
IN_COLS = sum(IN_SIZES)
IN_OFFS = tuple(sum(IN_SIZES[:i]) for i in range(len(IN_SIZES)))
(OFF_RQ, OFF_RK, OFF_RV, OFF_RG, OFF_MQ, OFF_MK, OFF_MV,
 OFF_HQ, OFF_HF, OFF_HI, OFF_HG, OFF_GR, OFF_GM, OFF_GH) = IN_OFFS

V7X_VMEM_BYTES = 64 * 1024 * 1024
LANES = 128
SUBLANES = 8
MASK_VALUE = -1e30

NT_DIMS = (((1,), (1,)), ((), ()))
TN_DIMS = (((0,), (0,)), ((), ()))


def _compiler_params(n_grid_dims, vmem_mib):
    assert vmem_mib * 1024 * 1024 < V7X_VMEM_BYTES
    return pltpu.CompilerParams(
        dimension_semantics=("arbitrary",) * n_grid_dims,
        vmem_limit_bytes=vmem_mib * 1024 * 1024)


def _silu(x):
    return x * jax.nn.sigmoid(x)


def _layer_weight_spec(layer, rows, tn, col_of):
    return pl.BlockSpec((None, rows, tn), lambda *g: (layer, 0, col_of(*g)))


def _rms_normalize(x, g):
    r = lax.rsqrt(jnp.mean(x * x, axis=-1, keepdims=True) + NORM_EPS)
    return x * r * g


def _norm_matmul_kernel(x_ref, g_ref, w_ref, o_ref, h_ref):
    @pl.when(pl.program_id(1) == 0)
    def _():
        h_ref[...] = _rms_normalize(x_ref[...], g_ref[...]).astype(BF16)

    o_ref[...] = jnp.dot(h_ref[...], w_ref[...], preferred_element_type=F32)


def norm_matmul(x, g, w, layer, *, tm=1024, tn=1024):
    n, d = x.shape
    cols = w.shape[2]
    assert n % tm == 0 and cols % tn == 0
    return pl.pallas_call(
        _norm_matmul_kernel,
        grid=(n // tm, cols // tn),
        in_specs=[pl.BlockSpec((tm, d), lambda i, j: (i, 0)),
                  pl.BlockSpec((1, d), lambda i, j: (0, 0)),
                  _layer_weight_spec(layer, d, tn, lambda i, j: j)],
        out_specs=pl.BlockSpec((tm, tn), lambda i, j: (i, j)),
        out_shape=jax.ShapeDtypeStruct((n, cols), F32),
        scratch_shapes=[pltpu.VMEM((tm, d), BF16)],
        compiler_params=_compiler_params(2, 48),
        name="norm_inproj",
    )(x, g.reshape(1, d), w)


def _ffn_up_kernel(x_ref, g_ref, wg_ref, wu_ref, o_ref, h_ref):
    @pl.when(pl.program_id(1) == 0)
    def _():
        h_ref[...] = _rms_normalize(x_ref[...], g_ref[...]).astype(BF16)

    h = h_ref[...]
    gate = jnp.dot(h, wg_ref[...], preferred_element_type=F32)
    up = jnp.dot(h, wu_ref[...], preferred_element_type=F32)
    o_ref[...] = (_silu(gate) * up).astype(BF16)


def ffn_up(x, g, wg, wu, layer, *, tm=1024, tn=512):
    n, d = x.shape
    hidden = wg.shape[2]
    assert n % tm == 0 and hidden % tn == 0
    w_spec = _layer_weight_spec(layer, d, tn, lambda i, j: j)
    return pl.pallas_call(
        _ffn_up_kernel,
        grid=(n // tm, hidden // tn),
        in_specs=[pl.BlockSpec((tm, d), lambda i, j: (i, 0)),
                  pl.BlockSpec((1, d), lambda i, j: (0, 0)),
                  w_spec, w_spec],
        out_specs=pl.BlockSpec((tm, tn), lambda i, j: (i, j)),
        out_shape=jax.ShapeDtypeStruct((n, hidden), BF16),
        scratch_shapes=[pltpu.VMEM((tm, d), BF16)],
        compiler_params=_compiler_params(2, 48),
        name="ffn_up",
    )(x, g.reshape(1, d), wg, wu)


def _matmul_residual_kernel(a_ref, w_ref, r_ref, o_ref):
    o_ref[...] = r_ref[...] + jnp.dot(a_ref[...], w_ref[...], preferred_element_type=F32)


def matmul_residual(a, w, res, layer, *, tm=1024, tn=512, name):
    n, k = a.shape
    cols = w.shape[2]
    assert n % tm == 0 and cols % tn == 0
    return pl.pallas_call(
        _matmul_residual_kernel,
        grid=(n // tm, cols // tn),
        in_specs=[pl.BlockSpec((tm, k), lambda i, j: (i, 0)),
                  _layer_weight_spec(layer, k, tn, lambda i, j: j),
                  pl.BlockSpec((tm, tn), lambda i, j: (i, j))],
        out_specs=pl.BlockSpec((tm, tn), lambda i, j: (i, j)),
        out_shape=jax.ShapeDtypeStruct((n, cols), F32),
        compiler_params=_compiler_params(2, 52),
        name=name,
    )(a, w, res)


MERGE_GATE_TILE = 1024


def _merge_out_kernel(*refs):
    n_gate = D_MODEL // MERGE_GATE_TILE
    branch_refs = refs[:3]
    gate_refs = refs[3:3 + 3 * n_gate]
    wb_refs = refs[3 + 3 * n_gate:6 + 3 * n_gate]
    wo_ref, res_ref, o_ref = refs[6 + 3 * n_gate:]
    parts = []
    for c in range(n_gate):
        cols = slice(c * MERGE_GATE_TILE, (c + 1) * MERGE_GATE_TILE)
        mixed = None
        for b in range(3):
            term = jax.nn.sigmoid(gate_refs[b * n_gate + c][...]) * jnp.dot(
                branch_refs[b][...], wb_refs[b][:, cols], preferred_element_type=F32)
            mixed = term if mixed is None else mixed + term
        parts.append(mixed.astype(BF16))
    mixed = jnp.concatenate(parts, axis=1)
    o_ref[...] = res_ref[...] + jnp.dot(mixed, wo_ref[...], preferred_element_type=F32)


def merge_out_proj(ret, moba, hgrn, proj, wr, wm, wh, wo, res, layer, *, tm=256):
    n = ret.shape[0]
    assert n % tm == 0 and D_MODEL % MERGE_GATE_TILE == 0
    assert all(off % MERGE_GATE_TILE == 0 for off in (OFF_GR, OFF_GM, OFF_GH))
    n_gate = D_MODEL // MERGE_GATE_TILE

    def branch_spec(width):
        return pl.BlockSpec((tm, width), lambda i: (i, 0))

    def gate_specs(off):
        return [pl.BlockSpec((tm, MERGE_GATE_TILE),
                             functools.partial(lambda i, col: (i, col), col=off // MERGE_GATE_TILE + c))
                for c in range(n_gate)]

    def weight_spec(rows):
        return pl.BlockSpec((None, rows, D_MODEL), lambda i: (layer, 0, 0),
                            pipeline_mode=pl.Buffered(1))

    row_spec = pl.BlockSpec((tm, D_MODEL), lambda i: (i, 0))
    return pl.pallas_call(
        _merge_out_kernel,
        grid=(n // tm,),
        in_specs=[branch_spec(RET_WIDTH), branch_spec(MOBA_WIDTH), branch_spec(HGRN_WIDTH),
                  *gate_specs(OFF_GR), *gate_specs(OFF_GM), *gate_specs(OFF_GH),
                  weight_spec(RET_WIDTH), weight_spec(MOBA_WIDTH), weight_spec(HGRN_WIDTH),
                  weight_spec(D_MODEL), row_spec],
        out_specs=row_spec,
        out_shape=jax.ShapeDtypeStruct((n, D_MODEL), F32),
        compiler_params=_compiler_params(1, 52),
        name="merge_out_proj",
    )(ret, moba, hgrn, *([proj] * (3 * n_gate)), wr, wm, wh, wo, res)


def _rmsnorm_kernel(x_ref, g_ref, o_ref):
    o_ref[...] = _rms_normalize(x_ref[...], g_ref[...])


def rmsnorm(x, g, *, tm=1024):
    n, d = x.shape
    assert n % tm == 0
    return pl.pallas_call(
        _rmsnorm_kernel,
        grid=(n // tm,),
        in_specs=[pl.BlockSpec((tm, d), lambda i: (i, 0)),
                  pl.BlockSpec((1, d), lambda i: (0, 0))],
        out_specs=pl.BlockSpec((tm, d), lambda i: (i, 0)),
        out_shape=jax.ShapeDtypeStruct((n, d), F32),
        compiler_params=_compiler_params(1, 40),
        name="final_rmsnorm",
    )(x, g.reshape(1, d))


def _retention_kernel(lg_ref, q_ref, k_ref, v_ref, g_ref, cos_ref, sin_ref, nrm_ref,
                      o_ref, state_ref, *, chunks):
    c_len = RET_CHUNK
    d = RET_HEAD_DIM
    half = d // 2

    @pl.when(pl.program_id(1) == 0)
    def _():
        state_ref[...] = jnp.zeros_like(state_ref)

    t_col = lax.broadcasted_iota(jnp.int32, (c_len, 1), 0).astype(F32)
    rel = (lax.broadcasted_iota(jnp.int32, (c_len, c_len), 0)
           - lax.broadcasted_iota(jnp.int32, (c_len, c_len), 1)).astype(F32)

    def rotate(x, cos, sin):
        x1 = x[:, :half]
        x2 = x[:, half:]
        return jnp.concatenate([x1 * cos - x2 * sin, x2 * cos + x1 * sin], axis=1)

    for h in range(RET_HEADS):
        cols = slice(h * d, (h + 1) * d)
        lg = lg_ref[h]
        decay = jnp.where(rel >= 0, jnp.exp(lg * jnp.maximum(rel, 0.0)), 0.0)
        q_weight = jnp.exp(lg * (t_col + 1.0))
        k_weight = jnp.exp(lg * (c_len - 1.0 - t_col))
        chunk_decay = jnp.exp(jnp.full((1, d), lg * c_len, F32))
        for c in range(chunks):
            rows = slice(c * c_len, (c + 1) * c_len)
            cos = cos_ref[rows, :]
            sin = sin_ref[rows, :]
            q = rotate(q_ref[rows, cols], cos, sin)
            k = rotate(k_ref[rows, cols], cos, sin) * (d ** -0.5)
            v = v_ref[rows, cols].astype(BF16)
            state = state_ref[h]

            scores = lax.dot_general(q.astype(BF16), k.astype(BF16), NT_DIMS,
                                     preferred_element_type=F32) * decay
            intra = jnp.dot(scores.astype(BF16), v, preferred_element_type=F32)
            cross = jnp.dot((q * q_weight).astype(BF16), state.astype(BF16),
                            preferred_element_type=F32)
            kv = lax.dot_general((k * k_weight).astype(BF16), v, TN_DIMS,
                                 preferred_element_type=F32)
            state_ref[h] = chunk_decay * state + kv

            out = intra + cross
            centered = out - jnp.mean(out, axis=-1, keepdims=True)
            normed = centered * lax.rsqrt(
                jnp.mean(centered * centered, axis=-1, keepdims=True) + NORM_EPS)
            gated = (normed * nrm_ref[:, cols]) * _silu(g_ref[rows, cols])
            o_ref[rows, cols] = gated.astype(BF16)


def retention_mixer(proj, ret_norm, batch, seq, *, rows_per_step=512):
    n = proj.shape[0]
    d = RET_HEAD_DIM
    assert seq % rows_per_step == 0 and rows_per_step % RET_CHUNK == 0
    steps = seq // rows_per_step
    log_gamma = jnp.log1p(-jnp.exp2(-5.0 - jnp.arange(RET_HEADS, dtype=F32)))
    half = d // 2
    inv_freq = RET_ROPE_BASE ** (-jnp.arange(half, dtype=F32) * 2.0 / d)
    ang = jnp.arange(seq).astype(F32)[:, None] * inv_freq[None, :]
    cos, sin = jnp.cos(ang), jnp.sin(ang)

    def col_spec(off):
        return pl.BlockSpec((rows_per_step, RET_WIDTH),
                            lambda b, s: (b * steps + s, off // RET_WIDTH))

    table_spec = pl.BlockSpec((rows_per_step, half), lambda b, s: (s, 0))
    return pl.pallas_call(
        functools.partial(_retention_kernel, chunks=rows_per_step // RET_CHUNK),
        grid=(batch, steps),
        in_specs=[pl.BlockSpec(memory_space=pltpu.SMEM),
                  col_spec(OFF_RQ), col_spec(OFF_RK), col_spec(OFF_RV), col_spec(OFF_RG),
                  table_spec, table_spec,
                  pl.BlockSpec((1, RET_WIDTH), lambda b, s: (0, 0))],
        out_specs=pl.BlockSpec((rows_per_step, RET_WIDTH), lambda b, s: (b * steps + s, 0)),
        out_shape=jax.ShapeDtypeStruct((n, RET_WIDTH), BF16),
        scratch_shapes=[pltpu.VMEM((RET_HEADS, d, d), F32)],
        compiler_params=_compiler_params(2, 40),
        name="retention",
    )(log_gamma, proj, proj, proj, proj, cos, sin, ret_norm.reshape(1, RET_WIDTH))


MOBA_PAIRS_PER_STEP = 2


def _moba_rotate(x, cos, sin_lo, sin_hi):
    half = ROPE_DIM // 2
    return (x * cos + pltpu.roll(x, MOBA_HEAD_DIM - half, 1) * sin_lo
            + pltpu.roll(x, half, 1) * sin_hi)


def _split_dot(a, b):
    a_hi = a.astype(BF16)
    a_lo = (a - a_hi.astype(F32)).astype(BF16)
    b_hi = b.astype(BF16)
    b_lo = (b - b_hi.astype(F32)).astype(BF16)
    return (jnp.dot(a_hi, b_hi, preferred_element_type=F32)
            + jnp.dot(a_hi, b_lo, preferred_element_type=F32)
            + jnp.dot(a_lo, b_hi, preferred_element_type=F32))


def _moba_kernel(q_ref, k_ref, v_ref, cos_ref, slo_ref, shi_ref, o_ref,
                 kaug_ref, vt_ref, kmean_ref, qt_ref, s_ref, acc_ref, snap_ref, *, n_blocks):
    blk = MOBA_BLOCK
    d = MOBA_HEAD_DIM
    step = pl.program_id(2)
    n_past = n_blocks - 1
    pairs = snap_ref.shape[0]
    gate_rows = kmean_ref.shape[0]
    groups = blk // SUBLANES
    q_scale = (d ** -0.5) * 1.4426950408889634

    def rotate(ref, start):
        rows = pl.ds(start, blk)
        return _moba_rotate(ref[rows, :], cos_ref[rows, :], slo_ref[rows, :], shi_ref[rows, :])

    @pl.when(step == 0)
    def _():
        lane = lax.broadcasted_iota(jnp.int32, (blk, LANES), 1)
        kmean_ref[...] = jnp.zeros_like(kmean_ref)
        for j in range(n_blocks):
            kr = rotate(k_ref, j * blk)
            kaug_ref[j, :, :d] = kr.astype(BF16)
            kaug_ref[j, :, d:] = jnp.where(lane == j, 1.0, 0.0).astype(BF16)
            vt_ref[j] = v_ref[j * blk:(j + 1) * blk, :].T.astype(BF16)
            kmean_ref[j:j + 1, :] = jnp.mean(kr, axis=0, keepdims=True)
        row = lax.broadcasted_iota(jnp.int32, (gate_rows, blk), 0)
        row_f = row.astype(F32)
        pad = jnp.zeros((LANES - gate_rows, blk), F32)
        for i in range(n_blocks):
            q_t = rotate(q_ref, i * blk).T
            gate = _split_dot(kmean_ref[...], q_t)
            g = jnp.where(row < i, gate, -jnp.inf)
            bias = jnp.full((gate_rows, blk), MASK_VALUE, F32)
            for _ in range(min(MOBA_TOPK, n_blocks)):
                best = jnp.max(g, axis=0, keepdims=True)
                hit = (g == best) & (g > -jnp.inf)
                first = jnp.min(jnp.where(hit, row_f, float(LANES)), axis=0, keepdims=True)
                pick = row_f == first
                bias = jnp.where(pick, 0.0, bias)
                g = jnp.where(pick, -jnp.inf, g)
            bias = jnp.where(row == i, 0.0, bias)
            qt_ref[i] = jnp.concatenate([q_t * q_scale, bias, pad], axis=0).astype(BF16)

    def group_reduce(x, op):
        return op(x.reshape(groups, SUBLANES, blk), axis=0)

    def score_own_block(i, own_slot):
        s_t = jnp.dot(kaug_ref[i], qt_ref[i], preferred_element_type=F32)
        key = lax.broadcasted_iota(jnp.int32, (blk, blk), 0)
        qry = lax.broadcasted_iota(jnp.int32, (blk, blk), 1)
        s_t = jnp.where(key <= qry, s_t, MASK_VALUE)
        s_ref[own_slot] = s_t
        return group_reduce(s_t, jnp.max)

    def tile_of(t, p):
        in_b = t >= p
        return in_b, jnp.where(in_b, t - p, t)

    def col_max(m8):
        return jnp.broadcast_to(jnp.max(m8, axis=0, keepdims=True), (SUBLANES, blk))

    def weights(slot, m8):
        p_t = jnp.exp2(s_ref[slot].reshape(groups, SUBLANES, blk) - m8[None])
        return jnp.sum(p_t, axis=0), p_t.reshape(blk, blk).astype(BF16)

    def values(j, p_bf):
        return jnp.dot(vt_ref[j], p_bf, preferred_element_type=F32)

    never = jnp.full((SUBLANES, blk), -jnp.inf, F32)
    scored = []
    for k in range(pairs):
        p = step * pairs + k
        base = k * (n_past + 2)
        i_a, i_b = p, n_past - p
        m_a = score_own_block(i_a, base + n_past)
        m_b = score_own_block(i_b, base + n_past + 1)
        for t in range(n_past):
            in_b, j = tile_of(t, p)
            s_t = jnp.dot(kaug_ref[j], qt_ref[jnp.where(in_b, i_b, i_a)],
                          preferred_element_type=F32)
            s_ref[base + t] = s_t
            gm = group_reduce(s_t, jnp.max)
            m_a = jnp.maximum(m_a, jnp.where(in_b, never, gm))
            m_b = jnp.maximum(m_b, jnp.where(in_b, gm, never))
        scored.append((p, base, i_a, i_b, col_max(m_a), col_max(m_b)))

    for k, (p, base, i_a, i_b, m_a, m_b) in enumerate(scored):
        l_a, p_own = weights(base + n_past, m_a)
        acc_ref[2 * k] = values(i_a, p_own)
        l_b, p_own = weights(base + n_past + 1, m_b)
        acc_ref[2 * k + 1] = values(i_b, p_own)
        acc = jnp.zeros((d, blk), F32)
        for t in range(n_past):
            in_b, j = tile_of(t, p)
            l_part, p_bf = weights(base + t, jnp.where(in_b, m_b, m_a))
            acc = acc * jnp.where(t == p, 0.0, 1.0) + values(j, p_bf)
            l_a = l_a + jnp.where(in_b, 0.0, l_part)
            l_b = l_b + jnp.where(in_b, l_part, 0.0)
            if t < snap_ref.shape[1]:
                snap_ref[k, t] = acc
        past_a = snap_ref[k, jnp.maximum(p - 1, 0)] * jnp.where(p > 0, 1.0, 0.0)
        for w, i, l8, past in ((2 * k, i_a, l_a, past_a), (2 * k + 1, i_b, l_b, acc)):
            l = jnp.sum(l8, axis=0, keepdims=True)
            out_t = (acc_ref[w] + past) / l
            o_ref[pl.ds(pl.multiple_of(i * blk, blk), blk), :] = out_t.T.astype(BF16)


def moba_mixer(proj, batch, seq):
    n = proj.shape[0]
    d = MOBA_HEAD_DIM
    blk = MOBA_BLOCK
    assert seq % blk == 0
    n_blocks = seq // blk
    assert n_blocks <= LANES and n_blocks % 2 == 0
    half = ROPE_DIM // 2
    inv_freq = ROPE_THETA ** (-jnp.arange(half, dtype=F32) * 2.0 / ROPE_DIM)
    ang = jnp.arange(seq).astype(F32)[:, None] * inv_freq[None, :]
    cos, sin = jnp.cos(ang), jnp.sin(ang)
    zeros = jnp.zeros((seq, d - ROPE_DIM), F32)
    zero_half = jnp.zeros((seq, half), F32)
    cos_t = jnp.concatenate([cos, cos, jnp.ones_like(zeros)], axis=1)
    sin_lo = jnp.concatenate([-sin, zero_half, zeros], axis=1)
    sin_hi = jnp.concatenate([zero_half, sin, zeros], axis=1)

    def col_spec(off):
        return pl.BlockSpec((seq, d), lambda b, h, p: (b, off // d + h))

    table_spec = pl.BlockSpec((seq, d), lambda b, h, p: (0, 0))
    gate_rows = -(-n_blocks // SUBLANES) * SUBLANES
    pairs = MOBA_PAIRS_PER_STEP
    assert (n_blocks // 2) % pairs == 0
    n_snap = max(n_blocks // 2 - 1, 1)
    return pl.pallas_call(
        functools.partial(_moba_kernel, n_blocks=n_blocks),
        grid=(batch, MOBA_HEADS, n_blocks // 2 // pairs),
        in_specs=[col_spec(OFF_MQ), col_spec(OFF_MK), col_spec(OFF_MV),
                  table_spec, table_spec, table_spec],
        out_specs=pl.BlockSpec((seq, d), lambda b, h, p: (b, h)),
        out_shape=jax.ShapeDtypeStruct((n, MOBA_WIDTH), BF16),
        scratch_shapes=[pltpu.VMEM((n_blocks, blk, 2 * d), BF16),
                        pltpu.VMEM((n_blocks, d, blk), BF16),
                        pltpu.VMEM((gate_rows, d), F32),
                        pltpu.VMEM((n_blocks, 2 * d, blk), BF16),
                        pltpu.VMEM((pairs * (n_blocks + 1), blk, blk), F32),
                        pltpu.VMEM((2 * pairs, d, blk), F32),
                        pltpu.VMEM((pairs, n_snap, d, blk), F32)],
        compiler_params=_compiler_params(3, 52),
        name="moba",
    )(proj, proj, proj, cos_t, sin_lo, sin_hi)


def _hgrn_kernel(lbp_ref, tril_ref, q_ref, f_ref, v_ref, g_ref, nrm_ref, o_ref,
                 state_ref, b_ref, k_ref, *, layer, rows_per_step):
    c_len = HGRN_CHUNK
    sub = HGRN_SUB
    n_sub = c_len // sub
    d = HGRN_HEAD_DIM

    @pl.when(pl.program_id(2) == 0)
    def _():
        state_ref[...] = jnp.zeros_like(state_ref)

    params = lbp_ref[...]
    e = jnp.exp(params - jnp.max(params, axis=0, keepdims=True))
    soft = e / jnp.sum(e, axis=0, keepdims=True)
    lower = jnp.zeros((1, d), F32)
    for r in range(1, layer + 1):
        lower = lower + soft[r:r + 1, :]

    tril = tril_ref[...]
    seg_len = tril_ref.shape[0]
    for seg0 in range(0, rows_per_step, seg_len):
        seg = slice(seg0, seg0 + seg_len)
        forget = lower + (1.0 - lower) * jax.nn.sigmoid(f_ref[seg, :])
        k_ref[seg, :] = 1.0 - forget
        log_f = jnp.log2(forget)
        hi = log_f.astype(BF16)
        rest = log_f - hi.astype(F32)
        mid = rest.astype(BF16)
        lo = (rest - mid.astype(F32)).astype(BF16)
        b_ref[seg, :] = (jnp.dot(tril, hi, preferred_element_type=F32)
                         + jnp.dot(tril, mid, preferred_element_type=F32)
                         + jnp.dot(tril, lo, preferred_element_type=F32))

    row8 = lax.broadcasted_iota(jnp.int32, (SUBLANES, d), 0)
    assert sub & (sub - 1) == 0
    bound = sub * n_sub * (n_sub - 1) // 2
    width = -(-bound // LANES) * LANES
    grp_row = jnp.right_shift(lax.broadcasted_iota(jnp.int32, (c_len, width), 0),
                              sub.bit_length() - 1)
    col = lax.broadcasted_iota(jnp.int32, (c_len, width), 1)
    grp_col = jnp.full((c_len, width), -1, jnp.int32)
    for grp in range(1, n_sub):
        start = sub * grp * (grp - 1) // 2
        grp_col = jnp.where((col >= start) & (col < start + grp * sub), grp, grp_col)
    off_mask = grp_row == grp_col

    def bcast_row(ref, r, rows):
        return jnp.broadcast_to(ref[r:r + 1, :], (rows, d))

    for c in range(rows_per_step // c_len):
        r0 = c * c_len
        rows = slice(r0, r0 + c_len)
        b = b_ref[rows, :]
        q = _silu(q_ref[rows, :])
        k = k_ref[rows, :]
        v = v_ref[rows, :]

        diag_parts = []
        for s_idx in range(n_sub):
            s0 = r0 + s_idx * sub
            for half_idx in range(sub // SUBLANES):
                lo_row = half_idx * SUBLANES
                bi = b_ref[s0 + lo_row:s0 + lo_row + SUBLANES, :]
                qi = q[s_idx * sub + lo_row:s_idx * sub + lo_row + SUBLANES, :]
                acc = jnp.zeros((SUBLANES, d), F32)
                for j in range(lo_row + SUBLANES):
                    bj = bcast_row(b_ref, s0 + j, SUBLANES)
                    kj = bcast_row(k_ref, s0 + j, SUBLANES)
                    vj = bcast_row(v_ref, s0 + j, SUBLANES)
                    expo = bi - bj
                    if j > lo_row:
                        expo = jnp.where(row8 + lo_row >= j, expo, -jnp.inf)
                    a = jnp.sum(qi * kj * jnp.exp2(expo), axis=-1, keepdims=True)
                    acc = acc + a * vj
                diag_parts.append(acc)
        diag = jnp.concatenate(diag_parts, axis=0)

        q_parts = [q[:sub, :] * jnp.exp2(b[:sub, :])]
        k_parts = []
        v_parts = []
        for s_idx in range(1, n_sub):
            b_anchor = b_ref[r0 + s_idx * sub - 1:r0 + s_idx * sub, :]
            lo_r, hi_r = s_idx * sub, (s_idx + 1) * sub
            q_parts.append(q[lo_r:hi_r, :] * jnp.exp2(b[lo_r:hi_r, :] - b_anchor))
            k_parts.append(k[:lo_r, :] * jnp.exp2(b_anchor - b[:lo_r, :]))
            v_parts.append(v[:lo_r, :])
        pad = [jnp.zeros((width - bound, d), F32)] if width > bound else []
        q_tilde = jnp.concatenate(q_parts, axis=0).astype(BF16)
        k_hat = jnp.concatenate(k_parts + pad, axis=0).astype(BF16)
        v_hat = jnp.concatenate(v_parts + pad, axis=0).astype(BF16)
        pair = lax.dot_general(q_tilde, k_hat, NT_DIMS, preferred_element_type=F32)
        off = jnp.dot(jnp.where(off_mask, pair, 0.0).astype(BF16), v_hat,
                      preferred_element_type=F32)

        state_t = state_ref[...]
        cross = lax.dot_general((q * jnp.exp2(b)).astype(BF16), state_t.astype(BF16), NT_DIMS,
                                preferred_element_type=F32)
        b_last = b_ref[r0 + c_len - 1:r0 + c_len, :]
        k_dec = k * jnp.exp2(b_last - b)
        state_ref[...] = state_t * jnp.exp2(b_last) + lax.dot_general(
            v.astype(BF16), k_dec.astype(BF16), TN_DIMS, preferred_element_type=F32)

        out = diag + off + cross
        normed = out * lax.rsqrt(jnp.mean(out * out, axis=-1, keepdims=True) + NORM_EPS)
        gated = (normed * nrm_ref[...]) * _silu(g_ref[rows, :])
        o_ref[rows, :] = gated.astype(BF16)


def hgrn_mixer(proj, lower_bound_params, hgrn_norm, layer, batch, seq, *, rows_per_step=1024):
    n = proj.shape[0]
    d = HGRN_HEAD_DIM
    depth = lower_bound_params.shape[0]
    seg_len = min(rows_per_step, HGRN_CUMSUM_ROWS)
    assert seq % rows_per_step == 0 and rows_per_step % seg_len == 0
    assert seg_len % HGRN_CHUNK == 0
    steps = seq // rows_per_step
    r = jnp.arange(seg_len)
    tril = ((r[None, :] <= r[:, None])
            & (r[None, :] // HGRN_CHUNK == r[:, None] // HGRN_CHUNK)).astype(BF16)

    def col_spec(off):
        return pl.BlockSpec((rows_per_step, d), lambda b, h, s: (b * steps + s, off // d + h))

    return pl.pallas_call(
        functools.partial(_hgrn_kernel, layer=layer, rows_per_step=rows_per_step),
        grid=(batch, HGRN_HEADS, steps),
        in_specs=[pl.BlockSpec((depth, d), lambda b, h, s: (0, h)),
                  pl.BlockSpec((seg_len, seg_len), lambda b, h, s: (0, 0)),
                  col_spec(OFF_HQ), col_spec(OFF_HF), col_spec(OFF_HI), col_spec(OFF_HG),
                  pl.BlockSpec((1, d), lambda b, h, s: (0, h))],
        out_specs=pl.BlockSpec((rows_per_step, d), lambda b, h, s: (b * steps + s, h)),
        out_shape=jax.ShapeDtypeStruct((n, HGRN_WIDTH), BF16),
        scratch_shapes=[pltpu.VMEM((d, d), F32),
                        pltpu.VMEM((rows_per_step, d), F32),
                        pltpu.VMEM((rows_per_step, d), F32)],
        compiler_params=_compiler_params(3, 32),
        name="hgrn2",
    )(lower_bound_params.astype(F32), tril, proj, proj, proj, proj,
      hgrn_norm.reshape(1, HGRN_WIDTH))


def kernel(x, hgrn_lower_bounds, norm_mix, w_in, ret_norm, hgrn_norm, w_branch_ret, w_branch_moba,
           w_branch_hgrn, w_out, norm_ffn, w_ffn_gate, w_ffn_up, w_ffn_down, final_norm):
    batch, seq, d_model = x.shape
    assert d_model == D_MODEL
    depth = w_in.shape[0]
    (w_in, w_branch_ret, w_branch_moba, w_branch_hgrn, w_out, w_ffn_gate, w_ffn_up,
     w_ffn_down) = (w.astype(BF16) for w in (w_in, w_branch_ret, w_branch_moba, w_branch_hgrn,
                                             w_out, w_ffn_gate, w_ffn_up, w_ffn_down))
    h = x.reshape(batch * seq, d_model)
    for layer in range(depth):
        proj = norm_matmul(h, norm_mix[layer], w_in, layer)
        ret = retention_mixer(proj, ret_norm[layer], batch, seq)
        moba = moba_mixer(proj, batch, seq)
        hgrn = hgrn_mixer(proj, hgrn_lower_bounds, hgrn_norm[layer], layer, batch, seq)
        h = merge_out_proj(ret, moba, hgrn, proj, w_branch_ret, w_branch_moba, w_branch_hgrn,
                           w_out, h, layer)
        act = ffn_up(h, norm_ffn[layer], w_ffn_gate, w_ffn_up, layer)
        h = matmul_residual(act, w_ffn_down, h, layer, tn=512, name="ffn_down")
    return rmsnorm(h, final_norm).reshape(batch, seq, d_model)
```

```python
import functools

import jax
import jax.numpy as jnp
import numpy as np
from jax import lax
from jax.experimental import pallas as pl
from jax.experimental.pallas import tpu as pltpu

F32 = jnp.float32
BF16 = jnp.bfloat16

D_MODEL = 2048
RET_HEADS = 4
RET_HEAD_DIM = 256
RET_WIDTH = RET_HEADS * RET_HEAD_DIM
RET_CHUNK = 128
RET_ROPE_BASE = 10000.0
MOBA_HEADS = 8
MOBA_HEAD_DIM = 128
MOBA_WIDTH = MOBA_HEADS * MOBA_HEAD_DIM
MOBA_BLOCK = 256
MOBA_TOPK = 3
ROPE_THETA = 500000.0
ROPE_DIM = MOBA_HEAD_DIM // 4
HGRN_HEADS = 8
HGRN_HEAD_DIM = 128
HGRN_WIDTH = HGRN_HEADS * HGRN_HEAD_DIM
HGRN_BLOCK = 256
NORM_EPS = 1e-6
IN_SIZES = (RET_WIDTH,) * 4 + (MOBA_WIDTH,) * 3 + (HGRN_WIDTH,) * 4 + (D_MODEL,) * 3
IN_COLS = sum(IN_SIZES)
IN_OFFS = tuple(sum(IN_SIZES[:i]) for i in range(len(IN_SIZES)))
(OFF_RQ, OFF_RK, OFF_RV, OFF_RG, OFF_MQ, OFF_MK, OFF_MV,
 OFF_HQ, OFF_HF, OFF_HI, OFF_HG, OFF_GR, OFF_GM, OFF_GH) = IN_OFFS

V7X_VMEM_BYTES = 64 * 1024 * 1024
LANES = 128
SUBLANES = 8
MASK_VALUE = -1e30

NT_DIMS = (((1,), (1,)), ((), ()))
TN_DIMS = (((0,), (0,)), ((), ()))


def _compiler_params(n_grid_dims, vmem_mib):
    assert vmem_mib * 1024 * 1024 < V7X_VMEM_BYTES
    return pltpu.CompilerParams(
        dimension_semantics=("arbitrary",) * n_grid_dims,
        vmem_limit_bytes=vmem_mib * 1024 * 1024)


def _silu(x):
    return x * jax.nn.sigmoid(x)


def _layer_weight_spec(layer, rows, tn, col_of):
    return pl.BlockSpec((None, rows, tn), lambda *g: (layer, 0, col_of(*g)))


def _rms_normalize(x, g):
    r = lax.rsqrt(jnp.mean(x * x, axis=-1, keepdims=True) + NORM_EPS)
    return x * r * g


def _norm_matmul_kernel(x_ref, g_ref, w_ref, o_ref, h_ref):
    @pl.when(pl.program_id(1) == 0)
    def _():
        h_ref[...] = _rms_normalize(x_ref[...], g_ref[...]).astype(BF16)

    o_ref[...] = jnp.dot(h_ref[...], w_ref[...], preferred_element_type=F32)


def norm_matmul(x, g, w, layer, *, tm=1024, tn=1024):
    n, d = x.shape
    cols = w.shape[2]
    assert n % tm == 0 and cols % tn == 0
    return pl.pallas_call(
        _norm_matmul_kernel,
        grid=(n // tm, cols // tn),
        in_specs=[pl.BlockSpec((tm, d), lambda i, j: (i, 0)),
                  pl.BlockSpec((1, d), lambda i, j: (0, 0)),
                  _layer_weight_spec(layer, d, tn, lambda i, j: j)],
        out_specs=pl.BlockSpec((tm, tn), lambda i, j: (i, j)),
        out_shape=jax.ShapeDtypeStruct((n, cols), F32),
        scratch_shapes=[pltpu.VMEM((tm, d), BF16)],
        compiler_params=_compiler_params(2, 48),
        name="norm_inproj",
    )(x, g.reshape(1, d), w)


def _ffn_up_kernel(x_ref, g_ref, wg_ref, wu_ref, o_ref, h_ref):
    @pl.when(pl.program_id(1) == 0)
    def _():
        h_ref[...] = _rms_normalize(x_ref[...], g_ref[...]).astype(BF16)

    h = h_ref[...]
    gate = jnp.dot(h, wg_ref[...], preferred_element_type=F32)
    up = jnp.dot(h, wu_ref[...], preferred_element_type=F32)
    o_ref[...] = (_silu(gate) * up).astype(BF16)


def ffn_up(x, g, wg, wu, layer, *, tm=1024, tn=512):
    n, d = x.shape
    hidden = wg.shape[2]
    assert n % tm == 0 and hidden % tn == 0
    w_spec = _layer_weight_spec(layer, d, tn, lambda i, j: j)
    return pl.pallas_call(
        _ffn_up_kernel,
        grid=(n // tm, hidden // tn),
        in_specs=[pl.BlockSpec((tm, d), lambda i, j: (i, 0)),
                  pl.BlockSpec((1, d), lambda i, j: (0, 0)),
                  w_spec, w_spec],
        out_specs=pl.BlockSpec((tm, tn), lambda i, j: (i, j)),
        out_shape=jax.ShapeDtypeStruct((n, hidden), BF16),
        scratch_shapes=[pltpu.VMEM((tm, d), BF16)],
        compiler_params=_compiler_params(2, 48),
        name="ffn_up",
    )(x, g.reshape(1, d), wg, wu)


def _matmul_residual_kernel(a_ref, w_ref, r_ref, o_ref):
    o_ref[...] = r_ref[...] + jnp.dot(a_ref[...], w_ref[...], preferred_element_type=F32)


def matmul_residual(a, w, res, layer, *, tm=1024, tn=512, name):
    n, k = a.shape
    cols = w.shape[2]
    assert n % tm == 0 and cols % tn == 0
    return pl.pallas_call(
        _matmul_residual_kernel,
        grid=(n // tm, cols // tn),
        in_specs=[pl.BlockSpec((tm, k), lambda i, j: (i, 0)),
                  _layer_weight_spec(layer, k, tn, lambda i, j: j),
                  pl.BlockSpec((tm, tn), lambda i, j: (i, j))],
        out_specs=pl.BlockSpec((tm, tn), lambda i, j: (i, j)),
        out_shape=jax.ShapeDtypeStruct((n, cols), F32),
        compiler_params=_compiler_params(2, 52),
        name=name,
    )(a, w, res)


MERGE_GATE_TILE = 1024


def _merge_out_kernel(*refs):
    n_gate = D_MODEL // MERGE_GATE_TILE
    branch_refs = refs[:3]
    gate_refs = refs[3:3 + 3 * n_gate]
    wb_refs = refs[3 + 3 * n_gate:6 + 3 * n_gate]
    wo_ref, res_ref, o_ref = refs[6 + 3 * n_gate:]
    parts = []
    for c in range(n_gate):
        cols = slice(c * MERGE_GATE_TILE, (c + 1) * MERGE_GATE_TILE)
        mixed = None
        for b in range(3):
            term = jax.nn.sigmoid(gate_refs[b * n_gate + c][...]) * jnp.dot(
                branch_refs[b][...], wb_refs[b][:, cols], preferred_element_type=F32)
            mixed = term if mixed is None else mixed + term
        parts.append(mixed.astype(BF16))
    mixed = jnp.concatenate(parts, axis=1)
    o_ref[...] = res_ref[...] + jnp.dot(mixed, wo_ref[...], preferred_element_type=F32)


def merge_out_proj(ret, moba, hgrn, proj, wr, wm, wh, wo, res, layer, *, tm=256):
    n = ret.shape[0]
    assert n % tm == 0 and D_MODEL % MERGE_GATE_TILE == 0
    assert all(off % MERGE_GATE_TILE == 0 for off in (OFF_GR, OFF_GM, OFF_GH))
    n_gate = D_MODEL // MERGE_GATE_TILE

    def branch_spec(width):
        return pl.BlockSpec((tm, width), lambda i: (i, 0))

    def gate_specs(off):
        return [pl.BlockSpec((tm, MERGE_GATE_TILE),
                             functools.partial(lambda i, col: (i, col), col=off // MERGE_GATE_TILE + c))
                for c in range(n_gate)]

    def weight_spec(rows):
        return pl.BlockSpec((None, rows, D_MODEL), lambda i: (layer, 0, 0),
                            pipeline_mode=pl.Buffered(1))

    row_spec = pl.BlockSpec((tm, D_MODEL), lambda i: (i, 0))
    return pl.pallas_call(
        _merge_out_kernel,
        grid=(n // tm,),
        in_specs=[branch_spec(RET_WIDTH), branch_spec(MOBA_WIDTH), branch_spec(HGRN_WIDTH),
                  *gate_specs(OFF_GR), *gate_specs(OFF_GM), *gate_specs(OFF_GH),
                  weight_spec(RET_WIDTH), weight_spec(MOBA_WIDTH), weight_spec(HGRN_WIDTH),
                  weight_spec(D_MODEL), row_spec],
        out_specs=row_spec,
        out_shape=jax.ShapeDtypeStruct((n, D_MODEL), F32),
        compiler_params=_compiler_params(1, 52),
        name="merge_out_proj",
    )(ret, moba, hgrn, *([proj] * (3 * n_gate)), wr, wm, wh, wo, res)


def _rmsnorm_kernel(x_ref, g_ref, o_ref):
    o_ref[...] = _rms_normalize(x_ref[...], g_ref[...])


def rmsnorm(x, g, *, tm=1024):
    n, d = x.shape
    assert n % tm == 0
    return pl.pallas_call(
        _rmsnorm_kernel,
        grid=(n // tm,),
        in_specs=[pl.BlockSpec((tm, d), lambda i: (i, 0)),
                  pl.BlockSpec((1, d), lambda i: (0, 0))],
        out_specs=pl.BlockSpec((tm, d), lambda i: (i, 0)),
        out_shape=jax.ShapeDtypeStruct((n, d), F32),
        compiler_params=_compiler_params(1, 40),
        name="final_rmsnorm",
    )(x, g.reshape(1, d))


def _retention_kernel(lg_ref, q_ref, k_ref, v_ref, g_ref, cos_ref, sin_ref, nrm_ref,
                      o_ref, state_ref, *, chunks):
    c_len = RET_CHUNK
    d = RET_HEAD_DIM
    half = d // 2

    @pl.when(pl.program_id(1) == 0)
    def _():
        state_ref[...] = jnp.zeros_like(state_ref)

    t_col = lax.broadcasted_iota(jnp.int32, (c_len, 1), 0).astype(F32)
    rel = (lax.broadcasted_iota(jnp.int32, (c_len, c_len), 0)
           - lax.broadcasted_iota(jnp.int32, (c_len, c_len), 1)).astype(F32)

    def rotate(x, cos, sin):
        x1 = x[:, :half]
        x2 = x[:, half:]
        return jnp.concatenate([x1 * cos - x2 * sin, x2 * cos + x1 * sin], axis=1)

    for h in range(RET_HEADS):
        cols = slice(h * d, (h + 1) * d)
        lg = lg_ref[h]
        decay = jnp.where(rel >= 0, jnp.exp(lg * jnp.maximum(rel, 0.0)), 0.0)
        q_weight = jnp.exp(lg * (t_col + 1.0))
        k_weight = jnp.exp(lg * (c_len - 1.0 - t_col))
        chunk_decay = jnp.exp(jnp.full((1, d), lg * c_len, F32))
        for c in range(chunks):
            rows = slice(c * c_len, (c + 1) * c_len)
            cos = cos_ref[rows, :]
            sin = sin_ref[rows, :]
            q = rotate(q_ref[rows, cols], cos, sin)
            k = rotate(k_ref[rows, cols], cos, sin) * (d ** -0.5)
            v = v_ref[rows, cols].astype(BF16)
            state = state_ref[h]

            scores = lax.dot_general(q.astype(BF16), k.astype(BF16), NT_DIMS,
                                     preferred_element_type=F32) * decay
            intra = jnp.dot(scores.astype(BF16), v, preferred_element_type=F32)
            cross = jnp.dot((q * q_weight).astype(BF16), state.astype(BF16),
                            preferred_element_type=F32)
            kv = lax.dot_general((k * k_weight).astype(BF16), v, TN_DIMS,
                                 preferred_element_type=F32)
            state_ref[h] = chunk_decay * state + kv

            out = intra + cross
            centered = out - jnp.mean(out, axis=-1, keepdims=True)
            normed = centered * lax.rsqrt(
                jnp.mean(centered * centered, axis=-1, keepdims=True) + NORM_EPS)
            gated = (normed * nrm_ref[:, cols]) * _silu(g_ref[rows, cols])
            o_ref[rows, cols] = gated.astype(BF16)


def retention_mixer(proj, ret_norm, batch, seq, *, rows_per_step=512):
    n = proj.shape[0]
    d = RET_HEAD_DIM
    assert seq % rows_per_step == 0 and rows_per_step % RET_CHUNK == 0
    steps = seq // rows_per_step
    log_gamma = jnp.log1p(-jnp.exp2(-5.0 - jnp.arange(RET_HEADS, dtype=F32)))
    half = d // 2
    inv_freq = RET_ROPE_BASE ** (-jnp.arange(half, dtype=F32) * 2.0 / d)
    ang = jnp.arange(seq).astype(F32)[:, None] * inv_freq[None, :]
    cos, sin = jnp.cos(ang), jnp.sin(ang)

    def col_spec(off):
        return pl.BlockSpec((rows_per_step, RET_WIDTH),
                            lambda b, s: (b * steps + s, off // RET_WIDTH))

    table_spec = pl.BlockSpec((rows_per_step, half), lambda b, s: (s, 0))
    return pl.pallas_call(
        functools.partial(_retention_kernel, chunks=rows_per_step // RET_CHUNK),
        grid=(batch, steps),
        in_specs=[pl.BlockSpec(memory_space=pltpu.SMEM),
                  col_spec(OFF_RQ), col_spec(OFF_RK), col_spec(OFF_RV), col_spec(OFF_RG),
                  table_spec, table_spec,
                  pl.BlockSpec((1, RET_WIDTH), lambda b, s: (0, 0))],
        out_specs=pl.BlockSpec((rows_per_step, RET_WIDTH), lambda b, s: (b * steps + s, 0)),
        out_shape=jax.ShapeDtypeStruct((n, RET_WIDTH), BF16),
        scratch_shapes=[pltpu.VMEM((RET_HEADS, d, d), F32)],
        compiler_params=_compiler_params(2, 40),
        name="retention",
    )(log_gamma, proj, proj, proj, proj, cos, sin, ret_norm.reshape(1, RET_WIDTH))


MOBA_PAIRS_PER_STEP = 2


def _moba_rotate(x, cos, sin_lo, sin_hi):
    half = ROPE_DIM // 2
    return (x * cos + pltpu.roll(x, MOBA_HEAD_DIM - half, 1) * sin_lo
            + pltpu.roll(x, half, 1) * sin_hi)


def _split_dot(a, b):
    a_hi = a.astype(BF16)
    a_lo = (a - a_hi.astype(F32)).astype(BF16)
    b_hi = b.astype(BF16)
    b_lo = (b - b_hi.astype(F32)).astype(BF16)
    return (jnp.dot(a_hi, b_hi, preferred_element_type=F32)
            + jnp.dot(a_hi, b_lo, preferred_element_type=F32)
            + jnp.dot(a_lo, b_hi, preferred_element_type=F32))


def _moba_kernel(q_ref, k_ref, v_ref, cos_ref, slo_ref, shi_ref, o_ref,
                 kaug_ref, vt_ref, kmean_ref, qt_ref, s_ref, acc_ref, snap_ref, *, n_blocks):
    blk = MOBA_BLOCK
    d = MOBA_HEAD_DIM
    step = pl.program_id(2)
    n_past = n_blocks - 1
    pairs = snap_ref.shape[0]
    gate_rows = kmean_ref.shape[0]
    groups = blk // SUBLANES
    q_scale = (d ** -0.5) * 1.4426950408889634

    def rotate(ref, start):
        rows = pl.ds(start, blk)
        return _moba_rotate(ref[rows, :], cos_ref[rows, :], slo_ref[rows, :], shi_ref[rows, :])

    @pl.when(step == 0)
    def _():
        lane = lax.broadcasted_iota(jnp.int32, (blk, LANES), 1)
        kmean_ref[...] = jnp.zeros_like(kmean_ref)
        for j in range(n_blocks):
            kr = rotate(k_ref, j * blk)
            kaug_ref[j, :, :d] = kr.astype(BF16)
            kaug_ref[j, :, d:] = jnp.where(lane == j, 1.0, 0.0).astype(BF16)
            vt_ref[j] = v_ref[j * blk:(j + 1) * blk, :].T.astype(BF16)
            kmean_ref[j:j + 1, :] = jnp.mean(kr, axis=0, keepdims=True)
        row = lax.broadcasted_iota(jnp.int32, (gate_rows, blk), 0)
        row_f = row.astype(F32)
        pad = jnp.zeros((LANES - gate_rows, blk), F32)
        for i in range(n_blocks):
            q_t = rotate(q_ref, i * blk).T
            gate = _split_dot(kmean_ref[...], q_t)
            g = jnp.where(row < i, gate, -jnp.inf)
            bias = jnp.full((gate_rows, blk), MASK_VALUE, F32)
            for _ in range(min(MOBA_TOPK, n_blocks)):
                best = jnp.max(g, axis=0, keepdims=True)
                hit = (g == best) & (g > -jnp.inf)
                first = jnp.min(jnp.where(hit, row_f, float(LANES)), axis=0, keepdims=True)
                pick = row_f == first
                bias = jnp.where(pick, 0.0, bias)
                g = jnp.where(pick, -jnp.inf, g)
            bias = jnp.where(row == i, 0.0, bias)
            qt_ref[i] = jnp.concatenate([q_t * q_scale, bias, pad], axis=0).astype(BF16)

    def group_reduce(x, op):
        return op(x.reshape(groups, SUBLANES, blk), axis=0)

    def score_own_block(i, own_slot):
        s_t = jnp.dot(kaug_ref[i], qt_ref[i], preferred_element_type=F32)
        key = lax.broadcasted_iota(jnp.int32, (blk, blk), 0)
        qry = lax.broadcasted_iota(jnp.int32, (blk, blk), 1)
        s_t = jnp.where(key <= qry, s_t, MASK_VALUE)
        s_ref[own_slot] = s_t
        return group_reduce(s_t, jnp.max)

    def tile_of(t, p):
        in_b = t >= p
        return in_b, jnp.where(in_b, t - p, t)

    def col_max(m8):
        return jnp.broadcast_to(jnp.max(m8, axis=0, keepdims=True), (SUBLANES, blk))

    def weights(slot, m8):
        p_t = jnp.exp2(s_ref[slot].reshape(groups, SUBLANES, blk) - m8[None])
        return jnp.sum(p_t, axis=0), p_t.reshape(blk, blk).astype(BF16)

    def values(j, p_bf):
        return jnp.dot(vt_ref[j], p_bf, preferred_element_type=F32)

    never = jnp.full((SUBLANES, blk), -jnp.inf, F32)
    scored = []
    for k in range(pairs):
        p = step * pairs + k
        base = k * (n_past + 2)
        i_a, i_b = p, n_past - p
        m_a = score_own_block(i_a, base + n_past)
        m_b = score_own_block(i_b, base + n_past + 1)
        for t in range(n_past):
            in_b, j = tile_of(t, p)
            s_t = jnp.dot(kaug_ref[j], qt_ref[jnp.where(in_b, i_b, i_a)],
                          preferred_element_type=F32)
            s_ref[base + t] = s_t
            gm = group_reduce(s_t, jnp.max)
            m_a = jnp.maximum(m_a, jnp.where(in_b, never, gm))
            m_b = jnp.maximum(m_b, jnp.where(in_b, gm, never))
        scored.append((p, base, i_a, i_b, col_max(m_a), col_max(m_b)))

    for k, (p, base, i_a, i_b, m_a, m_b) in enumerate(scored):
        l_a, p_own = weights(base + n_past, m_a)
        acc_ref[2 * k] = values(i_a, p_own)
        l_b, p_own = weights(base + n_past + 1, m_b)
        acc_ref[2 * k + 1] = values(i_b, p_own)
        acc = jnp.zeros((d, blk), F32)
        for t in range(n_past):
            in_b, j = tile_of(t, p)
            l_part, p_bf = weights(base + t, jnp.where(in_b, m_b, m_a))
            acc = acc * jnp.where(t == p, 0.0, 1.0) + values(j, p_bf)
            l_a = l_a + jnp.where(in_b, 0.0, l_part)
            l_b = l_b + jnp.where(in_b, l_part, 0.0)
            if t < snap_ref.shape[1]:
                snap_ref[k, t] = acc
        past_a = snap_ref[k, jnp.maximum(p - 1, 0)] * jnp.where(p > 0, 1.0, 0.0)
        for w, i, l8, past in ((2 * k, i_a, l_a, past_a), (2 * k + 1, i_b, l_b, acc)):
            l = jnp.sum(l8, axis=0, keepdims=True)
            out_t = (acc_ref[w] + past) / l
            o_ref[pl.ds(pl.multiple_of(i * blk, blk), blk), :] = out_t.T.astype(BF16)


def moba_mixer(proj, batch, seq):
    n = proj.shape[0]
    d = MOBA_HEAD_DIM
    blk = MOBA_BLOCK
    assert seq % blk == 0
    n_blocks = seq // blk
    assert n_blocks <= LANES and n_blocks % 2 == 0
    half = ROPE_DIM // 2
    inv_freq = ROPE_THETA ** (-jnp.arange(half, dtype=F32) * 2.0 / ROPE_DIM)
    ang = jnp.arange(seq).astype(F32)[:, None] * inv_freq[None, :]
    cos, sin = jnp.cos(ang), jnp.sin(ang)
    zeros = jnp.zeros((seq, d - ROPE_DIM), F32)
    zero_half = jnp.zeros((seq, half), F32)
    cos_t = jnp.concatenate([cos, cos, jnp.ones_like(zeros)], axis=1)
    sin_lo = jnp.concatenate([-sin, zero_half, zeros], axis=1)
    sin_hi = jnp.concatenate([zero_half, sin, zeros], axis=1)

    def col_spec(off):
        return pl.BlockSpec((seq, d), lambda b, h, p: (b, off // d + h))

    table_spec = pl.BlockSpec((seq, d), lambda b, h, p: (0, 0))
    gate_rows = -(-n_blocks // SUBLANES) * SUBLANES
    pairs = MOBA_PAIRS_PER_STEP
    assert (n_blocks // 2) % pairs == 0
    n_snap = max(n_blocks // 2 - 1, 1)
    return pl.pallas_call(
        functools.partial(_moba_kernel, n_blocks=n_blocks),
        grid=(batch, MOBA_HEADS, n_blocks // 2 // pairs),
        in_specs=[col_spec(OFF_MQ), col_spec(OFF_MK), col_spec(OFF_MV),
                  table_spec, table_spec, table_spec],
        out_specs=pl.BlockSpec((seq, d), lambda b, h, p: (b, h)),
        out_shape=jax.ShapeDtypeStruct((n, MOBA_WIDTH), BF16),
        scratch_shapes=[pltpu.VMEM((n_blocks, blk, 2 * d), BF16),
                        pltpu.VMEM((n_blocks, d, blk), BF16),
                        pltpu.VMEM((gate_rows, d), F32),
                        pltpu.VMEM((n_blocks, 2 * d, blk), BF16),
                        pltpu.VMEM((pairs * (n_blocks + 1), blk, blk), F32),
                        pltpu.VMEM((2 * pairs, d, blk), F32),
                        pltpu.VMEM((pairs, n_snap, d, blk), F32)],
        compiler_params=_compiler_params(3, 52),
        name="moba",
    )(proj, proj, proj, cos_t, sin_lo, sin_hi)


def _hgrn_pair_codes(block):
    i = np.arange(block)[:, None]
    j = np.arange(block)[None, :]
    top_bit = np.floor(np.log2(np.maximum(i ^ j, 1))).astype(np.int32)
    n_levels = block.bit_length() - 1
    return np.where(i > j, top_bit, np.where(i == j, n_levels, -1)).astype(np.int32)


def _hgrn_kernel(lbp_ref, tril_ref, code_ref, q_ref, f_ref, v_ref, g_ref, nrm_ref, o_ref,
                 state_ref, b_ref, k_ref, *, layer, rows_per_step):
    c_len = HGRN_BLOCK
    d = HGRN_HEAD_DIM

    @pl.when(pl.program_id(2) == 0)
    def _():
        state_ref[...] = jnp.zeros_like(state_ref)

    params = lbp_ref[...]
    e = jnp.exp(params - jnp.max(params, axis=0, keepdims=True))
    soft = e / jnp.sum(e, axis=0, keepdims=True)
    lower = jnp.zeros((1, d), F32)
    for r in range(1, layer + 1):
        lower = lower + soft[r:r + 1, :]

    tril = tril_ref[...]
    for seg0 in range(0, rows_per_step, c_len):
        seg = slice(seg0, seg0 + c_len)
        forget = lower + (1.0 - lower) * jax.nn.sigmoid(f_ref[seg, :])
        k_ref[seg, :] = 1.0 - forget
        log_f = jnp.log2(forget)
        hi = log_f.astype(BF16)
        rest = log_f - hi.astype(F32)
        mid = rest.astype(BF16)
        lo = (rest - mid.astype(F32)).astype(BF16)
        b_ref[seg, :] = (jnp.dot(tril, hi, preferred_element_type=F32)
                         + jnp.dot(tril, mid, preferred_element_type=F32)
                         + jnp.dot(tril, lo, preferred_element_type=F32))

    assert c_len & (c_len - 1) == 0 and c_len % LANES == 0
    levels = [1 << s for s in range(c_len.bit_length() - 1)]
    n_grp = c_len // SUBLANES
    row8 = lax.broadcasted_iota(jnp.int32, (SUBLANES, d), 0)
    zero8 = jnp.zeros((SUBLANES, d), F32)
    codes = _hgrn_pair_codes(c_len)
    tiles = [(g, cb) for g in range(n_grp) for cb in range(c_len // LANES)]

    def code_tile(g, cb):
        return codes[SUBLANES * g:SUBLANES * (g + 1), LANES * cb:LANES * (cb + 1)]

    level_tiles = [[(g, cb, bool(np.all(code_tile(g, cb) == bit))) for g, cb in tiles
                    if np.any(code_tile(g, cb) == bit)] for bit in range(len(levels) + 1)]

    def bcast_row(ref, r):
        return jnp.broadcast_to(ref[r:r + 1, :], (SUBLANES, d))

    def level_operands(r0, h, bg, qg, kg):
        q_parts, k_parts = [], []
        for g in range(n_grp):
            base = SUBLANES * g
            if h >= SUBLANES:
                b_a = bcast_row(b_ref, r0 + base // (2 * h) * (2 * h) + h - 1)
                if base & h:
                    q_parts.append(qg[g] * jnp.exp2(bg[g] - b_a))
                    k_parts.append(zero8)
                else:
                    q_parts.append(zero8)
                    k_parts.append(kg[g] * jnp.exp2(b_a - bg[g]))
                continue
            upper = (row8 & h) != 0
            if h == 1:
                q_parts.append(jnp.where(upper, qg[g] * (1.0 - kg[g]), 0.0))
                k_parts.append(jnp.where(upper, 0.0, kg[g]))
                continue
            b_a = bcast_row(b_ref, r0 + base + h - 1)
            for s in range(2 * h, SUBLANES, 2 * h):
                b_a = jnp.where(row8 >= s, bcast_row(b_ref, r0 + base + s + h - 1), b_a)
            decay = jnp.exp2(-jnp.abs(bg[g] - b_a))
            q_parts.append(jnp.where(upper, qg[g] * decay, 0.0))
            k_parts.append(jnp.where(upper, 0.0, kg[g] * decay))
        return (jnp.concatenate(q_parts, axis=0).astype(BF16),
                jnp.concatenate(k_parts, axis=0).astype(BF16))

    for c in range(rows_per_step // c_len):
        r0 = c * c_len
        rows = slice(r0, r0 + c_len)
        b = b_ref[rows, :]
        q = _silu(q_ref[rows, :])
        k = k_ref[rows, :]
        v = v_ref[rows, :]

        groups = [slice(SUBLANES * g, SUBLANES * (g + 1)) for g in range(n_grp)]
        bg = [b[g, :] for g in groups]
        qg = [q[g, :] for g in groups]
        kg = [k[g, :] for g in groups]
        def code_of(g, cb):
            return code_ref[SUBLANES * g:SUBLANES * (g + 1), LANES * cb:LANES * (cb + 1)]

        a_tiles = {}
        self_weight = jnp.sum(q * k, axis=-1, keepdims=True)
        for g, cb, _ in level_tiles[len(levels)]:
            a_tiles[g, cb] = jnp.where(code_of(g, cb) == len(levels),
                                       self_weight[groups[g], :], 0.0)
        for bit, h in enumerate(levels):
            q_h, k_h = level_operands(r0, h, bg, qg, kg)
            pair = lax.dot_general(q_h, k_h, NT_DIMS, preferred_element_type=F32)
            for g, cb, owns_tile in level_tiles[bit]:
                piece = pair[groups[g], LANES * cb:LANES * (cb + 1)]
                if owns_tile:
                    a_tiles[g, cb] = piece
                else:
                    a_tiles[g, cb] = jnp.where(code_of(g, cb) == bit, piece,
                                               a_tiles.get((g, cb), 0.0))
        zero_tile = jnp.zeros((SUBLANES, LANES), F32)
        a_mat = jnp.concatenate(
            [jnp.concatenate([a_tiles.get((g, cb), zero_tile) for g in range(n_grp)], axis=0)
             for cb in range(c_len // LANES)], axis=1)
        intra = jnp.dot(a_mat.astype(BF16), v.astype(BF16), preferred_element_type=F32)

        state_t = state_ref[...]
        cross = lax.dot_general((q * jnp.exp2(b)).astype(BF16), state_t.astype(BF16), NT_DIMS,
                                preferred_element_type=F32)
        b_last = b_ref[r0 + c_len - 1:r0 + c_len, :]
        k_dec = k * jnp.exp2(b_last - b)
        state_ref[...] = state_t * jnp.exp2(b_last) + lax.dot_general(
            v.astype(BF16), k_dec.astype(BF16), TN_DIMS, preferred_element_type=F32)

        out = intra + cross
        normed = out * lax.rsqrt(jnp.mean(out * out, axis=-1, keepdims=True) + NORM_EPS)
        gated = (normed * nrm_ref[...]) * _silu(g_ref[rows, :])
        o_ref[rows, :] = gated.astype(BF16)


def hgrn_mixer(proj, lower_bound_params, hgrn_norm, layer, batch, seq, *, rows_per_step=1024):
    n = proj.shape[0]
    d = HGRN_HEAD_DIM
    depth = lower_bound_params.shape[0]
    blk = HGRN_BLOCK
    assert seq % rows_per_step == 0 and rows_per_step % blk == 0
    steps = seq // rows_per_step
    codes = _hgrn_pair_codes(blk)
    tril = jnp.asarray(codes >= 0, BF16)

    def col_spec(off):
        return pl.BlockSpec((rows_per_step, d), lambda b, h, s: (b * steps + s, off // d + h))

    return pl.pallas_call(
        functools.partial(_hgrn_kernel, layer=layer, rows_per_step=rows_per_step),
        grid=(batch, HGRN_HEADS, steps),
        in_specs=[pl.BlockSpec((depth, d), lambda b, h, s: (0, h)),
                  pl.BlockSpec((blk, blk), lambda b, h, s: (0, 0)),
                  pl.BlockSpec((blk, blk), lambda b, h, s: (0, 0)),
                  col_spec(OFF_HQ), col_spec(OFF_HF), col_spec(OFF_HI), col_spec(OFF_HG),
                  pl.BlockSpec((1, d), lambda b, h, s: (0, h))],
        out_specs=pl.BlockSpec((rows_per_step, d), lambda b, h, s: (b * steps + s, h)),
        out_shape=jax.ShapeDtypeStruct((n, HGRN_WIDTH), BF16),
        scratch_shapes=[pltpu.VMEM((d, d), F32),
                        pltpu.VMEM((rows_per_step, d), F32),
                        pltpu.VMEM((rows_per_step, d), F32)],
        compiler_params=_compiler_params(3, 32),
        name="hgrn2",
    )(lower_bound_params.astype(F32), tril, jnp.asarray(codes), proj, proj, proj, proj,
      hgrn_norm.reshape(1, HGRN_WIDTH))


def kernel(x, hgrn_lower_bounds, norm_mix, w_in, ret_norm, hgrn_norm, w_branch_ret, w_branch_moba,
           w_branch_hgrn, w_out, norm_ffn, w_ffn_gate, w_ffn_up, w_ffn_down, final_norm):
    batch, seq, d_model = x.shape
    assert d_model == D_MODEL
    depth = w_in.shape[0]
    (w_in, w_branch_ret, w_branch_moba, w_branch_hgrn, w_out, w_ffn_gate, w_ffn_up,
     w_ffn_down) = (w.astype(BF16) for w in (w_in, w_branch_ret, w_branch_moba, w_branch_hgrn,
                                             w_out, w_ffn_gate, w_ffn_up, w_ffn_down))
    h = x.reshape(batch * seq, d_model)
    for layer in range(depth):
        proj = norm_matmul(h, norm_mix[layer], w_in, layer)
        ret = retention_mixer(proj, ret_norm[layer], batch, seq)
        moba = moba_mixer(proj, batch, seq)
        hgrn = hgrn_mixer(proj, hgrn_lower_bounds, hgrn_norm[layer], layer, batch, seq)
        h = merge_out_proj(ret, moba, hgrn, proj, w_branch_ret, w_branch_moba, w_branch_hgrn,
                           w_out, h, layer)
        act = ffn_up(h, norm_ffn[layer], w_ffn_gate, w_ffn_up, layer)
        h = matmul_residual(act, w_ffn_down, h, layer, tn=512, name="ffn_down")
    return rmsnorm(h, final_norm).reshape(batch, seq, d_model)
```

```python
import functools

import jax
import jax.numpy as jnp
import numpy as np
from jax import lax
from jax.experimental import pallas as pl
from jax.experimental.pallas import tpu as pltpu

F32 = jnp.float32
BF16 = jnp.bfloat16

D_MODEL = 2048
RET_HEADS = 4
RET_HEAD_DIM = 256
RET_WIDTH = RET_HEADS * RET_HEAD_DIM
RET_CHUNK = 128
RET_ROPE_BASE = 10000.0
MOBA_HEADS = 8
MOBA_HEAD_DIM = 128
MOBA_WIDTH = MOBA_HEADS * MOBA_HEAD_DIM
MOBA_BLOCK = 256
MOBA_TOPK = 3
ROPE_THETA = 500000.0
ROPE_DIM = MOBA_HEAD_DIM // 4
HGRN_HEADS = 8
HGRN_HEAD_DIM = 128
HGRN_WIDTH = HGRN_HEADS * HGRN_HEAD_DIM
HGRN_BLOCK = 256
NORM_EPS = 1e-6
IN_SIZES = (RET_WIDTH,) * 4 + (MOBA_WIDTH,) * 3 + (HGRN_WIDTH,) * 4 + (D_MODEL,) * 3
IN_COLS = sum(IN_SIZES)
IN_OFFS = tuple(sum(IN_SIZES[:i]) for i in range(len(IN_SIZES)))
(OFF_RQ, OFF_RK, OFF_RV, OFF_RG, OFF_MQ, OFF_MK, OFF_MV,
 OFF_HQ, OFF_HF, OFF_HI, OFF_HG, OFF_GR, OFF_GM, OFF_GH) = IN_OFFS

V7X_VMEM_BYTES = 64 * 1024 * 1024
LANES = 128
SUBLANES = 8
MASK_VALUE = -1e30

NT_DIMS = (((1,), (1,)), ((), ()))
TN_DIMS = (((0,), (0,)), ((), ()))


def _compiler_params(n_grid_dims, vmem_mib):
    assert vmem_mib * 1024 * 1024 < V7X_VMEM_BYTES
    return pltpu.CompilerParams(
        dimension_semantics=("arbitrary",) * n_grid_dims,
        vmem_limit_bytes=vmem_mib * 1024 * 1024)


def _silu(x):
    return x * jax.nn.sigmoid(x)


def _layer_weight_spec(layer, rows, tn, col_of):
    return pl.BlockSpec((None, rows, tn), lambda *g: (layer, 0, col_of(*g)))


def _rms_normalize(x, g):
    r = lax.rsqrt(jnp.mean(x * x, axis=-1, keepdims=True) + NORM_EPS)
    return x * r * g


def _norm_matmul_kernel(x_ref, g_ref, w_ref, o_ref, h_ref):
    @pl.when(pl.program_id(1) == 0)
    def _():
        h_ref[...] = _rms_normalize(x_ref[...], g_ref[...]).astype(BF16)

    o_ref[...] = jnp.dot(h_ref[...], w_ref[...], preferred_element_type=F32)


def norm_matmul(x, g, w, layer, *, tm=1024, tn=1024):
    n, d = x.shape
    cols = w.shape[2]
    assert n % tm == 0 and cols % tn == 0
    return pl.pallas_call(
        _norm_matmul_kernel,
        grid=(n // tm, cols // tn),
        in_specs=[pl.BlockSpec((tm, d), lambda i, j: (i, 0)),
                  pl.BlockSpec((1, d), lambda i, j: (0, 0)),
                  _layer_weight_spec(layer, d, tn, lambda i, j: j)],
        out_specs=pl.BlockSpec((tm, tn), lambda i, j: (i, j)),
        out_shape=jax.ShapeDtypeStruct((n, cols), F32),
        scratch_shapes=[pltpu.VMEM((tm, d), BF16)],
        compiler_params=_compiler_params(2, 48),
        name="norm_inproj",
    )(x, g.reshape(1, d), w)


def _ffn_up_kernel(x_ref, g_ref, wg_ref, wu_ref, o_ref, h_ref):
    @pl.when(pl.program_id(1) == 0)
    def _():
        h_ref[...] = _rms_normalize(x_ref[...], g_ref[...]).astype(BF16)

    h = h_ref[...]
    gate = jnp.dot(h, wg_ref[...], preferred_element_type=F32)
    up = jnp.dot(h, wu_ref[...], preferred_element_type=F32)
    o_ref[...] = (_silu(gate) * up).astype(BF16)


def ffn_up(x, g, wg, wu, layer, *, tm=1024, tn=512):
    n, d = x.shape
    hidden = wg.shape[2]
    assert n % tm == 0 and hidden % tn == 0
    w_spec = _layer_weight_spec(layer, d, tn, lambda i, j: j)
    return pl.pallas_call(
        _ffn_up_kernel,
        grid=(n // tm, hidden // tn),
        in_specs=[pl.BlockSpec((tm, d), lambda i, j: (i, 0)),
                  pl.BlockSpec((1, d), lambda i, j: (0, 0)),
                  w_spec, w_spec],
        out_specs=pl.BlockSpec((tm, tn), lambda i, j: (i, j)),
        out_shape=jax.ShapeDtypeStruct((n, hidden), BF16),
        scratch_shapes=[pltpu.VMEM((tm, d), BF16)],
        compiler_params=_compiler_params(2, 48),
        name="ffn_up",
    )(x, g.reshape(1, d), wg, wu)


def _ffn_down_kernel(a_ref, w_ref, r_ref, g_ref, o_ref, *, normalize):
    out = r_ref[...] + jnp.dot(a_ref[...], w_ref[...], preferred_element_type=F32)
    o_ref[...] = _rms_normalize(out, g_ref[...]) if normalize else out


def ffn_down(a, w, res, layer, gain, *, normalize, tm=512):
    n, k = a.shape
    cols = w.shape[2]
    assert n % tm == 0
    row_spec = pl.BlockSpec((tm, cols), lambda i: (i, 0))
    return pl.pallas_call(
        functools.partial(_ffn_down_kernel, normalize=normalize),
        grid=(n // tm,),
        in_specs=[pl.BlockSpec((tm, k), lambda i: (i, 0)),
                  pl.BlockSpec((None, k, cols), lambda i: (layer, 0, 0),
                               pipeline_mode=pl.Buffered(1)),
                  row_spec,
                  pl.BlockSpec((1, cols), lambda i: (0, 0))],
        out_specs=row_spec,
        out_shape=jax.ShapeDtypeStruct((n, cols), F32),
        compiler_params=_compiler_params(1, 54),
        name="ffn_down",
    )(a, w, res, gain.reshape(1, cols))


MERGE_GATE_TILE = 1024


def _merge_out_kernel(*refs):
    n_gate = D_MODEL // MERGE_GATE_TILE
    branch_refs = refs[:3]
    gate_refs = refs[3:3 + 3 * n_gate]
    wb_refs = refs[3 + 3 * n_gate:6 + 3 * n_gate]
    wo_ref, res_ref, o_ref = refs[6 + 3 * n_gate:]
    parts = []
    for c in range(n_gate):
        cols = slice(c * MERGE_GATE_TILE, (c + 1) * MERGE_GATE_TILE)
        mixed = None
        for b in range(3):
            term = jax.nn.sigmoid(gate_refs[b * n_gate + c][...]) * jnp.dot(
                branch_refs[b][...], wb_refs[b][:, cols], preferred_element_type=F32)
            mixed = term if mixed is None else mixed + term
        parts.append(mixed.astype(BF16))
    mixed = jnp.concatenate(parts, axis=1)
    o_ref[...] = res_ref[...] + jnp.dot(mixed, wo_ref[...], preferred_element_type=F32)


def merge_out_proj(ret, moba, hgrn, proj, wr, wm, wh, wo, res, layer, *, tm=256):
    n = ret.shape[0]
    assert n % tm == 0 and D_MODEL % MERGE_GATE_TILE == 0
    assert all(off % MERGE_GATE_TILE == 0 for off in (OFF_GR, OFF_GM, OFF_GH))
    n_gate = D_MODEL // MERGE_GATE_TILE

    def branch_spec(width):
        return pl.BlockSpec((tm, width), lambda i: (i, 0))

    def gate_specs(off):
        return [pl.BlockSpec((tm, MERGE_GATE_TILE),
                             functools.partial(lambda i, col: (i, col), col=off // MERGE_GATE_TILE + c))
                for c in range(n_gate)]

    def weight_spec(rows):
        return pl.BlockSpec((None, rows, D_MODEL), lambda i: (layer, 0, 0),
                            pipeline_mode=pl.Buffered(1))

    row_spec = pl.BlockSpec((tm, D_MODEL), lambda i: (i, 0))
    return pl.pallas_call(
        _merge_out_kernel,
        grid=(n // tm,),
        in_specs=[branch_spec(RET_WIDTH), branch_spec(MOBA_WIDTH), branch_spec(HGRN_WIDTH),
                  *gate_specs(OFF_GR), *gate_specs(OFF_GM), *gate_specs(OFF_GH),
                  weight_spec(RET_WIDTH), weight_spec(MOBA_WIDTH), weight_spec(HGRN_WIDTH),
                  weight_spec(D_MODEL), row_spec],
        out_specs=row_spec,
        out_shape=jax.ShapeDtypeStruct((n, D_MODEL), F32),
        compiler_params=_compiler_params(1, 52),
        name="merge_out_proj",
    )(ret, moba, hgrn, *([proj] * (3 * n_gate)), wr, wm, wh, wo, res)


def _retention_kernel(lg_ref, q_ref, k_ref, v_ref, g_ref, cos_ref, sin_ref, nrm_ref,
                      o_ref, state_ref, *, chunks):
    c_len = RET_CHUNK
    d = RET_HEAD_DIM
    half = d // 2

    @pl.when(pl.program_id(1) == 0)
    def _():
        state_ref[...] = jnp.zeros_like(state_ref)

    t_col = lax.broadcasted_iota(jnp.int32, (c_len, 1), 0).astype(F32)
    rel = (lax.broadcasted_iota(jnp.int32, (c_len, c_len), 0)
           - lax.broadcasted_iota(jnp.int32, (c_len, c_len), 1)).astype(F32)

    def rotate(x, cos, sin):
        x1 = x[:, :half]
        x2 = x[:, half:]
        return jnp.concatenate([x1 * cos - x2 * sin, x2 * cos + x1 * sin], axis=1)

    for h in range(RET_HEADS):
        cols = slice(h * d, (h + 1) * d)
        lg = lg_ref[h]
        decay = jnp.where(rel >= 0, jnp.exp(lg * jnp.maximum(rel, 0.0)), 0.0)
        q_weight = jnp.exp(lg * (t_col + 1.0))
        k_weight = jnp.exp(lg * (c_len - 1.0 - t_col))
        chunk_decay = jnp.exp(jnp.full((1, d), lg * c_len, F32))
        for c in range(chunks):
            rows = slice(c * c_len, (c + 1) * c_len)
            cos = cos_ref[rows, :]
            sin = sin_ref[rows, :]
            q = rotate(q_ref[rows, cols], cos, sin)
            k = rotate(k_ref[rows, cols], cos, sin) * (d ** -0.5)
            v = v_ref[rows, cols].astype(BF16)
            state = state_ref[h]

            scores = lax.dot_general(q.astype(BF16), k.astype(BF16), NT_DIMS,
                                     preferred_element_type=F32) * decay
            intra = jnp.dot(scores.astype(BF16), v, preferred_element_type=F32)
            cross = jnp.dot((q * q_weight).astype(BF16), state.astype(BF16),
                            preferred_element_type=F32)
            kv = lax.dot_general((k * k_weight).astype(BF16), v, TN_DIMS,
                                 preferred_element_type=F32)
            state_ref[h] = chunk_decay * state + kv

            out = intra + cross
            centered = out - jnp.mean(out, axis=-1, keepdims=True)
            normed = centered * lax.rsqrt(
                jnp.mean(centered * centered, axis=-1, keepdims=True) + NORM_EPS)
            gated = (normed * nrm_ref[:, cols]) * _silu(g_ref[rows, cols])
            o_ref[rows, cols] = gated.astype(BF16)


def retention_mixer(proj, ret_norm, batch, seq, *, rows_per_step=512):
    n = proj.shape[0]
    d = RET_HEAD_DIM
    assert seq % rows_per_step == 0 and rows_per_step % RET_CHUNK == 0
    steps = seq // rows_per_step
    log_gamma = jnp.log1p(-jnp.exp2(-5.0 - jnp.arange(RET_HEADS, dtype=F32)))
    half = d // 2
    inv_freq = RET_ROPE_BASE ** (-jnp.arange(half, dtype=F32) * 2.0 / d)
    ang = jnp.arange(seq).astype(F32)[:, None] * inv_freq[None, :]
    cos, sin = jnp.cos(ang), jnp.sin(ang)

    def col_spec(off):
        return pl.BlockSpec((rows_per_step, RET_WIDTH),
                            lambda b, s: (b * steps + s, off // RET_WIDTH))

    table_spec = pl.BlockSpec((rows_per_step, half), lambda b, s: (s, 0))
    return pl.pallas_call(
        functools.partial(_retention_kernel, chunks=rows_per_step // RET_CHUNK),
        grid=(batch, steps),
        in_specs=[pl.BlockSpec(memory_space=pltpu.SMEM),
                  col_spec(OFF_RQ), col_spec(OFF_RK), col_spec(OFF_RV), col_spec(OFF_RG),
                  table_spec, table_spec,
                  pl.BlockSpec((1, RET_WIDTH), lambda b, s: (0, 0))],
        out_specs=pl.BlockSpec((rows_per_step, RET_WIDTH), lambda b, s: (b * steps + s, 0)),
        out_shape=jax.ShapeDtypeStruct((n, RET_WIDTH), BF16),
        scratch_shapes=[pltpu.VMEM((RET_HEADS, d, d), F32)],
        compiler_params=_compiler_params(2, 40),
        name="retention",
    )(log_gamma, proj, proj, proj, proj, cos, sin, ret_norm.reshape(1, RET_WIDTH))


MOBA_PAIRS_PER_STEP = 2


def _moba_rotate(x, cos, sin_lo, sin_hi):
    half = ROPE_DIM // 2
    return (x * cos + pltpu.roll(x, MOBA_HEAD_DIM - half, 1) * sin_lo
            + pltpu.roll(x, half, 1) * sin_hi)


def _split_dot(a, b):
    a_hi = a.astype(BF16)
    a_lo = (a - a_hi.astype(F32)).astype(BF16)
    b_hi = b.astype(BF16)
    b_lo = (b - b_hi.astype(F32)).astype(BF16)
    return (jnp.dot(a_hi, b_hi, preferred_element_type=F32)
            + jnp.dot(a_hi, b_lo, preferred_element_type=F32)
            + jnp.dot(a_lo, b_hi, preferred_element_type=F32))


def _moba_kernel(q_ref, k_ref, v_ref, cos_ref, slo_ref, shi_ref, o_ref,
                 kaug_ref, vt_ref, kmean_ref, qt_ref, s_ref, acc_ref, snap_ref, *, n_blocks):
    blk = MOBA_BLOCK
    d = MOBA_HEAD_DIM
    step = pl.program_id(2)
    n_past = n_blocks - 1
    pairs = snap_ref.shape[0]
    gate_rows = kmean_ref.shape[0]
    groups = blk // SUBLANES
    q_scale = (d ** -0.5) * 1.4426950408889634

    def rotate(ref, start):
        rows = pl.ds(start, blk)
        return _moba_rotate(ref[rows, :], cos_ref[rows, :], slo_ref[rows, :], shi_ref[rows, :])

    @pl.when(step == 0)
    def _():
        lane = lax.broadcasted_iota(jnp.int32, (blk, LANES), 1)
        kmean_ref[...] = jnp.zeros_like(kmean_ref)
        for j in range(n_blocks):
            kr = rotate(k_ref, j * blk)
            kaug_ref[j, :, :d] = kr.astype(BF16)
            kaug_ref[j, :, d:] = jnp.where(lane == j, 1.0, 0.0).astype(BF16)
            vt_ref[j] = v_ref[j * blk:(j + 1) * blk, :].T.astype(BF16)
            kmean_ref[j:j + 1, :] = jnp.mean(kr, axis=0, keepdims=True)
        row = lax.broadcasted_iota(jnp.int32, (gate_rows, blk), 0)
        row_f = row.astype(F32)
        pad = jnp.zeros((LANES - gate_rows, blk), F32)
        for i in range(n_blocks):
            q_t = rotate(q_ref, i * blk).T
            gate = _split_dot(kmean_ref[...], q_t)
            g = jnp.where(row < i, gate, -jnp.inf)
            bias = jnp.full((gate_rows, blk), MASK_VALUE, F32)
            for _ in range(min(MOBA_TOPK, n_blocks)):
                best = jnp.max(g, axis=0, keepdims=True)
                hit = (g == best) & (g > -jnp.inf)
                first = jnp.min(jnp.where(hit, row_f, float(LANES)), axis=0, keepdims=True)
                pick = row_f == first
                bias = jnp.where(pick, 0.0, bias)
                g = jnp.where(pick, -jnp.inf, g)
            bias = jnp.where(row == i, 0.0, bias)
            qt_ref[i] = jnp.concatenate([q_t * q_scale, bias, pad], axis=0).astype(BF16)

    def group_reduce(x, op):
        return op(x.reshape(groups, SUBLANES, blk), axis=0)

    def score_own_block(i, own_slot):
        s_t = jnp.dot(kaug_ref[i], qt_ref[i], preferred_element_type=F32)
        key = lax.broadcasted_iota(jnp.int32, (blk, blk), 0)
        qry = lax.broadcasted_iota(jnp.int32, (blk, blk), 1)
        s_t = jnp.where(key <= qry, s_t, MASK_VALUE)
        s_ref[own_slot] = s_t
        return group_reduce(s_t, jnp.max)

    def tile_of(t, p):
        in_b = t >= p
        return in_b, jnp.where(in_b, t - p, t)

    def col_max(m8):
        return jnp.broadcast_to(jnp.max(m8, axis=0, keepdims=True), (SUBLANES, blk))

    def weights(slot, m8):
        p_t = jnp.exp2(s_ref[slot].reshape(groups, SUBLANES, blk) - m8[None])
        return jnp.sum(p_t, axis=0), p_t.reshape(blk, blk).astype(BF16)

    def values(j, p_bf):
        return jnp.dot(vt_ref[j], p_bf, preferred_element_type=F32)

    never = jnp.full((SUBLANES, blk), -jnp.inf, F32)
    scored = []
    for k in range(pairs):
        p = step * pairs + k
        base = k * (n_past + 2)
        i_a, i_b = p, n_past - p
        m_a = score_own_block(i_a, base + n_past)
        m_b = score_own_block(i_b, base + n_past + 1)
        for t in range(n_past):
            in_b, j = tile_of(t, p)
            s_t = jnp.dot(kaug_ref[j], qt_ref[jnp.where(in_b, i_b, i_a)],
                          preferred_element_type=F32)
            s_ref[base + t] = s_t
            gm = group_reduce(s_t, jnp.max)
            m_a = jnp.maximum(m_a, jnp.where(in_b, never, gm))
            m_b = jnp.maximum(m_b, jnp.where(in_b, gm, never))
        scored.append((p, base, i_a, i_b, col_max(m_a), col_max(m_b)))

    for k, (p, base, i_a, i_b, m_a, m_b) in enumerate(scored):
        l_a, p_own = weights(base + n_past, m_a)
        acc_ref[2 * k] = values(i_a, p_own)
        l_b, p_own = weights(base + n_past + 1, m_b)
        acc_ref[2 * k + 1] = values(i_b, p_own)
        acc = jnp.zeros((d, blk), F32)
        for t in range(n_past):
            in_b, j = tile_of(t, p)
            l_part, p_bf = weights(base + t, jnp.where(in_b, m_b, m_a))
            acc = acc * jnp.where(t == p, 0.0, 1.0) + values(j, p_bf)
            l_a = l_a + jnp.where(in_b, 0.0, l_part)
            l_b = l_b + jnp.where(in_b, l_part, 0.0)
            if t < snap_ref.shape[1]:
                snap_ref[k, t] = acc
        past_a = snap_ref[k, jnp.maximum(p - 1, 0)] * jnp.where(p > 0, 1.0, 0.0)
        for w, i, l8, past in ((2 * k, i_a, l_a, past_a), (2 * k + 1, i_b, l_b, acc)):
            l = jnp.sum(l8, axis=0, keepdims=True)
            out_t = (acc_ref[w] + past) / l
            o_ref[pl.ds(pl.multiple_of(i * blk, blk), blk), :] = out_t.T.astype(BF16)


def moba_mixer(proj, batch, seq):
    n = proj.shape[0]
    d = MOBA_HEAD_DIM
    blk = MOBA_BLOCK
    assert seq % blk == 0
    n_blocks = seq // blk
    assert n_blocks <= LANES and n_blocks % 2 == 0
    half = ROPE_DIM // 2
    inv_freq = ROPE_THETA ** (-jnp.arange(half, dtype=F32) * 2.0 / ROPE_DIM)
    ang = jnp.arange(seq).astype(F32)[:, None] * inv_freq[None, :]
    cos, sin = jnp.cos(ang), jnp.sin(ang)
    zeros = jnp.zeros((seq, d - ROPE_DIM), F32)
    zero_half = jnp.zeros((seq, half), F32)
    cos_t = jnp.concatenate([cos, cos, jnp.ones_like(zeros)], axis=1)
    sin_lo = jnp.concatenate([-sin, zero_half, zeros], axis=1)
    sin_hi = jnp.concatenate([zero_half, sin, zeros], axis=1)

    def col_spec(off):
        return pl.BlockSpec((seq, d), lambda b, h, p: (b, off // d + h))

    table_spec = pl.BlockSpec((seq, d), lambda b, h, p: (0, 0))
    gate_rows = -(-n_blocks // SUBLANES) * SUBLANES
    pairs = MOBA_PAIRS_PER_STEP
    assert (n_blocks // 2) % pairs == 0
    n_snap = max(n_blocks // 2 - 1, 1)
    return pl.pallas_call(
        functools.partial(_moba_kernel, n_blocks=n_blocks),
        grid=(batch, MOBA_HEADS, n_blocks // 2 // pairs),
        in_specs=[col_spec(OFF_MQ), col_spec(OFF_MK), col_spec(OFF_MV),
                  table_spec, table_spec, table_spec],
        out_specs=pl.BlockSpec((seq, d), lambda b, h, p: (b, h)),
        out_shape=jax.ShapeDtypeStruct((n, MOBA_WIDTH), BF16),
        scratch_shapes=[pltpu.VMEM((n_blocks, blk, 2 * d), BF16),
                        pltpu.VMEM((n_blocks, d, blk), BF16),
                        pltpu.VMEM((gate_rows, d), F32),
                        pltpu.VMEM((n_blocks, 2 * d, blk), BF16),
                        pltpu.VMEM((pairs * (n_blocks + 1), blk, blk), F32),
                        pltpu.VMEM((2 * pairs, d, blk), F32),
                        pltpu.VMEM((pairs, n_snap, d, blk), F32)],
        compiler_params=_compiler_params(3, 52),
        name="moba",
    )(proj, proj, proj, cos_t, sin_lo, sin_hi)


def _hgrn_pair_codes(block):
    i = np.arange(block)[:, None]
    j = np.arange(block)[None, :]
    top_bit = np.floor(np.log2(np.maximum(i ^ j, 1))).astype(np.int32)
    n_levels = block.bit_length() - 1
    return np.where(i > j, top_bit, np.where(i == j, n_levels, -1)).astype(np.int32)


def _hgrn_kernel(lbp_ref, tril_ref, code_ref, q_ref, f_ref, v_ref, g_ref, nrm_ref, o_ref,
                 state_ref, b_ref, k_ref, *, layer, rows_per_step):
    c_len = HGRN_BLOCK
    d = HGRN_HEAD_DIM

    @pl.when(pl.program_id(2) == 0)
    def _():
        state_ref[...] = jnp.zeros_like(state_ref)

    params = lbp_ref[...]
    e = jnp.exp(params - jnp.max(params, axis=0, keepdims=True))
    soft = e / jnp.sum(e, axis=0, keepdims=True)
    lower = jnp.zeros((1, d), F32)
    for r in range(1, layer + 1):
        lower = lower + soft[r:r + 1, :]

    tril = tril_ref[...]
    for seg0 in range(0, rows_per_step, c_len):
        seg = slice(seg0, seg0 + c_len)
        forget = lower + (1.0 - lower) * jax.nn.sigmoid(f_ref[seg, :])
        k_ref[seg, :] = 1.0 - forget
        log_f = jnp.log2(forget)
        hi = log_f.astype(BF16)
        rest = log_f - hi.astype(F32)
        mid = rest.astype(BF16)
        lo = (rest - mid.astype(F32)).astype(BF16)
        b_ref[seg, :] = (jnp.dot(tril, hi, preferred_element_type=F32)
                         + jnp.dot(tril, mid, preferred_element_type=F32)
                         + jnp.dot(tril, lo, preferred_element_type=F32))

    assert c_len & (c_len - 1) == 0 and c_len % LANES == 0
    levels = [1 << s for s in range(c_len.bit_length() - 1)]
    n_grp = c_len // SUBLANES
    row8 = lax.broadcasted_iota(jnp.int32, (SUBLANES, d), 0)
    zero8 = jnp.zeros((SUBLANES, d), F32)
    codes = _hgrn_pair_codes(c_len)
    tiles = [(g, cb) for g in range(n_grp) for cb in range(c_len // LANES)]

    def code_tile(g, cb):
        return codes[SUBLANES * g:SUBLANES * (g + 1), LANES * cb:LANES * (cb + 1)]

    level_tiles = [[(g, cb, bool(np.all(code_tile(g, cb) == bit))) for g, cb in tiles
                    if np.any(code_tile(g, cb) == bit)] for bit in range(len(levels) + 1)]

    def bcast_row(ref, r):
        return jnp.broadcast_to(ref[r:r + 1, :], (SUBLANES, d))

    def level_operands(r0, h, bg, qg, kg):
        q_parts, k_parts = [], []
        for g in range(n_grp):
            base = SUBLANES * g
            if h >= SUBLANES:
                b_a = bcast_row(b_ref, r0 + base // (2 * h) * (2 * h) + h - 1)
                if base & h:
                    q_parts.append(qg[g] * jnp.exp2(bg[g] - b_a))
                    k_parts.append(zero8)
                else:
                    q_parts.append(zero8)
                    k_parts.append(kg[g] * jnp.exp2(b_a - bg[g]))
                continue
            upper = (row8 & h) != 0
            if h == 1:
                q_parts.append(jnp.where(upper, qg[g] * (1.0 - kg[g]), 0.0))
                k_parts.append(jnp.where(upper, 0.0, kg[g]))
                continue
            b_a = bcast_row(b_ref, r0 + base + h - 1)
            for s in range(2 * h, SUBLANES, 2 * h):
                b_a = jnp.where(row8 >= s, bcast_row(b_ref, r0 + base + s + h - 1), b_a)
            decay = jnp.exp2(-jnp.abs(bg[g] - b_a))
            q_parts.append(jnp.where(upper, qg[g] * decay, 0.0))
            k_parts.append(jnp.where(upper, 0.0, kg[g] * decay))
        return (jnp.concatenate(q_parts, axis=0).astype(BF16),
                jnp.concatenate(k_parts, axis=0).astype(BF16))

    for c in range(rows_per_step // c_len):
        r0 = c * c_len
        rows = slice(r0, r0 + c_len)
        b = b_ref[rows, :]
        q = _silu(q_ref[rows, :])
        k = k_ref[rows, :]
        v = v_ref[rows, :]

        groups = [slice(SUBLANES * g, SUBLANES * (g + 1)) for g in range(n_grp)]
        bg = [b[g, :] for g in groups]
        qg = [q[g, :] for g in groups]
        kg = [k[g, :] for g in groups]
        def code_of(g, cb):
            return code_ref[SUBLANES * g:SUBLANES * (g + 1), LANES * cb:LANES * (cb + 1)]

        a_tiles = {}
        self_weight = jnp.sum(q * k, axis=-1, keepdims=True)
        for g, cb, _ in level_tiles[len(levels)]:
            a_tiles[g, cb] = jnp.where(code_of(g, cb) == len(levels),
                                       self_weight[groups[g], :], 0.0)
        for bit, h in enumerate(levels):
            q_h, k_h = level_operands(r0, h, bg, qg, kg)
            for cb in range(c_len // LANES):
                k_lo = LANES * cb
                if 2 * h >= LANES:
                    start = k_lo // (2 * h) * (2 * h)
                    if k_lo >= start + h:
                        continue
                    q_lo, q_hi = start + h, start + 2 * h
                else:
                    q_lo, q_hi = k_lo, k_lo + LANES
                pair = lax.dot_general(q_h[q_lo:q_hi, :], k_h[k_lo:k_lo + LANES, :], NT_DIMS,
                                       preferred_element_type=F32)
                for g, tile_cb, owns_tile in level_tiles[bit]:
                    if tile_cb != cb:
                        continue
                    assert q_lo <= SUBLANES * g < q_hi
                    piece = pair[SUBLANES * g - q_lo:SUBLANES * (g + 1) - q_lo, :]
                    if owns_tile:
                        a_tiles[g, cb] = piece
                    else:
                        a_tiles[g, cb] = jnp.where(code_of(g, cb) == bit, piece,
                                                   a_tiles.get((g, cb), 0.0))
        zero_tile = jnp.zeros((SUBLANES, LANES), F32)
        a_mat = jnp.concatenate(
            [jnp.concatenate([a_tiles.get((g, cb), zero_tile) for g in range(n_grp)], axis=0)
             for cb in range(c_len // LANES)], axis=1)
        intra = jnp.dot(a_mat.astype(BF16), v.astype(BF16), preferred_element_type=F32)

        state_t = state_ref[...]
        cross = lax.dot_general((q * jnp.exp2(b)).astype(BF16), state_t.astype(BF16), NT_DIMS,
                                preferred_element_type=F32)
        b_last = b_ref[r0 + c_len - 1:r0 + c_len, :]
        k_dec = k * jnp.exp2(b_last - b)
        state_ref[...] = state_t * jnp.exp2(b_last) + lax.dot_general(
            v.astype(BF16), k_dec.astype(BF16), TN_DIMS, preferred_element_type=F32)

        out = intra + cross
        normed = out * lax.rsqrt(jnp.mean(out * out, axis=-1, keepdims=True) + NORM_EPS)
        gated = (normed * nrm_ref[...]) * _silu(g_ref[rows, :])
        o_ref[rows, :] = gated.astype(BF16)


def hgrn_mixer(proj, lower_bound_params, hgrn_norm, layer, batch, seq, *, rows_per_step=1024):
    n = proj.shape[0]
    d = HGRN_HEAD_DIM
    depth = lower_bound_params.shape[0]
    blk = HGRN_BLOCK
    assert seq % rows_per_step == 0 and rows_per_step % blk == 0
    steps = seq // rows_per_step
    codes = _hgrn_pair_codes(blk)
    tril = jnp.asarray(codes >= 0, BF16)

    def col_spec(off):
        return pl.BlockSpec((rows_per_step, d), lambda b, h, s: (b * steps + s, off // d + h))

    return pl.pallas_call(
        functools.partial(_hgrn_kernel, layer=layer, rows_per_step=rows_per_step),
        grid=(batch, HGRN_HEADS, steps),
        in_specs=[pl.BlockSpec((depth, d), lambda b, h, s: (0, h)),
                  pl.BlockSpec((blk, blk), lambda b, h, s: (0, 0)),
                  pl.BlockSpec((blk, blk), lambda b, h, s: (0, 0)),
                  col_spec(OFF_HQ), col_spec(OFF_HF), col_spec(OFF_HI), col_spec(OFF_HG),
                  pl.BlockSpec((1, d), lambda b, h, s: (0, h))],
        out_specs=pl.BlockSpec((rows_per_step, d), lambda b, h, s: (b * steps + s, h)),
        out_shape=jax.ShapeDtypeStruct((n, HGRN_WIDTH), BF16),
        scratch_shapes=[pltpu.VMEM((d, d), F32),
                        pltpu.VMEM((rows_per_step, d), F32),
                        pltpu.VMEM((rows_per_step, d), F32)],
        compiler_params=_compiler_params(3, 32),
        name="hgrn2",
    )(lower_bound_params.astype(F32), tril, jnp.asarray(codes), proj, proj, proj, proj,
      hgrn_norm.reshape(1, HGRN_WIDTH))


def kernel(x, hgrn_lower_bounds, norm_mix, w_in, ret_norm, hgrn_norm, w_branch_ret, w_branch_moba,
           w_branch_hgrn, w_out, norm_ffn, w_ffn_gate, w_ffn_up, w_ffn_down, final_norm):
    batch, seq, d_model = x.shape
    assert d_model == D_MODEL
    depth = w_in.shape[0]
    (w_in, w_branch_ret, w_branch_moba, w_branch_hgrn, w_out, w_ffn_gate, w_ffn_up,
     w_ffn_down) = (w.astype(BF16) for w in (w_in, w_branch_ret, w_branch_moba, w_branch_hgrn,
                                             w_out, w_ffn_gate, w_ffn_up, w_ffn_down))
    h = x.reshape(batch * seq, d_model)
    for layer in range(depth):
        proj = norm_matmul(h, norm_mix[layer], w_in, layer)
        ret = retention_mixer(proj, ret_norm[layer], batch, seq)
        moba = moba_mixer(proj, batch, seq)
        hgrn = hgrn_mixer(proj, hgrn_lower_bounds, hgrn_norm[layer], layer, batch, seq)
        h = merge_out_proj(ret, moba, hgrn, proj, w_branch_ret, w_branch_moba, w_branch_hgrn,
                           w_out, h, layer)
        act = ffn_up(h, norm_ffn[layer], w_ffn_gate, w_ffn_up, layer)
        h = ffn_down(act, w_ffn_down, h, layer, final_norm, normalize=layer == depth - 1)
    return h.reshape(batch, seq, d_model)
```

```python
import functools

import jax
import jax.numpy as jnp
import numpy as np
from jax import lax
from jax.experimental import pallas as pl
from jax.experimental.pallas import tpu as pltpu

F32 = jnp.float32
BF16 = jnp.bfloat16

D_MODEL = 2048
RET_HEADS = 4
RET_HEAD_DIM = 256
RET_WIDTH = RET_HEADS * RET_HEAD_DIM
RET_CHUNK = 128
RET_ROPE_BASE = 10000.0
MOBA_HEADS = 8
MOBA_HEAD_DIM = 128
MOBA_WIDTH = MOBA_HEADS * MOBA_HEAD_DIM
MOBA_BLOCK = 256
MOBA_TOPK = 3
ROPE_THETA = 500000.0
ROPE_DIM = MOBA_HEAD_DIM // 4
HGRN_HEADS = 8
HGRN_HEAD_DIM = 128
HGRN_WIDTH = HGRN_HEADS * HGRN_HEAD_DIM
HGRN_BLOCK = 256
NORM_EPS = 1e-6
IN_SIZES = (RET_WIDTH,) * 4 + (MOBA_WIDTH,) * 3 + (HGRN_WIDTH,) * 4 + (D_MODEL,) * 3
IN_COLS = sum(IN_SIZES)
IN_OFFS = tuple(sum(IN_SIZES[:i]) for i in range(len(IN_SIZES)))
(OFF_RQ, OFF_RK, OFF_RV, OFF_RG, OFF_MQ, OFF_MK, OFF_MV,
 OFF_HQ, OFF_HF, OFF_HI, OFF_HG, OFF_GR, OFF_GM, OFF_GH) = IN_OFFS

V7X_VMEM_BYTES = 64 * 1024 * 1024
LANES = 128
SUBLANES = 8
MASK_VALUE = -1e30

NT_DIMS = (((1,), (1,)), ((), ()))
TN_DIMS = (((0,), (0,)), ((), ()))


def _compiler_params(n_grid_dims, vmem_mib):
    assert vmem_mib * 1024 * 1024 < V7X_VMEM_BYTES
    return pltpu.CompilerParams(
        dimension_semantics=("arbitrary",) * n_grid_dims,
        vmem_limit_bytes=vmem_mib * 1024 * 1024)


def _silu(x):
    return x * jax.nn.sigmoid(x)


SIDE_CAST_COLS = 2048


class SideCast:
    def __init__(self, array, n_steps, part=0, n_parts=1):
        assert array.size % (n_parts * n_steps * SIDE_CAST_COLS) == 0
        self.rows = array.size // (n_parts * n_steps * SIDE_CAST_COLS)
        assert self.rows % SUBLANES == 0
        self.n_steps = n_steps
        self.first_slab = part * n_steps
        self.operand = array.reshape(n_parts * n_steps, self.rows, SIDE_CAST_COLS)
        self.part_shape = (array.shape[0] // n_parts,) + array.shape[1:]

    def in_spec(self, step_of):
        return pl.BlockSpec((None, self.rows, SIDE_CAST_COLS),
                            lambda *g: (self.first_slab + step_of(*g), 0, 0))

    def out_spec(self, step_of):
        return pl.BlockSpec((None, self.rows, SIDE_CAST_COLS), lambda *g: (step_of(*g), 0, 0))

    def out_shape(self):
        return jax.ShapeDtypeStruct((self.n_steps, self.rows, SIDE_CAST_COLS), BF16)

    def finish(self, cast):
        return cast.reshape(self.part_shape)


def _run_side_casts(in_refs, out_refs):
    for src, dst in zip(in_refs, out_refs):
        dst[...] = src[...].astype(BF16)


def _layer_weight_spec(layer, rows, tn, col_of):
    return pl.BlockSpec((None, rows, tn), lambda *g: (layer, 0, col_of(*g)))


def _rms_normalize(x, g):
    r = lax.rsqrt(jnp.mean(x * x, axis=-1, keepdims=True) + NORM_EPS)
    return x * r * g


def _norm_matmul_kernel(x_ref, g_ref, w_ref, o_ref, h_ref):
    @pl.when(pl.program_id(1) == 0)
    def _():
        h_ref[...] = _rms_normalize(x_ref[...], g_ref[...]).astype(BF16)

    o_ref[...] = jnp.dot(h_ref[...], w_ref[...], preferred_element_type=F32)


def norm_matmul(x, g, w, layer, *, tm=1024, tn=1024):
    n, d = x.shape
    cols = w.shape[2]
    assert n % tm == 0 and cols % tn == 0
    return pl.pallas_call(
        _norm_matmul_kernel,
        grid=(n // tm, cols // tn),
        in_specs=[pl.BlockSpec((tm, d), lambda i, j: (i, 0)),
                  pl.BlockSpec((1, d), lambda i, j: (0, 0)),
                  _layer_weight_spec(layer, d, tn, lambda i, j: j)],
        out_specs=pl.BlockSpec((tm, tn), lambda i, j: (i, j)),
        out_shape=jax.ShapeDtypeStruct((n, cols), F32),
        scratch_shapes=[pltpu.VMEM((tm, d), BF16)],
        compiler_params=_compiler_params(2, 48),
        name="norm_inproj",
    )(x, g.reshape(1, d), w)


def _ffn_up_kernel(x_ref, g_ref, wg_ref, wu_ref, o_ref, h_ref):
    @pl.when(pl.program_id(1) == 0)
    def _():
        h_ref[...] = _rms_normalize(x_ref[...], g_ref[...]).astype(BF16)

    h = h_ref[...]
    gate = jnp.dot(h, wg_ref[...], preferred_element_type=F32)
    up = jnp.dot(h, wu_ref[...], preferred_element_type=F32)
    o_ref[...] = (_silu(gate) * up).astype(BF16)


def ffn_up(x, g, wg, wu, layer, *, tm=1024, tn=512):
    n, d = x.shape
    hidden = wg.shape[2]
    assert n % tm == 0 and hidden % tn == 0
    w_spec = _layer_weight_spec(layer, d, tn, lambda i, j: j)
    return pl.pallas_call(
        _ffn_up_kernel,
        grid=(n // tm, hidden // tn),
        in_specs=[pl.BlockSpec((tm, d), lambda i, j: (i, 0)),
                  pl.BlockSpec((1, d), lambda i, j: (0, 0)),
                  w_spec, w_spec],
        out_specs=pl.BlockSpec((tm, tn), lambda i, j: (i, j)),
        out_shape=jax.ShapeDtypeStruct((n, hidden), BF16),
        scratch_shapes=[pltpu.VMEM((tm, d), BF16)],
        compiler_params=_compiler_params(2, 48),
        name="ffn_up",
    )(x, g.reshape(1, d), wg, wu)


def _ffn_down_kernel(a_ref, w_ref, r_ref, g_ref, o_ref, *, normalize):
    out = r_ref[...] + jnp.dot(a_ref[...], w_ref[...], preferred_element_type=F32)
    o_ref[...] = _rms_normalize(out, g_ref[...]) if normalize else out


def ffn_down(a, w, res, layer, gain, *, normalize, tm=512):
    n, k = a.shape
    cols = w.shape[2]
    assert n % tm == 0
    row_spec = pl.BlockSpec((tm, cols), lambda i: (i, 0))
    return pl.pallas_call(
        functools.partial(_ffn_down_kernel, normalize=normalize),
        grid=(n // tm,),
        in_specs=[pl.BlockSpec((tm, k), lambda i: (i, 0)),
                  pl.BlockSpec((None, k, cols), lambda i: (layer, 0, 0),
                               pipeline_mode=pl.Buffered(1)),
                  row_spec,
                  pl.BlockSpec((1, cols), lambda i: (0, 0))],
        out_specs=row_spec,
        out_shape=jax.ShapeDtypeStruct((n, cols), F32),
        compiler_params=_compiler_params(1, 54),
        name="ffn_down",
    )(a, w, res, gain.reshape(1, cols))


MERGE_GATE_TILE = 1024


def _merge_out_kernel(*refs):
    n_gate = D_MODEL // MERGE_GATE_TILE
    branch_refs = refs[:3]
    gate_refs = refs[3:3 + 3 * n_gate]
    wb_refs = refs[3 + 3 * n_gate:6 + 3 * n_gate]
    wo_ref, res_ref, o_ref = refs[6 + 3 * n_gate:]
    parts = []
    for c in range(n_gate):
        cols = slice(c * MERGE_GATE_TILE, (c + 1) * MERGE_GATE_TILE)
        mixed = None
        for b in range(3):
            term = jax.nn.sigmoid(gate_refs[b * n_gate + c][...]) * jnp.dot(
                branch_refs[b][...], wb_refs[b][:, cols], preferred_element_type=F32)
            mixed = term if mixed is None else mixed + term
        parts.append(mixed.astype(BF16))
    mixed = jnp.concatenate(parts, axis=1)
    o_ref[...] = res_ref[...] + jnp.dot(mixed, wo_ref[...], preferred_element_type=F32)


def merge_out_proj(ret, moba, hgrn, proj, wr, wm, wh, wo, res, layer, *, tm=256):
    n = ret.shape[0]
    assert n % tm == 0 and D_MODEL % MERGE_GATE_TILE == 0
    assert all(off % MERGE_GATE_TILE == 0 for off in (OFF_GR, OFF_GM, OFF_GH))
    n_gate = D_MODEL // MERGE_GATE_TILE

    def branch_spec(width):
        return pl.BlockSpec((tm, width), lambda i: (i, 0))

    def gate_specs(off):
        return [pl.BlockSpec((tm, MERGE_GATE_TILE),
                             functools.partial(lambda i, col: (i, col), col=off // MERGE_GATE_TILE + c))
                for c in range(n_gate)]

    def weight_spec(rows):
        return pl.BlockSpec((None, rows, D_MODEL), lambda i: (layer, 0, 0),
                            pipeline_mode=pl.Buffered(1))

    row_spec = pl.BlockSpec((tm, D_MODEL), lambda i: (i, 0))
    return pl.pallas_call(
        _merge_out_kernel,
        grid=(n // tm,),
        in_specs=[branch_spec(RET_WIDTH), branch_spec(MOBA_WIDTH), branch_spec(HGRN_WIDTH),
                  *gate_specs(OFF_GR), *gate_specs(OFF_GM), *gate_specs(OFF_GH),
                  weight_spec(RET_WIDTH), weight_spec(MOBA_WIDTH), weight_spec(HGRN_WIDTH),
                  weight_spec(D_MODEL), row_spec],
        out_specs=row_spec,
        out_shape=jax.ShapeDtypeStruct((n, D_MODEL), F32),
        compiler_params=_compiler_params(1, 52),
        name="merge_out_proj",
    )(ret, moba, hgrn, *([proj] * (3 * n_gate)), wr, wm, wh, wo, res)


def _retention_kernel(lg_ref, q_ref, k_ref, v_ref, g_ref, cos_ref, sin_ref, nrm_ref,
                      o_ref, state_ref, *, chunks):
    c_len = RET_CHUNK
    d = RET_HEAD_DIM
    half = d // 2

    @pl.when(pl.program_id(1) == 0)
    def _():
        state_ref[...] = jnp.zeros_like(state_ref)

    t_col = lax.broadcasted_iota(jnp.int32, (c_len, 1), 0).astype(F32)
    rel = (lax.broadcasted_iota(jnp.int32, (c_len, c_len), 0)
           - lax.broadcasted_iota(jnp.int32, (c_len, c_len), 1)).astype(F32)

    def rotate(x, cos, sin):
        x1 = x[:, :half]
        x2 = x[:, half:]
        return jnp.concatenate([x1 * cos - x2 * sin, x2 * cos + x1 * sin], axis=1)

    for h in range(RET_HEADS):
        cols = slice(h * d, (h + 1) * d)
        lg = lg_ref[h]
        decay = jnp.where(rel >= 0, jnp.exp(lg * jnp.maximum(rel, 0.0)), 0.0)
        q_weight = jnp.exp(lg * (t_col + 1.0))
        k_weight = jnp.exp(lg * (c_len - 1.0 - t_col))
        chunk_decay = jnp.exp(jnp.full((1, d), lg * c_len, F32))
        for c in range(chunks):
            rows = slice(c * c_len, (c + 1) * c_len)
            cos = cos_ref[rows, :]
            sin = sin_ref[rows, :]
            q = rotate(q_ref[rows, cols], cos, sin)
            k = rotate(k_ref[rows, cols], cos, sin) * (d ** -0.5)
            v = v_ref[rows, cols].astype(BF16)
            state = state_ref[h]

            scores = lax.dot_general(q.astype(BF16), k.astype(BF16), NT_DIMS,
                                     preferred_element_type=F32) * decay
            intra = jnp.dot(scores.astype(BF16), v, preferred_element_type=F32)
            cross = jnp.dot((q * q_weight).astype(BF16), state.astype(BF16),
                            preferred_element_type=F32)
            kv = lax.dot_general((k * k_weight).astype(BF16), v, TN_DIMS,
                                 preferred_element_type=F32)
            state_ref[h] = chunk_decay * state + kv

            out = intra + cross
            centered = out - jnp.mean(out, axis=-1, keepdims=True)
            normed = centered * lax.rsqrt(
                jnp.mean(centered * centered, axis=-1, keepdims=True) + NORM_EPS)
            gated = (normed * nrm_ref[:, cols]) * _silu(g_ref[rows, cols])
            o_ref[rows, cols] = gated.astype(BF16)


def retention_mixer(proj, ret_norm, batch, seq, *, rows_per_step=512):
    n = proj.shape[0]
    d = RET_HEAD_DIM
    assert seq % rows_per_step == 0 and rows_per_step % RET_CHUNK == 0
    steps = seq // rows_per_step
    log_gamma = jnp.log1p(-jnp.exp2(-5.0 - jnp.arange(RET_HEADS, dtype=F32)))
    half = d // 2
    inv_freq = RET_ROPE_BASE ** (-jnp.arange(half, dtype=F32) * 2.0 / d)
    ang = jnp.arange(seq).astype(F32)[:, None] * inv_freq[None, :]
    cos, sin = jnp.cos(ang), jnp.sin(ang)

    def col_spec(off):
        return pl.BlockSpec((rows_per_step, RET_WIDTH),
                            lambda b, s: (b * steps + s, off // RET_WIDTH))

    table_spec = pl.BlockSpec((rows_per_step, half), lambda b, s: (s, 0))
    return pl.pallas_call(
        functools.partial(_retention_kernel, chunks=rows_per_step // RET_CHUNK),
        grid=(batch, steps),
        in_specs=[pl.BlockSpec(memory_space=pltpu.SMEM),
                  col_spec(OFF_RQ), col_spec(OFF_RK), col_spec(OFF_RV), col_spec(OFF_RG),
                  table_spec, table_spec,
                  pl.BlockSpec((1, RET_WIDTH), lambda b, s: (0, 0))],
        out_specs=pl.BlockSpec((rows_per_step, RET_WIDTH), lambda b, s: (b * steps + s, 0)),
        out_shape=jax.ShapeDtypeStruct((n, RET_WIDTH), BF16),
        scratch_shapes=[pltpu.VMEM((RET_HEADS, d, d), F32)],
        compiler_params=_compiler_params(2, 40),
        name="retention",
    )(log_gamma, proj, proj, proj, proj, cos, sin, ret_norm.reshape(1, RET_WIDTH))


MOBA_PAIRS_PER_STEP = 2


def _moba_rotate(x, cos, sin_lo, sin_hi):
    half = ROPE_DIM // 2
    return (x * cos + pltpu.roll(x, MOBA_HEAD_DIM - half, 1) * sin_lo
            + pltpu.roll(x, half, 1) * sin_hi)


def _split_dot(a, b):
    a_hi = a.astype(BF16)
    a_lo = (a - a_hi.astype(F32)).astype(BF16)
    b_hi = b.astype(BF16)
    b_lo = (b - b_hi.astype(F32)).astype(BF16)
    return (jnp.dot(a_hi, b_hi, preferred_element_type=F32)
            + jnp.dot(a_hi, b_lo, preferred_element_type=F32)
            + jnp.dot(a_lo, b_hi, preferred_element_type=F32))


def _moba_kernel(*refs, n_blocks, n_side):
    q_ref, k_ref, v_ref, cos_ref, slo_ref, shi_ref = refs[:6]
    o_ref = refs[6 + n_side]
    (kaug_ref, vt_ref, kmean_ref, qt_ref, s_ref, acc_ref,
     snap_ref) = refs[7 + 2 * n_side:]
    _run_side_casts(refs[6:6 + n_side], refs[7 + n_side:7 + 2 * n_side])
    blk = MOBA_BLOCK
    d = MOBA_HEAD_DIM
    step = pl.program_id(2)
    n_past = n_blocks - 1
    pairs = snap_ref.shape[0]
    gate_rows = kmean_ref.shape[0]
    groups = blk // SUBLANES
    q_scale = (d ** -0.5) * 1.4426950408889634

    def rotate(ref, start):
        rows = pl.ds(start, blk)
        return _moba_rotate(ref[rows, :], cos_ref[rows, :], slo_ref[rows, :], shi_ref[rows, :])

    @pl.when(step == 0)
    def _():
        lane = lax.broadcasted_iota(jnp.int32, (blk, LANES), 1)
        kmean_ref[...] = jnp.zeros_like(kmean_ref)
        for j in range(n_blocks):
            kr = rotate(k_ref, j * blk)
            kaug_ref[j, :, :d] = kr.astype(BF16)
            kaug_ref[j, :, d:] = jnp.where(lane == j, 1.0, 0.0).astype(BF16)
            vt_ref[j] = v_ref[j * blk:(j + 1) * blk, :].T.astype(BF16)
            kmean_ref[j:j + 1, :] = jnp.mean(kr, axis=0, keepdims=True)
        row = lax.broadcasted_iota(jnp.int32, (gate_rows, blk), 0)
        row_f = row.astype(F32)
        pad = jnp.zeros((LANES - gate_rows, blk), F32)
        for i in range(n_blocks):
            q_t = rotate(q_ref, i * blk).T
            gate = _split_dot(kmean_ref[...], q_t)
            g = jnp.where(row < i, gate, -jnp.inf)
            bias = jnp.full((gate_rows, blk), MASK_VALUE, F32)
            for _ in range(min(MOBA_TOPK, n_blocks)):
                best = jnp.max(g, axis=0, keepdims=True)
                hit = (g == best) & (g > -jnp.inf)
                first = jnp.min(jnp.where(hit, row_f, float(LANES)), axis=0, keepdims=True)
                pick = row_f == first
                bias = jnp.where(pick, 0.0, bias)
                g = jnp.where(pick, -jnp.inf, g)
            bias = jnp.where(row == i, 0.0, bias)
            qt_ref[i] = jnp.concatenate([q_t * q_scale, bias, pad], axis=0).astype(BF16)

    def group_reduce(x, op):
        return op(x.reshape(groups, SUBLANES, blk), axis=0)

    def score_own_block(i, own_slot):
        s_t = jnp.dot(kaug_ref[i], qt_ref[i], preferred_element_type=F32)
        key = lax.broadcasted_iota(jnp.int32, (blk, blk), 0)
        qry = lax.broadcasted_iota(jnp.int32, (blk, blk), 1)
        s_t = jnp.where(key <= qry, s_t, MASK_VALUE)
        s_ref[own_slot] = s_t
        return group_reduce(s_t, jnp.max)

    def tile_of(t, p):
        in_b = t >= p
        return in_b, jnp.where(in_b, t - p, t)

    def col_max(m8):
        return jnp.broadcast_to(jnp.max(m8, axis=0, keepdims=True), (SUBLANES, blk))

    def weights(slot, m8):
        p_t = jnp.exp2(s_ref[slot].reshape(groups, SUBLANES, blk) - m8[None])
        return jnp.sum(p_t, axis=0), p_t.reshape(blk, blk).astype(BF16)

    def values(j, p_bf):
        return jnp.dot(vt_ref[j], p_bf, preferred_element_type=F32)

    never = jnp.full((SUBLANES, blk), -jnp.inf, F32)
    scored = []
    for k in range(pairs):
        p = step * pairs + k
        base = k * (n_past + 2)
        i_a, i_b = p, n_past - p
        m_a = score_own_block(i_a, base + n_past)
        m_b = score_own_block(i_b, base + n_past + 1)
        for t in range(n_past):
            in_b, j = tile_of(t, p)
            s_t = jnp.dot(kaug_ref[j], qt_ref[jnp.where(in_b, i_b, i_a)],
                          preferred_element_type=F32)
            s_ref[base + t] = s_t
            gm = group_reduce(s_t, jnp.max)
            m_a = jnp.maximum(m_a, jnp.where(in_b, never, gm))
            m_b = jnp.maximum(m_b, jnp.where(in_b, gm, never))
        scored.append((p, base, i_a, i_b, col_max(m_a), col_max(m_b)))

    for k, (p, base, i_a, i_b, m_a, m_b) in enumerate(scored):
        l_a, p_own = weights(base + n_past, m_a)
        acc_ref[2 * k] = values(i_a, p_own)
        l_b, p_own = weights(base + n_past + 1, m_b)
        acc_ref[2 * k + 1] = values(i_b, p_own)
        acc = jnp.zeros((d, blk), F32)
        for t in range(n_past):
            in_b, j = tile_of(t, p)
            l_part, p_bf = weights(base + t, jnp.where(in_b, m_b, m_a))
            acc = acc * jnp.where(t == p, 0.0, 1.0) + values(j, p_bf)
            l_a = l_a + jnp.where(in_b, 0.0, l_part)
            l_b = l_b + jnp.where(in_b, l_part, 0.0)
            if t < snap_ref.shape[1]:
                snap_ref[k, t] = acc
        past_a = snap_ref[k, jnp.maximum(p - 1, 0)] * jnp.where(p > 0, 1.0, 0.0)
        for w, i, l8, past in ((2 * k, i_a, l_a, past_a), (2 * k + 1, i_b, l_b, acc)):
            l = jnp.sum(l8, axis=0, keepdims=True)
            out_t = (acc_ref[w] + past) / l
            o_ref[pl.ds(pl.multiple_of(i * blk, blk), blk), :] = out_t.T.astype(BF16)


def moba_grid_steps(batch, seq):
    return batch * MOBA_HEADS * (seq // MOBA_BLOCK // 2 // MOBA_PAIRS_PER_STEP)


def moba_mixer(proj, batch, seq, side_casts=()):
    n = proj.shape[0]
    d = MOBA_HEAD_DIM
    blk = MOBA_BLOCK
    assert seq % blk == 0
    n_blocks = seq // blk
    assert n_blocks <= LANES and n_blocks % 2 == 0
    half = ROPE_DIM // 2
    inv_freq = ROPE_THETA ** (-jnp.arange(half, dtype=F32) * 2.0 / ROPE_DIM)
    ang = jnp.arange(seq).astype(F32)[:, None] * inv_freq[None, :]
    cos, sin = jnp.cos(ang), jnp.sin(ang)
    zeros = jnp.zeros((seq, d - ROPE_DIM), F32)
    zero_half = jnp.zeros((seq, half), F32)
    cos_t = jnp.concatenate([cos, cos, jnp.ones_like(zeros)], axis=1)
    sin_lo = jnp.concatenate([-sin, zero_half, zeros], axis=1)
    sin_hi = jnp.concatenate([zero_half, sin, zeros], axis=1)

    def col_spec(off):
        return pl.BlockSpec((seq, d), lambda b, h, p: (b, off // d + h))

    table_spec = pl.BlockSpec((seq, d), lambda b, h, p: (0, 0))
    gate_rows = -(-n_blocks // SUBLANES) * SUBLANES
    pairs = MOBA_PAIRS_PER_STEP
    assert (n_blocks // 2) % pairs == 0
    n_snap = max(n_blocks // 2 - 1, 1)
    steps = n_blocks // 2 // pairs

    def step_of(b, h, p):
        return (b * MOBA_HEADS + h) * steps + p

    out, *casts = pl.pallas_call(
        functools.partial(_moba_kernel, n_blocks=n_blocks, n_side=len(side_casts)),
        grid=(batch, MOBA_HEADS, steps),
        in_specs=[col_spec(OFF_MQ), col_spec(OFF_MK), col_spec(OFF_MV),
                  table_spec, table_spec, table_spec,
                  *[c.in_spec(step_of) for c in side_casts]],
        out_specs=[pl.BlockSpec((seq, d), lambda b, h, p: (b, h)),
                   *[c.out_spec(step_of) for c in side_casts]],
        out_shape=[jax.ShapeDtypeStruct((n, MOBA_WIDTH), BF16),
                   *[c.out_shape() for c in side_casts]],
        scratch_shapes=[pltpu.VMEM((n_blocks, blk, 2 * d), BF16),
                        pltpu.VMEM((n_blocks, d, blk), BF16),
                        pltpu.VMEM((gate_rows, d), F32),
                        pltpu.VMEM((n_blocks, 2 * d, blk), BF16),
                        pltpu.VMEM((pairs * (n_blocks + 1), blk, blk), F32),
                        pltpu.VMEM((2 * pairs, d, blk), F32),
                        pltpu.VMEM((pairs, n_snap, d, blk), F32)],
        compiler_params=_compiler_params(3, 54),
        name="moba",
    )(proj, proj, proj, cos_t, sin_lo, sin_hi, *[c.operand for c in side_casts])
    return out, [c.finish(x) for c, x in zip(side_casts, casts)]


def _hgrn_pair_codes(block):
    i = np.arange(block)[:, None]
    j = np.arange(block)[None, :]
    top_bit = np.floor(np.log2(np.maximum(i ^ j, 1))).astype(np.int32)
    n_levels = block.bit_length() - 1
    return np.where(i > j, top_bit, np.where(i == j, n_levels, -1)).astype(np.int32)


def _hgrn_kernel(*refs, layer, rows_per_step, n_side):
    lbp_ref, tril_ref, code_ref, q_ref, f_ref, v_ref, g_ref, nrm_ref = refs[:8]
    o_ref = refs[8 + n_side]
    state_ref, b_ref, k_ref = refs[9 + 2 * n_side:]
    _run_side_casts(refs[8:8 + n_side], refs[9 + n_side:9 + 2 * n_side])
    c_len = HGRN_BLOCK
    d = HGRN_HEAD_DIM

    @pl.when(pl.program_id(2) == 0)
    def _():
        state_ref[...] = jnp.zeros_like(state_ref)

    params = lbp_ref[...]
    e = jnp.exp(params - jnp.max(params, axis=0, keepdims=True))
    soft = e / jnp.sum(e, axis=0, keepdims=True)
    lower = jnp.zeros((1, d), F32)
    for r in range(1, layer + 1):
        lower = lower + soft[r:r + 1, :]

    tril = tril_ref[...]
    for seg0 in range(0, rows_per_step, c_len):
        seg = slice(seg0, seg0 + c_len)
        forget = lower + (1.0 - lower) * jax.nn.sigmoid(f_ref[seg, :])
        k_ref[seg, :] = 1.0 - forget
        log_f = jnp.log2(forget)
        hi = log_f.astype(BF16)
        rest = log_f - hi.astype(F32)
        mid = rest.astype(BF16)
        lo = (rest - mid.astype(F32)).astype(BF16)
        b_ref[seg, :] = (jnp.dot(tril, hi, preferred_element_type=F32)
                         + jnp.dot(tril, mid, preferred_element_type=F32)
                         + jnp.dot(tril, lo, preferred_element_type=F32))

    assert c_len & (c_len - 1) == 0 and c_len % LANES == 0
    levels = [1 << s for s in range(c_len.bit_length() - 1)]
    n_grp = c_len // SUBLANES
    row8 = lax.broadcasted_iota(jnp.int32, (SUBLANES, d), 0)
    zero8 = jnp.zeros((SUBLANES, d), F32)
    codes = _hgrn_pair_codes(c_len)
    tiles = [(g, cb) for g in range(n_grp) for cb in range(c_len // LANES)]

    def code_tile(g, cb):
        return codes[SUBLANES * g:SUBLANES * (g + 1), LANES * cb:LANES * (cb + 1)]

    level_tiles = [[(g, cb, bool(np.all(code_tile(g, cb) == bit))) for g, cb in tiles
                    if np.any(code_tile(g, cb) == bit)] for bit in range(len(levels) + 1)]

    def bcast_row(ref, r):
        return jnp.broadcast_to(ref[r:r + 1, :], (SUBLANES, d))

    def level_operands(r0, h, bg, qg, kg):
        q_parts, k_parts = [], []
        for g in range(n_grp):
            base = SUBLANES * g
            if h >= SUBLANES:
                b_a = bcast_row(b_ref, r0 + base // (2 * h) * (2 * h) + h - 1)
                if base & h:
                    q_parts.append(qg[g] * jnp.exp2(bg[g] - b_a))
                    k_parts.append(zero8)
                else:
                    q_parts.append(zero8)
                    k_parts.append(kg[g] * jnp.exp2(b_a - bg[g]))
                continue
            upper = (row8 & h) != 0
            if h == 1:
                q_parts.append(jnp.where(upper, qg[g] * (1.0 - kg[g]), 0.0))
                k_parts.append(jnp.where(upper, 0.0, kg[g]))
                continue
            b_a = bcast_row(b_ref, r0 + base + h - 1)
            for s in range(2 * h, SUBLANES, 2 * h):
                b_a = jnp.where(row8 >= s, bcast_row(b_ref, r0 + base + s + h - 1), b_a)
            decay = jnp.exp2(-jnp.abs(bg[g] - b_a))
            q_parts.append(jnp.where(upper, qg[g] * decay, 0.0))
            k_parts.append(jnp.where(upper, 0.0, kg[g] * decay))
        return (jnp.concatenate(q_parts, axis=0).astype(BF16),
                jnp.concatenate(k_parts, axis=0).astype(BF16))

    for c in range(rows_per_step // c_len):
        r0 = c * c_len
        rows = slice(r0, r0 + c_len)
        b = b_ref[rows, :]
        q = _silu(q_ref[rows, :])
        k = k_ref[rows, :]
        v = v_ref[rows, :]

        groups = [slice(SUBLANES * g, SUBLANES * (g + 1)) for g in range(n_grp)]
        bg = [b[g, :] for g in groups]
        qg = [q[g, :] for g in groups]
        kg = [k[g, :] for g in groups]
        def code_of(g, cb):
            return code_ref[SUBLANES * g:SUBLANES * (g + 1), LANES * cb:LANES * (cb + 1)]

        a_tiles = {}
        self_weight = jnp.sum(q * k, axis=-1, keepdims=True)
        for g, cb, _ in level_tiles[len(levels)]:
            a_tiles[g, cb] = jnp.where(code_of(g, cb) == len(levels),
                                       self_weight[groups[g], :], 0.0)
        for bit, h in enumerate(levels):
            q_h, k_h = level_operands(r0, h, bg, qg, kg)
            for cb in range(c_len // LANES):
                k_lo = LANES * cb
                if 2 * h >= LANES:
                    start = k_lo // (2 * h) * (2 * h)
                    if k_lo >= start + h:
                        continue
                    q_lo, q_hi = start + h, start + 2 * h
                else:
                    q_lo, q_hi = k_lo, k_lo + LANES
                pair = lax.dot_general(q_h[q_lo:q_hi, :], k_h[k_lo:k_lo + LANES, :], NT_DIMS,
                                       preferred_element_type=F32)
                for g, tile_cb, owns_tile in level_tiles[bit]:
                    if tile_cb != cb:
                        continue
                    assert q_lo <= SUBLANES * g < q_hi
                    piece = pair[SUBLANES * g - q_lo:SUBLANES * (g + 1) - q_lo, :]
                    if owns_tile:
                        a_tiles[g, cb] = piece
                    else:
                        a_tiles[g, cb] = jnp.where(code_of(g, cb) == bit, piece,
                                                   a_tiles.get((g, cb), 0.0))
        zero_tile = jnp.zeros((SUBLANES, LANES), F32)
        a_mat = jnp.concatenate(
            [jnp.concatenate([a_tiles.get((g, cb), zero_tile) for g in range(n_grp)], axis=0)
             for cb in range(c_len // LANES)], axis=1)
        intra = jnp.dot(a_mat.astype(BF16), v.astype(BF16), preferred_element_type=F32)

        state_t = state_ref[...]
        cross = lax.dot_general((q * jnp.exp2(b)).astype(BF16), state_t.astype(BF16), NT_DIMS,
                                preferred_element_type=F32)
        b_last = b_ref[r0 + c_len - 1:r0 + c_len, :]
        k_dec = k * jnp.exp2(b_last - b)
        state_ref[...] = state_t * jnp.exp2(b_last) + lax.dot_general(
            v.astype(BF16), k_dec.astype(BF16), TN_DIMS, preferred_element_type=F32)

        out = intra + cross
        normed = out * lax.rsqrt(jnp.mean(out * out, axis=-1, keepdims=True) + NORM_EPS)
        gated = (normed * nrm_ref[...]) * _silu(g_ref[rows, :])
        o_ref[rows, :] = gated.astype(BF16)


HGRN_ROWS_PER_STEP = 1024


def hgrn_grid_steps(batch, seq):
    return batch * HGRN_HEADS * (seq // min(seq, HGRN_ROWS_PER_STEP))


def hgrn_mixer(proj, lower_bound_params, hgrn_norm, layer, batch, seq, side_casts=()):
    rows_per_step = min(seq, HGRN_ROWS_PER_STEP)
    n = proj.shape[0]
    d = HGRN_HEAD_DIM
    depth = lower_bound_params.shape[0]
    blk = HGRN_BLOCK
    assert seq % rows_per_step == 0 and rows_per_step % blk == 0
    steps = seq // rows_per_step
    codes = _hgrn_pair_codes(blk)
    tril = jnp.asarray(codes >= 0, BF16)

    def col_spec(off):
        return pl.BlockSpec((rows_per_step, d), lambda b, h, s: (b * steps + s, off // d + h))

    def step_of(b, h, s):
        return (b * HGRN_HEADS + h) * steps + s

    out, *casts = pl.pallas_call(
        functools.partial(_hgrn_kernel, layer=layer, rows_per_step=rows_per_step,
                          n_side=len(side_casts)),
        grid=(batch, HGRN_HEADS, steps),
        in_specs=[pl.BlockSpec((depth, d), lambda b, h, s: (0, h)),
                  pl.BlockSpec((blk, blk), lambda b, h, s: (0, 0)),
                  pl.BlockSpec((blk, blk), lambda b, h, s: (0, 0)),
                  col_spec(OFF_HQ), col_spec(OFF_HF), col_spec(OFF_HI), col_spec(OFF_HG),
                  pl.BlockSpec((1, d), lambda b, h, s: (0, h)),
                  *[c.in_spec(step_of) for c in side_casts]],
        out_specs=[pl.BlockSpec((rows_per_step, d), lambda b, h, s: (b * steps + s, h)),
                   *[c.out_spec(step_of) for c in side_casts]],
        out_shape=[jax.ShapeDtypeStruct((n, HGRN_WIDTH), BF16),
                   *[c.out_shape() for c in side_casts]],
        scratch_shapes=[pltpu.VMEM((d, d), F32),
                        pltpu.VMEM((rows_per_step, d), F32),
                        pltpu.VMEM((rows_per_step, d), F32)],
        compiler_params=_compiler_params(3, 40),
        name="hgrn2",
    )(lower_bound_params.astype(F32), tril, jnp.asarray(codes), proj, proj, proj, proj,
      hgrn_norm.reshape(1, HGRN_WIDTH), *[c.operand for c in side_casts])
    return out, [c.finish(x) for c, x in zip(side_casts, casts)]


def kernel(x, hgrn_lower_bounds, norm_mix, w_in, ret_norm, hgrn_norm, w_branch_ret, w_branch_moba,
           w_branch_hgrn, w_out, norm_ffn, w_ffn_gate, w_ffn_up, w_ffn_down, final_norm):
    batch, seq, d_model = x.shape
    assert d_model == D_MODEL
    depth = w_in.shape[0]
    w_in_layers = [w_in[:1].astype(BF16)]
    moba_steps = moba_grid_steps(batch, seq)
    hgrn_steps = hgrn_grid_steps(batch, seq)
    moba_casts = [SideCast(w, moba_steps) for w in (w_ffn_gate, w_ffn_up, w_ffn_down)]
    hgrn_casts = [SideCast(w, hgrn_steps)
                  for w in (w_branch_ret, w_branch_moba, w_branch_hgrn, w_out)]
    hgrn_casts += [SideCast(w_in, hgrn_steps, part=l, n_parts=depth) for l in range(1, depth)]
    h = x.reshape(batch * seq, d_model)
    for layer in range(depth):
        proj = norm_matmul(h, norm_mix[layer], w_in_layers[layer], 0)
        ret = retention_mixer(proj, ret_norm[layer], batch, seq)
        moba, cast = moba_mixer(proj, batch, seq, moba_casts if layer == 0 else ())
        if layer == 0:
            w_ffn_gate, w_ffn_up, w_ffn_down = cast
        hgrn, cast = hgrn_mixer(proj, hgrn_lower_bounds, hgrn_norm[layer], layer, batch, seq,
                                hgrn_casts if layer == 0 else ())
        if layer == 0:
            w_branch_ret, w_branch_moba, w_branch_hgrn, w_out = cast[:4]
            w_in_layers += cast[4:]
        h = merge_out_proj(ret, moba, hgrn, proj, w_branch_ret, w_branch_moba, w_branch_hgrn,
                           w_out, h, layer)
        act = ffn_up(h, norm_ffn[layer], w_ffn_gate, w_ffn_up, layer)
        h = ffn_down(act, w_ffn_down, h, layer, final_norm, normalize=layer == depth - 1)
    return h.reshape(batch, seq, d_model)
```

```python
import functools

import jax
import jax.numpy as jnp
import numpy as np
from jax import lax
from jax.experimental import pallas as pl
from jax.experimental.pallas import tpu as pltpu

F32 = jnp.float32
BF16 = jnp.bfloat16

D_MODEL = 2048
RET_HEADS = 4
RET_HEAD_DIM = 256
RET_WIDTH = RET_HEADS * RET_HEAD_DIM
RET_CHUNK = 128
RET_ROPE_BASE = 10000.0
MOBA_HEADS = 8
MOBA_HEAD_DIM = 128
MOBA_WIDTH = MOBA_HEADS * MOBA_HEAD_DIM
MOBA_BLOCK = 256
MOBA_TOPK = 3
ROPE_THETA = 500000.0
ROPE_DIM = MOBA_HEAD_DIM // 4
HGRN_HEADS = 8
HGRN_HEAD_DIM = 128
HGRN_WIDTH = HGRN_HEADS * HGRN_HEAD_DIM
HGRN_BLOCK = 256
NORM_EPS = 1e-6
IN_SIZES = (RET_WIDTH,) * 4 + (MOBA_WIDTH,) * 3 + (HGRN_WIDTH,) * 4 + (D_MODEL,) * 3
IN_COLS = sum(IN_SIZES)
IN_OFFS = tuple(sum(IN_SIZES[:i]) for i in range(len(IN_SIZES)))
(OFF_RQ, OFF_RK, OFF_RV, OFF_RG, OFF_MQ, OFF_MK, OFF_MV,
 OFF_HQ, OFF_HF, OFF_HI, OFF_HG, OFF_GR, OFF_GM, OFF_GH) = IN_OFFS

V7X_VMEM_BYTES = 64 * 1024 * 1024
LANES = 128
SUBLANES = 8
MASK_VALUE = -1e30

NT_DIMS = (((1,), (1,)), ((), ()))
TN_DIMS = (((0,), (0,)), ((), ()))


def _compiler_params(n_grid_dims, vmem_mib):
    assert vmem_mib * 1024 * 1024 < V7X_VMEM_BYTES
    return pltpu.CompilerParams(
        dimension_semantics=("arbitrary",) * n_grid_dims,
        vmem_limit_bytes=vmem_mib * 1024 * 1024)


def _silu(x):
    return x * jax.nn.sigmoid(x)


BF16_SUBLANES = 16


class SideCast:
    def __init__(self, array, n_steps, part=0, n_parts=1):
        layers, rows, self.cols = array.shape
        part_rows = layers // n_parts * rows
        self.steps_per_slab = 1
        while part_rows * self.steps_per_slab % (n_steps * BF16_SUBLANES):
            self.steps_per_slab *= 2
        assert n_steps % self.steps_per_slab == 0
        self.n_slabs = n_steps // self.steps_per_slab
        self.slab_rows = part_rows // self.n_slabs
        self.first_slab = part * self.n_slabs
        self.operand = array.reshape(n_parts * self.n_slabs, self.slab_rows, self.cols)
        self.part_shape = (layers // n_parts, rows, self.cols)

    def in_spec(self, step_of):
        return pl.BlockSpec(
            (None, self.slab_rows, self.cols),
            lambda *g: (self.first_slab + step_of(*g) // self.steps_per_slab, 0, 0))

    def out_spec(self, step_of):
        return pl.BlockSpec((None, self.slab_rows, self.cols),
                            lambda *g: (step_of(*g) // self.steps_per_slab, 0, 0))

    def out_shape(self):
        return jax.ShapeDtypeStruct((self.n_slabs, self.slab_rows, self.cols), BF16)

    def finish(self, cast):
        return cast.reshape(self.part_shape)


def _run_side_casts(in_refs, out_refs):
    for src, dst in zip(in_refs, out_refs):
        dst[...] = src[...].astype(BF16)


def _cast_kernel(src_ref, dst_ref):
    dst_ref[...] = src_ref[...].astype(BF16)


def cast_layer(w, layer, *, tn=1024):
    _, rows, cols = w.shape
    assert cols % tn == 0
    return pl.pallas_call(
        _cast_kernel,
        grid=(cols // tn,),
        in_specs=[pl.BlockSpec((None, rows, tn), lambda j: (layer, 0, j))],
        out_specs=pl.BlockSpec((None, rows, tn), lambda j: (0, 0, j)),
        out_shape=jax.ShapeDtypeStruct((1, rows, cols), BF16),
        compiler_params=_compiler_params(1, 40),
        name="cast_layer",
    )(w)


def _layer_weight_spec(layer, rows, tn, col_of):
    return pl.BlockSpec((None, rows, tn), lambda *g: (layer, 0, col_of(*g)))


def _rms_normalize(x, g):
    r = lax.rsqrt(jnp.mean(x * x, axis=-1, keepdims=True) + NORM_EPS)
    return x * r * g


def _norm_matmul_kernel(x_ref, g_ref, w_ref, o_ref, h_ref):
    @pl.when(pl.program_id(1) == 0)
    def _():
        h_ref[...] = _rms_normalize(x_ref[...], g_ref[...]).astype(BF16)

    o_ref[...] = jnp.dot(h_ref[...], w_ref[...], preferred_element_type=F32)


def norm_matmul(x, g, w, layer, *, tm=1024, tn=1024):
    n, d = x.shape
    cols = w.shape[2]
    assert n % tm == 0 and cols % tn == 0
    return pl.pallas_call(
        _norm_matmul_kernel,
        grid=(n // tm, cols // tn),
        in_specs=[pl.BlockSpec((tm, d), lambda i, j: (i, 0)),
                  pl.BlockSpec((1, d), lambda i, j: (0, 0)),
                  _layer_weight_spec(layer, d, tn, lambda i, j: j)],
        out_specs=pl.BlockSpec((tm, tn), lambda i, j: (i, j)),
        out_shape=jax.ShapeDtypeStruct((n, cols), F32),
        scratch_shapes=[pltpu.VMEM((tm, d), BF16)],
        compiler_params=_compiler_params(2, 48),
        name="norm_inproj",
    )(x, g.reshape(1, d), w)


def _ffn_up_kernel(x_ref, g_ref, wg_ref, wu_ref, o_ref, h_ref):
    @pl.when(pl.program_id(1) == 0)
    def _():
        h_ref[...] = _rms_normalize(x_ref[...], g_ref[...]).astype(BF16)

    h = h_ref[...]
    gate = jnp.dot(h, wg_ref[...], preferred_element_type=F32)
    up = jnp.dot(h, wu_ref[...], preferred_element_type=F32)
    o_ref[...] = (_silu(gate) * up).astype(BF16)


def ffn_up(x, g, wg, wu, layer, *, tm=1024, tn=512):
    n, d = x.shape
    hidden = wg.shape[2]
    assert n % tm == 0 and hidden % tn == 0
    w_spec = _layer_weight_spec(layer, d, tn, lambda i, j: j)
    return pl.pallas_call(
        _ffn_up_kernel,
        grid=(n // tm, hidden // tn),
        in_specs=[pl.BlockSpec((tm, d), lambda i, j: (i, 0)),
                  pl.BlockSpec((1, d), lambda i, j: (0, 0)),
                  w_spec, w_spec],
        out_specs=pl.BlockSpec((tm, tn), lambda i, j: (i, j)),
        out_shape=jax.ShapeDtypeStruct((n, hidden), BF16),
        scratch_shapes=[pltpu.VMEM((tm, d), BF16)],
        compiler_params=_compiler_params(2, 48),
        name="ffn_up",
    )(x, g.reshape(1, d), wg, wu)


def _ffn_down_kernel(a_ref, w_ref, r_ref, g_ref, o_ref, *, normalize):
    out = r_ref[...] + jnp.dot(a_ref[...], w_ref[...], preferred_element_type=F32)
    o_ref[...] = _rms_normalize(out, g_ref[...]) if normalize else out


def ffn_down(a, w, res, layer, gain, *, normalize, tm=512):
    n, k = a.shape
    cols = w.shape[2]
    assert n % tm == 0
    row_spec = pl.BlockSpec((tm, cols), lambda i: (i, 0))
    return pl.pallas_call(
        functools.partial(_ffn_down_kernel, normalize=normalize),
        grid=(n // tm,),
        in_specs=[pl.BlockSpec((tm, k), lambda i: (i, 0)),
                  pl.BlockSpec((None, k, cols), lambda i: (layer, 0, 0),
                               pipeline_mode=pl.Buffered(1)),
                  row_spec,
                  pl.BlockSpec((1, cols), lambda i: (0, 0))],
        out_specs=row_spec,
        out_shape=jax.ShapeDtypeStruct((n, cols), F32),
        compiler_params=_compiler_params(1, 54),
        name="ffn_down",
    )(a, w, res, gain.reshape(1, cols))


MERGE_GATE_TILE = 1024


def _merge_out_kernel(*refs):
    n_gate = D_MODEL // MERGE_GATE_TILE
    branch_refs = refs[:3]
    gate_refs = refs[3:3 + 3 * n_gate]
    wb_refs = refs[3 + 3 * n_gate:6 + 3 * n_gate]
    wo_ref, res_ref, o_ref = refs[6 + 3 * n_gate:]
    parts = []
    for c in range(n_gate):
        cols = slice(c * MERGE_GATE_TILE, (c + 1) * MERGE_GATE_TILE)
        mixed = None
        for b in range(3):
            term = jax.nn.sigmoid(gate_refs[b * n_gate + c][...]) * jnp.dot(
                branch_refs[b][...], wb_refs[b][:, cols], preferred_element_type=F32)
            mixed = term if mixed is None else mixed + term
        parts.append(mixed.astype(BF16))
    mixed = jnp.concatenate(parts, axis=1)
    o_ref[...] = res_ref[...] + jnp.dot(mixed, wo_ref[...], preferred_element_type=F32)


def merge_out_proj(ret, moba, hgrn, proj, wr, wm, wh, wo, res, layer, *, tm=256):
    n = ret.shape[0]
    assert n % tm == 0 and D_MODEL % MERGE_GATE_TILE == 0
    assert all(off % MERGE_GATE_TILE == 0 for off in (OFF_GR, OFF_GM, OFF_GH))
    n_gate = D_MODEL // MERGE_GATE_TILE

    def branch_spec(width):
        return pl.BlockSpec((tm, width), lambda i: (i, 0))

    def gate_specs(off):
        return [pl.BlockSpec((tm, MERGE_GATE_TILE),
                             functools.partial(lambda i, col: (i, col), col=off // MERGE_GATE_TILE + c))
                for c in range(n_gate)]

    def weight_spec(rows):
        return pl.BlockSpec((None, rows, D_MODEL), lambda i: (layer, 0, 0),
                            pipeline_mode=pl.Buffered(1))

    row_spec = pl.BlockSpec((tm, D_MODEL), lambda i: (i, 0))
    return pl.pallas_call(
        _merge_out_kernel,
        grid=(n // tm,),
        in_specs=[branch_spec(RET_WIDTH), branch_spec(MOBA_WIDTH), branch_spec(HGRN_WIDTH),
                  *gate_specs(OFF_GR), *gate_specs(OFF_GM), *gate_specs(OFF_GH),
                  weight_spec(RET_WIDTH), weight_spec(MOBA_WIDTH), weight_spec(HGRN_WIDTH),
                  weight_spec(D_MODEL), row_spec],
        out_specs=row_spec,
        out_shape=jax.ShapeDtypeStruct((n, D_MODEL), F32),
        compiler_params=_compiler_params(1, 52),
        name="merge_out_proj",
    )(ret, moba, hgrn, *([proj] * (3 * n_gate)), wr, wm, wh, wo, res)


def _retention_kernel(lg_ref, q_ref, k_ref, v_ref, g_ref, cos_ref, sin_ref, nrm_ref,
                      o_ref, state_ref, *, chunks):
    c_len = RET_CHUNK
    d = RET_HEAD_DIM
    half = d // 2

    @pl.when(pl.program_id(1) == 0)
    def _():
        state_ref[...] = jnp.zeros_like(state_ref)

    t_col = lax.broadcasted_iota(jnp.int32, (c_len, 1), 0).astype(F32)
    rel = (lax.broadcasted_iota(jnp.int32, (c_len, c_len), 0)
           - lax.broadcasted_iota(jnp.int32, (c_len, c_len), 1)).astype(F32)

    def rotate(x, cos, sin):
        x1 = x[:, :half]
        x2 = x[:, half:]
        return jnp.concatenate([x1 * cos - x2 * sin, x2 * cos + x1 * sin], axis=1)

    for h in range(RET_HEADS):
        cols = slice(h * d, (h + 1) * d)
        lg = lg_ref[h]
        decay = jnp.where(rel >= 0, jnp.exp(lg * jnp.maximum(rel, 0.0)), 0.0)
        q_weight = jnp.exp(lg * (t_col + 1.0))
        k_weight = jnp.exp(lg * (c_len - 1.0 - t_col))
        chunk_decay = jnp.exp(jnp.full((1, d), lg * c_len, F32))
        for c in range(chunks):
            rows = slice(c * c_len, (c + 1) * c_len)
            cos = cos_ref[rows, :]
            sin = sin_ref[rows, :]
            q = rotate(q_ref[rows, cols], cos, sin)
            k = rotate(k_ref[rows, cols], cos, sin) * (d ** -0.5)
            v = v_ref[rows, cols].astype(BF16)
            state = state_ref[h]

            scores = lax.dot_general(q.astype(BF16), k.astype(BF16), NT_DIMS,
                                     preferred_element_type=F32) * decay
            intra = jnp.dot(scores.astype(BF16), v, preferred_element_type=F32)
            cross = jnp.dot((q * q_weight).astype(BF16), state.astype(BF16),
                            preferred_element_type=F32)
            kv = lax.dot_general((k * k_weight).astype(BF16), v, TN_DIMS,
                                 preferred_element_type=F32)
            state_ref[h] = chunk_decay * state + kv

            out = intra + cross
            centered = out - jnp.mean(out, axis=-1, keepdims=True)
            normed = centered * lax.rsqrt(
                jnp.mean(centered * centered, axis=-1, keepdims=True) + NORM_EPS)
            gated = (normed * nrm_ref[:, cols]) * _silu(g_ref[rows, cols])
            o_ref[rows, cols] = gated.astype(BF16)


def retention_mixer(proj, ret_norm, batch, seq, *, rows_per_step=512):
    n = proj.shape[0]
    d = RET_HEAD_DIM
    assert seq % rows_per_step == 0 and rows_per_step % RET_CHUNK == 0
    steps = seq // rows_per_step
    log_gamma = jnp.log1p(-jnp.exp2(-5.0 - jnp.arange(RET_HEADS, dtype=F32)))
    half = d // 2
    inv_freq = RET_ROPE_BASE ** (-jnp.arange(half, dtype=F32) * 2.0 / d)
    ang = jnp.arange(seq).astype(F32)[:, None] * inv_freq[None, :]
    cos, sin = jnp.cos(ang), jnp.sin(ang)

    def col_spec(off):
        return pl.BlockSpec((rows_per_step, RET_WIDTH),
                            lambda b, s: (b * steps + s, off // RET_WIDTH))

    table_spec = pl.BlockSpec((rows_per_step, half), lambda b, s: (s, 0))
    return pl.pallas_call(
        functools.partial(_retention_kernel, chunks=rows_per_step // RET_CHUNK),
        grid=(batch, steps),
        in_specs=[pl.BlockSpec(memory_space=pltpu.SMEM),
                  col_spec(OFF_RQ), col_spec(OFF_RK), col_spec(OFF_RV), col_spec(OFF_RG),
                  table_spec, table_spec,
                  pl.BlockSpec((1, RET_WIDTH), lambda b, s: (0, 0))],
        out_specs=pl.BlockSpec((rows_per_step, RET_WIDTH), lambda b, s: (b * steps + s, 0)),
        out_shape=jax.ShapeDtypeStruct((n, RET_WIDTH), BF16),
        scratch_shapes=[pltpu.VMEM((RET_HEADS, d, d), F32)],
        compiler_params=_compiler_params(2, 40),
        name="retention",
    )(log_gamma, proj, proj, proj, proj, cos, sin, ret_norm.reshape(1, RET_WIDTH))


MOBA_PAIRS_PER_STEP = 2


def _moba_rotate(x, cos, sin_lo, sin_hi):
    half = ROPE_DIM // 2
    return (x * cos + pltpu.roll(x, MOBA_HEAD_DIM - half, 1) * sin_lo
            + pltpu.roll(x, half, 1) * sin_hi)


def _split_dot(a, b):
    a_hi = a.astype(BF16)
    a_lo = (a - a_hi.astype(F32)).astype(BF16)
    b_hi = b.astype(BF16)
    b_lo = (b - b_hi.astype(F32)).astype(BF16)
    return (jnp.dot(a_hi, b_hi, preferred_element_type=F32)
            + jnp.dot(a_hi, b_lo, preferred_element_type=F32)
            + jnp.dot(a_lo, b_hi, preferred_element_type=F32))


def _moba_kernel(*refs, n_blocks, n_side):
    q_ref, k_ref, v_ref, cos_ref, slo_ref, shi_ref = refs[:6]
    o_ref = refs[6 + n_side]
    (kaug_ref, vt_ref, kmean_ref, qt_ref, s_ref, acc_ref,
     snap_ref) = refs[7 + 2 * n_side:]
    _run_side_casts(refs[6:6 + n_side], refs[7 + n_side:7 + 2 * n_side])
    blk = MOBA_BLOCK
    d = MOBA_HEAD_DIM
    step = pl.program_id(2)
    n_past = n_blocks - 1
    pairs = snap_ref.shape[0]
    gate_rows = kmean_ref.shape[0]
    groups = blk // SUBLANES
    q_scale = (d ** -0.5) * 1.4426950408889634

    def rotate(ref, start):
        rows = pl.ds(start, blk)
        return _moba_rotate(ref[rows, :], cos_ref[rows, :], slo_ref[rows, :], shi_ref[rows, :])

    @pl.when(step == 0)
    def _():
        lane = lax.broadcasted_iota(jnp.int32, (blk, LANES), 1)
        kmean_ref[...] = jnp.zeros_like(kmean_ref)
        for j in range(n_blocks):
            kr = rotate(k_ref, j * blk)
            kaug_ref[j, :, :d] = kr.astype(BF16)
            kaug_ref[j, :, d:] = jnp.where(lane == j, 1.0, 0.0).astype(BF16)
            vt_ref[j] = v_ref[j * blk:(j + 1) * blk, :].T.astype(BF16)
            kmean_ref[j:j + 1, :] = jnp.mean(kr, axis=0, keepdims=True)
        row = lax.broadcasted_iota(jnp.int32, (gate_rows, blk), 0)
        row_f = row.astype(F32)
        pad = jnp.zeros((LANES - gate_rows, blk), F32)
        for i in range(n_blocks):
            q_t = rotate(q_ref, i * blk).T
            gate = _split_dot(kmean_ref[...], q_t)
            g = jnp.where(row < i, gate, -jnp.inf)
            bias = jnp.full((gate_rows, blk), MASK_VALUE, F32)
            for _ in range(min(MOBA_TOPK, n_blocks)):
                best = jnp.max(g, axis=0, keepdims=True)
                hit = (g == best) & (g > -jnp.inf)
                first = jnp.min(jnp.where(hit, row_f, float(LANES)), axis=0, keepdims=True)
                pick = row_f == first
                bias = jnp.where(pick, 0.0, bias)
                g = jnp.where(pick, -jnp.inf, g)
            bias = jnp.where(row == i, 0.0, bias)
            qt_ref[i] = jnp.concatenate([q_t * q_scale, bias, pad], axis=0).astype(BF16)

    def group_reduce(x, op):
        return op(x.reshape(groups, SUBLANES, blk), axis=0)

    def score_own_block(i, own_slot):
        s_t = jnp.dot(kaug_ref[i], qt_ref[i], preferred_element_type=F32)
        key = lax.broadcasted_iota(jnp.int32, (blk, blk), 0)
        qry = lax.broadcasted_iota(jnp.int32, (blk, blk), 1)
        s_t = jnp.where(key <= qry, s_t, MASK_VALUE)
        s_ref[own_slot] = s_t
        return group_reduce(s_t, jnp.max)

    def tile_of(t, p):
        in_b = t >= p
        return in_b, jnp.where(in_b, t - p, t)

    def col_max(m8):
        return jnp.broadcast_to(jnp.max(m8, axis=0, keepdims=True), (SUBLANES, blk))

    def weights(slot, m8):
        p_t = jnp.exp2(s_ref[slot].reshape(groups, SUBLANES, blk) - m8[None])
        return jnp.sum(p_t, axis=0), p_t.reshape(blk, blk).astype(BF16)

    def values(j, p_bf):
        return jnp.dot(vt_ref[j], p_bf, preferred_element_type=F32)

    never = jnp.full((SUBLANES, blk), -jnp.inf, F32)
    scored = []
    for k in range(pairs):
        p = step * pairs + k
        base = k * (n_past + 2)
        i_a, i_b = p, n_past - p
        m_a = score_own_block(i_a, base + n_past)
        m_b = score_own_block(i_b, base + n_past + 1)
        for t in range(n_past):
            in_b, j = tile_of(t, p)
            s_t = jnp.dot(kaug_ref[j], qt_ref[jnp.where(in_b, i_b, i_a)],
                          preferred_element_type=F32)
            s_ref[base + t] = s_t
            gm = group_reduce(s_t, jnp.max)
            m_a = jnp.maximum(m_a, jnp.where(in_b, never, gm))
            m_b = jnp.maximum(m_b, jnp.where(in_b, gm, never))
        scored.append((p, base, i_a, i_b, col_max(m_a), col_max(m_b)))

    for k, (p, base, i_a, i_b, m_a, m_b) in enumerate(scored):
        l_a, p_own = weights(base + n_past, m_a)
        acc_ref[2 * k] = values(i_a, p_own)
        l_b, p_own = weights(base + n_past + 1, m_b)
        acc_ref[2 * k + 1] = values(i_b, p_own)
        acc = jnp.zeros((d, blk), F32)
        for t in range(n_past):
            in_b, j = tile_of(t, p)
            l_part, p_bf = weights(base + t, jnp.where(in_b, m_b, m_a))
            acc = acc * jnp.where(t == p, 0.0, 1.0) + values(j, p_bf)
            l_a = l_a + jnp.where(in_b, 0.0, l_part)
            l_b = l_b + jnp.where(in_b, l_part, 0.0)
            if t < snap_ref.shape[1]:
                snap_ref[k, t] = acc
        past_a = snap_ref[k, jnp.maximum(p - 1, 0)] * jnp.where(p > 0, 1.0, 0.0)
        for w, i, l8, past in ((2 * k, i_a, l_a, past_a), (2 * k + 1, i_b, l_b, acc)):
            l = jnp.sum(l8, axis=0, keepdims=True)
            out_t = (acc_ref[w] + past) / l
            o_ref[pl.ds(pl.multiple_of(i * blk, blk), blk), :] = out_t.T.astype(BF16)


def moba_grid_steps(batch, seq):
    return batch * MOBA_HEADS * (seq // MOBA_BLOCK // 2 // MOBA_PAIRS_PER_STEP)


def moba_mixer(proj, batch, seq, side_casts=()):
    n = proj.shape[0]
    d = MOBA_HEAD_DIM
    blk = MOBA_BLOCK
    assert seq % blk == 0
    n_blocks = seq // blk
    assert n_blocks <= LANES and n_blocks % 2 == 0
    half = ROPE_DIM // 2
    inv_freq = ROPE_THETA ** (-jnp.arange(half, dtype=F32) * 2.0 / ROPE_DIM)
    ang = jnp.arange(seq).astype(F32)[:, None] * inv_freq[None, :]
    cos, sin = jnp.cos(ang), jnp.sin(ang)
    zeros = jnp.zeros((seq, d - ROPE_DIM), F32)
    zero_half = jnp.zeros((seq, half), F32)
    cos_t = jnp.concatenate([cos, cos, jnp.ones_like(zeros)], axis=1)
    sin_lo = jnp.concatenate([-sin, zero_half, zeros], axis=1)
    sin_hi = jnp.concatenate([zero_half, sin, zeros], axis=1)

    def col_spec(off):
        return pl.BlockSpec((seq, d), lambda b, h, p: (b, off // d + h))

    table_spec = pl.BlockSpec((seq, d), lambda b, h, p: (0, 0))
    gate_rows = -(-n_blocks // SUBLANES) * SUBLANES
    pairs = MOBA_PAIRS_PER_STEP
    assert (n_blocks // 2) % pairs == 0
    n_snap = max(n_blocks // 2 - 1, 1)
    steps = n_blocks // 2 // pairs

    def step_of(b, h, p):
        return (b * MOBA_HEADS + h) * steps + p

    out, *casts = pl.pallas_call(
        functools.partial(_moba_kernel, n_blocks=n_blocks, n_side=len(side_casts)),
        grid=(batch, MOBA_HEADS, steps),
        in_specs=[col_spec(OFF_MQ), col_spec(OFF_MK), col_spec(OFF_MV),
                  table_spec, table_spec, table_spec,
                  *[c.in_spec(step_of) for c in side_casts]],
        out_specs=[pl.BlockSpec((seq, d), lambda b, h, p: (b, h)),
                   *[c.out_spec(step_of) for c in side_casts]],
        out_shape=[jax.ShapeDtypeStruct((n, MOBA_WIDTH), BF16),
                   *[c.out_shape() for c in side_casts]],
        scratch_shapes=[pltpu.VMEM((n_blocks, blk, 2 * d), BF16),
                        pltpu.VMEM((n_blocks, d, blk), BF16),
                        pltpu.VMEM((gate_rows, d), F32),
                        pltpu.VMEM((n_blocks, 2 * d, blk), BF16),
                        pltpu.VMEM((pairs * (n_blocks + 1), blk, blk), F32),
                        pltpu.VMEM((2 * pairs, d, blk), F32),
                        pltpu.VMEM((pairs, n_snap, d, blk), F32)],
        compiler_params=_compiler_params(3, 54),
        name="moba",
    )(proj, proj, proj, cos_t, sin_lo, sin_hi, *[c.operand for c in side_casts])
    return out, [c.finish(x) for c, x in zip(side_casts, casts)]


def _hgrn_pair_codes(block):
    i = np.arange(block)[:, None]
    j = np.arange(block)[None, :]
    top_bit = np.floor(np.log2(np.maximum(i ^ j, 1))).astype(np.int32)
    n_levels = block.bit_length() - 1
    return np.where(i > j, top_bit, np.where(i == j, n_levels, -1)).astype(np.int32)


def _hgrn_kernel(*refs, layer, rows_per_step, n_side):
    lbp_ref, tril_ref, code_ref, q_ref, f_ref, v_ref, g_ref, nrm_ref = refs[:8]
    o_ref = refs[8 + n_side]
    state_ref, b_ref, k_ref = refs[9 + 2 * n_side:]
    _run_side_casts(refs[8:8 + n_side], refs[9 + n_side:9 + 2 * n_side])
    c_len = HGRN_BLOCK
    d = HGRN_HEAD_DIM

    @pl.when(pl.program_id(2) == 0)
    def _():
        state_ref[...] = jnp.zeros_like(state_ref)

    params = lbp_ref[...]
    e = jnp.exp(params - jnp.max(params, axis=0, keepdims=True))
    soft = e / jnp.sum(e, axis=0, keepdims=True)
    lower = jnp.zeros((1, d), F32)
    for r in range(1, layer + 1):
        lower = lower + soft[r:r + 1, :]

    tril = tril_ref[...]
    for seg0 in range(0, rows_per_step, c_len):
        seg = slice(seg0, seg0 + c_len)
        forget = lower + (1.0 - lower) * jax.nn.sigmoid(f_ref[seg, :])
        k_ref[seg, :] = 1.0 - forget
        log_f = jnp.log2(forget)
        hi = log_f.astype(BF16)
        rest = log_f - hi.astype(F32)
        mid = rest.astype(BF16)
        lo = (rest - mid.astype(F32)).astype(BF16)
        b_ref[seg, :] = (jnp.dot(tril, hi, preferred_element_type=F32)
                         + jnp.dot(tril, mid, preferred_element_type=F32)
                         + jnp.dot(tril, lo, preferred_element_type=F32))

    assert c_len & (c_len - 1) == 0 and c_len % LANES == 0
    levels = [1 << s for s in range(c_len.bit_length() - 1)]
    n_grp = c_len // SUBLANES
    row8 = lax.broadcasted_iota(jnp.int32, (SUBLANES, d), 0)
    zero8 = jnp.zeros((SUBLANES, d), F32)
    codes = _hgrn_pair_codes(c_len)
    tiles = [(g, cb) for g in range(n_grp) for cb in range(c_len // LANES)]

    def code_tile(g, cb):
        return codes[SUBLANES * g:SUBLANES * (g + 1), LANES * cb:LANES * (cb + 1)]

    level_tiles = [[(g, cb, bool(np.all(code_tile(g, cb) == bit))) for g, cb in tiles
                    if np.any(code_tile(g, cb) == bit)] for bit in range(len(levels) + 1)]

    def bcast_row(ref, r):
        return jnp.broadcast_to(ref[r:r + 1, :], (SUBLANES, d))

    def level_operands(r0, h, bg, qg, kg):
        q_parts, k_parts = [], []
        for g in range(n_grp):
            base = SUBLANES * g
            if h >= SUBLANES:
                b_a = bcast_row(b_ref, r0 + base // (2 * h) * (2 * h) + h - 1)
                if base & h:
                    q_parts.append(qg[g] * jnp.exp2(bg[g] - b_a))
                    k_parts.append(zero8)
                else:
                    q_parts.append(zero8)
                    k_parts.append(kg[g] * jnp.exp2(b_a - bg[g]))
                continue
            upper = (row8 & h) != 0
            if h == 1:
                q_parts.append(jnp.where(upper, qg[g] * (1.0 - kg[g]), 0.0))
                k_parts.append(jnp.where(upper, 0.0, kg[g]))
                continue
            b_a = bcast_row(b_ref, r0 + base + h - 1)
            for s in range(2 * h, SUBLANES, 2 * h):
                b_a = jnp.where(row8 >= s, bcast_row(b_ref, r0 + base + s + h - 1), b_a)
            decay = jnp.exp2(-jnp.abs(bg[g] - b_a))
            q_parts.append(jnp.where(upper, qg[g] * decay, 0.0))
            k_parts.append(jnp.where(upper, 0.0, kg[g] * decay))
        return (jnp.concatenate(q_parts, axis=0).astype(BF16),
                jnp.concatenate(k_parts, axis=0).astype(BF16))

    for c in range(rows_per_step // c_len):
        r0 = c * c_len
        rows = slice(r0, r0 + c_len)
        b = b_ref[rows, :]
        q = _silu(q_ref[rows, :])
        k = k_ref[rows, :]
        v = v_ref[rows, :]

        groups = [slice(SUBLANES * g, SUBLANES * (g + 1)) for g in range(n_grp)]
        bg = [b[g, :] for g in groups]
        qg = [q[g, :] for g in groups]
        kg = [k[g, :] for g in groups]
        def code_of(g, cb):
            return code_ref[SUBLANES * g:SUBLANES * (g + 1), LANES * cb:LANES * (cb + 1)]

        a_tiles = {}
        self_weight = jnp.sum(q * k, axis=-1, keepdims=True)
        for g, cb, _ in level_tiles[len(levels)]:
            a_tiles[g, cb] = jnp.where(code_of(g, cb) == len(levels),
                                       self_weight[groups[g], :], 0.0)
        for bit, h in enumerate(levels):
            q_h, k_h = level_operands(r0, h, bg, qg, kg)
            for cb in range(c_len // LANES):
                k_lo = LANES * cb
                if 2 * h >= LANES:
                    start = k_lo // (2 * h) * (2 * h)
                    if k_lo >= start + h:
                        continue
                    q_lo, q_hi = start + h, start + 2 * h
                else:
                    q_lo, q_hi = k_lo, k_lo + LANES
                pair = lax.dot_general(q_h[q_lo:q_hi, :], k_h[k_lo:k_lo + LANES, :], NT_DIMS,
                                       preferred_element_type=F32)
                for g, tile_cb, owns_tile in level_tiles[bit]:
                    if tile_cb != cb:
                        continue
                    assert q_lo <= SUBLANES * g < q_hi
                    piece = pair[SUBLANES * g - q_lo:SUBLANES * (g + 1) - q_lo, :]
                    if owns_tile:
                        a_tiles[g, cb] = piece
                    else:
                        a_tiles[g, cb] = jnp.where(code_of(g, cb) == bit, piece,
                                                   a_tiles.get((g, cb), 0.0))
        zero_tile = jnp.zeros((SUBLANES, LANES), F32)
        a_mat = jnp.concatenate(
            [jnp.concatenate([a_tiles.get((g, cb), zero_tile) for g in range(n_grp)], axis=0)
             for cb in range(c_len // LANES)], axis=1)
        intra = jnp.dot(a_mat.astype(BF16), v.astype(BF16), preferred_element_type=F32)

        state_t = state_ref[...]
        cross = lax.dot_general((q * jnp.exp2(b)).astype(BF16), state_t.astype(BF16), NT_DIMS,
                                preferred_element_type=F32)
        b_last = b_ref[r0 + c_len - 1:r0 + c_len, :]
        k_dec = k * jnp.exp2(b_last - b)
        state_ref[...] = state_t * jnp.exp2(b_last) + lax.dot_general(
            v.astype(BF16), k_dec.astype(BF16), TN_DIMS, preferred_element_type=F32)

        out = intra + cross
        normed = out * lax.rsqrt(jnp.mean(out * out, axis=-1, keepdims=True) + NORM_EPS)
        gated = (normed * nrm_ref[...]) * _silu(g_ref[rows, :])
        o_ref[rows, :] = gated.astype(BF16)


HGRN_ROWS_PER_STEP = 1024


def hgrn_grid_steps(batch, seq):
    return batch * HGRN_HEADS * (seq // min(seq, HGRN_ROWS_PER_STEP))


def hgrn_mixer(proj, lower_bound_params, hgrn_norm, layer, batch, seq, side_casts=()):
    rows_per_step = min(seq, HGRN_ROWS_PER_STEP)
    n = proj.shape[0]
    d = HGRN_HEAD_DIM
    depth = lower_bound_params.shape[0]
    blk = HGRN_BLOCK
    assert seq % rows_per_step == 0 and rows_per_step % blk == 0
    steps = seq // rows_per_step
    codes = _hgrn_pair_codes(blk)
    tril = jnp.asarray(codes >= 0, BF16)

    def col_spec(off):
        return pl.BlockSpec((rows_per_step, d), lambda b, h, s: (b * steps + s, off // d + h))

    def step_of(b, h, s):
        return (b * HGRN_HEADS + h) * steps + s

    out, *casts = pl.pallas_call(
        functools.partial(_hgrn_kernel, layer=layer, rows_per_step=rows_per_step,
                          n_side=len(side_casts)),
        grid=(batch, HGRN_HEADS, steps),
        in_specs=[pl.BlockSpec((depth, d), lambda b, h, s: (0, h)),
                  pl.BlockSpec((blk, blk), lambda b, h, s: (0, 0)),
                  pl.BlockSpec((blk, blk), lambda b, h, s: (0, 0)),
                  col_spec(OFF_HQ), col_spec(OFF_HF), col_spec(OFF_HI), col_spec(OFF_HG),
                  pl.BlockSpec((1, d), lambda b, h, s: (0, h)),
                  *[c.in_spec(step_of) for c in side_casts]],
        out_specs=[pl.BlockSpec((rows_per_step, d), lambda b, h, s: (b * steps + s, h)),
                   *[c.out_spec(step_of) for c in side_casts]],
        out_shape=[jax.ShapeDtypeStruct((n, HGRN_WIDTH), BF16),
                   *[c.out_shape() for c in side_casts]],
        scratch_shapes=[pltpu.VMEM((d, d), F32),
                        pltpu.VMEM((rows_per_step, d), F32),
                        pltpu.VMEM((rows_per_step, d), F32)],
        compiler_params=_compiler_params(3, 40),
        name="hgrn2",
    )(lower_bound_params.astype(F32), tril, jnp.asarray(codes), proj, proj, proj, proj,
      hgrn_norm.reshape(1, HGRN_WIDTH), *[c.operand for c in side_casts])
    return out, [c.finish(x) for c, x in zip(side_casts, casts)]


def kernel(x, hgrn_lower_bounds, norm_mix, w_in, ret_norm, hgrn_norm, w_branch_ret, w_branch_moba,
           w_branch_hgrn, w_out, norm_ffn, w_ffn_gate, w_ffn_up, w_ffn_down, final_norm):
    batch, seq, d_model = x.shape
    assert d_model == D_MODEL
    depth = w_in.shape[0]
    w_in_layers = [cast_layer(w_in, 0)]
    moba_steps = moba_grid_steps(batch, seq)
    hgrn_steps = hgrn_grid_steps(batch, seq)
    moba_casts = [SideCast(w, moba_steps) for w in (w_ffn_gate, w_ffn_up, w_ffn_down)]
    hgrn_casts = [SideCast(w, hgrn_steps)
                  for w in (w_branch_ret, w_branch_moba, w_branch_hgrn, w_out)]
    hgrn_casts += [SideCast(w_in, hgrn_steps, part=l, n_parts=depth) for l in range(1, depth)]
    h = x.reshape(batch * seq, d_model)
    for layer in range(depth):
        proj = norm_matmul(h, norm_mix[layer], w_in_layers[layer], 0)
        ret = retention_mixer(proj, ret_norm[layer], batch, seq)
        moba, cast = moba_mixer(proj, batch, seq, moba_casts if layer == 0 else ())
        if layer == 0:
            w_ffn_gate, w_ffn_up, w_ffn_down = cast
        hgrn, cast = hgrn_mixer(proj, hgrn_lower_bounds, hgrn_norm[layer], layer, batch, seq,
                                hgrn_casts if layer == 0 else ())
        if layer == 0:
            w_branch_ret, w_branch_moba, w_branch_hgrn, w_out = cast[:4]
            w_in_layers += cast[4:]
        h = merge_out_proj(ret, moba, hgrn, proj, w_branch_ret, w_branch_moba, w_branch_hgrn,
                           w_out, h, layer)
        act = ffn_up(h, norm_ffn[layer], w_ffn_gate, w_ffn_up, layer)
        h = ffn_down(act, w_ffn_down, h, layer, final_norm, normalize=layer == depth - 1)
    return h.reshape(batch, seq, d_model)
```

```python
import functools

import jax
import jax.numpy as jnp
import numpy as np
from jax import lax
from jax.experimental import pallas as pl
from jax.experimental.pallas import tpu as pltpu

F32 = jnp.float32
BF16 = jnp.bfloat16

D_MODEL = 2048
RET_HEADS = 4
RET_HEAD_DIM = 256
RET_WIDTH = RET_HEADS * RET_HEAD_DIM
RET_CHUNK = 128
RET_ROPE_BASE = 10000.0
MOBA_HEADS = 8
MOBA_HEAD_DIM = 128
MOBA_WIDTH = MOBA_HEADS * MOBA_HEAD_DIM
MOBA_BLOCK = 256
MOBA_TOPK = 3
ROPE_THETA = 500000.0
ROPE_DIM = MOBA_HEAD_DIM // 4
HGRN_HEADS = 8
HGRN_HEAD_DIM = 128
HGRN_WIDTH = HGRN_HEADS * HGRN_HEAD_DIM
HGRN_BLOCK = 256
NORM_EPS = 1e-6
IN_SIZES = (RET_WIDTH,) * 4 + (MOBA_WIDTH,) * 3 + (HGRN_WIDTH,) * 4 + (D_MODEL,) * 3
IN_COLS = sum(IN_SIZES)
IN_OFFS = tuple(sum(IN_SIZES[:i]) for i in range(len(IN_SIZES)))
(OFF_RQ, OFF_RK, OFF_RV, OFF_RG, OFF_MQ, OFF_MK, OFF_MV,
 OFF_HQ, OFF_HF, OFF_HI, OFF_HG, OFF_GR, OFF_GM, OFF_GH) = IN_OFFS

V7X_VMEM_BYTES = 64 * 1024 * 1024
LANES = 128
SUBLANES = 8
MASK_VALUE = -1e30

NT_DIMS = (((1,), (1,)), ((), ()))
TN_DIMS = (((0,), (0,)), ((), ()))


def _compiler_params(n_grid_dims, vmem_mib):
    assert vmem_mib * 1024 * 1024 < V7X_VMEM_BYTES
    return pltpu.CompilerParams(
        dimension_semantics=("arbitrary",) * n_grid_dims,
        vmem_limit_bytes=vmem_mib * 1024 * 1024)


def _silu(x):
    return x * jax.nn.sigmoid(x)


BF16_SUBLANES = 16


class SideCast:
    def __init__(self, array, n_steps, part=0, n_parts=1):
        layers, rows, self.cols = array.shape
        part_rows = layers // n_parts * rows
        self.steps_per_slab = 1
        while part_rows * self.steps_per_slab % (n_steps * BF16_SUBLANES):
            self.steps_per_slab *= 2
        assert n_steps % self.steps_per_slab == 0
        self.n_slabs = n_steps // self.steps_per_slab
        self.slab_rows = part_rows // self.n_slabs
        self.first_slab = part * self.n_slabs
        self.operand = array.reshape(n_parts * self.n_slabs, self.slab_rows, self.cols)
        self.part_shape = (layers // n_parts, rows, self.cols)

    def in_spec(self, step_of):
        return pl.BlockSpec(
            (None, self.slab_rows, self.cols),
            lambda *g: (self.first_slab + step_of(*g) // self.steps_per_slab, 0, 0))

    def out_spec(self, step_of):
        return pl.BlockSpec((None, self.slab_rows, self.cols),
                            lambda *g: (step_of(*g) // self.steps_per_slab, 0, 0))

    def out_shape(self):
        return jax.ShapeDtypeStruct((self.n_slabs, self.slab_rows, self.cols), BF16)

    def finish(self, cast):
        return cast.reshape(self.part_shape)


def _run_side_casts(in_refs, out_refs):
    for src, dst in zip(in_refs, out_refs):
        dst[...] = src[...].astype(BF16)


def _cast_kernel(src_ref, dst_ref):
    dst_ref[...] = src_ref[...].astype(BF16)


def cast_layer(w, layer, *, tn=1024):
    _, rows, cols = w.shape
    assert cols % tn == 0
    return pl.pallas_call(
        _cast_kernel,
        grid=(cols // tn,),
        in_specs=[pl.BlockSpec((None, rows, tn), lambda j: (layer, 0, j))],
        out_specs=pl.BlockSpec((None, rows, tn), lambda j: (0, 0, j)),
        out_shape=jax.ShapeDtypeStruct((1, rows, cols), BF16),
        compiler_params=_compiler_params(1, 40),
        name="cast_layer",
    )(w)


def _layer_weight_spec(layer, rows, tn, col_of):
    return pl.BlockSpec((None, rows, tn), lambda *g: (layer, 0, col_of(*g)))


def _rms_normalize(x, g):
    r = lax.rsqrt(jnp.mean(x * x, axis=-1, keepdims=True) + NORM_EPS)
    return x * r * g


def _normed_matmul_kernel(*refs, slab, n_j, n_w, combine):
    x_first_ref, x_next_ref, g_ref = refs[:3]
    w_refs = refs[3:3 + n_w]
    o_ref, h_even_ref, h_odd_ref = refs[3 + n_w:]
    tm = h_even_ref.shape[0]
    t = pl.program_id(1)
    j = t % n_j

    @pl.when((pl.program_id(0) == 0) & (t == 0))
    def _():
        h_even_ref[...] = _rms_normalize(x_first_ref[...], g_ref[...]).astype(BF16)

    rows = pl.ds(pl.multiple_of(jnp.minimum(j, tm // slab - 1) * slab, slab), slab)

    def step(h_ref, h_next_ref):
        h_next_ref[rows, :] = _rms_normalize(x_next_ref[rows, :], g_ref[...]).astype(BF16)
        h = h_ref[...]
        products = [jnp.dot(h, w_ref[...], preferred_element_type=F32) for w_ref in w_refs]
        o_ref[...] = combine(*products).astype(o_ref.dtype)

    @pl.when(t < n_j)
    def _():
        step(h_even_ref, h_odd_ref)

    @pl.when(t >= n_j)
    def _():
        step(h_odd_ref, h_even_ref)


def normed_matmul(x, g, weights, layer, combine, out_dtype, *, tm, tn, vmem_mib, name):
    n, d = x.shape
    cols = weights[0].shape[2]
    tm = min(tm, n // 2)
    assert n % (2 * tm) == 0 and cols % tn == 0
    n_i, n_j = n // tm, cols // tn
    slab = tm // min(n_j, tm // BF16_SUBLANES)
    while tm % slab or slab % BF16_SUBLANES:
        slab += 1
    assert slab * n_j >= tm

    def row_tile(ip, t):
        return 2 * ip + t // n_j

    w_spec = _layer_weight_spec(layer, d, tn, lambda ip, t: t % n_j)
    return pl.pallas_call(
        functools.partial(_normed_matmul_kernel, slab=slab, n_j=n_j, n_w=len(weights),
                          combine=combine),
        grid=(n_i // 2, 2 * n_j),
        in_specs=[pl.BlockSpec((tm, d), lambda ip, t: (0, 0), pipeline_mode=pl.Buffered(1)),
                  pl.BlockSpec((tm, d),
                               lambda ip, t: (jnp.minimum(row_tile(ip, t) + 1, n_i - 1), 0)),
                  pl.BlockSpec((1, d), lambda ip, t: (0, 0)),
                  *[w_spec] * len(weights)],
        out_specs=pl.BlockSpec((tm, tn), lambda ip, t: (row_tile(ip, t), t % n_j)),
        out_shape=jax.ShapeDtypeStruct((n, cols), out_dtype),
        scratch_shapes=[pltpu.VMEM((tm, d), BF16), pltpu.VMEM((tm, d), BF16)],
        compiler_params=_compiler_params(2, vmem_mib),
        name=name,
    )(x, x, g.reshape(1, d), *weights)


def norm_matmul(x, g, w, layer):
    return normed_matmul(x, g, [w], layer, lambda y: y, F32,
                         tm=1024, tn=1024, vmem_mib=54, name="norm_inproj")


def ffn_up(x, g, wg, wu, layer):
    return normed_matmul(x, g, [wg, wu], layer, lambda gate, up: _silu(gate) * up, BF16,
                         tm=1024, tn=512, vmem_mib=50, name="ffn_up")


def _ffn_down_kernel(a_ref, w_ref, r_ref, g_ref, o_ref, *, normalize):
    out = r_ref[...] + jnp.dot(a_ref[...], w_ref[...], preferred_element_type=F32)
    o_ref[...] = _rms_normalize(out, g_ref[...]) if normalize else out


def ffn_down(a, w, res, layer, gain, *, normalize, tm=512):
    n, k = a.shape
    cols = w.shape[2]
    assert n % tm == 0
    row_spec = pl.BlockSpec((tm, cols), lambda i: (i, 0))
    return pl.pallas_call(
        functools.partial(_ffn_down_kernel, normalize=normalize),
        grid=(n // tm,),
        in_specs=[pl.BlockSpec((tm, k), lambda i: (i, 0)),
                  pl.BlockSpec((None, k, cols), lambda i: (layer, 0, 0),
                               pipeline_mode=pl.Buffered(1)),
                  row_spec,
                  pl.BlockSpec((1, cols), lambda i: (0, 0))],
        out_specs=row_spec,
        out_shape=jax.ShapeDtypeStruct((n, cols), F32),
        compiler_params=_compiler_params(1, 54),
        name="ffn_down",
    )(a, w, res, gain.reshape(1, cols))


MERGE_GATE_TILE = 1024


def _merge_out_kernel(*refs):
    n_gate = D_MODEL // MERGE_GATE_TILE
    branch_refs = refs[:3]
    gate_refs = refs[3:3 + 3 * n_gate]
    wb_refs = refs[3 + 3 * n_gate:6 + 3 * n_gate]
    wo_ref, res_ref, o_ref = refs[6 + 3 * n_gate:]
    parts = []
    for c in range(n_gate):
        cols = slice(c * MERGE_GATE_TILE, (c + 1) * MERGE_GATE_TILE)
        mixed = None
        for b in range(3):
            term = jax.nn.sigmoid(gate_refs[b * n_gate + c][...]) * jnp.dot(
                branch_refs[b][...], wb_refs[b][:, cols], preferred_element_type=F32)
            mixed = term if mixed is None else mixed + term
        parts.append(mixed.astype(BF16))
    mixed = jnp.concatenate(parts, axis=1)
    o_ref[...] = res_ref[...] + jnp.dot(mixed, wo_ref[...], preferred_element_type=F32)


def merge_out_proj(ret, moba, hgrn, proj, wr, wm, wh, wo, res, layer, *, tm=256):
    n = ret.shape[0]
    assert n % tm == 0 and D_MODEL % MERGE_GATE_TILE == 0
    assert all(off % MERGE_GATE_TILE == 0 for off in (OFF_GR, OFF_GM, OFF_GH))
    n_gate = D_MODEL // MERGE_GATE_TILE

    def branch_spec(width):
        return pl.BlockSpec((tm, width), lambda i: (i, 0))

    def gate_specs(off):
        return [pl.BlockSpec((tm, MERGE_GATE_TILE),
                             functools.partial(lambda i, col: (i, col), col=off // MERGE_GATE_TILE + c))
                for c in range(n_gate)]

    def weight_spec(rows):
        return pl.BlockSpec((None, rows, D_MODEL), lambda i: (layer, 0, 0),
                            pipeline_mode=pl.Buffered(1))

    row_spec = pl.BlockSpec((tm, D_MODEL), lambda i: (i, 0))
    return pl.pallas_call(
        _merge_out_kernel,
        grid=(n // tm,),
        in_specs=[branch_spec(RET_WIDTH), branch_spec(MOBA_WIDTH), branch_spec(HGRN_WIDTH),
                  *gate_specs(OFF_GR), *gate_specs(OFF_GM), *gate_specs(OFF_GH),
                  weight_spec(RET_WIDTH), weight_spec(MOBA_WIDTH), weight_spec(HGRN_WIDTH),
                  weight_spec(D_MODEL), row_spec],
        out_specs=row_spec,
        out_shape=jax.ShapeDtypeStruct((n, D_MODEL), F32),
        compiler_params=_compiler_params(1, 52),
        name="merge_out_proj",
    )(ret, moba, hgrn, *([proj] * (3 * n_gate)), wr, wm, wh, wo, res)


def _retention_kernel(lg_ref, q_ref, k_ref, v_ref, g_ref, cos_ref, sin_ref, nrm_ref,
                      o_ref, state_ref, *, chunks):
    c_len = RET_CHUNK
    d = RET_HEAD_DIM
    half = d // 2

    @pl.when(pl.program_id(1) == 0)
    def _():
        state_ref[...] = jnp.zeros_like(state_ref)

    t_col = lax.broadcasted_iota(jnp.int32, (c_len, 1), 0).astype(F32)
    rel = (lax.broadcasted_iota(jnp.int32, (c_len, c_len), 0)
           - lax.broadcasted_iota(jnp.int32, (c_len, c_len), 1)).astype(F32)

    def rotate(x, cos, sin):
        x1 = x[:, :half]
        x2 = x[:, half:]
        return jnp.concatenate([x1 * cos - x2 * sin, x2 * cos + x1 * sin], axis=1)

    for h in range(RET_HEADS):
        cols = slice(h * d, (h + 1) * d)
        lg = lg_ref[h]
        decay = jnp.where(rel >= 0, jnp.exp(lg * jnp.maximum(rel, 0.0)), 0.0)
        q_weight = jnp.exp(lg * (t_col + 1.0))
        k_weight = jnp.exp(lg * (c_len - 1.0 - t_col))
        chunk_decay = jnp.exp(jnp.full((1, d), lg * c_len, F32))
        for c in range(chunks):
            rows = slice(c * c_len, (c + 1) * c_len)
            cos = cos_ref[rows, :]
            sin = sin_ref[rows, :]
            q = rotate(q_ref[rows, cols], cos, sin)
            k = rotate(k_ref[rows, cols], cos, sin) * (d ** -0.5)
            v = v_ref[rows, cols].astype(BF16)
            state = state_ref[h]

            scores = lax.dot_general(q.astype(BF16), k.astype(BF16), NT_DIMS,
                                     preferred_element_type=F32) * decay
            intra = jnp.dot(scores.astype(BF16), v, preferred_element_type=F32)
            cross = jnp.dot((q * q_weight).astype(BF16), state.astype(BF16),
                            preferred_element_type=F32)
            kv = lax.dot_general((k * k_weight).astype(BF16), v, TN_DIMS,
                                 preferred_element_type=F32)
            state_ref[h] = chunk_decay * state + kv

            out = intra + cross
            centered = out - jnp.mean(out, axis=-1, keepdims=True)
            normed = centered * lax.rsqrt(
                jnp.mean(centered * centered, axis=-1, keepdims=True) + NORM_EPS)
            gated = (normed * nrm_ref[:, cols]) * _silu(g_ref[rows, cols])
            o_ref[rows, cols] = gated.astype(BF16)


def retention_mixer(proj, ret_norm, batch, seq, *, rows_per_step=512):
    n = proj.shape[0]
    d = RET_HEAD_DIM
    assert seq % rows_per_step == 0 and rows_per_step % RET_CHUNK == 0
    steps = seq // rows_per_step
    log_gamma = jnp.log1p(-jnp.exp2(-5.0 - jnp.arange(RET_HEADS, dtype=F32)))
    half = d // 2
    inv_freq = RET_ROPE_BASE ** (-jnp.arange(half, dtype=F32) * 2.0 / d)
    ang = jnp.arange(seq).astype(F32)[:, None] * inv_freq[None, :]
    cos, sin = jnp.cos(ang), jnp.sin(ang)

    def col_spec(off):
        return pl.BlockSpec((rows_per_step, RET_WIDTH),
                            lambda b, s: (b * steps + s, off // RET_WIDTH))

    table_spec = pl.BlockSpec((rows_per_step, half), lambda b, s: (s, 0))
    return pl.pallas_call(
        functools.partial(_retention_kernel, chunks=rows_per_step // RET_CHUNK),
        grid=(batch, steps),
        in_specs=[pl.BlockSpec(memory_space=pltpu.SMEM),
                  col_spec(OFF_RQ), col_spec(OFF_RK), col_spec(OFF_RV), col_spec(OFF_RG),
                  table_spec, table_spec,
                  pl.BlockSpec((1, RET_WIDTH), lambda b, s: (0, 0))],
        out_specs=pl.BlockSpec((rows_per_step, RET_WIDTH), lambda b, s: (b * steps + s, 0)),
        out_shape=jax.ShapeDtypeStruct((n, RET_WIDTH), BF16),
        scratch_shapes=[pltpu.VMEM((RET_HEADS, d, d), F32)],
        compiler_params=_compiler_params(2, 40),
        name="retention",
    )(log_gamma, proj, proj, proj, proj, cos, sin, ret_norm.reshape(1, RET_WIDTH))


MOBA_PAIRS_PER_STEP = 2


def _moba_rotate(x, cos, sin_lo, sin_hi):
    half = ROPE_DIM // 2
    return (x * cos + pltpu.roll(x, MOBA_HEAD_DIM - half, 1) * sin_lo
            + pltpu.roll(x, half, 1) * sin_hi)


def _split_dot(a, b):
    a_hi = a.astype(BF16)
    a_lo = (a - a_hi.astype(F32)).astype(BF16)
    b_hi = b.astype(BF16)
    b_lo = (b - b_hi.astype(F32)).astype(BF16)
    return (jnp.dot(a_hi, b_hi, preferred_element_type=F32)
            + jnp.dot(a_hi, b_lo, preferred_element_type=F32)
            + jnp.dot(a_lo, b_hi, preferred_element_type=F32))


def _moba_kernel(*refs, n_blocks, n_side):
    q_ref, k_ref, v_ref, cos_ref, slo_ref, shi_ref = refs[:6]
    o_ref = refs[6 + n_side]
    (kaug_ref, vt_ref, kmean_ref, qt_ref, s_ref, acc_ref,
     snap_ref) = refs[7 + 2 * n_side:]
    _run_side_casts(refs[6:6 + n_side], refs[7 + n_side:7 + 2 * n_side])
    blk = MOBA_BLOCK
    d = MOBA_HEAD_DIM
    step = pl.program_id(2)
    n_past = n_blocks - 1
    pairs = snap_ref.shape[0]
    gate_rows = kmean_ref.shape[0]
    groups = blk // SUBLANES
    q_scale = (d ** -0.5) * 1.4426950408889634

    def rotate(ref, start):
        rows = pl.ds(start, blk)
        return _moba_rotate(ref[rows, :], cos_ref[rows, :], slo_ref[rows, :], shi_ref[rows, :])

    @pl.when(step == 0)
    def _():
        lane = lax.broadcasted_iota(jnp.int32, (blk, LANES), 1)
        kmean_ref[...] = jnp.zeros_like(kmean_ref)
        for j in range(n_blocks):
            kr = rotate(k_ref, j * blk)
            kaug_ref[j, :, :d] = kr.astype(BF16)
            kaug_ref[j, :, d:] = jnp.where(lane == j, 1.0, 0.0).astype(BF16)
            vt_ref[j] = v_ref[j * blk:(j + 1) * blk, :].T.astype(BF16)
            kmean_ref[j:j + 1, :] = jnp.mean(kr, axis=0, keepdims=True)
        row = lax.broadcasted_iota(jnp.int32, (gate_rows, blk), 0)
        row_f = row.astype(F32)
        pad = jnp.zeros((LANES - gate_rows, blk), F32)
        for i in range(n_blocks):
            q_t = rotate(q_ref, i * blk).T
            gate = _split_dot(kmean_ref[...], q_t)
            g = jnp.where(row < i, gate, -jnp.inf)
            bias = jnp.full((gate_rows, blk), MASK_VALUE, F32)
            for _ in range(min(MOBA_TOPK, n_blocks)):
                best = jnp.max(g, axis=0, keepdims=True)
                hit = (g == best) & (g > -jnp.inf)
                first = jnp.min(jnp.where(hit, row_f, float(LANES)), axis=0, keepdims=True)
                pick = row_f == first
                bias = jnp.where(pick, 0.0, bias)
                g = jnp.where(pick, -jnp.inf, g)
            bias = jnp.where(row == i, 0.0, bias)
            qt_ref[i] = jnp.concatenate([q_t * q_scale, bias, pad], axis=0).astype(BF16)

    def group_reduce(x, op):
        return op(x.reshape(groups, SUBLANES, blk), axis=0)

    def score_own_block(i, own_slot):
        s_t = jnp.dot(kaug_ref[i], qt_ref[i], preferred_element_type=F32)
        key = lax.broadcasted_iota(jnp.int32, (blk, blk), 0)
        qry = lax.broadcasted_iota(jnp.int32, (blk, blk), 1)
        s_t = jnp.where(key <= qry, s_t, MASK_VALUE)
        s_ref[own_slot] = s_t
        return group_reduce(s_t, jnp.max)

    def tile_of(t, p):
        in_b = t >= p
        return in_b, jnp.where(in_b, t - p, t)

    def col_max(m8):
        return jnp.broadcast_to(jnp.max(m8, axis=0, keepdims=True), (SUBLANES, blk))

    def weights(slot, m8):
        p_t = jnp.exp2(s_ref[slot].reshape(groups, SUBLANES, blk) - m8[None])
        return jnp.sum(p_t, axis=0), p_t.reshape(blk, blk).astype(BF16)

    def values(j, p_bf):
        return jnp.dot(vt_ref[j], p_bf, preferred_element_type=F32)

    never = jnp.full((SUBLANES, blk), -jnp.inf, F32)
    scored = []
    for k in range(pairs):
        p = step * pairs + k
        base = k * (n_past + 2)
        i_a, i_b = p, n_past - p
        m_a = score_own_block(i_a, base + n_past)
        m_b = score_own_block(i_b, base + n_past + 1)
        for t in range(n_past):
            in_b, j = tile_of(t, p)
            s_t = jnp.dot(kaug_ref[j], qt_ref[jnp.where(in_b, i_b, i_a)],
                          preferred_element_type=F32)
            s_ref[base + t] = s_t
            gm = group_reduce(s_t, jnp.max)
            m_a = jnp.maximum(m_a, jnp.where(in_b, never, gm))
            m_b = jnp.maximum(m_b, jnp.where(in_b, gm, never))
        scored.append((p, base, i_a, i_b, col_max(m_a), col_max(m_b)))

    for k, (p, base, i_a, i_b, m_a, m_b) in enumerate(scored):
        l_a, p_own = weights(base + n_past, m_a)
        acc_ref[2 * k] = values(i_a, p_own)
        l_b, p_own = weights(base + n_past + 1, m_b)
        acc_ref[2 * k + 1] = values(i_b, p_own)
        acc = jnp.zeros((d, blk), F32)
        for t in range(n_past):
            in_b, j = tile_of(t, p)
            l_part, p_bf = weights(base + t, jnp.where(in_b, m_b, m_a))
            acc = acc * jnp.where(t == p, 0.0, 1.0) + values(j, p_bf)
            l_a = l_a + jnp.where(in_b, 0.0, l_part)
            l_b = l_b + jnp.where(in_b, l_part, 0.0)
            if t < snap_ref.shape[1]:
                snap_ref[k, t] = acc
        past_a = snap_ref[k, jnp.maximum(p - 1, 0)] * jnp.where(p > 0, 1.0, 0.0)
        for w, i, l8, past in ((2 * k, i_a, l_a, past_a), (2 * k + 1, i_b, l_b, acc)):
            l = jnp.sum(l8, axis=0, keepdims=True)
            out_t = (acc_ref[w] + past) / l
            o_ref[pl.ds(pl.multiple_of(i * blk, blk), blk), :] = out_t.T.astype(BF16)


def moba_grid_steps(batch, seq):
    return batch * MOBA_HEADS * (seq // MOBA_BLOCK // 2 // MOBA_PAIRS_PER_STEP)


def moba_mixer(proj, batch, seq, side_casts=()):
    n = proj.shape[0]
    d = MOBA_HEAD_DIM
    blk = MOBA_BLOCK
    assert seq % blk == 0
    n_blocks = seq // blk
    assert n_blocks <= LANES and n_blocks % 2 == 0
    half = ROPE_DIM // 2
    inv_freq = ROPE_THETA ** (-jnp.arange(half, dtype=F32) * 2.0 / ROPE_DIM)
    ang = jnp.arange(seq).astype(F32)[:, None] * inv_freq[None, :]
    cos, sin = jnp.cos(ang), jnp.sin(ang)
    zeros = jnp.zeros((seq, d - ROPE_DIM), F32)
    zero_half = jnp.zeros((seq, half), F32)
    cos_t = jnp.concatenate([cos, cos, jnp.ones_like(zeros)], axis=1)
    sin_lo = jnp.concatenate([-sin, zero_half, zeros], axis=1)
    sin_hi = jnp.concatenate([zero_half, sin, zeros], axis=1)

    def col_spec(off):
        return pl.BlockSpec((seq, d), lambda b, h, p: (b, off // d + h))

    table_spec = pl.BlockSpec((seq, d), lambda b, h, p: (0, 0))
    gate_rows = -(-n_blocks // SUBLANES) * SUBLANES
    pairs = MOBA_PAIRS_PER_STEP
    assert (n_blocks // 2) % pairs == 0
    n_snap = max(n_blocks // 2 - 1, 1)
    steps = n_blocks // 2 // pairs

    def step_of(b, h, p):
        return (b * MOBA_HEADS + h) * steps + p

    out, *casts = pl.pallas_call(
        functools.partial(_moba_kernel, n_blocks=n_blocks, n_side=len(side_casts)),
        grid=(batch, MOBA_HEADS, steps),
        in_specs=[col_spec(OFF_MQ), col_spec(OFF_MK), col_spec(OFF_MV),
                  table_spec, table_spec, table_spec,
                  *[c.in_spec(step_of) for c in side_casts]],
        out_specs=[pl.BlockSpec((seq, d), lambda b, h, p: (b, h)),
                   *[c.out_spec(step_of) for c in side_casts]],
        out_shape=[jax.ShapeDtypeStruct((n, MOBA_WIDTH), BF16),
                   *[c.out_shape() for c in side_casts]],
        scratch_shapes=[pltpu.VMEM((n_blocks, blk, 2 * d), BF16),
                        pltpu.VMEM((n_blocks, d, blk), BF16),
                        pltpu.VMEM((gate_rows, d), F32),
                        pltpu.VMEM((n_blocks, 2 * d, blk), BF16),
                        pltpu.VMEM((pairs * (n_blocks + 1), blk, blk), F32),
                        pltpu.VMEM((2 * pairs, d, blk), F32),
                        pltpu.VMEM((pairs, n_snap, d, blk), F32)],
        compiler_params=_compiler_params(3, 54),
        name="moba",
    )(proj, proj, proj, cos_t, sin_lo, sin_hi, *[c.operand for c in side_casts])
    return out, [c.finish(x) for c, x in zip(side_casts, casts)]


def _hgrn_pair_codes(block):
    i = np.arange(block)[:, None]
    j = np.arange(block)[None, :]
    top_bit = np.floor(np.log2(np.maximum(i ^ j, 1))).astype(np.int32)
    n_levels = block.bit_length() - 1
    return np.where(i > j, top_bit, np.where(i == j, n_levels, -1)).astype(np.int32)


def _hgrn_kernel(*refs, layer, rows_per_step, n_side):
    lbp_ref, tril_ref, code_ref, q_ref, f_ref, v_ref, g_ref, nrm_ref = refs[:8]
    o_ref = refs[8 + n_side]
    state_ref, b_ref, k_ref = refs[9 + 2 * n_side:]
    _run_side_casts(refs[8:8 + n_side], refs[9 + n_side:9 + 2 * n_side])
    c_len = HGRN_BLOCK
    d = HGRN_HEAD_DIM

    @pl.when(pl.program_id(2) == 0)
    def _():
        state_ref[...] = jnp.zeros_like(state_ref)

    params = lbp_ref[...]
    e = jnp.exp(params - jnp.max(params, axis=0, keepdims=True))
    soft = e / jnp.sum(e, axis=0, keepdims=True)
    lower = jnp.zeros((1, d), F32)
    for r in range(1, layer + 1):
        lower = lower + soft[r:r + 1, :]

    tril = tril_ref[...]
    for seg0 in range(0, rows_per_step, c_len):
        seg = slice(seg0, seg0 + c_len)
        forget = lower + (1.0 - lower) * jax.nn.sigmoid(f_ref[seg, :])
        k_ref[seg, :] = 1.0 - forget
        log_f = jnp.log2(forget)
        hi = log_f.astype(BF16)
        rest = log_f - hi.astype(F32)
        mid = rest.astype(BF16)
        lo = (rest - mid.astype(F32)).astype(BF16)
        b_ref[seg, :] = (jnp.dot(tril, hi, preferred_element_type=F32)
                         + jnp.dot(tril, mid, preferred_element_type=F32)
                         + jnp.dot(tril, lo, preferred_element_type=F32))

    assert c_len & (c_len - 1) == 0 and c_len % LANES == 0
    levels = [1 << s for s in range(c_len.bit_length() - 1)]
    n_grp = c_len // SUBLANES
    row8 = lax.broadcasted_iota(jnp.int32, (SUBLANES, d), 0)
    zero8 = jnp.zeros((SUBLANES, d), F32)
    codes = _hgrn_pair_codes(c_len)
    tiles = [(g, cb) for g in range(n_grp) for cb in range(c_len // LANES)]

    def code_tile(g, cb):
        return codes[SUBLANES * g:SUBLANES * (g + 1), LANES * cb:LANES * (cb + 1)]

    level_tiles = [[(g, cb, bool(np.all(code_tile(g, cb) == bit))) for g, cb in tiles
                    if np.any(code_tile(g, cb) == bit)] for bit in range(len(levels) + 1)]

    def bcast_row(ref, r):
        return jnp.broadcast_to(ref[r:r + 1, :], (SUBLANES, d))

    def level_operands(r0, h, bg, qg, kg):
        q_parts, k_parts = [], []
        for g in range(n_grp):
            base = SUBLANES * g
            if h >= SUBLANES:
                b_a = bcast_row(b_ref, r0 + base // (2 * h) * (2 * h) + h - 1)
                if base & h:
                    q_parts.append(qg[g] * jnp.exp2(bg[g] - b_a))
                    k_parts.append(zero8)
                else:
                    q_parts.append(zero8)
                    k_parts.append(kg[g] * jnp.exp2(b_a - bg[g]))
                continue
            upper = (row8 & h) != 0
            if h == 1:
                q_parts.append(jnp.where(upper, qg[g] * (1.0 - kg[g]), 0.0))
                k_parts.append(jnp.where(upper, 0.0, kg[g]))
                continue
            b_a = bcast_row(b_ref, r0 + base + h - 1)
            for s in range(2 * h, SUBLANES, 2 * h):
                b_a = jnp.where(row8 >= s, bcast_row(b_ref, r0 + base + s + h - 1), b_a)
            decay = jnp.exp2(-jnp.abs(bg[g] - b_a))
            q_parts.append(jnp.where(upper, qg[g] * decay, 0.0))
            k_parts.append(jnp.where(upper, 0.0, kg[g] * decay))
        return (jnp.concatenate(q_parts, axis=0).astype(BF16),
                jnp.concatenate(k_parts, axis=0).astype(BF16))

    for c in range(rows_per_step // c_len):
        r0 = c * c_len
        rows = slice(r0, r0 + c_len)
        b = b_ref[rows, :]
        q = _silu(q_ref[rows, :])
        k = k_ref[rows, :]
        v = v_ref[rows, :]

        groups = [slice(SUBLANES * g, SUBLANES * (g + 1)) for g in range(n_grp)]
        bg = [b[g, :] for g in groups]
        qg = [q[g, :] for g in groups]
        kg = [k[g, :] for g in groups]
        def code_of(g, cb):
            return code_ref[SUBLANES * g:SUBLANES * (g + 1), LANES * cb:LANES * (cb + 1)]

        a_tiles = {}
        self_weight = jnp.sum(q * k, axis=-1, keepdims=True)
        for g, cb, _ in level_tiles[len(levels)]:
            a_tiles[g, cb] = jnp.where(code_of(g, cb) == len(levels),
                                       self_weight[groups[g], :], 0.0)
        for bit, h in enumerate(levels):
            q_h, k_h = level_operands(r0, h, bg, qg, kg)
            for cb in range(c_len // LANES):
                k_lo = LANES * cb
                if 2 * h >= LANES:
                    start = k_lo // (2 * h) * (2 * h)
                    if k_lo >= start + h:
                        continue
                    q_lo, q_hi = start + h, start + 2 * h
                else:
                    q_lo, q_hi = k_lo, k_lo + LANES
                pair = lax.dot_general(q_h[q_lo:q_hi, :], k_h[k_lo:k_lo + LANES, :], NT_DIMS,
                                       preferred_element_type=F32)
                for g, tile_cb, owns_tile in level_tiles[bit]:
                    if tile_cb != cb:
                        continue
                    assert q_lo <= SUBLANES * g < q_hi
                    piece = pair[SUBLANES * g - q_lo:SUBLANES * (g + 1) - q_lo, :]
                    if owns_tile:
                        a_tiles[g, cb] = piece
                    else:
                        a_tiles[g, cb] = jnp.where(code_of(g, cb) == bit, piece,
                                                   a_tiles.get((g, cb), 0.0))
        zero_tile = jnp.zeros((SUBLANES, LANES), F32)
        a_mat = jnp.concatenate(
            [jnp.concatenate([a_tiles.get((g, cb), zero_tile) for g in range(n_grp)], axis=0)
             for cb in range(c_len // LANES)], axis=1)
        intra = jnp.dot(a_mat.astype(BF16), v.astype(BF16), preferred_element_type=F32)

        state_t = state_ref[...]
        cross = lax.dot_general((q * jnp.exp2(b)).astype(BF16), state_t.astype(BF16), NT_DIMS,
                                preferred_element_type=F32)
        b_last = b_ref[r0 + c_len - 1:r0 + c_len, :]
        k_dec = k * jnp.exp2(b_last - b)
        state_ref[...] = state_t * jnp.exp2(b_last) + lax.dot_general(
            v.astype(BF16), k_dec.astype(BF16), TN_DIMS, preferred_element_type=F32)

        out = intra + cross
        normed = out * lax.rsqrt(jnp.mean(out * out, axis=-1, keepdims=True) + NORM_EPS)
        gated = (normed * nrm_ref[...]) * _silu(g_ref[rows, :])
        o_ref[rows, :] = gated.astype(BF16)


HGRN_ROWS_PER_STEP = 1024


def hgrn_grid_steps(batch, seq):
    return batch * HGRN_HEADS * (seq // min(seq, HGRN_ROWS_PER_STEP))


def hgrn_mixer(proj, lower_bound_params, hgrn_norm, layer, batch, seq, side_casts=()):
    rows_per_step = min(seq, HGRN_ROWS_PER_STEP)
    n = proj.shape[0]
    d = HGRN_HEAD_DIM
    depth = lower_bound_params.shape[0]
    blk = HGRN_BLOCK
    assert seq % rows_per_step == 0 and rows_per_step % blk == 0
    steps = seq // rows_per_step
    codes = _hgrn_pair_codes(blk)
    tril = jnp.asarray(codes >= 0, BF16)

    def col_spec(off):
        return pl.BlockSpec((rows_per_step, d), lambda b, h, s: (b * steps + s, off // d + h))

    def step_of(b, h, s):
        return (b * HGRN_HEADS + h) * steps + s

    out, *casts = pl.pallas_call(
        functools.partial(_hgrn_kernel, layer=layer, rows_per_step=rows_per_step,
                          n_side=len(side_casts)),
        grid=(batch, HGRN_HEADS, steps),
        in_specs=[pl.BlockSpec((depth, d), lambda b, h, s: (0, h)),
                  pl.BlockSpec((blk, blk), lambda b, h, s: (0, 0)),
                  pl.BlockSpec((blk, blk), lambda b, h, s: (0, 0)),
                  col_spec(OFF_HQ), col_spec(OFF_HF), col_spec(OFF_HI), col_spec(OFF_HG),
                  pl.BlockSpec((1, d), lambda b, h, s: (0, h)),
                  *[c.in_spec(step_of) for c in side_casts]],
        out_specs=[pl.BlockSpec((rows_per_step, d), lambda b, h, s: (b * steps + s, h)),
                   *[c.out_spec(step_of) for c in side_casts]],
        out_shape=[jax.ShapeDtypeStruct((n, HGRN_WIDTH), BF16),
                   *[c.out_shape() for c in side_casts]],
        scratch_shapes=[pltpu.VMEM((d, d), F32),
                        pltpu.VMEM((rows_per_step, d), F32),
                        pltpu.VMEM((rows_per_step, d), F32)],
        compiler_params=_compiler_params(3, 40),
        name="hgrn2",
    )(lower_bound_params.astype(F32), tril, jnp.asarray(codes), proj, proj, proj, proj,
      hgrn_norm.reshape(1, HGRN_WIDTH), *[c.operand for c in side_casts])
    return out, [c.finish(x) for c, x in zip(side_casts, casts)]


def kernel(x, hgrn_lower_bounds, norm_mix, w_in, ret_norm, hgrn_norm, w_branch_ret, w_branch_moba,
           w_branch_hgrn, w_out, norm_ffn, w_ffn_gate, w_ffn_up, w_ffn_down, final_norm):
    batch, seq, d_model = x.shape
    assert d_model == D_MODEL
    depth = w_in.shape[0]
    w_in_layers = [cast_layer(w_in, 0)]
    moba_steps = moba_grid_steps(batch, seq)
    hgrn_steps = hgrn_grid_steps(batch, seq)
    moba_casts = [SideCast(w, moba_steps) for w in (w_ffn_gate, w_ffn_up, w_ffn_down)]
    hgrn_casts = [SideCast(w, hgrn_steps)
                  for w in (w_branch_ret, w_branch_moba, w_branch_hgrn, w_out)]
    hgrn_casts += [SideCast(w_in, hgrn_steps, part=l, n_parts=depth) for l in range(1, depth)]
    h = x.reshape(batch * seq, d_model)
    for layer in range(depth):
        proj = norm_matmul(h, norm_mix[layer], w_in_layers[layer], 0)
        ret = retention_mixer(proj, ret_norm[layer], batch, seq)
        moba, cast = moba_mixer(proj, batch, seq, moba_casts if layer == 0 else ())
        if layer == 0:
            w_ffn_gate, w_ffn_up, w_ffn_down = cast
        hgrn, cast = hgrn_mixer(proj, hgrn_lower_bounds, hgrn_norm[layer], layer, batch, seq,
                                hgrn_casts if layer == 0 else ())
        if layer == 0:
            w_branch_ret, w_branch_moba, w_branch_hgrn, w_out = cast[:4]
            w_in_layers += cast[4:]
        h = merge_out_proj(ret, moba, hgrn, proj, w_branch_ret, w_branch_moba, w_branch_hgrn,
                           w_out, h, layer)
        act = ffn_up(h, norm_ffn[layer], w_ffn_gate, w_ffn_up, layer)
        h = ffn_down(act, w_ffn_down, h, layer, final_norm, normalize=layer == depth - 1)
    return h.reshape(batch, seq, d_model)
```

```python
import functools
import math

import jax
import jax.numpy as jnp
import numpy as np
from jax import lax
from jax.experimental import pallas as pl
from jax.experimental.pallas import tpu as pltpu

F32 = jnp.float32
BF16 = jnp.bfloat16

D_MODEL = 2048
RET_HEADS = 4
RET_HEAD_DIM = 256
RET_WIDTH = RET_HEADS * RET_HEAD_DIM
RET_CHUNK = 128
RET_ROPE_BASE = 10000.0
MOBA_HEADS = 8
MOBA_HEAD_DIM = 128
MOBA_WIDTH = MOBA_HEADS * MOBA_HEAD_DIM
MOBA_BLOCK = 256
MOBA_TOPK = 3
ROPE_THETA = 500000.0
ROPE_DIM = MOBA_HEAD_DIM // 4
HGRN_HEADS = 8
HGRN_HEAD_DIM = 128
HGRN_WIDTH = HGRN_HEADS * HGRN_HEAD_DIM
HGRN_BLOCK = 256
NORM_EPS = 1e-6
IN_SIZES = (RET_WIDTH,) * 4 + (MOBA_WIDTH,) * 3 + (HGRN_WIDTH,) * 4 + (D_MODEL,) * 3
IN_COLS = sum(IN_SIZES)
IN_OFFS = tuple(sum(IN_SIZES[:i]) for i in range(len(IN_SIZES)))
(OFF_RQ, OFF_RK, OFF_RV, OFF_RG, OFF_MQ, OFF_MK, OFF_MV,
 OFF_HQ, OFF_HF, OFF_HI, OFF_HG, OFF_GR, OFF_GM, OFF_GH) = IN_OFFS

V7X_VMEM_BYTES = 64 * 1024 * 1024
LANES = 128
SUBLANES = 8
MASK_VALUE = -1e30
LOG2_E = math.log2(math.e)

NT_DIMS = (((1,), (1,)), ((), ()))
TN_DIMS = (((0,), (0,)), ((), ()))


def _compiler_params(n_grid_dims, vmem_mib):
    assert vmem_mib * 1024 * 1024 < V7X_VMEM_BYTES
    return pltpu.CompilerParams(
        dimension_semantics=("arbitrary",) * n_grid_dims,
        vmem_limit_bytes=vmem_mib * 1024 * 1024)


def _silu(x):
    return x * jax.nn.sigmoid(x)


BF16_SUBLANES = 16


class SideCast:
    def __init__(self, array, n_steps, part=0, n_parts=1):
        layers, rows, self.cols = array.shape
        part_rows = layers // n_parts * rows
        self.steps_per_slab = 1
        while part_rows * self.steps_per_slab % (n_steps * BF16_SUBLANES):
            self.steps_per_slab *= 2
        assert n_steps % self.steps_per_slab == 0
        self.n_slabs = n_steps // self.steps_per_slab
        self.slab_rows = part_rows // self.n_slabs
        self.first_slab = part * self.n_slabs
        self.operand = array.reshape(n_parts * self.n_slabs, self.slab_rows, self.cols)
        self.part_shape = (layers // n_parts, rows, self.cols)

    def in_spec(self, step_of):
        return pl.BlockSpec(
            (None, self.slab_rows, self.cols),
            lambda *g: (self.first_slab + step_of(*g) // self.steps_per_slab, 0, 0))

    def out_spec(self, step_of):
        return pl.BlockSpec((None, self.slab_rows, self.cols),
                            lambda *g: (step_of(*g) // self.steps_per_slab, 0, 0))

    def out_shape(self):
        return jax.ShapeDtypeStruct((self.n_slabs, self.slab_rows, self.cols), BF16)

    def finish(self, cast):
        return cast.reshape(self.part_shape)


def _run_side_casts(in_refs, out_refs):
    for src, dst in zip(in_refs, out_refs):
        dst[...] = src[...].astype(BF16)


def _cast_kernel(src_ref, dst_ref):
    dst_ref[...] = src_ref[...].astype(BF16)


def cast_layer(w, layer, *, tn=1024):
    _, rows, cols = w.shape
    assert cols % tn == 0
    return pl.pallas_call(
        _cast_kernel,
        grid=(cols // tn,),
        in_specs=[pl.BlockSpec((None, rows, tn), lambda j: (layer, 0, j))],
        out_specs=pl.BlockSpec((None, rows, tn), lambda j: (0, 0, j)),
        out_shape=jax.ShapeDtypeStruct((1, rows, cols), BF16),
        compiler_params=_compiler_params(1, 40),
        name="cast_layer",
    )(w)


def _layer_weight_spec(layer, rows, tn, col_of):
    return pl.BlockSpec((None, rows, tn), lambda *g: (layer, 0, col_of(*g)))


def _rms_normalize(x, g):
    r = lax.rsqrt(jnp.mean(x * x, axis=-1, keepdims=True) + NORM_EPS)
    return x * r * g


def _norm_matmul_kernel(x_ref, g_ref, w_ref, o_ref, h_ref):
    @pl.when(pl.program_id(1) == 0)
    def _():
        h_ref[...] = _rms_normalize(x_ref[...], g_ref[...]).astype(BF16)

    o_ref[...] = jnp.dot(h_ref[...], w_ref[...], preferred_element_type=F32)


def norm_matmul(x, g, w, layer, *, tm=1024, tn=1024):
    n, d = x.shape
    cols = w.shape[2]
    assert n % tm == 0 and cols % tn == 0
    return pl.pallas_call(
        _norm_matmul_kernel,
        grid=(n // tm, cols // tn),
        in_specs=[pl.BlockSpec((tm, d), lambda i, j: (i, 0)),
                  pl.BlockSpec((1, d), lambda i, j: (0, 0)),
                  _layer_weight_spec(layer, d, tn, lambda i, j: j)],
        out_specs=pl.BlockSpec((tm, tn), lambda i, j: (i, j)),
        out_shape=jax.ShapeDtypeStruct((n, cols), F32),
        scratch_shapes=[pltpu.VMEM((tm, d), BF16)],
        compiler_params=_compiler_params(2, 48),
        name="norm_inproj",
    )(x, g.reshape(1, d), w)


def _ffn_up_kernel(x_ref, g_ref, wg_ref, wu_ref, o_ref, h_ref):
    @pl.when(pl.program_id(1) == 0)
    def _():
        h_ref[...] = _rms_normalize(x_ref[...], g_ref[...]).astype(BF16)

    h = h_ref[...]
    gate = jnp.dot(h, wg_ref[...], preferred_element_type=F32)
    up = jnp.dot(h, wu_ref[...], preferred_element_type=F32)
    o_ref[...] = (_silu(gate) * up).astype(BF16)


def ffn_up(x, g, wg, wu, layer, *, tm=1024, tn=512):
    n, d = x.shape
    hidden = wg.shape[2]
    assert n % tm == 0 and hidden % tn == 0
    w_spec = _layer_weight_spec(layer, d, tn, lambda i, j: j)
    return pl.pallas_call(
        _ffn_up_kernel,
        grid=(n // tm, hidden // tn),
        in_specs=[pl.BlockSpec((tm, d), lambda i, j: (i, 0)),
                  pl.BlockSpec((1, d), lambda i, j: (0, 0)),
                  w_spec, w_spec],
        out_specs=pl.BlockSpec((tm, tn), lambda i, j: (i, j)),
        out_shape=jax.ShapeDtypeStruct((n, hidden), BF16),
        scratch_shapes=[pltpu.VMEM((tm, d), BF16)],
        compiler_params=_compiler_params(2, 48),
        name="ffn_up",
    )(x, g.reshape(1, d), wg, wu)


def _ffn_down_kernel(a_ref, w_ref, r_ref, g_ref, o_ref, *, normalize):
    out = r_ref[...] + jnp.dot(a_ref[...], w_ref[...], preferred_element_type=F32)
    o_ref[...] = _rms_normalize(out, g_ref[...]) if normalize else out


def ffn_down(a, w, res, layer, gain, *, normalize, tm=512):
    n, k = a.shape
    cols = w.shape[2]
    assert n % tm == 0
    row_spec = pl.BlockSpec((tm, cols), lambda i: (i, 0))
    return pl.pallas_call(
        functools.partial(_ffn_down_kernel, normalize=normalize),
        grid=(n // tm,),
        in_specs=[pl.BlockSpec((tm, k), lambda i: (i, 0)),
                  pl.BlockSpec((None, k, cols), lambda i: (layer, 0, 0),
                               pipeline_mode=pl.Buffered(1)),
                  row_spec,
                  pl.BlockSpec((1, cols), lambda i: (0, 0))],
        out_specs=row_spec,
        out_shape=jax.ShapeDtypeStruct((n, cols), F32),
        compiler_params=_compiler_params(1, 54),
        name="ffn_down",
    )(a, w, res, gain.reshape(1, cols))


MERGE_GATE_TILE = 1024


def _merge_out_kernel(*refs):
    n_gate = D_MODEL // MERGE_GATE_TILE
    branch_refs = refs[:3]
    gate_refs = refs[3:3 + 3 * n_gate]
    wb_refs = refs[3 + 3 * n_gate:6 + 3 * n_gate]
    wo_ref, res_ref, o_ref = refs[6 + 3 * n_gate:]
    parts = []
    for c in range(n_gate):
        cols = slice(c * MERGE_GATE_TILE, (c + 1) * MERGE_GATE_TILE)
        mixed = None
        for b in range(3):
            term = jax.nn.sigmoid(gate_refs[b * n_gate + c][...]) * jnp.dot(
                branch_refs[b][...], wb_refs[b][:, cols], preferred_element_type=F32)
            mixed = term if mixed is None else mixed + term
        parts.append(mixed.astype(BF16))
    mixed = jnp.concatenate(parts, axis=1)
    o_ref[...] = res_ref[...] + jnp.dot(mixed, wo_ref[...], preferred_element_type=F32)


def merge_out_proj(ret, moba, hgrn, proj, wr, wm, wh, wo, res, layer, *, tm=256):
    n = ret.shape[0]
    assert n % tm == 0 and D_MODEL % MERGE_GATE_TILE == 0
    assert all(off % MERGE_GATE_TILE == 0 for off in (OFF_GR, OFF_GM, OFF_GH))
    n_gate = D_MODEL // MERGE_GATE_TILE

    def branch_spec(width):
        return pl.BlockSpec((tm, width), lambda i: (i, 0))

    def gate_specs(off):
        return [pl.BlockSpec((tm, MERGE_GATE_TILE),
                             functools.partial(lambda i, col: (i, col), col=off // MERGE_GATE_TILE + c))
                for c in range(n_gate)]

    def weight_spec(rows):
        return pl.BlockSpec((None, rows, D_MODEL), lambda i: (layer, 0, 0),
                            pipeline_mode=pl.Buffered(1))

    row_spec = pl.BlockSpec((tm, D_MODEL), lambda i: (i, 0))
    return pl.pallas_call(
        _merge_out_kernel,
        grid=(n // tm,),
        in_specs=[branch_spec(RET_WIDTH), branch_spec(MOBA_WIDTH), branch_spec(HGRN_WIDTH),
                  *gate_specs(OFF_GR), *gate_specs(OFF_GM), *gate_specs(OFF_GH),
                  weight_spec(RET_WIDTH), weight_spec(MOBA_WIDTH), weight_spec(HGRN_WIDTH),
                  weight_spec(D_MODEL), row_spec],
        out_specs=row_spec,
        out_shape=jax.ShapeDtypeStruct((n, D_MODEL), F32),
        compiler_params=_compiler_params(1, 52),
        name="merge_out_proj",
    )(ret, moba, hgrn, *([proj] * (3 * n_gate)), wr, wm, wh, wo, res)


def _retention_kernel(lg_ref, q_ref, k_ref, v_ref, g_ref, cos_ref, sin_ref, nrm_ref,
                      o_ref, state_ref, *, chunks):
    c_len = RET_CHUNK
    d = RET_HEAD_DIM
    half = d // 2

    @pl.when(pl.program_id(1) == 0)
    def _():
        state_ref[...] = jnp.zeros_like(state_ref)

    t_col = lax.broadcasted_iota(jnp.int32, (c_len, 1), 0).astype(F32)
    rel = (lax.broadcasted_iota(jnp.int32, (c_len, c_len), 0)
           - lax.broadcasted_iota(jnp.int32, (c_len, c_len), 1)).astype(F32)

    def rotate(x, cos, sin):
        x1 = x[:, :half]
        x2 = x[:, half:]
        return jnp.concatenate([x1 * cos - x2 * sin, x2 * cos + x1 * sin], axis=1)

    for h in range(RET_HEADS):
        cols = slice(h * d, (h + 1) * d)
        lg = lg_ref[h]
        decay = jnp.where(rel >= 0, jnp.exp(lg * jnp.maximum(rel, 0.0)), 0.0)
        q_weight = jnp.exp(lg * (t_col + 1.0))
        k_weight = jnp.exp(lg * (c_len - 1.0 - t_col))
        chunk_decay = jnp.exp(jnp.full((1, d), lg * c_len, F32))
        for c in range(chunks):
            rows = slice(c * c_len, (c + 1) * c_len)
            cos = cos_ref[rows, :]
            sin = sin_ref[rows, :]
            q = rotate(q_ref[rows, cols], cos, sin)
            k = rotate(k_ref[rows, cols], cos, sin) * (d ** -0.5)
            v = v_ref[rows, cols].astype(BF16)
            state = state_ref[h]

            scores = lax.dot_general(q.astype(BF16), k.astype(BF16), NT_DIMS,
                                     preferred_element_type=F32) * decay
            intra = jnp.dot(scores.astype(BF16), v, preferred_element_type=F32)
            cross = jnp.dot((q * q_weight).astype(BF16), state.astype(BF16),
                            preferred_element_type=F32)
            kv = lax.dot_general((k * k_weight).astype(BF16), v, TN_DIMS,
                                 preferred_element_type=F32)
            state_ref[h] = chunk_decay * state + kv

            out = intra + cross
            centered = out - jnp.mean(out, axis=-1, keepdims=True)
            normed = centered * lax.rsqrt(
                jnp.mean(centered * centered, axis=-1, keepdims=True) + NORM_EPS)
            gated = (normed * nrm_ref[:, cols]) * _silu(g_ref[rows, cols])
            o_ref[rows, cols] = gated.astype(BF16)


def retention_mixer(proj, ret_norm, batch, seq, *, rows_per_step=512):
    n = proj.shape[0]
    d = RET_HEAD_DIM
    assert seq % rows_per_step == 0 and rows_per_step % RET_CHUNK == 0
    steps = seq // rows_per_step
    log_gamma = jnp.log1p(-jnp.exp2(-5.0 - jnp.arange(RET_HEADS, dtype=F32)))
    half = d // 2
    inv_freq = RET_ROPE_BASE ** (-jnp.arange(half, dtype=F32) * 2.0 / d)
    ang = jnp.arange(seq).astype(F32)[:, None] * inv_freq[None, :]
    cos, sin = jnp.cos(ang), jnp.sin(ang)

    def col_spec(off):
        return pl.BlockSpec((rows_per_step, RET_WIDTH),
                            lambda b, s: (b * steps + s, off // RET_WIDTH))

    table_spec = pl.BlockSpec((rows_per_step, half), lambda b, s: (s, 0))
    return pl.pallas_call(
        functools.partial(_retention_kernel, chunks=rows_per_step // RET_CHUNK),
        grid=(batch, steps),
        in_specs=[pl.BlockSpec(memory_space=pltpu.SMEM),
                  col_spec(OFF_RQ), col_spec(OFF_RK), col_spec(OFF_RV), col_spec(OFF_RG),
                  table_spec, table_spec,
                  pl.BlockSpec((1, RET_WIDTH), lambda b, s: (0, 0))],
        out_specs=pl.BlockSpec((rows_per_step, RET_WIDTH), lambda b, s: (b * steps + s, 0)),
        out_shape=jax.ShapeDtypeStruct((n, RET_WIDTH), BF16),
        scratch_shapes=[pltpu.VMEM((RET_HEADS, d, d), F32)],
        compiler_params=_compiler_params(2, 40),
        name="retention",
    )(log_gamma, proj, proj, proj, proj, cos, sin, ret_norm.reshape(1, RET_WIDTH))


MOBA_PAIRS_PER_STEP = 2


def _moba_rotate(x, cos, sin_lo, sin_hi):
    half = ROPE_DIM // 2
    return (x * cos + pltpu.roll(x, MOBA_HEAD_DIM - half, 1) * sin_lo
            + pltpu.roll(x, half, 1) * sin_hi)


def _split_dot_nt(a, b):
    a_hi = a.astype(BF16)
    a_lo = (a - a_hi.astype(F32)).astype(BF16)
    b_hi = b.astype(BF16)
    b_lo = (b - b_hi.astype(F32)).astype(BF16)

    def dot_nt(x, y):
        return lax.dot_general(x, y, NT_DIMS, preferred_element_type=F32)

    return dot_nt(a_hi, b_hi) + dot_nt(a_hi, b_lo) + dot_nt(a_lo, b_hi)


def _moba_kernel(*refs, n_blocks, n_side):
    q_ref, k_ref, v_ref, cos_ref, slo_ref, shi_ref = refs[:6]
    o_ref = refs[6 + n_side]
    (kaug_ref, vt_ref, kmean_ref, qt_ref, s_ref, acc_ref,
     snap_ref) = refs[7 + 2 * n_side:]
    _run_side_casts(refs[6:6 + n_side], refs[7 + n_side:7 + 2 * n_side])
    blk = MOBA_BLOCK
    d = MOBA_HEAD_DIM
    step = pl.program_id(2)
    n_past = n_blocks - 1
    pairs = snap_ref.shape[0]
    gate_rows = kmean_ref.shape[0]
    groups = blk // SUBLANES
    q_scale = (d ** -0.5) * LOG2_E

    def rotate(ref, start):
        rows = pl.ds(start, blk)
        return _moba_rotate(ref[rows, :], cos_ref[rows, :], slo_ref[rows, :], shi_ref[rows, :])

    @pl.when(step == 0)
    def _():
        lane = lax.broadcasted_iota(jnp.int32, (blk, LANES), 1)
        eye = jnp.where(lax.broadcasted_iota(jnp.int32, (d, d), 0)
                        == lax.broadcasted_iota(jnp.int32, (d, d), 1), 1.0, 0.0).astype(BF16)

        def transposed(x_bf):
            return lax.dot_general(eye, x_bf, NT_DIMS, preferred_element_type=F32)

        kmean_ref[...] = jnp.zeros_like(kmean_ref)
        for j in range(n_blocks):
            kr = rotate(k_ref, j * blk)
            kaug_ref[j, :, :d] = kr.astype(BF16)
            kaug_ref[j, :, d:] = jnp.where(lane == j, 1.0, 0.0).astype(BF16)
            vt_ref[j] = transposed(v_ref[j * blk:(j + 1) * blk, :].astype(BF16)).astype(BF16)
            kmean_ref[j:j + 1, :] = jnp.mean(kr, axis=0, keepdims=True)
        row = lax.broadcasted_iota(jnp.int32, (gate_rows, blk), 0)
        row_f = row.astype(F32)
        pad = jnp.zeros((LANES - gate_rows, blk), F32)
        for i in range(n_blocks):
            q = rotate(q_ref, i * blk)
            gate = _split_dot_nt(kmean_ref[...], q)
            q_t = transposed((q * q_scale).astype(BF16))
            g = jnp.where(row < i, gate, -jnp.inf)
            bias = jnp.full((gate_rows, blk), MASK_VALUE, F32)
            for _ in range(min(MOBA_TOPK, n_blocks)):
                best = jnp.max(g, axis=0, keepdims=True)
                hit = (g == best) & (g > -jnp.inf)
                first = jnp.min(jnp.where(hit, row_f, float(LANES)), axis=0, keepdims=True)
                pick = row_f == first
                bias = jnp.where(pick, 0.0, bias)
                g = jnp.where(pick, -jnp.inf, g)
            bias = jnp.where(row == i, 0.0, bias)
            qt_ref[i] = jnp.concatenate([q_t, bias, pad], axis=0).astype(BF16)

    def group_reduce(x, op):
        return op(x.reshape(groups, SUBLANES, blk), axis=0)

    def score_own_block(i, own_slot):
        s_t = jnp.dot(kaug_ref[i], qt_ref[i], preferred_element_type=F32)
        key = lax.broadcasted_iota(jnp.int32, (blk, blk), 0)
        qry = lax.broadcasted_iota(jnp.int32, (blk, blk), 1)
        s_t = jnp.where(key <= qry, s_t, MASK_VALUE)
        s_ref[own_slot] = s_t
        return group_reduce(s_t, jnp.max)

    def tile_of(t, p):
        in_b = t >= p
        return in_b, jnp.where(in_b, t - p, t)

    def col_max(m8):
        return jnp.broadcast_to(jnp.max(m8, axis=0, keepdims=True), (SUBLANES, blk))

    def weights(slot, m8):
        p_t = jnp.exp2(s_ref[slot].reshape(groups, SUBLANES, blk) - m8[None])
        return jnp.sum(p_t, axis=0), p_t.reshape(blk, blk).astype(BF16)

    def values(j, p_bf):
        return jnp.dot(vt_ref[j], p_bf, preferred_element_type=F32)

    never = jnp.full((SUBLANES, blk), -jnp.inf, F32)
    scored = []
    for k in range(pairs):
        p = step * pairs + k
        base = k * (n_past + 2)
        i_a, i_b = p, n_past - p
        m_a = score_own_block(i_a, base + n_past)
        m_b = score_own_block(i_b, base + n_past + 1)
        for t in range(n_past):
            in_b, j = tile_of(t, p)
            s_t = jnp.dot(kaug_ref[j], qt_ref[jnp.where(in_b, i_b, i_a)],
                          preferred_element_type=F32)
            s_ref[base + t] = s_t
            gm = group_reduce(s_t, jnp.max)
            m_a = jnp.maximum(m_a, jnp.where(in_b, never, gm))
            m_b = jnp.maximum(m_b, jnp.where(in_b, gm, never))
        scored.append((p, base, i_a, i_b, col_max(m_a), col_max(m_b)))

    for k, (p, base, i_a, i_b, m_a, m_b) in enumerate(scored):
        l_a, p_own = weights(base + n_past, m_a)
        acc_ref[2 * k] = values(i_a, p_own)
        l_b, p_own = weights(base + n_past + 1, m_b)
        acc_ref[2 * k + 1] = values(i_b, p_own)
        acc = jnp.zeros((d, blk), F32)
        for t in range(n_past):
            in_b, j = tile_of(t, p)
            l_part, p_bf = weights(base + t, jnp.where(in_b, m_b, m_a))
            acc = acc * jnp.where(t == p, 0.0, 1.0) + values(j, p_bf)
            l_a = l_a + jnp.where(in_b, 0.0, l_part)
            l_b = l_b + jnp.where(in_b, l_part, 0.0)
            if t < snap_ref.shape[1]:
                snap_ref[k, t] = acc
        past_a = snap_ref[k, jnp.maximum(p - 1, 0)] * jnp.where(p > 0, 1.0, 0.0)
        for w, i, l8, past in ((2 * k, i_a, l_a, past_a), (2 * k + 1, i_b, l_b, acc)):
            l = jnp.sum(l8, axis=0, keepdims=True)
            out_t = (acc_ref[w] + past) / l
            o_ref[pl.ds(pl.multiple_of(i * blk, blk), blk), :] = out_t.T.astype(BF16)


def moba_grid_steps(batch, seq):
    return batch * MOBA_HEADS * (seq // MOBA_BLOCK // 2 // MOBA_PAIRS_PER_STEP)


def moba_mixer(proj, batch, seq, side_casts=()):
    n = proj.shape[0]
    d = MOBA_HEAD_DIM
    blk = MOBA_BLOCK
    assert seq % blk == 0
    n_blocks = seq // blk
    assert n_blocks <= LANES and n_blocks % 2 == 0
    half = ROPE_DIM // 2
    inv_freq = ROPE_THETA ** (-jnp.arange(half, dtype=F32) * 2.0 / ROPE_DIM)
    ang = jnp.arange(seq).astype(F32)[:, None] * inv_freq[None, :]
    cos, sin = jnp.cos(ang), jnp.sin(ang)
    zeros = jnp.zeros((seq, d - ROPE_DIM), F32)
    zero_half = jnp.zeros((seq, half), F32)
    cos_t = jnp.concatenate([cos, cos, jnp.ones_like(zeros)], axis=1)
    sin_lo = jnp.concatenate([-sin, zero_half, zeros], axis=1)
    sin_hi = jnp.concatenate([zero_half, sin, zeros], axis=1)

    def col_spec(off):
        return pl.BlockSpec((seq, d), lambda b, h, p: (b, off // d + h))

    table_spec = pl.BlockSpec((seq, d), lambda b, h, p: (0, 0))
    gate_rows = -(-n_blocks // SUBLANES) * SUBLANES
    pairs = MOBA_PAIRS_PER_STEP
    assert (n_blocks // 2) % pairs == 0
    n_snap = max(n_blocks // 2 - 1, 1)
    steps = n_blocks // 2 // pairs

    def step_of(b, h, p):
        return (b * MOBA_HEADS + h) * steps + p

    out, *casts = pl.pallas_call(
        functools.partial(_moba_kernel, n_blocks=n_blocks, n_side=len(side_casts)),
        grid=(batch, MOBA_HEADS, steps),
        in_specs=[col_spec(OFF_MQ), col_spec(OFF_MK), col_spec(OFF_MV),
                  table_spec, table_spec, table_spec,
                  *[c.in_spec(step_of) for c in side_casts]],
        out_specs=[pl.BlockSpec((seq, d), lambda b, h, p: (b, h)),
                   *[c.out_spec(step_of) for c in side_casts]],
        out_shape=[jax.ShapeDtypeStruct((n, MOBA_WIDTH), BF16),
                   *[c.out_shape() for c in side_casts]],
        scratch_shapes=[pltpu.VMEM((n_blocks, blk, 2 * d), BF16),
                        pltpu.VMEM((n_blocks, d, blk), BF16),
                        pltpu.VMEM((gate_rows, d), F32),
                        pltpu.VMEM((n_blocks, 2 * d, blk), BF16),
                        pltpu.VMEM((pairs * (n_blocks + 1), blk, blk), F32),
                        pltpu.VMEM((2 * pairs, d, blk), F32),
                        pltpu.VMEM((pairs, n_snap, d, blk), F32)],
        compiler_params=_compiler_params(3, 54),
        name="moba",
    )(proj, proj, proj, cos_t, sin_lo, sin_hi, *[c.operand for c in side_casts])
    return out, [c.finish(x) for c, x in zip(side_casts, casts)]


def _hgrn_pair_codes(block):
    i = np.arange(block)[:, None]
    j = np.arange(block)[None, :]
    top_bit = np.floor(np.log2(np.maximum(i ^ j, 1))).astype(np.int32)
    n_levels = block.bit_length() - 1
    return np.where(i > j, top_bit, np.where(i == j, n_levels, -1)).astype(np.int32)


def _hgrn_kernel(*refs, layer, rows_per_step, n_side):
    lbp_ref, tril_ref, code_ref, q_ref, f_ref, v_ref, g_ref, nrm_ref = refs[:8]
    o_ref = refs[8 + n_side]
    state_ref, b_ref, k_ref = refs[9 + 2 * n_side:]
    _run_side_casts(refs[8:8 + n_side], refs[9 + n_side:9 + 2 * n_side])
    c_len = HGRN_BLOCK
    d = HGRN_HEAD_DIM

    @pl.when(pl.program_id(2) == 0)
    def _():
        state_ref[...] = jnp.zeros_like(state_ref)

    params = lbp_ref[...]
    e = jnp.exp(params - jnp.max(params, axis=0, keepdims=True))
    soft = e / jnp.sum(e, axis=0, keepdims=True)
    lower = jnp.zeros((1, d), F32)
    for r in range(1, layer + 1):
        lower = lower + soft[r:r + 1, :]

    tril = tril_ref[...]
    for seg0 in range(0, rows_per_step, c_len):
        seg = slice(seg0, seg0 + c_len)
        forget = lower + (1.0 - lower) * jax.nn.sigmoid(f_ref[seg, :])
        k_ref[seg, :] = 1.0 - forget
        log_f = jnp.log2(forget)
        hi = log_f.astype(BF16)
        rest = log_f - hi.astype(F32)
        mid = rest.astype(BF16)
        lo = (rest - mid.astype(F32)).astype(BF16)
        b_ref[seg, :] = (jnp.dot(tril, hi, preferred_element_type=F32)
                         + jnp.dot(tril, mid, preferred_element_type=F32)
                         + jnp.dot(tril, lo, preferred_element_type=F32))

    assert c_len & (c_len - 1) == 0 and c_len % LANES == 0
    levels = [1 << s for s in range(c_len.bit_length() - 1)]
    n_grp = c_len // SUBLANES
    row8 = lax.broadcasted_iota(jnp.int32, (SUBLANES, d), 0)
    zero8 = jnp.zeros((SUBLANES, d), F32)
    codes = _hgrn_pair_codes(c_len)
    tiles = [(g, cb) for g in range(n_grp) for cb in range(c_len // LANES)]

    def code_tile(g, cb):
        return codes[SUBLANES * g:SUBLANES * (g + 1), LANES * cb:LANES * (cb + 1)]

    level_tiles = [[(g, cb, bool(np.all(code_tile(g, cb) == bit))) for g, cb in tiles
                    if np.any(code_tile(g, cb) == bit)] for bit in range(len(levels) + 1)]

    def bcast_row(ref, r):
        return jnp.broadcast_to(ref[r:r + 1, :], (SUBLANES, d))

    def level_operands(r0, h, bg, qg, kg):
        q_parts, k_parts = [], []
        for g in range(n_grp):
            base = SUBLANES * g
            if h >= SUBLANES:
                b_a = bcast_row(b_ref, r0 + base // (2 * h) * (2 * h) + h - 1)
                if base & h:
                    q_parts.append(qg[g] * jnp.exp2(bg[g] - b_a))
                    k_parts.append(zero8)
                else:
                    q_parts.append(zero8)
                    k_parts.append(kg[g] * jnp.exp2(b_a - bg[g]))
                continue
            upper = (row8 & h) != 0
            if h == 1:
                q_parts.append(jnp.where(upper, qg[g] * (1.0 - kg[g]), 0.0))
                k_parts.append(jnp.where(upper, 0.0, kg[g]))
                continue
            b_a = bcast_row(b_ref, r0 + base + h - 1)
            for s in range(2 * h, SUBLANES, 2 * h):
                b_a = jnp.where(row8 >= s, bcast_row(b_ref, r0 + base + s + h - 1), b_a)
            decay = jnp.exp2(-jnp.abs(bg[g] - b_a))
            q_parts.append(jnp.where(upper, qg[g] * decay, 0.0))
            k_parts.append(jnp.where(upper, 0.0, kg[g] * decay))
        return (jnp.concatenate(q_parts, axis=0).astype(BF16),
                jnp.concatenate(k_parts, axis=0).astype(BF16))

    for c in range(rows_per_step // c_len):
        r0 = c * c_len
        rows = slice(r0, r0 + c_len)
        b = b_ref[rows, :]
        q = _silu(q_ref[rows, :])
        k = k_ref[rows, :]
        v = v_ref[rows, :]

        groups = [slice(SUBLANES * g, SUBLANES * (g + 1)) for g in range(n_grp)]
        bg = [b[g, :] for g in groups]
        qg = [q[g, :] for g in groups]
        kg = [k[g, :] for g in groups]
        def code_of(g, cb):
            return code_ref[SUBLANES * g:SUBLANES * (g + 1), LANES * cb:LANES * (cb + 1)]

        a_tiles = {}
        self_weight = jnp.sum(q * k, axis=-1, keepdims=True)
        for g, cb, _ in level_tiles[len(levels)]:
            a_tiles[g, cb] = jnp.where(code_of(g, cb) == len(levels),
                                       self_weight[groups[g], :], 0.0)
        for bit, h in enumerate(levels):
            q_h, k_h = level_operands(r0, h, bg, qg, kg)
            for cb in range(c_len // LANES):
                k_lo = LANES * cb
                if 2 * h >= LANES:
                    start = k_lo // (2 * h) * (2 * h)
                    if k_lo >= start + h:
                        continue
                    q_lo, q_hi = start + h, start + 2 * h
                else:
                    q_lo, q_hi = k_lo, k_lo + LANES
                pair = lax.dot_general(q_h[q_lo:q_hi, :], k_h[k_lo:k_lo + LANES, :], NT_DIMS,
                                       preferred_element_type=F32)
                for g, tile_cb, owns_tile in level_tiles[bit]:
                    if tile_cb != cb:
                        continue
                    assert q_lo <= SUBLANES * g < q_hi
                    piece = pair[SUBLANES * g - q_lo:SUBLANES * (g + 1) - q_lo, :]
                    if owns_tile:
                        a_tiles[g, cb] = piece
                    else:
                        a_tiles[g, cb] = jnp.where(code_of(g, cb) == bit, piece,
                                                   a_tiles.get((g, cb), 0.0))
        zero_tile = jnp.zeros((SUBLANES, LANES), F32)
        a_mat = jnp.concatenate(
            [jnp.concatenate([a_tiles.get((g, cb), zero_tile) for g in range(n_grp)], axis=0)
             for cb in range(c_len // LANES)], axis=1)
        intra = jnp.dot(a_mat.astype(BF16), v.astype(BF16), preferred_element_type=F32)

        state_t = state_ref[...]
        cross = lax.dot_general((q * jnp.exp2(b)).astype(BF16), state_t.astype(BF16), NT_DIMS,
                                preferred_element_type=F32)
        b_last = b_ref[r0 + c_len - 1:r0 + c_len, :]
        k_dec = k * jnp.exp2(b_last - b)
        state_ref[...] = state_t * jnp.exp2(b_last) + lax.dot_general(
            v.astype(BF16), k_dec.astype(BF16), TN_DIMS, preferred_element_type=F32)

        out = intra + cross
        normed = out * lax.rsqrt(jnp.mean(out * out, axis=-1, keepdims=True) + NORM_EPS)
        gated = (normed * nrm_ref[...]) * _silu(g_ref[rows, :])
        o_ref[rows, :] = gated.astype(BF16)


HGRN_ROWS_PER_STEP = 1024


def hgrn_grid_steps(batch, seq):
    return batch * HGRN_HEADS * (seq // min(seq, HGRN_ROWS_PER_STEP))


def hgrn_mixer(proj, lower_bound_params, hgrn_norm, layer, batch, seq, side_casts=()):
    rows_per_step = min(seq, HGRN_ROWS_PER_STEP)
    n = proj.shape[0]
    d = HGRN_HEAD_DIM
    depth = lower_bound_params.shape[0]
    blk = HGRN_BLOCK
    assert seq % rows_per_step == 0 and rows_per_step % blk == 0
    steps = seq // rows_per_step
    codes = _hgrn_pair_codes(blk)
    tril = jnp.asarray(codes >= 0, BF16)

    def col_spec(off):
        return pl.BlockSpec((rows_per_step, d), lambda b, h, s: (b * steps + s, off // d + h))

    def step_of(b, h, s):
        return (b * HGRN_HEADS + h) * steps + s

    out, *casts = pl.pallas_call(
        functools.partial(_hgrn_kernel, layer=layer, rows_per_step=rows_per_step,
                          n_side=len(side_casts)),
        grid=(batch, HGRN_HEADS, steps),
        in_specs=[pl.BlockSpec((depth, d), lambda b, h, s: (0, h)),
                  pl.BlockSpec((blk, blk), lambda b, h, s: (0, 0)),
                  pl.BlockSpec((blk, blk), lambda b, h, s: (0, 0)),
                  col_spec(OFF_HQ), col_spec(OFF_HF), col_spec(OFF_HI), col_spec(OFF_HG),
                  pl.BlockSpec((1, d), lambda b, h, s: (0, h)),
                  *[c.in_spec(step_of) for c in side_casts]],
        out_specs=[pl.BlockSpec((rows_per_step, d), lambda b, h, s: (b * steps + s, h)),
                   *[c.out_spec(step_of) for c in side_casts]],
        out_shape=[jax.ShapeDtypeStruct((n, HGRN_WIDTH), BF16),
                   *[c.out_shape() for c in side_casts]],
        scratch_shapes=[pltpu.VMEM((d, d), F32),
                        pltpu.VMEM((rows_per_step, d), F32),
                        pltpu.VMEM((rows_per_step, d), F32)],
        compiler_params=_compiler_params(3, 40),
        name="hgrn2",
    )(lower_bound_params.astype(F32), tril, jnp.asarray(codes), proj, proj, proj, proj,
      hgrn_norm.reshape(1, HGRN_WIDTH), *[c.operand for c in side_casts])
    return out, [c.finish(x) for c, x in zip(side_casts, casts)]


def kernel(x, hgrn_lower_bounds, norm_mix, w_in, ret_norm, hgrn_norm, w_branch_ret, w_branch_moba,
           w_branch_hgrn, w_out, norm_ffn, w_ffn_gate, w_ffn_up, w_ffn_down, final_norm):
    batch, seq, d_model = x.shape
    assert d_model == D_MODEL
    depth = w_in.shape[0]
    w_in_layers = [cast_layer(w_in, 0)]
    moba_steps = moba_grid_steps(batch, seq)
    hgrn_steps = hgrn_grid_steps(batch, seq)
    moba_casts = [SideCast(w, moba_steps) for w in (w_ffn_gate, w_ffn_up, w_ffn_down)]
    hgrn_casts = [SideCast(w, hgrn_steps)
                  for w in (w_branch_ret, w_branch_moba, w_branch_hgrn, w_out)]
    hgrn_casts += [SideCast(w_in, hgrn_steps, part=l, n_parts=depth) for l in range(1, depth)]
    h = x.reshape(batch * seq, d_model)
    for layer in range(depth):
        proj = norm_matmul(h, norm_mix[layer], w_in_layers[layer], 0)
        ret = retention_mixer(proj, ret_norm[layer], batch, seq)
        moba, cast = moba_mixer(proj, batch, seq, moba_casts if layer == 0 else ())
        if layer == 0:
            w_ffn_gate, w_ffn_up, w_ffn_down = cast
        hgrn, cast = hgrn_mixer(proj, hgrn_lower_bounds, hgrn_norm[layer], layer, batch, seq,
                                hgrn_casts if layer == 0 else ())
        if layer == 0:
            w_branch_ret, w_branch_moba, w_branch_hgrn, w_out = cast[:4]
            w_in_layers += cast[4:]
        h = merge_out_proj(ret, moba, hgrn, proj, w_branch_ret, w_branch_moba, w_branch_hgrn,
                           w_out, h, layer)
        act = ffn_up(h, norm_ffn[layer], w_ffn_gate, w_ffn_up, layer)
        h = ffn_down(act, w_ffn_down, h, layer, final_norm, normalize=layer == depth - 1)
    return h.reshape(batch, seq, d_model)
```

```python
import functools
import math

import jax
import jax.numpy as jnp
import numpy as np
from jax import lax
from jax.experimental import pallas as pl
from jax.experimental.pallas import tpu as pltpu

F32 = jnp.float32
BF16 = jnp.bfloat16

D_MODEL = 2048
RET_HEADS = 4
RET_HEAD_DIM = 256
RET_WIDTH = RET_HEADS * RET_HEAD_DIM
RET_CHUNK = 128
RET_ROPE_BASE = 10000.0
MOBA_HEADS = 8
MOBA_HEAD_DIM = 128
MOBA_WIDTH = MOBA_HEADS * MOBA_HEAD_DIM
MOBA_BLOCK = 256
MOBA_TOPK = 3
ROPE_THETA = 500000.0
ROPE_DIM = MOBA_HEAD_DIM // 4
HGRN_HEADS = 8
HGRN_HEAD_DIM = 128
HGRN_WIDTH = HGRN_HEADS * HGRN_HEAD_DIM
HGRN_BLOCK = 256
NORM_EPS = 1e-6
IN_SIZES = (RET_WIDTH,) * 4 + (MOBA_WIDTH,) * 3 + (HGRN_WIDTH,) * 4 + (D_MODEL,) * 3
IN_COLS = sum(IN_SIZES)
IN_OFFS = tuple(sum(IN_SIZES[:i]) for i in range(len(IN_SIZES)))
(OFF_RQ, OFF_RK, OFF_RV, OFF_RG, OFF_MQ, OFF_MK, OFF_MV,
 OFF_HQ, OFF_HF, OFF_HI, OFF_HG, OFF_GR, OFF_GM, OFF_GH) = IN_OFFS

V7X_VMEM_BYTES = 64 * 1024 * 1024
LANES = 128
SUBLANES = 8
MASK_VALUE = -1e30
LOG2_E = math.log2(math.e)

NT_DIMS = (((1,), (1,)), ((), ()))
TN_DIMS = (((0,), (0,)), ((), ()))


def _compiler_params(n_grid_dims, vmem_mib):
    assert vmem_mib * 1024 * 1024 < V7X_VMEM_BYTES
    return pltpu.CompilerParams(
        dimension_semantics=("arbitrary",) * n_grid_dims,
        vmem_limit_bytes=vmem_mib * 1024 * 1024)


def _silu(x):
    return x * jax.nn.sigmoid(x)


BF16_SUBLANES = 16


class SideCast:
    def __init__(self, array, n_steps, part=0, n_parts=1):
        layers, rows, self.cols = array.shape
        part_rows = layers // n_parts * rows
        self.steps_per_slab = 1
        while part_rows * self.steps_per_slab % (n_steps * BF16_SUBLANES):
            self.steps_per_slab *= 2
        assert n_steps % self.steps_per_slab == 0
        self.n_slabs = n_steps // self.steps_per_slab
        self.slab_rows = part_rows // self.n_slabs
        self.first_slab = part * self.n_slabs
        self.operand = array.reshape(n_parts * self.n_slabs, self.slab_rows, self.cols)
        self.part_shape = (layers // n_parts, rows, self.cols)

    def in_spec(self, step_of):
        return pl.BlockSpec(
            (None, self.slab_rows, self.cols),
            lambda *g: (self.first_slab + step_of(*g) // self.steps_per_slab, 0, 0))

    def out_spec(self, step_of):
        return pl.BlockSpec((None, self.slab_rows, self.cols),
                            lambda *g: (step_of(*g) // self.steps_per_slab, 0, 0))

    def out_shape(self):
        return jax.ShapeDtypeStruct((self.n_slabs, self.slab_rows, self.cols), BF16)

    def finish(self, cast):
        return cast.reshape(self.part_shape)


def _run_side_casts(in_refs, out_refs):
    for src, dst in zip(in_refs, out_refs):
        dst[...] = src[...].astype(BF16)


def _cast_kernel(src_ref, dst_ref):
    dst_ref[...] = src_ref[...].astype(BF16)


def cast_layer(w, layer, *, tn=1024):
    _, rows, cols = w.shape
    assert cols % tn == 0
    return pl.pallas_call(
        _cast_kernel,
        grid=(cols // tn,),
        in_specs=[pl.BlockSpec((None, rows, tn), lambda j: (layer, 0, j))],
        out_specs=pl.BlockSpec((None, rows, tn), lambda j: (0, 0, j)),
        out_shape=jax.ShapeDtypeStruct((1, rows, cols), BF16),
        compiler_params=_compiler_params(1, 40),
        name="cast_layer",
    )(w)


def _layer_weight_spec(layer, rows, tn, col_of):
    return pl.BlockSpec((None, rows, tn), lambda *g: (layer, 0, col_of(*g)))


def _rms_normalize(x, g):
    r = lax.rsqrt(jnp.mean(x * x, axis=-1, keepdims=True) + NORM_EPS)
    return x * r * g


def _norm_matmul_kernel(x_ref, g_ref, w_ref, o_ref, h_ref):
    @pl.when(pl.program_id(1) == 0)
    def _():
        h_ref[...] = _rms_normalize(x_ref[...], g_ref[...]).astype(BF16)

    o_ref[...] = jnp.dot(h_ref[...], w_ref[...], preferred_element_type=F32)


def norm_matmul(x, g, w, layer, *, tm=1024, tn=1024):
    n, d = x.shape
    cols = w.shape[2]
    assert n % tm == 0 and cols % tn == 0
    return pl.pallas_call(
        _norm_matmul_kernel,
        grid=(n // tm, cols // tn),
        in_specs=[pl.BlockSpec((tm, d), lambda i, j: (i, 0)),
                  pl.BlockSpec((1, d), lambda i, j: (0, 0)),
                  _layer_weight_spec(layer, d, tn, lambda i, j: j)],
        out_specs=pl.BlockSpec((tm, tn), lambda i, j: (i, j)),
        out_shape=jax.ShapeDtypeStruct((n, cols), F32),
        scratch_shapes=[pltpu.VMEM((tm, d), BF16)],
        compiler_params=_compiler_params(2, 48),
        name="norm_inproj",
    )(x, g.reshape(1, d), w)


def _ffn_up_kernel(x_ref, g_ref, wg_ref, wu_ref, o_ref, h_ref):
    @pl.when(pl.program_id(1) == 0)
    def _():
        h_ref[...] = _rms_normalize(x_ref[...], g_ref[...]).astype(BF16)

    h = h_ref[...]
    gate = jnp.dot(h, wg_ref[...], preferred_element_type=F32)
    up = jnp.dot(h, wu_ref[...], preferred_element_type=F32)
    o_ref[...] = (_silu(gate) * up).astype(BF16)


def ffn_up(x, g, wg, wu, layer, *, tm=1024, tn=512):
    n, d = x.shape
    hidden = wg.shape[2]
    assert n % tm == 0 and hidden % tn == 0
    w_spec = _layer_weight_spec(layer, d, tn, lambda i, j: j)
    return pl.pallas_call(
        _ffn_up_kernel,
        grid=(n // tm, hidden // tn),
        in_specs=[pl.BlockSpec((tm, d), lambda i, j: (i, 0)),
                  pl.BlockSpec((1, d), lambda i, j: (0, 0)),
                  w_spec, w_spec],
        out_specs=pl.BlockSpec((tm, tn), lambda i, j: (i, j)),
        out_shape=jax.ShapeDtypeStruct((n, hidden), BF16),
        scratch_shapes=[pltpu.VMEM((tm, d), BF16)],
        compiler_params=_compiler_params(2, 48),
        name="ffn_up",
    )(x, g.reshape(1, d), wg, wu)


def _ffn_down_kernel(a_ref, w_ref, r_ref, g_ref, o_ref, *, normalize):
    out = r_ref[...] + jnp.dot(a_ref[...], w_ref[...], preferred_element_type=F32)
    o_ref[...] = _rms_normalize(out, g_ref[...]) if normalize else out


def ffn_down(a, w, res, layer, gain, *, normalize, tm=512):
    n, k = a.shape
    cols = w.shape[2]
    assert n % tm == 0
    row_spec = pl.BlockSpec((tm, cols), lambda i: (i, 0))
    return pl.pallas_call(
        functools.partial(_ffn_down_kernel, normalize=normalize),
        grid=(n // tm,),
        in_specs=[pl.BlockSpec((tm, k), lambda i: (i, 0)),
                  pl.BlockSpec((None, k, cols), lambda i: (layer, 0, 0),
                               pipeline_mode=pl.Buffered(1)),
                  row_spec,
                  pl.BlockSpec((1, cols), lambda i: (0, 0))],
        out_specs=row_spec,
        out_shape=jax.ShapeDtypeStruct((n, cols), F32),
        compiler_params=_compiler_params(1, 54),
        name="ffn_down",
    )(a, w, res, gain.reshape(1, cols))


MERGE_GATE_TILE = 1024


def _merge_out_kernel(*refs):
    n_gate = D_MODEL // MERGE_GATE_TILE
    branch_refs = refs[:3]
    gate_refs = refs[3:3 + 3 * n_gate]
    wb_refs = refs[3 + 3 * n_gate:6 + 3 * n_gate]
    wo_ref, res_ref, o_ref = refs[6 + 3 * n_gate:]
    parts = []
    for c in range(n_gate):
        cols = slice(c * MERGE_GATE_TILE, (c + 1) * MERGE_GATE_TILE)
        mixed = None
        for b in range(3):
            term = jax.nn.sigmoid(gate_refs[b * n_gate + c][...]) * jnp.dot(
                branch_refs[b][...], wb_refs[b][:, cols], preferred_element_type=F32)
            mixed = term if mixed is None else mixed + term
        parts.append(mixed.astype(BF16))
    mixed = jnp.concatenate(parts, axis=1)
    o_ref[...] = res_ref[...] + jnp.dot(mixed, wo_ref[...], preferred_element_type=F32)


def merge_out_proj(ret, moba, hgrn, proj, wr, wm, wh, wo, res, layer, *, tm=256):
    n = ret.shape[0]
    assert n % tm == 0 and D_MODEL % MERGE_GATE_TILE == 0
    assert all(off % MERGE_GATE_TILE == 0 for off in (OFF_GR, OFF_GM, OFF_GH))
    n_gate = D_MODEL // MERGE_GATE_TILE

    def branch_spec(width):
        return pl.BlockSpec((tm, width), lambda i: (i, 0))

    def gate_specs(off):
        return [pl.BlockSpec((tm, MERGE_GATE_TILE),
                             functools.partial(lambda i, col: (i, col), col=off // MERGE_GATE_TILE + c))
                for c in range(n_gate)]

    def weight_spec(rows):
        return pl.BlockSpec((None, rows, D_MODEL), lambda i: (layer, 0, 0),
                            pipeline_mode=pl.Buffered(1))

    row_spec = pl.BlockSpec((tm, D_MODEL), lambda i: (i, 0))
    return pl.pallas_call(
        _merge_out_kernel,
        grid=(n // tm,),
        in_specs=[branch_spec(RET_WIDTH), branch_spec(MOBA_WIDTH), branch_spec(HGRN_WIDTH),
                  *gate_specs(OFF_GR), *gate_specs(OFF_GM), *gate_specs(OFF_GH),
                  weight_spec(RET_WIDTH), weight_spec(MOBA_WIDTH), weight_spec(HGRN_WIDTH),
                  weight_spec(D_MODEL), row_spec],
        out_specs=row_spec,
        out_shape=jax.ShapeDtypeStruct((n, D_MODEL), F32),
        compiler_params=_compiler_params(1, 52),
        name="merge_out_proj",
    )(ret, moba, hgrn, *([proj] * (3 * n_gate)), wr, wm, wh, wo, res)


def _retention_kernel(lg_ref, q_ref, k_ref, v_ref, g_ref, cos_ref, sin_ref, nrm_ref,
                      o_ref, state_ref, *, chunks):
    c_len = RET_CHUNK
    d = RET_HEAD_DIM
    half = d // 2

    @pl.when(pl.program_id(1) == 0)
    def _():
        state_ref[...] = jnp.zeros_like(state_ref)

    t_col = lax.broadcasted_iota(jnp.int32, (c_len, 1), 0).astype(F32)
    rel = (lax.broadcasted_iota(jnp.int32, (c_len, c_len), 0)
           - lax.broadcasted_iota(jnp.int32, (c_len, c_len), 1)).astype(F32)

    def rotate(x, cos, sin):
        x1 = x[:, :half]
        x2 = x[:, half:]
        return jnp.concatenate([x1 * cos - x2 * sin, x2 * cos + x1 * sin], axis=1)

    for h in range(RET_HEADS):
        cols = slice(h * d, (h + 1) * d)
        lg = lg_ref[h]
        decay = jnp.where(rel >= 0, jnp.exp(lg * jnp.maximum(rel, 0.0)), 0.0)
        q_weight = jnp.exp(lg * (t_col + 1.0))
        k_weight = jnp.exp(lg * (c_len - 1.0 - t_col))
        chunk_decay = jnp.exp(jnp.full((1, d), lg * c_len, F32))
        for c in range(chunks):
            rows = slice(c * c_len, (c + 1) * c_len)
            cos = cos_ref[rows, :]
            sin = sin_ref[rows, :]
            q = rotate(q_ref[rows, cols], cos, sin)
            k = rotate(k_ref[rows, cols], cos, sin) * (d ** -0.5)
            v = v_ref[rows, cols].astype(BF16)
            state = state_ref[h]

            scores = lax.dot_general(q.astype(BF16), k.astype(BF16), NT_DIMS,
                                     preferred_element_type=F32) * decay
            intra = jnp.dot(scores.astype(BF16), v, preferred_element_type=F32)
            cross = jnp.dot((q * q_weight).astype(BF16), state.astype(BF16),
                            preferred_element_type=F32)
            kv = lax.dot_general((k * k_weight).astype(BF16), v, TN_DIMS,
                                 preferred_element_type=F32)
            state_ref[h] = chunk_decay * state + kv

            out = intra + cross
            centered = out - jnp.mean(out, axis=-1, keepdims=True)
            normed = centered * lax.rsqrt(
                jnp.mean(centered * centered, axis=-1, keepdims=True) + NORM_EPS)
            gated = (normed * nrm_ref[:, cols]) * _silu(g_ref[rows, cols])
            o_ref[rows, cols] = gated.astype(BF16)


def retention_mixer(proj, ret_norm, batch, seq, *, rows_per_step=512):
    n = proj.shape[0]
    d = RET_HEAD_DIM
    assert seq % rows_per_step == 0 and rows_per_step % RET_CHUNK == 0
    steps = seq // rows_per_step
    log_gamma = jnp.log1p(-jnp.exp2(-5.0 - jnp.arange(RET_HEADS, dtype=F32)))
    half = d // 2
    inv_freq = RET_ROPE_BASE ** (-jnp.arange(half, dtype=F32) * 2.0 / d)
    ang = jnp.arange(seq).astype(F32)[:, None] * inv_freq[None, :]
    cos, sin = jnp.cos(ang), jnp.sin(ang)

    def col_spec(off):
        return pl.BlockSpec((rows_per_step, RET_WIDTH),
                            lambda b, s: (b * steps + s, off // RET_WIDTH))

    table_spec = pl.BlockSpec((rows_per_step, half), lambda b, s: (s, 0))
    return pl.pallas_call(
        functools.partial(_retention_kernel, chunks=rows_per_step // RET_CHUNK),
        grid=(batch, steps),
        in_specs=[pl.BlockSpec(memory_space=pltpu.SMEM),
                  col_spec(OFF_RQ), col_spec(OFF_RK), col_spec(OFF_RV), col_spec(OFF_RG),
                  table_spec, table_spec,
                  pl.BlockSpec((1, RET_WIDTH), lambda b, s: (0, 0))],
        out_specs=pl.BlockSpec((rows_per_step, RET_WIDTH), lambda b, s: (b * steps + s, 0)),
        out_shape=jax.ShapeDtypeStruct((n, RET_WIDTH), BF16),
        scratch_shapes=[pltpu.VMEM((RET_HEADS, d, d), F32)],
        compiler_params=_compiler_params(2, 40),
        name="retention",
    )(log_gamma, proj, proj, proj, proj, cos, sin, ret_norm.reshape(1, RET_WIDTH))


MOBA_PAIRS_PER_STEP = 2


def _moba_rotate(x, cos, sin_lo, sin_hi):
    half = ROPE_DIM // 2
    return (x * cos + pltpu.roll(x, MOBA_HEAD_DIM - half, 1) * sin_lo
            + pltpu.roll(x, half, 1) * sin_hi)


def _split_dot_nt(a, b):
    a_hi = a.astype(BF16)
    a_lo = (a - a_hi.astype(F32)).astype(BF16)
    b_hi = b.astype(BF16)
    b_lo = (b - b_hi.astype(F32)).astype(BF16)

    def dot_nt(x, y):
        return lax.dot_general(x, y, NT_DIMS, preferred_element_type=F32)

    return dot_nt(a_hi, b_hi) + dot_nt(a_hi, b_lo) + dot_nt(a_lo, b_hi)


def _moba_kernel(*refs, n_blocks, n_side):
    q_ref, k_ref, v_ref, cos_ref, slo_ref, shi_ref = refs[:6]
    o_ref = refs[6 + n_side]
    (kaug_ref, vt_ref, kmean_ref, qt_ref, s_ref, acc_ref,
     snap_ref) = refs[7 + 2 * n_side:]
    _run_side_casts(refs[6:6 + n_side], refs[7 + n_side:7 + 2 * n_side])
    blk = MOBA_BLOCK
    d = MOBA_HEAD_DIM
    step = pl.program_id(2)
    n_past = n_blocks - 1
    pairs = snap_ref.shape[0]
    gate_rows = kmean_ref.shape[0]
    groups = blk // SUBLANES
    q_scale = (d ** -0.5) * LOG2_E

    def rotate(ref, start):
        rows = pl.ds(start, blk)
        return _moba_rotate(ref[rows, :], cos_ref[rows, :], slo_ref[rows, :], shi_ref[rows, :])

    @pl.when(step == 0)
    def _():
        lane = lax.broadcasted_iota(jnp.int32, (blk, LANES), 1)
        eye = jnp.where(lax.broadcasted_iota(jnp.int32, (d, d), 0)
                        == lax.broadcasted_iota(jnp.int32, (d, d), 1), 1.0, 0.0).astype(BF16)

        def transposed(x_bf):
            return lax.dot_general(eye, x_bf, NT_DIMS, preferred_element_type=F32)

        kmean_ref[...] = jnp.zeros_like(kmean_ref)
        for j in range(n_blocks):
            kr = rotate(k_ref, j * blk)
            kaug_ref[j, :, :d] = kr.astype(BF16)
            kaug_ref[j, :, d:] = jnp.where(lane == j, 1.0, 0.0).astype(BF16)
            vt_ref[j] = transposed(v_ref[j * blk:(j + 1) * blk, :].astype(BF16)).astype(BF16)
            kmean_ref[j:j + 1, :] = jnp.mean(kr, axis=0, keepdims=True)
        row = lax.broadcasted_iota(jnp.int32, (gate_rows, blk), 0)
        row_f = row.astype(F32)
        pad = jnp.zeros((LANES - gate_rows, blk), F32)
        for i in range(n_blocks):
            q = rotate(q_ref, i * blk)
            gate = _split_dot_nt(kmean_ref[...], q)
            q_t = transposed((q * q_scale).astype(BF16))
            g = jnp.where(row < i, gate, -jnp.inf)
            bias = jnp.full((gate_rows, blk), MASK_VALUE, F32)
            for _ in range(min(MOBA_TOPK, n_blocks)):
                best = jnp.max(g, axis=0, keepdims=True)
                hit = (g == best) & (g > -jnp.inf)
                first = jnp.min(jnp.where(hit, row_f, float(LANES)), axis=0, keepdims=True)
                pick = row_f == first
                bias = jnp.where(pick, 0.0, bias)
                g = jnp.where(pick, -jnp.inf, g)
            bias = jnp.where(row == i, 0.0, bias)
            qt_ref[i] = jnp.concatenate([q_t, bias, pad], axis=0).astype(BF16)

    def group_reduce(x, op):
        return op(x.reshape(groups, SUBLANES, blk), axis=0)

    def score_own_block(i, own_slot):
        s_t = jnp.dot(kaug_ref[i], qt_ref[i], preferred_element_type=F32)
        key = lax.broadcasted_iota(jnp.int32, (blk, blk), 0)
        qry = lax.broadcasted_iota(jnp.int32, (blk, blk), 1)
        s_t = jnp.where(key <= qry, s_t, MASK_VALUE)
        s_ref[own_slot] = s_t
        return group_reduce(s_t, jnp.max)

    def tile_of(t, p):
        in_b = t >= p
        return in_b, jnp.where(in_b, t - p, t)

    def col_max(m8):
        return jnp.broadcast_to(jnp.max(m8, axis=0, keepdims=True), (SUBLANES, blk))

    def weights(slot, m8):
        p_t = jnp.exp2(s_ref[slot].reshape(groups, SUBLANES, blk) - m8[None])
        return jnp.sum(p_t, axis=0), p_t.reshape(blk, blk).astype(BF16)

    def values(j, p_bf):
        return jnp.dot(vt_ref[j], p_bf, preferred_element_type=F32)

    never = jnp.full((SUBLANES, blk), -jnp.inf, F32)
    scored = []
    for k in range(pairs):
        p = step * pairs + k
        base = k * (n_past + 2)
        i_a, i_b = p, n_past - p
        m_a = score_own_block(i_a, base + n_past)
        m_b = score_own_block(i_b, base + n_past + 1)
        for t in range(n_past):
            in_b, j = tile_of(t, p)
            s_t = jnp.dot(kaug_ref[j], qt_ref[jnp.where(in_b, i_b, i_a)],
                          preferred_element_type=F32)
            s_ref[base + t] = s_t
            gm = group_reduce(s_t, jnp.max)
            m_a = jnp.maximum(m_a, jnp.where(in_b, never, gm))
            m_b = jnp.maximum(m_b, jnp.where(in_b, gm, never))
        scored.append((p, base, i_a, i_b, col_max(m_a), col_max(m_b)))

    for k, (p, base, i_a, i_b, m_a, m_b) in enumerate(scored):
        l_a, p_own = weights(base + n_past, m_a)
        acc_ref[2 * k] = values(i_a, p_own)
        l_b, p_own = weights(base + n_past + 1, m_b)
        acc_ref[2 * k + 1] = values(i_b, p_own)
        acc = jnp.zeros((d, blk), F32)
        for t in range(n_past):
            in_b, j = tile_of(t, p)
            l_part, p_bf = weights(base + t, jnp.where(in_b, m_b, m_a))
            acc = acc * jnp.where(t == p, 0.0, 1.0) + values(j, p_bf)
            l_a = l_a + jnp.where(in_b, 0.0, l_part)
            l_b = l_b + jnp.where(in_b, l_part, 0.0)
            if t < snap_ref.shape[1]:
                snap_ref[k, t] = acc
        past_a = snap_ref[k, jnp.maximum(p - 1, 0)] * jnp.where(p > 0, 1.0, 0.0)
        for w, i, l8, past in ((2 * k, i_a, l_a, past_a), (2 * k + 1, i_b, l_b, acc)):
            l = jnp.sum(l8, axis=0, keepdims=True)
            out_t = (acc_ref[w] + past) / l
            o_ref[pl.ds(pl.multiple_of(i * blk, blk), blk), :] = out_t.T.astype(BF16)


def moba_grid_steps(batch, seq):
    return batch * MOBA_HEADS * (seq // MOBA_BLOCK // 2 // MOBA_PAIRS_PER_STEP)


def moba_mixer(proj, batch, seq, side_casts=()):
    n = proj.shape[0]
    d = MOBA_HEAD_DIM
    blk = MOBA_BLOCK
    assert seq % blk == 0
    n_blocks = seq // blk
    assert n_blocks <= LANES and n_blocks % 2 == 0
    half = ROPE_DIM // 2
    inv_freq = ROPE_THETA ** (-jnp.arange(half, dtype=F32) * 2.0 / ROPE_DIM)
    ang = jnp.arange(seq).astype(F32)[:, None] * inv_freq[None, :]
    cos, sin = jnp.cos(ang), jnp.sin(ang)
    zeros = jnp.zeros((seq, d - ROPE_DIM), F32)
    zero_half = jnp.zeros((seq, half), F32)
    cos_t = jnp.concatenate([cos, cos, jnp.ones_like(zeros)], axis=1)
    sin_lo = jnp.concatenate([-sin, zero_half, zeros], axis=1)
    sin_hi = jnp.concatenate([zero_half, sin, zeros], axis=1)

    def col_spec(off):
        return pl.BlockSpec((seq, d), lambda b, h, p: (b, off // d + h))

    table_spec = pl.BlockSpec((seq, d), lambda b, h, p: (0, 0))
    gate_rows = -(-n_blocks // SUBLANES) * SUBLANES
    pairs = MOBA_PAIRS_PER_STEP
    assert (n_blocks // 2) % pairs == 0
    n_snap = max(n_blocks // 2 - 1, 1)
    steps = n_blocks // 2 // pairs

    def step_of(b, h, p):
        return (b * MOBA_HEADS + h) * steps + p

    out, *casts = pl.pallas_call(
        functools.partial(_moba_kernel, n_blocks=n_blocks, n_side=len(side_casts)),
        grid=(batch, MOBA_HEADS, steps),
        in_specs=[col_spec(OFF_MQ), col_spec(OFF_MK), col_spec(OFF_MV),
                  table_spec, table_spec, table_spec,
                  *[c.in_spec(step_of) for c in side_casts]],
        out_specs=[pl.BlockSpec((seq, d), lambda b, h, p: (b, h)),
                   *[c.out_spec(step_of) for c in side_casts]],
        out_shape=[jax.ShapeDtypeStruct((n, MOBA_WIDTH), BF16),
                   *[c.out_shape() for c in side_casts]],
        scratch_shapes=[pltpu.VMEM((n_blocks, blk, 2 * d), BF16),
                        pltpu.VMEM((n_blocks, d, blk), BF16),
                        pltpu.VMEM((gate_rows, d), F32),
                        pltpu.VMEM((n_blocks, 2 * d, blk), BF16),
                        pltpu.VMEM((pairs * (n_blocks + 1), blk, blk), F32),
                        pltpu.VMEM((2 * pairs, d, blk), F32),
                        pltpu.VMEM((pairs, n_snap, d, blk), F32)],
        compiler_params=_compiler_params(3, 54),
        name="moba",
    )(proj, proj, proj, cos_t, sin_lo, sin_hi, *[c.operand for c in side_casts])
    return out, [c.finish(x) for c, x in zip(side_casts, casts)]


def _hgrn_pair_codes(block):
    i = np.arange(block)[:, None]
    j = np.arange(block)[None, :]
    top_bit = np.floor(np.log2(np.maximum(i ^ j, 1))).astype(np.int32)
    n_levels = block.bit_length() - 1
    return np.where(i > j, top_bit, np.where(i == j, n_levels, -1)).astype(np.int32)


def _hgrn_kernel(*refs, layer, rows_per_step, n_side):
    lbp_ref, tril_ref, code_ref, q_ref, f_ref, v_ref, g_ref, nrm_ref = refs[:8]
    o_ref = refs[8 + n_side]
    state_ref, b_ref, k_ref = refs[9 + 2 * n_side:]
    _run_side_casts(refs[8:8 + n_side], refs[9 + n_side:9 + 2 * n_side])
    c_len = HGRN_BLOCK
    d = HGRN_HEAD_DIM

    @pl.when(pl.program_id(2) == 0)
    def _():
        state_ref[...] = jnp.zeros_like(state_ref)

    params = lbp_ref[...]
    e = jnp.exp(params - jnp.max(params, axis=0, keepdims=True))
    soft = e / jnp.sum(e, axis=0, keepdims=True)
    lower = jnp.zeros((1, d), F32)
    for r in range(1, layer + 1):
        lower = lower + soft[r:r + 1, :]

    tril = tril_ref[...]
    for seg0 in range(0, rows_per_step, c_len):
        seg = slice(seg0, seg0 + c_len)
        forget = lower + (1.0 - lower) * jax.nn.sigmoid(f_ref[seg, :])
        k_ref[seg, :] = 1.0 - forget
        log_f = jnp.log2(forget)
        hi = log_f.astype(BF16)
        rest = log_f - hi.astype(F32)
        mid = rest.astype(BF16)
        lo = (rest - mid.astype(F32)).astype(BF16)
        b_ref[seg, :] = (jnp.dot(tril, hi, preferred_element_type=F32)
                         + jnp.dot(tril, mid, preferred_element_type=F32)
                         + jnp.dot(tril, lo, preferred_element_type=F32))

    assert c_len & (c_len - 1) == 0 and c_len % LANES == 0
    levels = [1 << s for s in range(c_len.bit_length() - 1)]
    n_grp = c_len // SUBLANES
    row8 = lax.broadcasted_iota(jnp.int32, (SUBLANES, d), 0)
    zero8 = jnp.zeros((SUBLANES, d), F32)
    codes = _hgrn_pair_codes(c_len)
    tiles = [(g, cb) for g in range(n_grp) for cb in range(c_len // LANES)]

    def code_tile(g, cb):
        return codes[SUBLANES * g:SUBLANES * (g + 1), LANES * cb:LANES * (cb + 1)]

    level_tiles = [[(g, cb, bool(np.all(code_tile(g, cb) == bit))) for g, cb in tiles
                    if np.any(code_tile(g, cb) == bit)] for bit in range(len(levels) + 1)]

    def bcast_row(ref, r):
        return jnp.broadcast_to(ref[r:r + 1, :], (SUBLANES, d))

    def level_operands(r0, h, bg, qg, kg):
        q_parts, k_parts = [], []
        for g in range(n_grp):
            base = SUBLANES * g
            if h >= SUBLANES:
                b_a = bcast_row(b_ref, r0 + base // (2 * h) * (2 * h) + h - 1)
                if base & h:
                    q_parts.append(qg[g] * jnp.exp2(bg[g] - b_a))
                    k_parts.append(zero8)
                else:
                    q_parts.append(zero8)
                    k_parts.append(kg[g] * jnp.exp2(b_a - bg[g]))
                continue
            upper = (row8 & h) != 0
            if h == 1:
                q_parts.append(jnp.where(upper, qg[g] * (1.0 - kg[g]), 0.0))
                k_parts.append(jnp.where(upper, 0.0, kg[g]))
                continue
            b_a = bcast_row(b_ref, r0 + base + h - 1)
            for s in range(2 * h, SUBLANES, 2 * h):
                b_a = jnp.where(row8 >= s, bcast_row(b_ref, r0 + base + s + h - 1), b_a)
            decay = jnp.exp2(-jnp.abs(bg[g] - b_a))
            q_parts.append(jnp.where(upper, qg[g] * decay, 0.0))
            k_parts.append(jnp.where(upper, 0.0, kg[g] * decay))
        return (jnp.concatenate(q_parts, axis=0).astype(BF16),
                jnp.concatenate(k_parts, axis=0).astype(BF16))

    for c in range(rows_per_step // c_len):
        r0 = c * c_len
        rows = slice(r0, r0 + c_len)
        b = b_ref[rows, :]
        q = _silu(q_ref[rows, :])
        k = k_ref[rows, :]
        v = v_ref[rows, :]

        groups = [slice(SUBLANES * g, SUBLANES * (g + 1)) for g in range(n_grp)]
        bg = [b[g, :] for g in groups]
        qg = [q[g, :] for g in groups]
        kg = [k[g, :] for g in groups]
        def code_of(g, cb):
            return code_ref[SUBLANES * g:SUBLANES * (g + 1), LANES * cb:LANES * (cb + 1)]

        a_tiles = {}
        self_weight = jnp.sum(q * k, axis=-1, keepdims=True)
        for g, cb, _ in level_tiles[len(levels)]:
            a_tiles[g, cb] = jnp.where(code_of(g, cb) == len(levels),
                                       self_weight[groups[g], :], 0.0)
        for bit, h in enumerate(levels):
            q_h, k_h = level_operands(r0, h, bg, qg, kg)
            for cb in range(c_len // LANES):
                k_lo = LANES * cb
                if 2 * h >= LANES:
                    start = k_lo // (2 * h) * (2 * h)
                    if k_lo >= start + h:
                        continue
                    q_lo, q_hi = start + h, start + 2 * h
                else:
                    q_lo, q_hi = k_lo, k_lo + LANES
                pair = lax.dot_general(q_h[q_lo:q_hi, :], k_h[k_lo:k_lo + LANES, :], NT_DIMS,
                                       preferred_element_type=F32)
                for g, tile_cb, owns_tile in level_tiles[bit]:
                    if tile_cb != cb:
                        continue
                    assert q_lo <= SUBLANES * g < q_hi
                    piece = pair[SUBLANES * g - q_lo:SUBLANES * (g + 1) - q_lo, :]
                    if owns_tile:
                        a_tiles[g, cb] = piece
                    else:
                        a_tiles[g, cb] = jnp.where(code_of(g, cb) == bit, piece,
                                                   a_tiles.get((g, cb), 0.0))
        zero_tile = jnp.zeros((SUBLANES, LANES), F32)
        a_mat = jnp.concatenate(
            [jnp.concatenate([a_tiles.get((g, cb), zero_tile) for g in range(n_grp)], axis=0)
             for cb in range(c_len // LANES)], axis=1)
        intra = jnp.dot(a_mat.astype(BF16), v.astype(BF16), preferred_element_type=F32)

        state_t = state_ref[...]
        cross = lax.dot_general((q * jnp.exp2(b)).astype(BF16), state_t.astype(BF16), NT_DIMS,
                                preferred_element_type=F32)
        b_last = b_ref[r0 + c_len - 1:r0 + c_len, :]
        k_dec = k * jnp.exp2(b_last - b)
        state_ref[...] = state_t * jnp.exp2(b_last) + lax.dot_general(
            v.astype(BF16), k_dec.astype(BF16), TN_DIMS, preferred_element_type=F32)

        out = intra + cross
        normed = out * lax.rsqrt(jnp.mean(out * out, axis=-1, keepdims=True) + NORM_EPS)
        gated = (normed * nrm_ref[...]) * _silu(g_ref[rows, :])
        o_ref[rows, :] = gated.astype(BF16)


HGRN_ROWS_PER_STEP = 2048


def hgrn_grid_steps(batch, seq):
    return batch * HGRN_HEADS * (seq // min(seq, HGRN_ROWS_PER_STEP))


def hgrn_mixer(proj, lower_bound_params, hgrn_norm, layer, batch, seq, side_casts=()):
    rows_per_step = min(seq, HGRN_ROWS_PER_STEP)
    n = proj.shape[0]
    d = HGRN_HEAD_DIM
    depth = lower_bound_params.shape[0]
    blk = HGRN_BLOCK
    assert seq % rows_per_step == 0 and rows_per_step % blk == 0
    steps = seq // rows_per_step
    codes = _hgrn_pair_codes(blk)
    tril = jnp.asarray(codes >= 0, BF16)

    def col_spec(off):
        return pl.BlockSpec((rows_per_step, d), lambda b, h, s: (b * steps + s, off // d + h))

    def step_of(b, h, s):
        return (b * HGRN_HEADS + h) * steps + s

    out, *casts = pl.pallas_call(
        functools.partial(_hgrn_kernel, layer=layer, rows_per_step=rows_per_step,
                          n_side=len(side_casts)),
        grid=(batch, HGRN_HEADS, steps),
        in_specs=[pl.BlockSpec((depth, d), lambda b, h, s: (0, h)),
                  pl.BlockSpec((blk, blk), lambda b, h, s: (0, 0)),
                  pl.BlockSpec((blk, blk), lambda b, h, s: (0, 0)),
                  col_spec(OFF_HQ), col_spec(OFF_HF), col_spec(OFF_HI), col_spec(OFF_HG),
                  pl.BlockSpec((1, d), lambda b, h, s: (0, h)),
                  *[c.in_spec(step_of) for c in side_casts]],
        out_specs=[pl.BlockSpec((rows_per_step, d), lambda b, h, s: (b * steps + s, h)),
                   *[c.out_spec(step_of) for c in side_casts]],
        out_shape=[jax.ShapeDtypeStruct((n, HGRN_WIDTH), BF16),
                   *[c.out_shape() for c in side_casts]],
        scratch_shapes=[pltpu.VMEM((d, d), F32),
                        pltpu.VMEM((rows_per_step, d), F32),
                        pltpu.VMEM((rows_per_step, d), F32)],
        compiler_params=_compiler_params(3, 40),
        name="hgrn2",
    )(lower_bound_params.astype(F32), tril, jnp.asarray(codes), proj, proj, proj, proj,
      hgrn_norm.reshape(1, HGRN_WIDTH), *[c.operand for c in side_casts])
    return out, [c.finish(x) for c, x in zip(side_casts, casts)]


def kernel(x, hgrn_lower_bounds, norm_mix, w_in, ret_norm, hgrn_norm, w_branch_ret, w_branch_moba,
           w_branch_hgrn, w_out, norm_ffn, w_ffn_gate, w_ffn_up, w_ffn_down, final_norm):
    batch, seq, d_model = x.shape
    assert d_model == D_MODEL
    depth = w_in.shape[0]
    w_in_layers = [cast_layer(w_in, 0)]
    moba_steps = moba_grid_steps(batch, seq)
    hgrn_steps = hgrn_grid_steps(batch, seq)
    moba_casts = [SideCast(w, moba_steps) for w in (w_ffn_gate, w_ffn_up, w_ffn_down)]
    hgrn_casts = [SideCast(w, hgrn_steps)
                  for w in (w_branch_ret, w_branch_moba, w_branch_hgrn, w_out)]
    hgrn_casts += [SideCast(w_in, hgrn_steps, part=l, n_parts=depth) for l in range(1, depth)]
    h = x.reshape(batch * seq, d_model)
    for layer in range(depth):
        proj = norm_matmul(h, norm_mix[layer], w_in_layers[layer], 0)
        ret = retention_mixer(proj, ret_norm[layer], batch, seq)
        moba, cast = moba_mixer(proj, batch, seq, moba_casts if layer == 0 else ())
        if layer == 0:
            w_ffn_gate, w_ffn_up, w_ffn_down = cast
        hgrn, cast = hgrn_mixer(proj, hgrn_lower_bounds, hgrn_norm[layer], layer, batch, seq,
                                hgrn_casts if layer == 0 else ())
        if layer == 0:
            w_branch_ret, w_branch_moba, w_branch_hgrn, w_out = cast[:4]
            w_in_layers += cast[4:]
        h = merge_out_proj(ret, moba, hgrn, proj, w_branch_ret, w_branch_moba, w_branch_hgrn,
                           w_out, h, layer)
        act = ffn_up(h, norm_ffn[layer], w_ffn_gate, w_ffn_up, layer)
        h = ffn_down(act, w_ffn_down, h, layer, final_norm, normalize=layer == depth - 1)
    return h.reshape(batch, seq, d_model)
```

```python
import functools
import math

import jax
import jax.numpy as jnp
import numpy as np
from jax import lax
from jax.experimental import pallas as pl
from jax.experimental.pallas import tpu as pltpu

F32 = jnp.float32
BF16 = jnp.bfloat16

D_MODEL = 2048
RET_HEADS = 4
RET_HEAD_DIM = 256
RET_WIDTH = RET_HEADS * RET_HEAD_DIM
RET_CHUNK = 128
RET_ROPE_BASE = 10000.0
MOBA_HEADS = 8
MOBA_HEAD_DIM = 128
MOBA_WIDTH = MOBA_HEADS * MOBA_HEAD_DIM
MOBA_BLOCK = 256
MOBA_TOPK = 3
ROPE_THETA = 500000.0
ROPE_DIM = MOBA_HEAD_DIM // 4
HGRN_HEADS = 8
HGRN_HEAD_DIM = 128
HGRN_WIDTH = HGRN_HEADS * HGRN_HEAD_DIM
HGRN_BLOCK = 256
NORM_EPS = 1e-6
IN_SIZES = (RET_WIDTH,) * 4 + (MOBA_WIDTH,) * 3 + (HGRN_WIDTH,) * 4 + (D_MODEL,) * 3
IN_COLS = sum(IN_SIZES)
IN_OFFS = tuple(sum(IN_SIZES[:i]) for i in range(len(IN_SIZES)))
(OFF_RQ, OFF_RK, OFF_RV, OFF_RG, OFF_MQ, OFF_MK, OFF_MV,
 OFF_HQ, OFF_HF, OFF_HI, OFF_HG, OFF_GR, OFF_GM, OFF_GH) = IN_OFFS

V7X_VMEM_BYTES = 64 * 1024 * 1024
LANES = 128
SUBLANES = 8
MASK_VALUE = -1e30
LOG2_E = math.log2(math.e)

NT_DIMS = (((1,), (1,)), ((), ()))
TN_DIMS = (((0,), (0,)), ((), ()))


def _compiler_params(n_grid_dims, vmem_mib):
    assert vmem_mib * 1024 * 1024 < V7X_VMEM_BYTES
    return pltpu.CompilerParams(
        dimension_semantics=("arbitrary",) * n_grid_dims,
        vmem_limit_bytes=vmem_mib * 1024 * 1024)


def _silu(x):
    return x * jax.nn.sigmoid(x)


BF16_SUBLANES = 16


class SideCast:
    def __init__(self, array, n_steps, part=0, n_parts=1):
        layers, rows, self.cols = array.shape
        part_rows = layers // n_parts * rows
        self.steps_per_slab = 1
        while part_rows * self.steps_per_slab % (n_steps * BF16_SUBLANES):
            self.steps_per_slab *= 2
        assert n_steps % self.steps_per_slab == 0
        self.n_slabs = n_steps // self.steps_per_slab
        self.slab_rows = part_rows // self.n_slabs
        self.first_slab = part * self.n_slabs
        self.operand = array.reshape(n_parts * self.n_slabs, self.slab_rows, self.cols)
        self.part_shape = (layers // n_parts, rows, self.cols)

    def in_spec(self, step_of):
        return pl.BlockSpec(
            (None, self.slab_rows, self.cols),
            lambda *g: (self.first_slab + step_of(*g) // self.steps_per_slab, 0, 0))

    def out_spec(self, step_of):
        return pl.BlockSpec((None, self.slab_rows, self.cols),
                            lambda *g: (step_of(*g) // self.steps_per_slab, 0, 0))

    def out_shape(self):
        return jax.ShapeDtypeStruct((self.n_slabs, self.slab_rows, self.cols), BF16)

    def finish(self, cast):
        return cast.reshape(self.part_shape)


def _run_side_casts(in_refs, out_refs):
    for src, dst in zip(in_refs, out_refs):
        dst[...] = src[...].astype(BF16)


def _cast_kernel(src_ref, dst_ref):
    dst_ref[...] = src_ref[...].astype(BF16)


def cast_layer(w, layer, *, tn=1024):
    _, rows, cols = w.shape
    assert cols % tn == 0
    return pl.pallas_call(
        _cast_kernel,
        grid=(cols // tn,),
        in_specs=[pl.BlockSpec((None, rows, tn), lambda j: (layer, 0, j))],
        out_specs=pl.BlockSpec((None, rows, tn), lambda j: (0, 0, j)),
        out_shape=jax.ShapeDtypeStruct((1, rows, cols), BF16),
        compiler_params=_compiler_params(1, 40),
        name="cast_layer",
    )(w)


def _layer_weight_spec(layer, rows, tn, col_of):
    return pl.BlockSpec((None, rows, tn), lambda *g: (layer, 0, col_of(*g)))


def _rms_normalize(x, g):
    r = lax.rsqrt(jnp.mean(x * x, axis=-1, keepdims=True) + NORM_EPS)
    return x * r * g


def _norm_matmul_kernel(x_ref, g_ref, w_ref, o_ref, h_ref):
    @pl.when(pl.program_id(1) == 0)
    def _():
        h_ref[...] = _rms_normalize(x_ref[...], g_ref[...]).astype(BF16)

    o_ref[...] = jnp.dot(h_ref[...], w_ref[...], preferred_element_type=F32)


def norm_matmul(x, g, w, layer, *, tm=1024, tn=1024):
    n, d = x.shape
    cols = w.shape[2]
    assert n % tm == 0 and cols % tn == 0
    return pl.pallas_call(
        _norm_matmul_kernel,
        grid=(n // tm, cols // tn),
        in_specs=[pl.BlockSpec((tm, d), lambda i, j: (i, 0)),
                  pl.BlockSpec((1, d), lambda i, j: (0, 0)),
                  _layer_weight_spec(layer, d, tn, lambda i, j: j)],
        out_specs=pl.BlockSpec((tm, tn), lambda i, j: (i, j)),
        out_shape=jax.ShapeDtypeStruct((n, cols), F32),
        scratch_shapes=[pltpu.VMEM((tm, d), BF16)],
        compiler_params=_compiler_params(2, 48),
        name="norm_inproj",
    )(x, g.reshape(1, d), w)


def _ffn_up_kernel(x_ref, g_ref, wg_ref, wu_ref, o_ref, h_ref):
    @pl.when(pl.program_id(1) == 0)
    def _():
        h_ref[...] = _rms_normalize(x_ref[...], g_ref[...]).astype(BF16)

    h = h_ref[...]
    gate = jnp.dot(h, wg_ref[...], preferred_element_type=F32)
    up = jnp.dot(h, wu_ref[...], preferred_element_type=F32)
    o_ref[...] = (_silu(gate) * up).astype(BF16)


def ffn_up(x, g, wg, wu, layer, *, tm=1024, tn=512):
    n, d = x.shape
    hidden = wg.shape[2]
    assert n % tm == 0 and hidden % tn == 0
    w_spec = _layer_weight_spec(layer, d, tn, lambda i, j: j)
    return pl.pallas_call(
        _ffn_up_kernel,
        grid=(n // tm, hidden // tn),
        in_specs=[pl.BlockSpec((tm, d), lambda i, j: (i, 0)),
                  pl.BlockSpec((1, d), lambda i, j: (0, 0)),
                  w_spec, w_spec],
        out_specs=pl.BlockSpec((tm, tn), lambda i, j: (i, j)),
        out_shape=jax.ShapeDtypeStruct((n, hidden), BF16),
        scratch_shapes=[pltpu.VMEM((tm, d), BF16)],
        compiler_params=_compiler_params(2, 48),
        name="ffn_up",
    )(x, g.reshape(1, d), wg, wu)


def _ffn_down_kernel(a_ref, w_ref, r_ref, g_ref, o_ref, *, normalize):
    out = r_ref[...] + jnp.dot(a_ref[...], w_ref[...], preferred_element_type=F32)
    o_ref[...] = _rms_normalize(out, g_ref[...]) if normalize else out


def ffn_down(a, w, res, layer, gain, *, normalize, tm=512):
    n, k = a.shape
    cols = w.shape[2]
    assert n % tm == 0
    row_spec = pl.BlockSpec((tm, cols), lambda i: (i, 0))
    return pl.pallas_call(
        functools.partial(_ffn_down_kernel, normalize=normalize),
        grid=(n // tm,),
        in_specs=[pl.BlockSpec((tm, k), lambda i: (i, 0)),
                  pl.BlockSpec((None, k, cols), lambda i: (layer, 0, 0),
                               pipeline_mode=pl.Buffered(1)),
                  row_spec,
                  pl.BlockSpec((1, cols), lambda i: (0, 0))],
        out_specs=row_spec,
        out_shape=jax.ShapeDtypeStruct((n, cols), F32),
        compiler_params=_compiler_params(1, 54),
        name="ffn_down",
    )(a, w, res, gain.reshape(1, cols))


MERGE_GATE_TILE = 1024


def _merge_out_kernel(*refs):
    n_gate = D_MODEL // MERGE_GATE_TILE
    branch_refs = refs[:3]
    gate_refs = refs[3:3 + 3 * n_gate]
    wb_refs = refs[3 + 3 * n_gate:6 + 3 * n_gate]
    wo_ref, res_ref, o_ref = refs[6 + 3 * n_gate:]
    parts = []
    for c in range(n_gate):
        cols = slice(c * MERGE_GATE_TILE, (c + 1) * MERGE_GATE_TILE)
        mixed = None
        for b in range(3):
            term = jax.nn.sigmoid(gate_refs[b * n_gate + c][...]) * jnp.dot(
                branch_refs[b][...], wb_refs[b][:, cols], preferred_element_type=F32)
            mixed = term if mixed is None else mixed + term
        parts.append(mixed.astype(BF16))
    mixed = jnp.concatenate(parts, axis=1)
    o_ref[...] = res_ref[...] + jnp.dot(mixed, wo_ref[...], preferred_element_type=F32)


def merge_out_proj(ret, moba, hgrn, proj, wr, wm, wh, wo, res, layer, *, tm=256):
    n = ret.shape[0]
    assert n % tm == 0 and D_MODEL % MERGE_GATE_TILE == 0
    assert all(off % MERGE_GATE_TILE == 0 for off in (OFF_GR, OFF_GM, OFF_GH))
    n_gate = D_MODEL // MERGE_GATE_TILE

    def branch_spec(width):
        return pl.BlockSpec((tm, width), lambda i: (i, 0))

    def gate_specs(off):
        return [pl.BlockSpec((tm, MERGE_GATE_TILE),
                             functools.partial(lambda i, col: (i, col), col=off // MERGE_GATE_TILE + c))
                for c in range(n_gate)]

    def weight_spec(rows):
        return pl.BlockSpec((None, rows, D_MODEL), lambda i: (layer, 0, 0),
                            pipeline_mode=pl.Buffered(1))

    row_spec = pl.BlockSpec((tm, D_MODEL), lambda i: (i, 0))
    return pl.pallas_call(
        _merge_out_kernel,
        grid=(n // tm,),
        in_specs=[branch_spec(RET_WIDTH), branch_spec(MOBA_WIDTH), branch_spec(HGRN_WIDTH),
                  *gate_specs(OFF_GR), *gate_specs(OFF_GM), *gate_specs(OFF_GH),
                  weight_spec(RET_WIDTH), weight_spec(MOBA_WIDTH), weight_spec(HGRN_WIDTH),
                  weight_spec(D_MODEL), row_spec],
        out_specs=row_spec,
        out_shape=jax.ShapeDtypeStruct((n, D_MODEL), F32),
        compiler_params=_compiler_params(1, 52),
        name="merge_out_proj",
    )(ret, moba, hgrn, *([proj] * (3 * n_gate)), wr, wm, wh, wo, res)


def _retention_kernel(lg_ref, q_ref, k_ref, v_ref, g_ref, cos_ref, sin_ref, nrm_ref,
                      o_ref, state_ref, *, chunks):
    c_len = RET_CHUNK
    d = RET_HEAD_DIM
    half = d // 2

    @pl.when(pl.program_id(1) == 0)
    def _():
        state_ref[...] = jnp.zeros_like(state_ref)

    t_col = lax.broadcasted_iota(jnp.int32, (c_len, 1), 0).astype(F32)
    rel = (lax.broadcasted_iota(jnp.int32, (c_len, c_len), 0)
           - lax.broadcasted_iota(jnp.int32, (c_len, c_len), 1)).astype(F32)

    def rotate(x, cos, sin):
        x1 = x[:, :half]
        x2 = x[:, half:]
        return jnp.concatenate([x1 * cos - x2 * sin, x2 * cos + x1 * sin], axis=1)

    for h in range(RET_HEADS):
        cols = slice(h * d, (h + 1) * d)
        lg = lg_ref[h]
        decay = jnp.where(rel >= 0, jnp.exp(lg * jnp.maximum(rel, 0.0)), 0.0)
        q_weight = jnp.exp(lg * (t_col + 1.0))
        k_weight = jnp.exp(lg * (c_len - 1.0 - t_col))
        chunk_decay = jnp.exp(jnp.full((1, d), lg * c_len, F32))
        for c in range(chunks):
            rows = slice(c * c_len, (c + 1) * c_len)
            cos = cos_ref[rows, :]
            sin = sin_ref[rows, :]
            q = rotate(q_ref[rows, cols], cos, sin)
            k = rotate(k_ref[rows, cols], cos, sin) * (d ** -0.5)
            v = v_ref[rows, cols].astype(BF16)
            state = state_ref[h]

            scores = lax.dot_general(q.astype(BF16), k.astype(BF16), NT_DIMS,
                                     preferred_element_type=F32) * decay
            intra = jnp.dot(scores.astype(BF16), v, preferred_element_type=F32)
            cross = jnp.dot((q * q_weight).astype(BF16), state.astype(BF16),
                            preferred_element_type=F32)
            kv = lax.dot_general((k * k_weight).astype(BF16), v, TN_DIMS,
                                 preferred_element_type=F32)
            state_ref[h] = chunk_decay * state + kv

            out = intra + cross
            centered = out - jnp.mean(out, axis=-1, keepdims=True)
            normed = centered * lax.rsqrt(
                jnp.mean(centered * centered, axis=-1, keepdims=True) + NORM_EPS)
            gated = (normed * nrm_ref[:, cols]) * _silu(g_ref[rows, cols])
            o_ref[rows, cols] = gated.astype(BF16)


def retention_mixer(proj, ret_norm, batch, seq, *, rows_per_step=1024):
    n = proj.shape[0]
    d = RET_HEAD_DIM
    assert seq % rows_per_step == 0 and rows_per_step % RET_CHUNK == 0
    steps = seq // rows_per_step
    log_gamma = jnp.log1p(-jnp.exp2(-5.0 - jnp.arange(RET_HEADS, dtype=F32)))
    half = d // 2
    inv_freq = RET_ROPE_BASE ** (-jnp.arange(half, dtype=F32) * 2.0 / d)
    ang = jnp.arange(seq).astype(F32)[:, None] * inv_freq[None, :]
    cos, sin = jnp.cos(ang), jnp.sin(ang)

    def col_spec(off):
        return pl.BlockSpec((rows_per_step, RET_WIDTH),
                            lambda b, s: (b * steps + s, off // RET_WIDTH))

    table_spec = pl.BlockSpec((rows_per_step, half), lambda b, s: (s, 0))
    return pl.pallas_call(
        functools.partial(_retention_kernel, chunks=rows_per_step // RET_CHUNK),
        grid=(batch, steps),
        in_specs=[pl.BlockSpec(memory_space=pltpu.SMEM),
                  col_spec(OFF_RQ), col_spec(OFF_RK), col_spec(OFF_RV), col_spec(OFF_RG),
                  table_spec, table_spec,
                  pl.BlockSpec((1, RET_WIDTH), lambda b, s: (0, 0))],
        out_specs=pl.BlockSpec((rows_per_step, RET_WIDTH), lambda b, s: (b * steps + s, 0)),
        out_shape=jax.ShapeDtypeStruct((n, RET_WIDTH), BF16),
        scratch_shapes=[pltpu.VMEM((RET_HEADS, d, d), F32)],
        compiler_params=_compiler_params(2, 52),
        name="retention",
    )(log_gamma, proj, proj, proj, proj, cos, sin, ret_norm.reshape(1, RET_WIDTH))


MOBA_PAIRS_PER_STEP = 2


def _moba_rotate(x, cos, sin_lo, sin_hi):
    half = ROPE_DIM // 2
    return (x * cos + pltpu.roll(x, MOBA_HEAD_DIM - half, 1) * sin_lo
            + pltpu.roll(x, half, 1) * sin_hi)


def _split_dot_nt(a, b):
    a_hi = a.astype(BF16)
    a_lo = (a - a_hi.astype(F32)).astype(BF16)
    b_hi = b.astype(BF16)
    b_lo = (b - b_hi.astype(F32)).astype(BF16)

    def dot_nt(x, y):
        return lax.dot_general(x, y, NT_DIMS, preferred_element_type=F32)

    return dot_nt(a_hi, b_hi) + dot_nt(a_hi, b_lo) + dot_nt(a_lo, b_hi)


def _moba_kernel(*refs, n_blocks, n_side):
    q_ref, k_ref, v_ref, cos_ref, slo_ref, shi_ref = refs[:6]
    o_ref = refs[6 + n_side]
    (kaug_ref, vt_ref, kmean_ref, qt_ref, s_ref, acc_ref,
     snap_ref) = refs[7 + 2 * n_side:]
    _run_side_casts(refs[6:6 + n_side], refs[7 + n_side:7 + 2 * n_side])
    blk = MOBA_BLOCK
    d = MOBA_HEAD_DIM
    step = pl.program_id(2)
    n_past = n_blocks - 1
    pairs = snap_ref.shape[0]
    gate_rows = kmean_ref.shape[0]
    groups = blk // SUBLANES
    q_scale = (d ** -0.5) * LOG2_E

    def rotate(ref, start):
        rows = pl.ds(start, blk)
        return _moba_rotate(ref[rows, :], cos_ref[rows, :], slo_ref[rows, :], shi_ref[rows, :])

    @pl.when(step == 0)
    def _():
        lane = lax.broadcasted_iota(jnp.int32, (blk, LANES), 1)
        eye = jnp.where(lax.broadcasted_iota(jnp.int32, (d, d), 0)
                        == lax.broadcasted_iota(jnp.int32, (d, d), 1), 1.0, 0.0).astype(BF16)

        def transposed(x_bf):
            return lax.dot_general(eye, x_bf, NT_DIMS, preferred_element_type=F32)

        kmean_ref[...] = jnp.zeros_like(kmean_ref)
        for j in range(n_blocks):
            kr = rotate(k_ref, j * blk)
            kaug_ref[j, :, :d] = kr.astype(BF16)
            kaug_ref[j, :, d:] = jnp.where(lane == j, 1.0, 0.0).astype(BF16)
            vt_ref[j] = transposed(v_ref[j * blk:(j + 1) * blk, :].astype(BF16)).astype(BF16)
            kmean_ref[j:j + 1, :] = jnp.mean(kr, axis=0, keepdims=True)
        row = lax.broadcasted_iota(jnp.int32, (gate_rows, blk), 0)
        row_f = row.astype(F32)
        pad = jnp.zeros((LANES - gate_rows, blk), F32)
        for i in range(n_blocks):
            q = rotate(q_ref, i * blk)
            gate = _split_dot_nt(kmean_ref[...], q)
            q_t = transposed((q * q_scale).astype(BF16))
            g = jnp.where(row < i, gate, -jnp.inf)
            bias = jnp.full((gate_rows, blk), MASK_VALUE, F32)
            for _ in range(min(MOBA_TOPK, n_blocks)):
                best = jnp.max(g, axis=0, keepdims=True)
                hit = (g == best) & (g > -jnp.inf)
                first = jnp.min(jnp.where(hit, row_f, float(LANES)), axis=0, keepdims=True)
                pick = row_f == first
                bias = jnp.where(pick, 0.0, bias)
                g = jnp.where(pick, -jnp.inf, g)
            bias = jnp.where(row == i, 0.0, bias)
            qt_ref[i] = jnp.concatenate([q_t, bias, pad], axis=0).astype(BF16)

    def group_reduce(x, op):
        return op(x.reshape(groups, SUBLANES, blk), axis=0)

    def score_own_block(i, own_slot):
        s_t = jnp.dot(kaug_ref[i], qt_ref[i], preferred_element_type=F32)
        key = lax.broadcasted_iota(jnp.int32, (blk, blk), 0)
        qry = lax.broadcasted_iota(jnp.int32, (blk, blk), 1)
        s_t = jnp.where(key <= qry, s_t, MASK_VALUE)
        s_ref[own_slot] = s_t
        return group_reduce(s_t, jnp.max)

    def tile_of(t, p):
        in_b = t >= p
        return in_b, jnp.where(in_b, t - p, t)

    def col_max(m8):
        return jnp.broadcast_to(jnp.max(m8, axis=0, keepdims=True), (SUBLANES, blk))

    def weights(slot, m8):
        p_t = jnp.exp2(s_ref[slot].reshape(groups, SUBLANES, blk) - m8[None])
        return jnp.sum(p_t, axis=0), p_t.reshape(blk, blk).astype(BF16)

    def values(j, p_bf):
        return jnp.dot(vt_ref[j], p_bf, preferred_element_type=F32)

    never = jnp.full((SUBLANES, blk), -jnp.inf, F32)
    scored = []
    for k in range(pairs):
        p = step * pairs + k
        base = k * (n_past + 2)
        i_a, i_b = p, n_past - p
        m_a = score_own_block(i_a, base + n_past)
        m_b = score_own_block(i_b, base + n_past + 1)
        for t in range(n_past):
            in_b, j = tile_of(t, p)
            s_t = jnp.dot(kaug_ref[j], qt_ref[jnp.where(in_b, i_b, i_a)],
                          preferred_element_type=F32)
            s_ref[base + t] = s_t
            gm = group_reduce(s_t, jnp.max)
            m_a = jnp.maximum(m_a, jnp.where(in_b, never, gm))
            m_b = jnp.maximum(m_b, jnp.where(in_b, gm, never))
        scored.append((p, base, i_a, i_b, col_max(m_a), col_max(m_b)))

    for k, (p, base, i_a, i_b, m_a, m_b) in enumerate(scored):
        l_a, p_own = weights(base + n_past, m_a)
        acc_ref[2 * k] = values(i_a, p_own)
        l_b, p_own = weights(base + n_past + 1, m_b)
        acc_ref[2 * k + 1] = values(i_b, p_own)
        acc = jnp.zeros((d, blk), F32)
        for t in range(n_past):
            in_b, j = tile_of(t, p)
            l_part, p_bf = weights(base + t, jnp.where(in_b, m_b, m_a))
            acc = acc * jnp.where(t == p, 0.0, 1.0) + values(j, p_bf)
            l_a = l_a + jnp.where(in_b, 0.0, l_part)
            l_b = l_b + jnp.where(in_b, l_part, 0.0)
            if t < snap_ref.shape[1]:
                snap_ref[k, t] = acc
        past_a = snap_ref[k, jnp.maximum(p - 1, 0)] * jnp.where(p > 0, 1.0, 0.0)
        for w, i, l8, past in ((2 * k, i_a, l_a, past_a), (2 * k + 1, i_b, l_b, acc)):
            l = jnp.sum(l8, axis=0, keepdims=True)
            out_t = (acc_ref[w] + past) / l
            o_ref[pl.ds(pl.multiple_of(i * blk, blk), blk), :] = out_t.T.astype(BF16)


def moba_grid_steps(batch, seq):
    return batch * MOBA_HEADS * (seq // MOBA_BLOCK // 2 // MOBA_PAIRS_PER_STEP)


def moba_mixer(proj, batch, seq, side_casts=()):
    n = proj.shape[0]
    d = MOBA_HEAD_DIM
    blk = MOBA_BLOCK
    assert seq % blk == 0
    n_blocks = seq // blk
    assert n_blocks <= LANES and n_blocks % 2 == 0
    half = ROPE_DIM // 2
    inv_freq = ROPE_THETA ** (-jnp.arange(half, dtype=F32) * 2.0 / ROPE_DIM)
    ang = jnp.arange(seq).astype(F32)[:, None] * inv_freq[None, :]
    cos, sin = jnp.cos(ang), jnp.sin(ang)
    zeros = jnp.zeros((seq, d - ROPE_DIM), F32)
    zero_half = jnp.zeros((seq, half), F32)
    cos_t = jnp.concatenate([cos, cos, jnp.ones_like(zeros)], axis=1)
    sin_lo = jnp.concatenate([-sin, zero_half, zeros], axis=1)
    sin_hi = jnp.concatenate([zero_half, sin, zeros], axis=1)

    def col_spec(off):
        return pl.BlockSpec((seq, d), lambda b, h, p: (b, off // d + h))

    table_spec = pl.BlockSpec((seq, d), lambda b, h, p: (0, 0))
    gate_rows = -(-n_blocks // SUBLANES) * SUBLANES
    pairs = MOBA_PAIRS_PER_STEP
    assert (n_blocks // 2) % pairs == 0
    n_snap = max(n_blocks // 2 - 1, 1)
    steps = n_blocks // 2 // pairs

    def step_of(b, h, p):
        return (b * MOBA_HEADS + h) * steps + p

    out, *casts = pl.pallas_call(
        functools.partial(_moba_kernel, n_blocks=n_blocks, n_side=len(side_casts)),
        grid=(batch, MOBA_HEADS, steps),
        in_specs=[col_spec(OFF_MQ), col_spec(OFF_MK), col_spec(OFF_MV),
                  table_spec, table_spec, table_spec,
                  *[c.in_spec(step_of) for c in side_casts]],
        out_specs=[pl.BlockSpec((seq, d), lambda b, h, p: (b, h)),
                   *[c.out_spec(step_of) for c in side_casts]],
        out_shape=[jax.ShapeDtypeStruct((n, MOBA_WIDTH), BF16),
                   *[c.out_shape() for c in side_casts]],
        scratch_shapes=[pltpu.VMEM((n_blocks, blk, 2 * d), BF16),
                        pltpu.VMEM((n_blocks, d, blk), BF16),
                        pltpu.VMEM((gate_rows, d), F32),
                        pltpu.VMEM((n_blocks, 2 * d, blk), BF16),
                        pltpu.VMEM((pairs * (n_blocks + 1), blk, blk), F32),
                        pltpu.VMEM((2 * pairs, d, blk), F32),
                        pltpu.VMEM((pairs, n_snap, d, blk), F32)],
        compiler_params=_compiler_params(3, 54),
        name="moba",
    )(proj, proj, proj, cos_t, sin_lo, sin_hi, *[c.operand for c in side_casts])
    return out, [c.finish(x) for c, x in zip(side_casts, casts)]


def _hgrn_pair_codes(block):
    i = np.arange(block)[:, None]
    j = np.arange(block)[None, :]
    top_bit = np.floor(np.log2(np.maximum(i ^ j, 1))).astype(np.int32)
    n_levels = block.bit_length() - 1
    return np.where(i > j, top_bit, np.where(i == j, n_levels, -1)).astype(np.int32)


def _hgrn_kernel(*refs, layer, rows_per_step, n_side):
    lbp_ref, tril_ref, code_ref, q_ref, f_ref, v_ref, g_ref, nrm_ref = refs[:8]
    o_ref = refs[8 + n_side]
    state_ref, b_ref, k_ref = refs[9 + 2 * n_side:]
    _run_side_casts(refs[8:8 + n_side], refs[9 + n_side:9 + 2 * n_side])
    c_len = HGRN_BLOCK
    d = HGRN_HEAD_DIM

    @pl.when(pl.program_id(2) == 0)
    def _():
        state_ref[...] = jnp.zeros_like(state_ref)

    params = lbp_ref[...]
    e = jnp.exp(params - jnp.max(params, axis=0, keepdims=True))
    soft = e / jnp.sum(e, axis=0, keepdims=True)
    lower = jnp.zeros((1, d), F32)
    for r in range(1, layer + 1):
        lower = lower + soft[r:r + 1, :]

    tril = tril_ref[...]
    for seg0 in range(0, rows_per_step, c_len):
        seg = slice(seg0, seg0 + c_len)
        forget = lower + (1.0 - lower) * jax.nn.sigmoid(f_ref[seg, :])
        k_ref[seg, :] = 1.0 - forget
        log_f = jnp.log2(forget)
        hi = log_f.astype(BF16)
        rest = log_f - hi.astype(F32)
        mid = rest.astype(BF16)
        lo = (rest - mid.astype(F32)).astype(BF16)
        b_ref[seg, :] = (jnp.dot(tril, hi, preferred_element_type=F32)
                         + jnp.dot(tril, mid, preferred_element_type=F32)
                         + jnp.dot(tril, lo, preferred_element_type=F32))

    assert c_len & (c_len - 1) == 0 and c_len % LANES == 0
    levels = [1 << s for s in range(c_len.bit_length() - 1)]
    n_grp = c_len // SUBLANES
    row8 = lax.broadcasted_iota(jnp.int32, (SUBLANES, d), 0)
    zero8 = jnp.zeros((SUBLANES, d), F32)
    codes = _hgrn_pair_codes(c_len)
    tiles = [(g, cb) for g in range(n_grp) for cb in range(c_len // LANES)]

    def code_tile(g, cb):
        return codes[SUBLANES * g:SUBLANES * (g + 1), LANES * cb:LANES * (cb + 1)]

    level_tiles = [[(g, cb, bool(np.all(code_tile(g, cb) == bit))) for g, cb in tiles
                    if np.any(code_tile(g, cb) == bit)] for bit in range(len(levels) + 1)]

    def bcast_row(ref, r):
        return jnp.broadcast_to(ref[r:r + 1, :], (SUBLANES, d))

    def level_operands(r0, h, bg, qg, kg):
        q_parts, k_parts = [], []
        for g in range(n_grp):
            base = SUBLANES * g
            if h >= SUBLANES:
                b_a = bcast_row(b_ref, r0 + base // (2 * h) * (2 * h) + h - 1)
                if base & h:
                    q_parts.append(qg[g] * jnp.exp2(bg[g] - b_a))
                    k_parts.append(zero8)
                else:
                    q_parts.append(zero8)
                    k_parts.append(kg[g] * jnp.exp2(b_a - bg[g]))
                continue
            upper = (row8 & h) != 0
            if h == 1:
                q_parts.append(jnp.where(upper, qg[g] * (1.0 - kg[g]), 0.0))
                k_parts.append(jnp.where(upper, 0.0, kg[g]))
                continue
            b_a = bcast_row(b_ref, r0 + base + h - 1)
            for s in range(2 * h, SUBLANES, 2 * h):
                b_a = jnp.where(row8 >= s, bcast_row(b_ref, r0 + base + s + h - 1), b_a)
            decay = jnp.exp2(-jnp.abs(bg[g] - b_a))
            q_parts.append(jnp.where(upper, qg[g] * decay, 0.0))
            k_parts.append(jnp.where(upper, 0.0, kg[g] * decay))
        return (jnp.concatenate(q_parts, axis=0).astype(BF16),
                jnp.concatenate(k_parts, axis=0).astype(BF16))

    for c in range(rows_per_step // c_len):
        r0 = c * c_len
        rows = slice(r0, r0 + c_len)
        b = b_ref[rows, :]
        q = _silu(q_ref[rows, :])
        k = k_ref[rows, :]
        v = v_ref[rows, :]

        groups = [slice(SUBLANES * g, SUBLANES * (g + 1)) for g in range(n_grp)]
        bg = [b[g, :] for g in groups]
        qg = [q[g, :] for g in groups]
        kg = [k[g, :] for g in groups]
        def code_of(g, cb):
            return code_ref[SUBLANES * g:SUBLANES * (g + 1), LANES * cb:LANES * (cb + 1)]

        a_tiles = {}
        self_weight = jnp.sum(q * k, axis=-1, keepdims=True)
        for g, cb, _ in level_tiles[len(levels)]:
            a_tiles[g, cb] = jnp.where(code_of(g, cb) == len(levels),
                                       self_weight[groups[g], :], 0.0)
        for bit, h in enumerate(levels):
            q_h, k_h = level_operands(r0, h, bg, qg, kg)
            for cb in range(c_len // LANES):
                k_lo = LANES * cb
                if 2 * h >= LANES:
                    start = k_lo // (2 * h) * (2 * h)
                    if k_lo >= start + h:
                        continue
                    q_lo, q_hi = start + h, start + 2 * h
                else:
                    q_lo, q_hi = k_lo, k_lo + LANES
                pair = lax.dot_general(q_h[q_lo:q_hi, :], k_h[k_lo:k_lo + LANES, :], NT_DIMS,
                                       preferred_element_type=F32)
                for g, tile_cb, owns_tile in level_tiles[bit]:
                    if tile_cb != cb:
                        continue
                    assert q_lo <= SUBLANES * g < q_hi
                    piece = pair[SUBLANES * g - q_lo:SUBLANES * (g + 1) - q_lo, :]
                    if owns_tile:
                        a_tiles[g, cb] = piece
                    else:
                        a_tiles[g, cb] = jnp.where(code_of(g, cb) == bit, piece,
                                                   a_tiles.get((g, cb), 0.0))
        zero_tile = jnp.zeros((SUBLANES, LANES), F32)
        a_mat = jnp.concatenate(
            [jnp.concatenate([a_tiles.get((g, cb), zero_tile) for g in range(n_grp)], axis=0)
             for cb in range(c_len // LANES)], axis=1)
        intra = jnp.dot(a_mat.astype(BF16), v.astype(BF16), preferred_element_type=F32)

        state_t = state_ref[...]
        cross = lax.dot_general((q * jnp.exp2(b)).astype(BF16), state_t.astype(BF16), NT_DIMS,
                                preferred_element_type=F32)
        b_last = b_ref[r0 + c_len - 1:r0 + c_len, :]
        k_dec = k * jnp.exp2(b_last - b)
        state_ref[...] = state_t * jnp.exp2(b_last) + lax.dot_general(
            v.astype(BF16), k_dec.astype(BF16), TN_DIMS, preferred_element_type=F32)

        out = intra + cross
        normed = out * lax.rsqrt(jnp.mean(out * out, axis=-1, keepdims=True) + NORM_EPS)
        gated = (normed * nrm_ref[...]) * _silu(g_ref[rows, :])
        o_ref[rows, :] = gated.astype(BF16)


HGRN_ROWS_PER_STEP = 2048


def hgrn_grid_steps(batch, seq):
    return batch * HGRN_HEADS * (seq // min(seq, HGRN_ROWS_PER_STEP))


def hgrn_mixer(proj, lower_bound_params, hgrn_norm, layer, batch, seq, side_casts=()):
    rows_per_step = min(seq, HGRN_ROWS_PER_STEP)
    n = proj.shape[0]
    d = HGRN_HEAD_DIM
    depth = lower_bound_params.shape[0]
    blk = HGRN_BLOCK
    assert seq % rows_per_step == 0 and rows_per_step % blk == 0
    steps = seq // rows_per_step
    codes = _hgrn_pair_codes(blk)
    tril = jnp.asarray(codes >= 0, BF16)

    def col_spec(off):
        return pl.BlockSpec((rows_per_step, d), lambda b, h, s: (b * steps + s, off // d + h))

    def step_of(b, h, s):
        return (b * HGRN_HEADS + h) * steps + s

    out, *casts = pl.pallas_call(
        functools.partial(_hgrn_kernel, layer=layer, rows_per_step=rows_per_step,
                          n_side=len(side_casts)),
        grid=(batch, HGRN_HEADS, steps),
        in_specs=[pl.BlockSpec((depth, d), lambda b, h, s: (0, h)),
                  pl.BlockSpec((blk, blk), lambda b, h, s: (0, 0)),
                  pl.BlockSpec((blk, blk), lambda b, h, s: (0, 0)),
                  col_spec(OFF_HQ), col_spec(OFF_HF), col_spec(OFF_HI), col_spec(OFF_HG),
                  pl.BlockSpec((1, d), lambda b, h, s: (0, h)),
                  *[c.in_spec(step_of) for c in side_casts]],
        out_specs=[pl.BlockSpec((rows_per_step, d), lambda b, h, s: (b * steps + s, h)),
                   *[c.out_spec(step_of) for c in side_casts]],
        out_shape=[jax.ShapeDtypeStruct((n, HGRN_WIDTH), BF16),
                   *[c.out_shape() for c in side_casts]],
        scratch_shapes=[pltpu.VMEM((d, d), F32),
                        pltpu.VMEM((rows_per_step, d), F32),
                        pltpu.VMEM((rows_per_step, d), F32)],
        compiler_params=_compiler_params(3, 40),
        name="hgrn2",
    )(lower_bound_params.astype(F32), tril, jnp.asarray(codes), proj, proj, proj, proj,
      hgrn_norm.reshape(1, HGRN_WIDTH), *[c.operand for c in side_casts])
    return out, [c.finish(x) for c, x in zip(side_casts, casts)]


def kernel(x, hgrn_lower_bounds, norm_mix, w_in, ret_norm, hgrn_norm, w_branch_ret, w_branch_moba,
           w_branch_hgrn, w_out, norm_ffn, w_ffn_gate, w_ffn_up, w_ffn_down, final_norm):
    batch, seq, d_model = x.shape
    assert d_model == D_MODEL
    depth = w_in.shape[0]
    w_in_layers = [cast_layer(w_in, 0)]
    moba_steps = moba_grid_steps(batch, seq)
    hgrn_steps = hgrn_grid_steps(batch, seq)
    moba_casts = [SideCast(w, moba_steps) for w in (w_ffn_gate, w_ffn_up, w_ffn_down)]
    hgrn_casts = [SideCast(w, hgrn_steps)
                  for w in (w_branch_ret, w_branch_moba, w_branch_hgrn, w_out)]
    hgrn_casts += [SideCast(w_in, hgrn_steps, part=l, n_parts=depth) for l in range(1, depth)]
    h = x.reshape(batch * seq, d_model)
    for layer in range(depth):
        proj = norm_matmul(h, norm_mix[layer], w_in_layers[layer], 0)
        ret = retention_mixer(proj, ret_norm[layer], batch, seq)
        moba, cast = moba_mixer(proj, batch, seq, moba_casts if layer == 0 else ())
        if layer == 0:
            w_ffn_gate, w_ffn_up, w_ffn_down = cast
        hgrn, cast = hgrn_mixer(proj, hgrn_lower_bounds, hgrn_norm[layer], layer, batch, seq,
                                hgrn_casts if layer == 0 else ())
        if layer == 0:
            w_branch_ret, w_branch_moba, w_branch_hgrn, w_out = cast[:4]
            w_in_layers += cast[4:]
        h = merge_out_proj(ret, moba, hgrn, proj, w_branch_ret, w_branch_moba, w_branch_hgrn,
                           w_out, h, layer)
        act = ffn_up(h, norm_ffn[layer], w_ffn_gate, w_ffn_up, layer)
        h = ffn_down(act, w_ffn_down, h, layer, final_norm, normalize=layer == depth - 1)
    return h.reshape(batch, seq, d_model)
```

```python
import functools
import math

import jax
import jax.numpy as jnp
import numpy as np
from jax import lax
from jax.experimental import pallas as pl
from jax.experimental.pallas import tpu as pltpu

F32 = jnp.float32
BF16 = jnp.bfloat16

D_MODEL = 2048
RET_HEADS = 4
RET_HEAD_DIM = 256
RET_WIDTH = RET_HEADS * RET_HEAD_DIM
RET_CHUNK = 128
RET_ROPE_BASE = 10000.0
MOBA_HEADS = 8
MOBA_HEAD_DIM = 128
MOBA_WIDTH = MOBA_HEADS * MOBA_HEAD_DIM
MOBA_BLOCK = 256
MOBA_TOPK = 3
ROPE_THETA = 500000.0
ROPE_DIM = MOBA_HEAD_DIM // 4
HGRN_HEADS = 8
HGRN_HEAD_DIM = 128
HGRN_WIDTH = HGRN_HEADS * HGRN_HEAD_DIM
HGRN_BLOCK = 256
NORM_EPS = 1e-6
IN_SIZES = (RET_WIDTH,) * 4 + (MOBA_WIDTH,) * 3 + (HGRN_WIDTH,) * 4 + (D_MODEL,) * 3
IN_COLS = sum(IN_SIZES)
IN_OFFS = tuple(sum(IN_SIZES[:i]) for i in range(len(IN_SIZES)))
(OFF_RQ, OFF_RK, OFF_RV, OFF_RG, OFF_MQ, OFF_MK, OFF_MV,
 OFF_HQ, OFF_HF, OFF_HI, OFF_HG, OFF_GR, OFF_GM, OFF_GH) = IN_OFFS

V7X_VMEM_BYTES = 64 * 1024 * 1024
LANES = 128
SUBLANES = 8
MASK_VALUE = -1e30
LOG2_E = math.log2(math.e)

NT_DIMS = (((1,), (1,)), ((), ()))
TN_DIMS = (((0,), (0,)), ((), ()))


def _compiler_params(n_grid_dims, vmem_mib):
    assert vmem_mib * 1024 * 1024 < V7X_VMEM_BYTES
    return pltpu.CompilerParams(
        dimension_semantics=("arbitrary",) * n_grid_dims,
        vmem_limit_bytes=vmem_mib * 1024 * 1024)


def _silu(x):
    return x * jax.nn.sigmoid(x)


BF16_SUBLANES = 16


class SideCast:
    def __init__(self, array, n_steps, part=0, n_parts=1):
        layers, rows, self.cols = array.shape
        part_rows = layers // n_parts * rows
        self.steps_per_slab = 1
        while part_rows * self.steps_per_slab % (n_steps * BF16_SUBLANES):
            self.steps_per_slab *= 2
        assert n_steps % self.steps_per_slab == 0
        self.n_slabs = n_steps // self.steps_per_slab
        self.slab_rows = part_rows // self.n_slabs
        self.first_slab = part * self.n_slabs
        self.operand = array.reshape(n_parts * self.n_slabs, self.slab_rows, self.cols)
        self.part_shape = (layers // n_parts, rows, self.cols)

    def in_spec(self, step_of):
        return pl.BlockSpec(
            (None, self.slab_rows, self.cols),
            lambda *g: (self.first_slab + step_of(*g) // self.steps_per_slab, 0, 0))

    def out_spec(self, step_of):
        return pl.BlockSpec((None, self.slab_rows, self.cols),
                            lambda *g: (step_of(*g) // self.steps_per_slab, 0, 0))

    def out_shape(self):
        return jax.ShapeDtypeStruct((self.n_slabs, self.slab_rows, self.cols), BF16)

    def finish(self, cast):
        return cast.reshape(self.part_shape)


def _run_side_casts(in_refs, out_refs):
    for src, dst in zip(in_refs, out_refs):
        dst[...] = src[...].astype(BF16)


def _cast_kernel(src_ref, dst_ref):
    dst_ref[...] = src_ref[...].astype(BF16)


def cast_layer(w, layer, *, tn=1024):
    _, rows, cols = w.shape
    assert cols % tn == 0
    return pl.pallas_call(
        _cast_kernel,
        grid=(cols // tn,),
        in_specs=[pl.BlockSpec((None, rows, tn), lambda j: (layer, 0, j))],
        out_specs=pl.BlockSpec((None, rows, tn), lambda j: (0, 0, j)),
        out_shape=jax.ShapeDtypeStruct((1, rows, cols), BF16),
        compiler_params=_compiler_params(1, 40),
        name="cast_layer",
    )(w)


def _layer_weight_spec(layer, rows, tn, col_of):
    return pl.BlockSpec((None, rows, tn), lambda *g: (layer, 0, col_of(*g)))


def _rms_normalize(x, g):
    r = lax.rsqrt(jnp.mean(x * x, axis=-1, keepdims=True) + NORM_EPS)
    return x * r * g


def _norm_matmul_kernel(x_ref, g_ref, w_ref, o_ref, h_ref):
    @pl.when(pl.program_id(1) == 0)
    def _():
        h_ref[...] = _rms_normalize(x_ref[...], g_ref[...]).astype(BF16)

    o_ref[...] = jnp.dot(h_ref[...], w_ref[...].astype(BF16), preferred_element_type=F32)


def norm_matmul(x, g, w, layer, *, tm=1024, tn=1024):
    n, d = x.shape
    cols = w.shape[2]
    assert n % tm == 0 and cols % tn == 0
    return pl.pallas_call(
        _norm_matmul_kernel,
        grid=(n // tm, cols // tn),
        in_specs=[pl.BlockSpec((tm, d), lambda i, j: (i, 0)),
                  pl.BlockSpec((1, d), lambda i, j: (0, 0)),
                  _layer_weight_spec(layer, d, tn, lambda i, j: j)],
        out_specs=pl.BlockSpec((tm, tn), lambda i, j: (i, j)),
        out_shape=jax.ShapeDtypeStruct((n, cols), F32),
        scratch_shapes=[pltpu.VMEM((tm, d), BF16)],
        compiler_params=_compiler_params(2, 56),
        name="norm_inproj",
    )(x, g.reshape(1, d), w)


def _ffn_up_kernel(x_ref, g_ref, wg_ref, wu_ref, o_ref, h_ref):
    @pl.when(pl.program_id(1) == 0)
    def _():
        h_ref[...] = _rms_normalize(x_ref[...], g_ref[...]).astype(BF16)

    h = h_ref[...]
    gate = jnp.dot(h, wg_ref[...], preferred_element_type=F32)
    up = jnp.dot(h, wu_ref[...], preferred_element_type=F32)
    o_ref[...] = (_silu(gate) * up).astype(BF16)


def ffn_up(x, g, wg, wu, layer, *, tm=1024, tn=512):
    n, d = x.shape
    hidden = wg.shape[2]
    assert n % tm == 0 and hidden % tn == 0
    w_spec = _layer_weight_spec(layer, d, tn, lambda i, j: j)
    return pl.pallas_call(
        _ffn_up_kernel,
        grid=(n // tm, hidden // tn),
        in_specs=[pl.BlockSpec((tm, d), lambda i, j: (i, 0)),
                  pl.BlockSpec((1, d), lambda i, j: (0, 0)),
                  w_spec, w_spec],
        out_specs=pl.BlockSpec((tm, tn), lambda i, j: (i, j)),
        out_shape=jax.ShapeDtypeStruct((n, hidden), BF16),
        scratch_shapes=[pltpu.VMEM((tm, d), BF16)],
        compiler_params=_compiler_params(2, 48),
        name="ffn_up",
    )(x, g.reshape(1, d), wg, wu)


def _ffn_down_kernel(a_ref, w_ref, r_ref, g_ref, o_ref, *, normalize):
    out = r_ref[...] + jnp.dot(a_ref[...], w_ref[...], preferred_element_type=F32)
    o_ref[...] = _rms_normalize(out, g_ref[...]) if normalize else out


def ffn_down(a, w, res, layer, gain, *, normalize, tm=512):
    n, k = a.shape
    cols = w.shape[2]
    assert n % tm == 0
    row_spec = pl.BlockSpec((tm, cols), lambda i: (i, 0))
    return pl.pallas_call(
        functools.partial(_ffn_down_kernel, normalize=normalize),
        grid=(n // tm,),
        in_specs=[pl.BlockSpec((tm, k), lambda i: (i, 0)),
                  pl.BlockSpec((None, k, cols), lambda i: (layer, 0, 0),
                               pipeline_mode=pl.Buffered(1)),
                  row_spec,
                  pl.BlockSpec((1, cols), lambda i: (0, 0))],
        out_specs=row_spec,
        out_shape=jax.ShapeDtypeStruct((n, cols), F32),
        compiler_params=_compiler_params(1, 54),
        name="ffn_down",
    )(a, w, res, gain.reshape(1, cols))


MERGE_GATE_TILE = 1024


def _merge_out_kernel(*refs):
    n_gate = D_MODEL // MERGE_GATE_TILE
    branch_refs = refs[:3]
    gate_refs = refs[3:3 + 3 * n_gate]
    wb_refs = refs[3 + 3 * n_gate:6 + 3 * n_gate]
    wo_ref, res_ref, o_ref = refs[6 + 3 * n_gate:]
    parts = []
    for c in range(n_gate):
        cols = slice(c * MERGE_GATE_TILE, (c + 1) * MERGE_GATE_TILE)
        mixed = None
        for b in range(3):
            term = jax.nn.sigmoid(gate_refs[b * n_gate + c][...]) * jnp.dot(
                branch_refs[b][...], wb_refs[b][:, cols], preferred_element_type=F32)
            mixed = term if mixed is None else mixed + term
        parts.append(mixed.astype(BF16))
    mixed = jnp.concatenate(parts, axis=1)
    o_ref[...] = res_ref[...] + jnp.dot(mixed, wo_ref[...], preferred_element_type=F32)


def merge_out_proj(ret, moba, hgrn, proj, wr, wm, wh, wo, res, layer, *, tm=256):
    n = ret.shape[0]
    assert n % tm == 0 and D_MODEL % MERGE_GATE_TILE == 0
    assert all(off % MERGE_GATE_TILE == 0 for off in (OFF_GR, OFF_GM, OFF_GH))
    n_gate = D_MODEL // MERGE_GATE_TILE

    def branch_spec(width):
        return pl.BlockSpec((tm, width), lambda i: (i, 0))

    def gate_specs(off):
        return [pl.BlockSpec((tm, MERGE_GATE_TILE),
                             functools.partial(lambda i, col: (i, col), col=off // MERGE_GATE_TILE + c))
                for c in range(n_gate)]

    def weight_spec(rows):
        return pl.BlockSpec((None, rows, D_MODEL), lambda i: (layer, 0, 0),
                            pipeline_mode=pl.Buffered(1))

    row_spec = pl.BlockSpec((tm, D_MODEL), lambda i: (i, 0))
    return pl.pallas_call(
        _merge_out_kernel,
        grid=(n // tm,),
        in_specs=[branch_spec(RET_WIDTH), branch_spec(MOBA_WIDTH), branch_spec(HGRN_WIDTH),
                  *gate_specs(OFF_GR), *gate_specs(OFF_GM), *gate_specs(OFF_GH),
                  weight_spec(RET_WIDTH), weight_spec(MOBA_WIDTH), weight_spec(HGRN_WIDTH),
                  weight_spec(D_MODEL), row_spec],
        out_specs=row_spec,
        out_shape=jax.ShapeDtypeStruct((n, D_MODEL), F32),
        compiler_params=_compiler_params(1, 52),
        name="merge_out_proj",
    )(ret, moba, hgrn, *([proj] * (3 * n_gate)), wr, wm, wh, wo, res)


def _retention_kernel(lg_ref, q_ref, k_ref, v_ref, g_ref, cos_ref, sin_ref, nrm_ref,
                      o_ref, state_ref, *, chunks):
    c_len = RET_CHUNK
    d = RET_HEAD_DIM
    half = d // 2

    @pl.when(pl.program_id(1) == 0)
    def _():
        state_ref[...] = jnp.zeros_like(state_ref)

    t_col = lax.broadcasted_iota(jnp.int32, (c_len, 1), 0).astype(F32)
    rel = (lax.broadcasted_iota(jnp.int32, (c_len, c_len), 0)
           - lax.broadcasted_iota(jnp.int32, (c_len, c_len), 1)).astype(F32)

    def rotate(x, cos, sin):
        x1 = x[:, :half]
        x2 = x[:, half:]
        return jnp.concatenate([x1 * cos - x2 * sin, x2 * cos + x1 * sin], axis=1)

    for h in range(RET_HEADS):
        cols = slice(h * d, (h + 1) * d)
        lg = lg_ref[h]
        decay = jnp.where(rel >= 0, jnp.exp(lg * jnp.maximum(rel, 0.0)), 0.0)
        q_weight = jnp.exp(lg * (t_col + 1.0))
        k_weight = jnp.exp(lg * (c_len - 1.0 - t_col))
        chunk_decay = jnp.exp(jnp.full((1, d), lg * c_len, F32))
        for c in range(chunks):
            rows = slice(c * c_len, (c + 1) * c_len)
            cos = cos_ref[rows, :]
            sin = sin_ref[rows, :]
            q = rotate(q_ref[rows, cols], cos, sin)
            k = rotate(k_ref[rows, cols], cos, sin) * (d ** -0.5)
            v = v_ref[rows, cols].astype(BF16)
            state = state_ref[h]

            scores = lax.dot_general(q.astype(BF16), k.astype(BF16), NT_DIMS,
                                     preferred_element_type=F32) * decay
            intra = jnp.dot(scores.astype(BF16), v, preferred_element_type=F32)
            cross = jnp.dot((q * q_weight).astype(BF16), state.astype(BF16),
                            preferred_element_type=F32)
            kv = lax.dot_general((k * k_weight).astype(BF16), v, TN_DIMS,
                                 preferred_element_type=F32)
            state_ref[h] = chunk_decay * state + kv

            out = intra + cross
            centered = out - jnp.mean(out, axis=-1, keepdims=True)
            normed = centered * lax.rsqrt(
                jnp.mean(centered * centered, axis=-1, keepdims=True) + NORM_EPS)
            gated = (normed * nrm_ref[:, cols]) * _silu(g_ref[rows, cols])
            o_ref[rows, cols] = gated.astype(BF16)


def retention_mixer(proj, ret_norm, batch, seq, *, rows_per_step=1024):
    n = proj.shape[0]
    d = RET_HEAD_DIM
    assert seq % rows_per_step == 0 and rows_per_step % RET_CHUNK == 0
    steps = seq // rows_per_step
    log_gamma = jnp.log1p(-jnp.exp2(-5.0 - jnp.arange(RET_HEADS, dtype=F32)))
    half = d // 2
    inv_freq = RET_ROPE_BASE ** (-jnp.arange(half, dtype=F32) * 2.0 / d)
    ang = jnp.arange(seq).astype(F32)[:, None] * inv_freq[None, :]
    cos, sin = jnp.cos(ang), jnp.sin(ang)

    def col_spec(off):
        return pl.BlockSpec((rows_per_step, RET_WIDTH),
                            lambda b, s: (b * steps + s, off // RET_WIDTH))

    table_spec = pl.BlockSpec((rows_per_step, half), lambda b, s: (s, 0))
    return pl.pallas_call(
        functools.partial(_retention_kernel, chunks=rows_per_step // RET_CHUNK),
        grid=(batch, steps),
        in_specs=[pl.BlockSpec(memory_space=pltpu.SMEM),
                  col_spec(OFF_RQ), col_spec(OFF_RK), col_spec(OFF_RV), col_spec(OFF_RG),
                  table_spec, table_spec,
                  pl.BlockSpec((1, RET_WIDTH), lambda b, s: (0, 0))],
        out_specs=pl.BlockSpec((rows_per_step, RET_WIDTH), lambda b, s: (b * steps + s, 0)),
        out_shape=jax.ShapeDtypeStruct((n, RET_WIDTH), BF16),
        scratch_shapes=[pltpu.VMEM((RET_HEADS, d, d), F32)],
        compiler_params=_compiler_params(2, 52),
        name="retention",
    )(log_gamma, proj, proj, proj, proj, cos, sin, ret_norm.reshape(1, RET_WIDTH))


MOBA_PAIRS_PER_STEP = 2


def _moba_rotate(x, cos, sin_lo, sin_hi):
    half = ROPE_DIM // 2
    return (x * cos + pltpu.roll(x, MOBA_HEAD_DIM - half, 1) * sin_lo
            + pltpu.roll(x, half, 1) * sin_hi)


def _split_dot_nt(a, b):
    a_hi = a.astype(BF16)
    a_lo = (a - a_hi.astype(F32)).astype(BF16)
    b_hi = b.astype(BF16)
    b_lo = (b - b_hi.astype(F32)).astype(BF16)

    def dot_nt(x, y):
        return lax.dot_general(x, y, NT_DIMS, preferred_element_type=F32)

    return dot_nt(a_hi, b_hi) + dot_nt(a_hi, b_lo) + dot_nt(a_lo, b_hi)


def _moba_kernel(*refs, n_blocks, n_side):
    q_ref, k_ref, v_ref, cos_ref, slo_ref, shi_ref = refs[:6]
    o_ref = refs[6 + n_side]
    (kaug_ref, vt_ref, kmean_ref, qt_ref, s_ref, acc_ref,
     snap_ref) = refs[7 + 2 * n_side:]
    _run_side_casts(refs[6:6 + n_side], refs[7 + n_side:7 + 2 * n_side])
    blk = MOBA_BLOCK
    d = MOBA_HEAD_DIM
    step = pl.program_id(2)
    n_past = n_blocks - 1
    pairs = snap_ref.shape[0]
    gate_rows = kmean_ref.shape[0]
    groups = blk // SUBLANES
    q_scale = (d ** -0.5) * LOG2_E

    def rotate(ref, start):
        rows = pl.ds(start, blk)
        return _moba_rotate(ref[rows, :], cos_ref[rows, :], slo_ref[rows, :], shi_ref[rows, :])

    @pl.when(step == 0)
    def _():
        lane = lax.broadcasted_iota(jnp.int32, (blk, LANES), 1)
        eye = jnp.where(lax.broadcasted_iota(jnp.int32, (d, d), 0)
                        == lax.broadcasted_iota(jnp.int32, (d, d), 1), 1.0, 0.0).astype(BF16)

        def transposed(x_bf):
            return lax.dot_general(eye, x_bf, NT_DIMS, preferred_element_type=F32)

        kmean_ref[...] = jnp.zeros_like(kmean_ref)
        for j in range(n_blocks):
            kr = rotate(k_ref, j * blk)
            kaug_ref[j, :, :d] = kr.astype(BF16)
            kaug_ref[j, :, d:] = jnp.where(lane == j, 1.0, 0.0).astype(BF16)
            vt_ref[j] = transposed(v_ref[j * blk:(j + 1) * blk, :].astype(BF16)).astype(BF16)
            kmean_ref[j:j + 1, :] = jnp.mean(kr, axis=0, keepdims=True)
        row = lax.broadcasted_iota(jnp.int32, (gate_rows, blk), 0)
        row_f = row.astype(F32)
        pad = jnp.zeros((LANES - gate_rows, blk), F32)
        for i in range(n_blocks):
            q = rotate(q_ref, i * blk)
            gate = _split_dot_nt(kmean_ref[...], q)
            q_t = transposed((q * q_scale).astype(BF16))
            g = jnp.where(row < i, gate, -jnp.inf)
            bias = jnp.full((gate_rows, blk), MASK_VALUE, F32)
            for _ in range(min(MOBA_TOPK, n_blocks)):
                best = jnp.max(g, axis=0, keepdims=True)
                hit = (g == best) & (g > -jnp.inf)
                first = jnp.min(jnp.where(hit, row_f, float(LANES)), axis=0, keepdims=True)
                pick = row_f == first
                bias = jnp.where(pick, 0.0, bias)
                g = jnp.where(pick, -jnp.inf, g)
            bias = jnp.where(row == i, 0.0, bias)
            qt_ref[i] = jnp.concatenate([q_t, bias, pad], axis=0).astype(BF16)

    def group_reduce(x, op):
        return op(x.reshape(groups, SUBLANES, blk), axis=0)

    def score_own_block(i, own_slot):
        s_t = jnp.dot(kaug_ref[i], qt_ref[i], preferred_element_type=F32)
        key = lax.broadcasted_iota(jnp.int32, (blk, blk), 0)
        qry = lax.broadcasted_iota(jnp.int32, (blk, blk), 1)
        s_t = jnp.where(key <= qry, s_t, MASK_VALUE)
        s_ref[own_slot] = s_t
        return group_reduce(s_t, jnp.max)

    def tile_of(t, p):
        in_b = t >= p
        return in_b, jnp.where(in_b, t - p, t)

    def col_max(m8):
        return jnp.broadcast_to(jnp.max(m8, axis=0, keepdims=True), (SUBLANES, blk))

    def weights(slot, m8):
        p_t = jnp.exp2(s_ref[slot].reshape(groups, SUBLANES, blk) - m8[None])
        return jnp.sum(p_t, axis=0), p_t.reshape(blk, blk).astype(BF16)

    def values(j, p_bf):
        return jnp.dot(vt_ref[j], p_bf, preferred_element_type=F32)

    never = jnp.full((SUBLANES, blk), -jnp.inf, F32)
    scored = []
    for k in range(pairs):
        p = step * pairs + k
        base = k * (n_past + 2)
        i_a, i_b = p, n_past - p
        m_a = score_own_block(i_a, base + n_past)
        m_b = score_own_block(i_b, base + n_past + 1)
        for t in range(n_past):
            in_b, j = tile_of(t, p)
            s_t = jnp.dot(kaug_ref[j], qt_ref[jnp.where(in_b, i_b, i_a)],
                          preferred_element_type=F32)
            s_ref[base + t] = s_t
            gm = group_reduce(s_t, jnp.max)
            m_a = jnp.maximum(m_a, jnp.where(in_b, never, gm))
            m_b = jnp.maximum(m_b, jnp.where(in_b, gm, never))
        scored.append((p, base, i_a, i_b, col_max(m_a), col_max(m_b)))

    for k, (p, base, i_a, i_b, m_a, m_b) in enumerate(scored):
        l_a, p_own = weights(base + n_past, m_a)
        acc_ref[2 * k] = values(i_a, p_own)
        l_b, p_own = weights(base + n_past + 1, m_b)
        acc_ref[2 * k + 1] = values(i_b, p_own)
        acc = jnp.zeros((d, blk), F32)
        for t in range(n_past):
            in_b, j = tile_of(t, p)
            l_part, p_bf = weights(base + t, jnp.where(in_b, m_b, m_a))
            acc = acc * jnp.where(t == p, 0.0, 1.0) + values(j, p_bf)
            l_a = l_a + jnp.where(in_b, 0.0, l_part)
            l_b = l_b + jnp.where(in_b, l_part, 0.0)
            if t < snap_ref.shape[1]:
                snap_ref[k, t] = acc
        past_a = snap_ref[k, jnp.maximum(p - 1, 0)] * jnp.where(p > 0, 1.0, 0.0)
        for w, i, l8, past in ((2 * k, i_a, l_a, past_a), (2 * k + 1, i_b, l_b, acc)):
            l = jnp.sum(l8, axis=0, keepdims=True)
            out_t = (acc_ref[w] + past) / l
            o_ref[pl.ds(pl.multiple_of(i * blk, blk), blk), :] = out_t.T.astype(BF16)


def moba_grid_steps(batch, seq):
    return batch * MOBA_HEADS * (seq // MOBA_BLOCK // 2 // MOBA_PAIRS_PER_STEP)


def moba_mixer(proj, batch, seq, side_casts=()):
    n = proj.shape[0]
    d = MOBA_HEAD_DIM
    blk = MOBA_BLOCK
    assert seq % blk == 0
    n_blocks = seq // blk
    assert n_blocks <= LANES and n_blocks % 2 == 0
    half = ROPE_DIM // 2
    inv_freq = ROPE_THETA ** (-jnp.arange(half, dtype=F32) * 2.0 / ROPE_DIM)
    ang = jnp.arange(seq).astype(F32)[:, None] * inv_freq[None, :]
    cos, sin = jnp.cos(ang), jnp.sin(ang)
    zeros = jnp.zeros((seq, d - ROPE_DIM), F32)
    zero_half = jnp.zeros((seq, half), F32)
    cos_t = jnp.concatenate([cos, cos, jnp.ones_like(zeros)], axis=1)
    sin_lo = jnp.concatenate([-sin, zero_half, zeros], axis=1)
    sin_hi = jnp.concatenate([zero_half, sin, zeros], axis=1)

    def col_spec(off):
        return pl.BlockSpec((seq, d), lambda b, h, p: (b, off // d + h))

    table_spec = pl.BlockSpec((seq, d), lambda b, h, p: (0, 0))
    gate_rows = -(-n_blocks // SUBLANES) * SUBLANES
    pairs = MOBA_PAIRS_PER_STEP
    assert (n_blocks // 2) % pairs == 0
    n_snap = max(n_blocks // 2 - 1, 1)
    steps = n_blocks // 2 // pairs

    def step_of(b, h, p):
        return (b * MOBA_HEADS + h) * steps + p

    out, *casts = pl.pallas_call(
        functools.partial(_moba_kernel, n_blocks=n_blocks, n_side=len(side_casts)),
        grid=(batch, MOBA_HEADS, steps),
        in_specs=[col_spec(OFF_MQ), col_spec(OFF_MK), col_spec(OFF_MV),
                  table_spec, table_spec, table_spec,
                  *[c.in_spec(step_of) for c in side_casts]],
        out_specs=[pl.BlockSpec((seq, d), lambda b, h, p: (b, h)),
                   *[c.out_spec(step_of) for c in side_casts]],
        out_shape=[jax.ShapeDtypeStruct((n, MOBA_WIDTH), BF16),
                   *[c.out_shape() for c in side_casts]],
        scratch_shapes=[pltpu.VMEM((n_blocks, blk, 2 * d), BF16),
                        pltpu.VMEM((n_blocks, d, blk), BF16),
                        pltpu.VMEM((gate_rows, d), F32),
                        pltpu.VMEM((n_blocks, 2 * d, blk), BF16),
                        pltpu.VMEM((pairs * (n_blocks + 1), blk, blk), F32),
                        pltpu.VMEM((2 * pairs, d, blk), F32),
                        pltpu.VMEM((pairs, n_snap, d, blk), F32)],
        compiler_params=_compiler_params(3, 54),
        name="moba",
    )(proj, proj, proj, cos_t, sin_lo, sin_hi, *[c.operand for c in side_casts])
    return out, [c.finish(x) for c, x in zip(side_casts, casts)]


def _hgrn_pair_codes(block):
    i = np.arange(block)[:, None]
    j = np.arange(block)[None, :]
    top_bit = np.floor(np.log2(np.maximum(i ^ j, 1))).astype(np.int32)
    n_levels = block.bit_length() - 1
    return np.where(i > j, top_bit, np.where(i == j, n_levels, -1)).astype(np.int32)


def _hgrn_kernel(*refs, layer, rows_per_step, n_side):
    lbp_ref, tril_ref, code_ref, q_ref, f_ref, v_ref, g_ref, nrm_ref = refs[:8]
    o_ref = refs[8 + n_side]
    state_ref, b_ref, k_ref = refs[9 + 2 * n_side:]
    _run_side_casts(refs[8:8 + n_side], refs[9 + n_side:9 + 2 * n_side])
    c_len = HGRN_BLOCK
    d = HGRN_HEAD_DIM

    @pl.when(pl.program_id(2) == 0)
    def _():
        state_ref[...] = jnp.zeros_like(state_ref)

    params = lbp_ref[...]
    e = jnp.exp(params - jnp.max(params, axis=0, keepdims=True))
    soft = e / jnp.sum(e, axis=0, keepdims=True)
    lower = jnp.zeros((1, d), F32)
    for r in range(1, layer + 1):
        lower = lower + soft[r:r + 1, :]

    tril = tril_ref[...]
    for seg0 in range(0, rows_per_step, c_len):
        seg = slice(seg0, seg0 + c_len)
        forget = lower + (1.0 - lower) * jax.nn.sigmoid(f_ref[seg, :])
        k_ref[seg, :] = 1.0 - forget
        log_f = jnp.log2(forget)
        hi = log_f.astype(BF16)
        rest = log_f - hi.astype(F32)
        mid = rest.astype(BF16)
        lo = (rest - mid.astype(F32)).astype(BF16)
        b_ref[seg, :] = (jnp.dot(tril, hi, preferred_element_type=F32)
                         + jnp.dot(tril, mid, preferred_element_type=F32)
                         + jnp.dot(tril, lo, preferred_element_type=F32))

    assert c_len & (c_len - 1) == 0 and c_len % LANES == 0
    levels = [1 << s for s in range(c_len.bit_length() - 1)]
    n_grp = c_len // SUBLANES
    row8 = lax.broadcasted_iota(jnp.int32, (SUBLANES, d), 0)
    zero8 = jnp.zeros((SUBLANES, d), F32)
    codes = _hgrn_pair_codes(c_len)
    tiles = [(g, cb) for g in range(n_grp) for cb in range(c_len // LANES)]

    def code_tile(g, cb):
        return codes[SUBLANES * g:SUBLANES * (g + 1), LANES * cb:LANES * (cb + 1)]

    level_tiles = [[(g, cb, bool(np.all(code_tile(g, cb) == bit))) for g, cb in tiles
                    if np.any(code_tile(g, cb) == bit)] for bit in range(len(levels) + 1)]

    def bcast_row(ref, r):
        return jnp.broadcast_to(ref[r:r + 1, :], (SUBLANES, d))

    def level_operands(r0, h, bg, qg, kg):
        q_parts, k_parts = [], []
        for g in range(n_grp):
            base = SUBLANES * g
            if h >= SUBLANES:
                b_a = bcast_row(b_ref, r0 + base // (2 * h) * (2 * h) + h - 1)
                if base & h:
                    q_parts.append(qg[g] * jnp.exp2(bg[g] - b_a))
                    k_parts.append(zero8)
                else:
                    q_parts.append(zero8)
                    k_parts.append(kg[g] * jnp.exp2(b_a - bg[g]))
                continue
            upper = (row8 & h) != 0
            if h == 1:
                q_parts.append(jnp.where(upper, qg[g] * (1.0 - kg[g]), 0.0))
                k_parts.append(jnp.where(upper, 0.0, kg[g]))
                continue
            b_a = bcast_row(b_ref, r0 + base + h - 1)
            for s in range(2 * h, SUBLANES, 2 * h):
                b_a = jnp.where(row8 >= s, bcast_row(b_ref, r0 + base + s + h - 1), b_a)
            decay = jnp.exp2(-jnp.abs(bg[g] - b_a))
            q_parts.append(jnp.where(upper, qg[g] * decay, 0.0))
            k_parts.append(jnp.where(upper, 0.0, kg[g] * decay))
        return (jnp.concatenate(q_parts, axis=0).astype(BF16),
                jnp.concatenate(k_parts, axis=0).astype(BF16))

    for c in range(rows_per_step // c_len):
        r0 = c * c_len
        rows = slice(r0, r0 + c_len)
        b = b_ref[rows, :]
        q = _silu(q_ref[rows, :])
        k = k_ref[rows, :]
        v = v_ref[rows, :]

        groups = [slice(SUBLANES * g, SUBLANES * (g + 1)) for g in range(n_grp)]
        bg = [b[g, :] for g in groups]
        qg = [q[g, :] for g in groups]
        kg = [k[g, :] for g in groups]
        def code_of(g, cb):
            return code_ref[SUBLANES * g:SUBLANES * (g + 1), LANES * cb:LANES * (cb + 1)]

        a_tiles = {}
        self_weight = jnp.sum(q * k, axis=-1, keepdims=True)
        for g, cb, _ in level_tiles[len(levels)]:
            a_tiles[g, cb] = jnp.where(code_of(g, cb) == len(levels),
                                       self_weight[groups[g], :], 0.0)
        for bit, h in enumerate(levels):
            q_h, k_h = level_operands(r0, h, bg, qg, kg)
            for cb in range(c_len // LANES):
                k_lo = LANES * cb
                if 2 * h >= LANES:
                    start = k_lo // (2 * h) * (2 * h)
                    if k_lo >= start + h:
                        continue
                    q_lo, q_hi = start + h, start + 2 * h
                else:
                    q_lo, q_hi = k_lo, k_lo + LANES
                pair = lax.dot_general(q_h[q_lo:q_hi, :], k_h[k_lo:k_lo + LANES, :], NT_DIMS,
                                       preferred_element_type=F32)
                for g, tile_cb, owns_tile in level_tiles[bit]:
                    if tile_cb != cb:
                        continue
                    assert q_lo <= SUBLANES * g < q_hi
                    piece = pair[SUBLANES * g - q_lo:SUBLANES * (g + 1) - q_lo, :]
                    if owns_tile:
                        a_tiles[g, cb] = piece
                    else:
                        a_tiles[g, cb] = jnp.where(code_of(g, cb) == bit, piece,
                                                   a_tiles.get((g, cb), 0.0))
        zero_tile = jnp.zeros((SUBLANES, LANES), F32)
        a_mat = jnp.concatenate(
            [jnp.concatenate([a_tiles.get((g, cb), zero_tile) for g in range(n_grp)], axis=0)
             for cb in range(c_len // LANES)], axis=1)
        intra = jnp.dot(a_mat.astype(BF16), v.astype(BF16), preferred_element_type=F32)

        state_t = state_ref[...]
        cross = lax.dot_general((q * jnp.exp2(b)).astype(BF16), state_t.astype(BF16), NT_DIMS,
                                preferred_element_type=F32)
        b_last = b_ref[r0 + c_len - 1:r0 + c_len, :]
        k_dec = k * jnp.exp2(b_last - b)
        state_ref[...] = state_t * jnp.exp2(b_last) + lax.dot_general(
            v.astype(BF16), k_dec.astype(BF16), TN_DIMS, preferred_element_type=F32)

        out = intra + cross
        normed = out * lax.rsqrt(jnp.mean(out * out, axis=-1, keepdims=True) + NORM_EPS)
        gated = (normed * nrm_ref[...]) * _silu(g_ref[rows, :])
        o_ref[rows, :] = gated.astype(BF16)


HGRN_ROWS_PER_STEP = 2048


def hgrn_grid_steps(batch, seq):
    return batch * HGRN_HEADS * (seq // min(seq, HGRN_ROWS_PER_STEP))


def hgrn_mixer(proj, lower_bound_params, hgrn_norm, layer, batch, seq, side_casts=()):
    rows_per_step = min(seq, HGRN_ROWS_PER_STEP)
    n = proj.shape[0]
    d = HGRN_HEAD_DIM
    depth = lower_bound_params.shape[0]
    blk = HGRN_BLOCK
    assert seq % rows_per_step == 0 and rows_per_step % blk == 0
    steps = seq // rows_per_step
    codes = _hgrn_pair_codes(blk)
    tril = jnp.asarray(codes >= 0, BF16)

    def col_spec(off):
        return pl.BlockSpec((rows_per_step, d), lambda b, h, s: (b * steps + s, off // d + h))

    def step_of(b, h, s):
        return (b * HGRN_HEADS + h) * steps + s

    out, *casts = pl.pallas_call(
        functools.partial(_hgrn_kernel, layer=layer, rows_per_step=rows_per_step,
                          n_side=len(side_casts)),
        grid=(batch, HGRN_HEADS, steps),
        in_specs=[pl.BlockSpec((depth, d), lambda b, h, s: (0, h)),
                  pl.BlockSpec((blk, blk), lambda b, h, s: (0, 0)),
                  pl.BlockSpec((blk, blk), lambda b, h, s: (0, 0)),
                  col_spec(OFF_HQ), col_spec(OFF_HF), col_spec(OFF_HI), col_spec(OFF_HG),
                  pl.BlockSpec((1, d), lambda b, h, s: (0, h)),
                  *[c.in_spec(step_of) for c in side_casts]],
        out_specs=[pl.BlockSpec((rows_per_step, d), lambda b, h, s: (b * steps + s, h)),
                   *[c.out_spec(step_of) for c in side_casts]],
        out_shape=[jax.ShapeDtypeStruct((n, HGRN_WIDTH), BF16),
                   *[c.out_shape() for c in side_casts]],
        scratch_shapes=[pltpu.VMEM((d, d), F32),
                        pltpu.VMEM((rows_per_step, d), F32),
                        pltpu.VMEM((rows_per_step, d), F32)],
        compiler_params=_compiler_params(3, 40),
        name="hgrn2",
    )(lower_bound_params.astype(F32), tril, jnp.asarray(codes), proj, proj, proj, proj,
      hgrn_norm.reshape(1, HGRN_WIDTH), *[c.operand for c in side_casts])
    return out, [c.finish(x) for c, x in zip(side_casts, casts)]


def kernel(x, hgrn_lower_bounds, norm_mix, w_in, ret_norm, hgrn_norm, w_branch_ret, w_branch_moba,
           w_branch_hgrn, w_out, norm_ffn, w_ffn_gate, w_ffn_up, w_ffn_down, final_norm):
    batch, seq, d_model = x.shape
    assert d_model == D_MODEL
    depth = w_in.shape[0]
    w_in_layers = [w_in]
    moba_steps = moba_grid_steps(batch, seq)
    hgrn_steps = hgrn_grid_steps(batch, seq)
    moba_casts = [SideCast(w, moba_steps) for w in (w_ffn_gate, w_ffn_up, w_ffn_down)]
    hgrn_casts = [SideCast(w, hgrn_steps)
                  for w in (w_branch_ret, w_branch_moba, w_branch_hgrn, w_out)]
    hgrn_casts += [SideCast(w_in, hgrn_steps, part=l, n_parts=depth) for l in range(1, depth)]
    h = x.reshape(batch * seq, d_model)
    for layer in range(depth):
        proj = norm_matmul(h, norm_mix[layer], w_in_layers[layer], 0)
        ret = retention_mixer(proj, ret_norm[layer], batch, seq)
        moba, cast = moba_mixer(proj, batch, seq, moba_casts if layer == 0 else ())
        if layer == 0:
            w_ffn_gate, w_ffn_up, w_ffn_down = cast
        hgrn, cast = hgrn_mixer(proj, hgrn_lower_bounds, hgrn_norm[layer], layer, batch, seq,
                                hgrn_casts if layer == 0 else ())
        if layer == 0:
            w_branch_ret, w_branch_moba, w_branch_hgrn, w_out = cast[:4]
            w_in_layers += cast[4:]
        h = merge_out_proj(ret, moba, hgrn, proj, w_branch_ret, w_branch_moba, w_branch_hgrn,
                           w_out, h, layer)
        act = ffn_up(h, norm_ffn[layer], w_ffn_gate, w_ffn_up, layer)
        h = ffn_down(act, w_ffn_down, h, layer, final_norm, normalize=layer == depth - 1)
    return h.reshape(batch, seq, d_model)
```

```python
import functools
import math

import jax
import jax.numpy as jnp
import numpy as np
from jax import lax
from jax.experimental import pallas as pl
from jax.experimental.pallas import tpu as pltpu

F32 = jnp.float32
BF16 = jnp.bfloat16

D_MODEL = 2048
RET_HEADS = 4
RET_HEAD_DIM = 256
RET_WIDTH = RET_HEADS * RET_HEAD_DIM
RET_CHUNK = 128
RET_ROPE_BASE = 10000.0
MOBA_HEADS = 8
MOBA_HEAD_DIM = 128
MOBA_WIDTH = MOBA_HEADS * MOBA_HEAD_DIM
MOBA_BLOCK = 256
MOBA_TOPK = 3
ROPE_THETA = 500000.0
ROPE_DIM = MOBA_HEAD_DIM // 4
HGRN_HEADS = 8
HGRN_HEAD_DIM = 128
HGRN_WIDTH = HGRN_HEADS * HGRN_HEAD_DIM
HGRN_BLOCK = 256
NORM_EPS = 1e-6
IN_SIZES = (RET_WIDTH,) * 4 + (MOBA_WIDTH,) * 3 + (HGRN_WIDTH,) * 4 + (D_MODEL,) * 3
IN_COLS = sum(IN_SIZES)
IN_OFFS = tuple(sum(IN_SIZES[:i]) for i in range(len(IN_SIZES)))
(OFF_RQ, OFF_RK, OFF_RV, OFF_RG, OFF_MQ, OFF_MK, OFF_MV,
 OFF_HQ, OFF_HF, OFF_HI, OFF_HG, OFF_GR, OFF_GM, OFF_GH) = IN_OFFS

V7X_VMEM_BYTES = 64 * 1024 * 1024
LANES = 128
SUBLANES = 8
MASK_VALUE = -1e30
LOG2_E = math.log2(math.e)

NT_DIMS = (((1,), (1,)), ((), ()))
TN_DIMS = (((0,), (0,)), ((), ()))


def _compiler_params(n_grid_dims, vmem_mib):
    assert vmem_mib * 1024 * 1024 < V7X_VMEM_BYTES
    return pltpu.CompilerParams(
        dimension_semantics=("arbitrary",) * n_grid_dims,
        vmem_limit_bytes=vmem_mib * 1024 * 1024)


def _silu(x):
    return x * jax.nn.sigmoid(x)


BF16_SUBLANES = 16


class SideCast:
    def __init__(self, array, n_steps, part=0, n_parts=1):
        layers, rows, self.cols = array.shape
        part_rows = layers // n_parts * rows
        self.steps_per_slab = 1
        while part_rows * self.steps_per_slab % (n_steps * BF16_SUBLANES):
            self.steps_per_slab *= 2
        assert n_steps % self.steps_per_slab == 0
        self.n_slabs = n_steps // self.steps_per_slab
        self.slab_rows = part_rows // self.n_slabs
        self.first_slab = part * self.n_slabs
        self.operand = array.reshape(n_parts * self.n_slabs, self.slab_rows, self.cols)
        self.part_shape = (layers // n_parts, rows, self.cols)

    def in_spec(self, step_of):
        return pl.BlockSpec(
            (None, self.slab_rows, self.cols),
            lambda *g: (self.first_slab + step_of(*g) // self.steps_per_slab, 0, 0))

    def out_spec(self, step_of):
        return pl.BlockSpec((None, self.slab_rows, self.cols),
                            lambda *g: (step_of(*g) // self.steps_per_slab, 0, 0))

    def out_shape(self):
        return jax.ShapeDtypeStruct((self.n_slabs, self.slab_rows, self.cols), BF16)

    def finish(self, cast):
        return cast.reshape(self.part_shape)


def _run_side_casts(in_refs, out_refs):
    for src, dst in zip(in_refs, out_refs):
        dst[...] = src[...].astype(BF16)


def _cast_kernel(src_ref, dst_ref):
    dst_ref[...] = src_ref[...].astype(BF16)


def cast_layer(w, layer, *, tn=1024):
    _, rows, cols = w.shape
    assert cols % tn == 0
    return pl.pallas_call(
        _cast_kernel,
        grid=(cols // tn,),
        in_specs=[pl.BlockSpec((None, rows, tn), lambda j: (layer, 0, j))],
        out_specs=pl.BlockSpec((None, rows, tn), lambda j: (0, 0, j)),
        out_shape=jax.ShapeDtypeStruct((1, rows, cols), BF16),
        compiler_params=_compiler_params(1, 40),
        name="cast_layer",
    )(w)


def _rms_normalize(x, g):
    r = lax.rsqrt(jnp.mean(x * x, axis=-1, keepdims=True) + NORM_EPS)
    return x * r * g


def _norm_matmul_step(indices, x_ref, g_ref, w_ref, o_ref, h_ref):
    @pl.when(indices[1] == 0)
    def _():
        h_ref[...] = _rms_normalize(x_ref[...], g_ref[...]).astype(BF16)

    o_ref[...] = jnp.dot(h_ref[...], w_ref[...], preferred_element_type=F32)


def norm_matmul(x, g, w, layer, *, tm=1024, tn=1024):
    n, d = x.shape
    cols = w.shape[2]
    assert n % tm == 0 and cols % tn == 0
    pipeline = pltpu.emit_pipeline(
        _norm_matmul_step,
        grid=(n // tm, cols // tn),
        in_specs=[pl.BlockSpec((tm, d), lambda i, j: (i, 0)),
                  pl.BlockSpec((1, d), lambda i, j: (0, 0)),
                  pl.BlockSpec((d, tn), lambda i, j: (0, j))],
        out_specs=[pl.BlockSpec((tm, tn), lambda i, j: (i, j))],
        _explicit_indices=True)

    def call(x_hbm, g_hbm, w_hbm, o_hbm, h_ref):
        pipeline(x_hbm, g_hbm, w_hbm.at[layer], o_hbm, scratches=(h_ref,))

    any_spec = pl.BlockSpec(memory_space=pl.ANY)
    return pl.pallas_call(
        call,
        in_specs=[any_spec, any_spec, any_spec],
        out_specs=any_spec,
        out_shape=jax.ShapeDtypeStruct((n, cols), F32),
        scratch_shapes=[pltpu.VMEM((tm, d), BF16)],
        compiler_params=pltpu.CompilerParams(vmem_limit_bytes=48 * 1024 * 1024),
        name="norm_inproj",
    )(x, g.reshape(1, d), w)


def _ffn_up_step(indices, x_ref, g_ref, wg_ref, wu_ref, o_ref, h_ref):
    @pl.when(indices[1] == 0)
    def _():
        h_ref[...] = _rms_normalize(x_ref[...], g_ref[...]).astype(BF16)

    h = h_ref[...]
    gate = jnp.dot(h, wg_ref[...], preferred_element_type=F32)
    up = jnp.dot(h, wu_ref[...], preferred_element_type=F32)
    o_ref[...] = (_silu(gate) * up).astype(BF16)


def ffn_up(x, g, wg, wu, layer, *, tm=1024, tn=512):
    n, d = x.shape
    hidden = wg.shape[2]
    assert n % tm == 0 and hidden % tn == 0
    w_spec = pl.BlockSpec((d, tn), lambda i, j: (0, j))
    pipeline = pltpu.emit_pipeline(
        _ffn_up_step,
        grid=(n // tm, hidden // tn),
        in_specs=[pl.BlockSpec((tm, d), lambda i, j: (i, 0)),
                  pl.BlockSpec((1, d), lambda i, j: (0, 0)),
                  w_spec, w_spec],
        out_specs=[pl.BlockSpec((tm, tn), lambda i, j: (i, j))],
        _explicit_indices=True)

    def call(x_hbm, g_hbm, wg_hbm, wu_hbm, o_hbm, h_ref):
        pipeline(x_hbm, g_hbm, wg_hbm.at[layer], wu_hbm.at[layer], o_hbm, scratches=(h_ref,))

    any_spec = pl.BlockSpec(memory_space=pl.ANY)
    return pl.pallas_call(
        call,
        in_specs=[any_spec] * 4,
        out_specs=any_spec,
        out_shape=jax.ShapeDtypeStruct((n, hidden), BF16),
        scratch_shapes=[pltpu.VMEM((tm, d), BF16)],
        compiler_params=pltpu.CompilerParams(vmem_limit_bytes=48 * 1024 * 1024),
        name="ffn_up",
    )(x, g.reshape(1, d), wg, wu)


def _ffn_down_kernel(a_ref, w_ref, r_ref, g_ref, o_ref, *, normalize):
    out = r_ref[...] + jnp.dot(a_ref[...], w_ref[...], preferred_element_type=F32)
    o_ref[...] = _rms_normalize(out, g_ref[...]) if normalize else out


def ffn_down(a, w, res, layer, gain, *, normalize, tm=512):
    n, k = a.shape
    cols = w.shape[2]
    assert n % tm == 0
    row_spec = pl.BlockSpec((tm, cols), lambda i: (i, 0))
    return pl.pallas_call(
        functools.partial(_ffn_down_kernel, normalize=normalize),
        grid=(n // tm,),
        in_specs=[pl.BlockSpec((tm, k), lambda i: (i, 0)),
                  pl.BlockSpec((None, k, cols), lambda i: (layer, 0, 0),
                               pipeline_mode=pl.Buffered(1)),
                  row_spec,
                  pl.BlockSpec((1, cols), lambda i: (0, 0))],
        out_specs=row_spec,
        out_shape=jax.ShapeDtypeStruct((n, cols), F32),
        compiler_params=_compiler_params(1, 54),
        name="ffn_down",
    )(a, w, res, gain.reshape(1, cols))


MERGE_GATE_TILE = 1024


def _merge_out_kernel(*refs):
    n_gate = D_MODEL // MERGE_GATE_TILE
    branch_refs = refs[:3]
    gate_refs = refs[3:3 + 3 * n_gate]
    wb_refs = refs[3 + 3 * n_gate:6 + 3 * n_gate]
    wo_ref, res_ref, o_ref = refs[6 + 3 * n_gate:]
    parts = []
    for c in range(n_gate):
        cols = slice(c * MERGE_GATE_TILE, (c + 1) * MERGE_GATE_TILE)
        mixed = None
        for b in range(3):
            term = jax.nn.sigmoid(gate_refs[b * n_gate + c][...]) * jnp.dot(
                branch_refs[b][...], wb_refs[b][:, cols], preferred_element_type=F32)
            mixed = term if mixed is None else mixed + term
        parts.append(mixed.astype(BF16))
    mixed = jnp.concatenate(parts, axis=1)
    o_ref[...] = res_ref[...] + jnp.dot(mixed, wo_ref[...], preferred_element_type=F32)


def merge_out_proj(ret, moba, hgrn, proj, wr, wm, wh, wo, res, layer, *, tm=256):
    n = ret.shape[0]
    assert n % tm == 0 and D_MODEL % MERGE_GATE_TILE == 0
    assert all(off % MERGE_GATE_TILE == 0 for off in (OFF_GR, OFF_GM, OFF_GH))
    n_gate = D_MODEL // MERGE_GATE_TILE

    def branch_spec(width):
        return pl.BlockSpec((tm, width), lambda i: (i, 0))

    def gate_specs(off):
        return [pl.BlockSpec((tm, MERGE_GATE_TILE),
                             functools.partial(lambda i, col: (i, col), col=off // MERGE_GATE_TILE + c))
                for c in range(n_gate)]

    def weight_spec(rows):
        return pl.BlockSpec((None, rows, D_MODEL), lambda i: (layer, 0, 0),
                            pipeline_mode=pl.Buffered(1))

    row_spec = pl.BlockSpec((tm, D_MODEL), lambda i: (i, 0))
    return pl.pallas_call(
        _merge_out_kernel,
        grid=(n // tm,),
        in_specs=[branch_spec(RET_WIDTH), branch_spec(MOBA_WIDTH), branch_spec(HGRN_WIDTH),
                  *gate_specs(OFF_GR), *gate_specs(OFF_GM), *gate_specs(OFF_GH),
                  weight_spec(RET_WIDTH), weight_spec(MOBA_WIDTH), weight_spec(HGRN_WIDTH),
                  weight_spec(D_MODEL), row_spec],
        out_specs=row_spec,
        out_shape=jax.ShapeDtypeStruct((n, D_MODEL), F32),
        compiler_params=_compiler_params(1, 52),
        name="merge_out_proj",
    )(ret, moba, hgrn, *([proj] * (3 * n_gate)), wr, wm, wh, wo, res)


def _retention_kernel(lg_ref, q_ref, k_ref, v_ref, g_ref, cos_ref, sin_ref, nrm_ref,
                      o_ref, state_ref, *, chunks):
    c_len = RET_CHUNK
    d = RET_HEAD_DIM
    half = d // 2

    @pl.when(pl.program_id(1) == 0)
    def _():
        state_ref[...] = jnp.zeros_like(state_ref)

    t_col = lax.broadcasted_iota(jnp.int32, (c_len, 1), 0).astype(F32)
    rel = (lax.broadcasted_iota(jnp.int32, (c_len, c_len), 0)
           - lax.broadcasted_iota(jnp.int32, (c_len, c_len), 1)).astype(F32)

    def rotate(x, cos, sin):
        x1 = x[:, :half]
        x2 = x[:, half:]
        return jnp.concatenate([x1 * cos - x2 * sin, x2 * cos + x1 * sin], axis=1)

    for h in range(RET_HEADS):
        cols = slice(h * d, (h + 1) * d)
        lg = lg_ref[h]
        decay = jnp.where(rel >= 0, jnp.exp(lg * jnp.maximum(rel, 0.0)), 0.0)
        q_weight = jnp.exp(lg * (t_col + 1.0))
        k_weight = jnp.exp(lg * (c_len - 1.0 - t_col))
        chunk_decay = jnp.exp(jnp.full((1, d), lg * c_len, F32))
        for c in range(chunks):
            rows = slice(c * c_len, (c + 1) * c_len)
            cos = cos_ref[rows, :]
            sin = sin_ref[rows, :]
            q = rotate(q_ref[rows, cols], cos, sin)
            k = rotate(k_ref[rows, cols], cos, sin) * (d ** -0.5)
            v = v_ref[rows, cols].astype(BF16)
            state = state_ref[h]

            scores = lax.dot_general(q.astype(BF16), k.astype(BF16), NT_DIMS,
                                     preferred_element_type=F32) * decay
            intra = jnp.dot(scores.astype(BF16), v, preferred_element_type=F32)
            cross = jnp.dot((q * q_weight).astype(BF16), state.astype(BF16),
                            preferred_element_type=F32)
            kv = lax.dot_general((k * k_weight).astype(BF16), v, TN_DIMS,
                                 preferred_element_type=F32)
            state_ref[h] = chunk_decay * state + kv

            out = intra + cross
            centered = out - jnp.mean(out, axis=-1, keepdims=True)
            normed = centered * lax.rsqrt(
                jnp.mean(centered * centered, axis=-1, keepdims=True) + NORM_EPS)
            gated = (normed * nrm_ref[:, cols]) * _silu(g_ref[rows, cols])
            o_ref[rows, cols] = gated.astype(BF16)


def retention_mixer(proj, ret_norm, batch, seq, *, rows_per_step=1024):
    n = proj.shape[0]
    d = RET_HEAD_DIM
    assert seq % rows_per_step == 0 and rows_per_step % RET_CHUNK == 0
    steps = seq // rows_per_step
    log_gamma = jnp.log1p(-jnp.exp2(-5.0 - jnp.arange(RET_HEADS, dtype=F32)))
    half = d // 2
    inv_freq = RET_ROPE_BASE ** (-jnp.arange(half, dtype=F32) * 2.0 / d)
    ang = jnp.arange(seq).astype(F32)[:, None] * inv_freq[None, :]
    cos, sin = jnp.cos(ang), jnp.sin(ang)

    def col_spec(off):
        return pl.BlockSpec((rows_per_step, RET_WIDTH),
                            lambda b, s: (b * steps + s, off // RET_WIDTH))

    table_spec = pl.BlockSpec((rows_per_step, half), lambda b, s: (s, 0))
    return pl.pallas_call(
        functools.partial(_retention_kernel, chunks=rows_per_step // RET_CHUNK),
        grid=(batch, steps),
        in_specs=[pl.BlockSpec(memory_space=pltpu.SMEM),
                  col_spec(OFF_RQ), col_spec(OFF_RK), col_spec(OFF_RV), col_spec(OFF_RG),
                  table_spec, table_spec,
                  pl.BlockSpec((1, RET_WIDTH), lambda b, s: (0, 0))],
        out_specs=pl.BlockSpec((rows_per_step, RET_WIDTH), lambda b, s: (b * steps + s, 0)),
        out_shape=jax.ShapeDtypeStruct((n, RET_WIDTH), BF16),
        scratch_shapes=[pltpu.VMEM((RET_HEADS, d, d), F32)],
        compiler_params=_compiler_params(2, 52),
        name="retention",
    )(log_gamma, proj, proj, proj, proj, cos, sin, ret_norm.reshape(1, RET_WIDTH))


MOBA_PAIRS_PER_STEP = 2


def _moba_rotate(x, cos, sin_lo, sin_hi):
    half = ROPE_DIM // 2
    return (x * cos + pltpu.roll(x, MOBA_HEAD_DIM - half, 1) * sin_lo
            + pltpu.roll(x, half, 1) * sin_hi)


def _split_dot_nt(a, b):
    a_hi = a.astype(BF16)
    a_lo = (a - a_hi.astype(F32)).astype(BF16)
    b_hi = b.astype(BF16)
    b_lo = (b - b_hi.astype(F32)).astype(BF16)

    def dot_nt(x, y):
        return lax.dot_general(x, y, NT_DIMS, preferred_element_type=F32)

    return dot_nt(a_hi, b_hi) + dot_nt(a_hi, b_lo) + dot_nt(a_lo, b_hi)


def _moba_kernel(*refs, n_blocks, n_side):
    q_ref, k_ref, v_ref, cos_ref, slo_ref, shi_ref = refs[:6]
    o_ref = refs[6 + n_side]
    (kaug_ref, vt_ref, kmean_ref, qt_ref, s_ref, acc_ref,
     snap_ref) = refs[7 + 2 * n_side:]
    _run_side_casts(refs[6:6 + n_side], refs[7 + n_side:7 + 2 * n_side])
    blk = MOBA_BLOCK
    d = MOBA_HEAD_DIM
    step = pl.program_id(2)
    n_past = n_blocks - 1
    pairs = snap_ref.shape[0]
    gate_rows = kmean_ref.shape[0]
    groups = blk // SUBLANES
    q_scale = (d ** -0.5) * LOG2_E

    def rotate(ref, start):
        rows = pl.ds(start, blk)
        return _moba_rotate(ref[rows, :], cos_ref[rows, :], slo_ref[rows, :], shi_ref[rows, :])

    @pl.when(step == 0)
    def _():
        lane = lax.broadcasted_iota(jnp.int32, (blk, LANES), 1)
        eye = jnp.where(lax.broadcasted_iota(jnp.int32, (d, d), 0)
                        == lax.broadcasted_iota(jnp.int32, (d, d), 1), 1.0, 0.0).astype(BF16)

        def transposed(x_bf):
            return lax.dot_general(eye, x_bf, NT_DIMS, preferred_element_type=F32)

        kmean_ref[...] = jnp.zeros_like(kmean_ref)
        for j in range(n_blocks):
            kr = rotate(k_ref, j * blk)
            kaug_ref[j, :, :d] = kr.astype(BF16)
            kaug_ref[j, :, d:] = jnp.where(lane == j, 1.0, 0.0).astype(BF16)
            vt_ref[j] = transposed(v_ref[j * blk:(j + 1) * blk, :].astype(BF16)).astype(BF16)
            kmean_ref[j:j + 1, :] = jnp.mean(kr, axis=0, keepdims=True)
        row = lax.broadcasted_iota(jnp.int32, (gate_rows, blk), 0)
        row_f = row.astype(F32)
        pad = jnp.zeros((LANES - gate_rows, blk), F32)
        for i in range(n_blocks):
            q = rotate(q_ref, i * blk)
            gate = _split_dot_nt(kmean_ref[...], q)
            q_t = transposed((q * q_scale).astype(BF16))
            g = jnp.where(row < i, gate, -jnp.inf)
            bias = jnp.full((gate_rows, blk), MASK_VALUE, F32)
            for _ in range(min(MOBA_TOPK, n_blocks)):
                best = jnp.max(g, axis=0, keepdims=True)
                hit = (g == best) & (g > -jnp.inf)
                first = jnp.min(jnp.where(hit, row_f, float(LANES)), axis=0, keepdims=True)
                pick = row_f == first
                bias = jnp.where(pick, 0.0, bias)
                g = jnp.where(pick, -jnp.inf, g)
            bias = jnp.where(row == i, 0.0, bias)
            qt_ref[i] = jnp.concatenate([q_t, bias, pad], axis=0).astype(BF16)

    def group_reduce(x, op):
        return op(x.reshape(groups, SUBLANES, blk), axis=0)

    def score_own_block(i, own_slot):
        s_t = jnp.dot(kaug_ref[i], qt_ref[i], preferred_element_type=F32)
        key = lax.broadcasted_iota(jnp.int32, (blk, blk), 0)
        qry = lax.broadcasted_iota(jnp.int32, (blk, blk), 1)
        s_t = jnp.where(key <= qry, s_t, MASK_VALUE)
        s_ref[own_slot] = s_t
        return group_reduce(s_t, jnp.max)

    def tile_of(t, p):
        in_b = t >= p
        return in_b, jnp.where(in_b, t - p, t)

    def col_max(m8):
        return jnp.broadcast_to(jnp.max(m8, axis=0, keepdims=True), (SUBLANES, blk))

    def weights(slot, m8):
        p_t = jnp.exp2(s_ref[slot].reshape(groups, SUBLANES, blk) - m8[None])
        return jnp.sum(p_t, axis=0), p_t.reshape(blk, blk).astype(BF16)

    def values(j, p_bf):
        return jnp.dot(vt_ref[j], p_bf, preferred_element_type=F32)

    never = jnp.full((SUBLANES, blk), -jnp.inf, F32)
    scored = []
    for k in range(pairs):
        p = step * pairs + k
        base = k * (n_past + 2)
        i_a, i_b = p, n_past - p
        m_a = score_own_block(i_a, base + n_past)
        m_b = score_own_block(i_b, base + n_past + 1)
        for t in range(n_past):
            in_b, j = tile_of(t, p)
            s_t = jnp.dot(kaug_ref[j], qt_ref[jnp.where(in_b, i_b, i_a)],
                          preferred_element_type=F32)
            s_ref[base + t] = s_t
            gm = group_reduce(s_t, jnp.max)
            m_a = jnp.maximum(m_a, jnp.where(in_b, never, gm))
            m_b = jnp.maximum(m_b, jnp.where(in_b, gm, never))
        scored.append((p, base, i_a, i_b, col_max(m_a), col_max(m_b)))

    for k, (p, base, i_a, i_b, m_a, m_b) in enumerate(scored):
        l_a, p_own = weights(base + n_past, m_a)
        acc_ref[2 * k] = values(i_a, p_own)
        l_b, p_own = weights(base + n_past + 1, m_b)
        acc_ref[2 * k + 1] = values(i_b, p_own)
        acc = jnp.zeros((d, blk), F32)
        for t in range(n_past):
            in_b, j = tile_of(t, p)
            l_part, p_bf = weights(base + t, jnp.where(in_b, m_b, m_a))
            acc = acc * jnp.where(t == p, 0.0, 1.0) + values(j, p_bf)
            l_a = l_a + jnp.where(in_b, 0.0, l_part)
            l_b = l_b + jnp.where(in_b, l_part, 0.0)
            if t < snap_ref.shape[1]:
                snap_ref[k, t] = acc
        past_a = snap_ref[k, jnp.maximum(p - 1, 0)] * jnp.where(p > 0, 1.0, 0.0)
        for w, i, l8, past in ((2 * k, i_a, l_a, past_a), (2 * k + 1, i_b, l_b, acc)):
            l = jnp.sum(l8, axis=0, keepdims=True)
            out_t = (acc_ref[w] + past) / l
            o_ref[pl.ds(pl.multiple_of(i * blk, blk), blk), :] = out_t.T.astype(BF16)


def moba_grid_steps(batch, seq):
    return batch * MOBA_HEADS * (seq // MOBA_BLOCK // 2 // MOBA_PAIRS_PER_STEP)


def moba_mixer(proj, batch, seq, side_casts=()):
    n = proj.shape[0]
    d = MOBA_HEAD_DIM
    blk = MOBA_BLOCK
    assert seq % blk == 0
    n_blocks = seq // blk
    assert n_blocks <= LANES and n_blocks % 2 == 0
    half = ROPE_DIM // 2
    inv_freq = ROPE_THETA ** (-jnp.arange(half, dtype=F32) * 2.0 / ROPE_DIM)
    ang = jnp.arange(seq).astype(F32)[:, None] * inv_freq[None, :]
    cos, sin = jnp.cos(ang), jnp.sin(ang)
    zeros = jnp.zeros((seq, d - ROPE_DIM), F32)
    zero_half = jnp.zeros((seq, half), F32)
    cos_t = jnp.concatenate([cos, cos, jnp.ones_like(zeros)], axis=1)
    sin_lo = jnp.concatenate([-sin, zero_half, zeros], axis=1)
    sin_hi = jnp.concatenate([zero_half, sin, zeros], axis=1)

    def col_spec(off):
        return pl.BlockSpec((seq, d), lambda b, h, p: (b, off // d + h))

    table_spec = pl.BlockSpec((seq, d), lambda b, h, p: (0, 0))
    gate_rows = -(-n_blocks // SUBLANES) * SUBLANES
    pairs = MOBA_PAIRS_PER_STEP
    assert (n_blocks // 2) % pairs == 0
    n_snap = max(n_blocks // 2 - 1, 1)
    steps = n_blocks // 2 // pairs

    def step_of(b, h, p):
        return (b * MOBA_HEADS + h) * steps + p

    out, *casts = pl.pallas_call(
        functools.partial(_moba_kernel, n_blocks=n_blocks, n_side=len(side_casts)),
        grid=(batch, MOBA_HEADS, steps),
        in_specs=[col_spec(OFF_MQ), col_spec(OFF_MK), col_spec(OFF_MV),
                  table_spec, table_spec, table_spec,
                  *[c.in_spec(step_of) for c in side_casts]],
        out_specs=[pl.BlockSpec((seq, d), lambda b, h, p: (b, h)),
                   *[c.out_spec(step_of) for c in side_casts]],
        out_shape=[jax.ShapeDtypeStruct((n, MOBA_WIDTH), BF16),
                   *[c.out_shape() for c in side_casts]],
        scratch_shapes=[pltpu.VMEM((n_blocks, blk, 2 * d), BF16),
                        pltpu.VMEM((n_blocks, d, blk), BF16),
                        pltpu.VMEM((gate_rows, d), F32),
                        pltpu.VMEM((n_blocks, 2 * d, blk), BF16),
                        pltpu.VMEM((pairs * (n_blocks + 1), blk, blk), F32),
                        pltpu.VMEM((2 * pairs, d, blk), F32),
                        pltpu.VMEM((pairs, n_snap, d, blk), F32)],
        compiler_params=_compiler_params(3, 54),
        name="moba",
    )(proj, proj, proj, cos_t, sin_lo, sin_hi, *[c.operand for c in side_casts])
    return out, [c.finish(x) for c, x in zip(side_casts, casts)]


def _hgrn_pair_codes(block):
    i = np.arange(block)[:, None]
    j = np.arange(block)[None, :]
    top_bit = np.floor(np.log2(np.maximum(i ^ j, 1))).astype(np.int32)
    n_levels = block.bit_length() - 1
    return np.where(i > j, top_bit, np.where(i == j, n_levels, -1)).astype(np.int32)


def _hgrn_kernel(*refs, layer, rows_per_step, n_side):
    lbp_ref, tril_ref, code_ref, q_ref, f_ref, v_ref, g_ref, nrm_ref = refs[:8]
    o_ref = refs[8 + n_side]
    state_ref, b_ref, k_ref = refs[9 + 2 * n_side:]
    _run_side_casts(refs[8:8 + n_side], refs[9 + n_side:9 + 2 * n_side])
    c_len = HGRN_BLOCK
    d = HGRN_HEAD_DIM

    @pl.when(pl.program_id(2) == 0)
    def _():
        state_ref[...] = jnp.zeros_like(state_ref)

    params = lbp_ref[...]
    e = jnp.exp(params - jnp.max(params, axis=0, keepdims=True))
    soft = e / jnp.sum(e, axis=0, keepdims=True)
    lower = jnp.zeros((1, d), F32)
    for r in range(1, layer + 1):
        lower = lower + soft[r:r + 1, :]

    tril = tril_ref[...]
    for seg0 in range(0, rows_per_step, c_len):
        seg = slice(seg0, seg0 + c_len)
        forget = lower + (1.0 - lower) * jax.nn.sigmoid(f_ref[seg, :])
        k_ref[seg, :] = 1.0 - forget
        log_f = jnp.log2(forget)
        hi = log_f.astype(BF16)
        rest = log_f - hi.astype(F32)
        mid = rest.astype(BF16)
        lo = (rest - mid.astype(F32)).astype(BF16)
        b_ref[seg, :] = (jnp.dot(tril, hi, preferred_element_type=F32)
                         + jnp.dot(tril, mid, preferred_element_type=F32)
                         + jnp.dot(tril, lo, preferred_element_type=F32))

    assert c_len & (c_len - 1) == 0 and c_len % LANES == 0
    levels = [1 << s for s in range(c_len.bit_length() - 1)]
    n_grp = c_len // SUBLANES
    row8 = lax.broadcasted_iota(jnp.int32, (SUBLANES, d), 0)
    zero8 = jnp.zeros((SUBLANES, d), F32)
    codes = _hgrn_pair_codes(c_len)
    tiles = [(g, cb) for g in range(n_grp) for cb in range(c_len // LANES)]

    def code_tile(g, cb):
        return codes[SUBLANES * g:SUBLANES * (g + 1), LANES * cb:LANES * (cb + 1)]

    level_tiles = [[(g, cb, bool(np.all(code_tile(g, cb) == bit))) for g, cb in tiles
                    if np.any(code_tile(g, cb) == bit)] for bit in range(len(levels) + 1)]

    def bcast_row(ref, r):
        return jnp.broadcast_to(ref[r:r + 1, :], (SUBLANES, d))

    def level_operands(r0, h, bg, qg, kg):
        q_parts, k_parts = [], []
        for g in range(n_grp):
            base = SUBLANES * g
            if h >= SUBLANES:
                b_a = bcast_row(b_ref, r0 + base // (2 * h) * (2 * h) + h - 1)
                if base & h:
                    q_parts.append(qg[g] * jnp.exp2(bg[g] - b_a))
                    k_parts.append(zero8)
                else:
                    q_parts.append(zero8)
                    k_parts.append(kg[g] * jnp.exp2(b_a - bg[g]))
                continue
            upper = (row8 & h) != 0
            if h == 1:
                q_parts.append(jnp.where(upper, qg[g] * (1.0 - kg[g]), 0.0))
                k_parts.append(jnp.where(upper, 0.0, kg[g]))
                continue
            b_a = bcast_row(b_ref, r0 + base + h - 1)
            for s in range(2 * h, SUBLANES, 2 * h):
                b_a = jnp.where(row8 >= s, bcast_row(b_ref, r0 + base + s + h - 1), b_a)
            decay = jnp.exp2(-jnp.abs(bg[g] - b_a))
            q_parts.append(jnp.where(upper, qg[g] * decay, 0.0))
            k_parts.append(jnp.where(upper, 0.0, kg[g] * decay))
        return (jnp.concatenate(q_parts, axis=0).astype(BF16),
                jnp.concatenate(k_parts, axis=0).astype(BF16))

    for c in range(rows_per_step // c_len):
        r0 = c * c_len
        rows = slice(r0, r0 + c_len)
        b = b_ref[rows, :]
        q = _silu(q_ref[rows, :])
        k = k_ref[rows, :]
        v = v_ref[rows, :]

        groups = [slice(SUBLANES * g, SUBLANES * (g + 1)) for g in range(n_grp)]
        bg = [b[g, :] for g in groups]
        qg = [q[g, :] for g in groups]
        kg = [k[g, :] for g in groups]
        def code_of(g, cb):
            return code_ref[SUBLANES * g:SUBLANES * (g + 1), LANES * cb:LANES * (cb + 1)]

        a_tiles = {}
        self_weight = jnp.sum(q * k, axis=-1, keepdims=True)
        for g, cb, _ in level_tiles[len(levels)]:
            a_tiles[g, cb] = jnp.where(code_of(g, cb) == len(levels),
                                       self_weight[groups[g], :], 0.0)
        for bit, h in enumerate(levels):
            q_h, k_h = level_operands(r0, h, bg, qg, kg)
            for cb in range(c_len // LANES):
                k_lo = LANES * cb
                if 2 * h >= LANES:
                    start = k_lo // (2 * h) * (2 * h)
                    if k_lo >= start + h:
                        continue
                    q_lo, q_hi = start + h, start + 2 * h
                else:
                    q_lo, q_hi = k_lo, k_lo + LANES
                pair = lax.dot_general(q_h[q_lo:q_hi, :], k_h[k_lo:k_lo + LANES, :], NT_DIMS,
                                       preferred_element_type=F32)
                for g, tile_cb, owns_tile in level_tiles[bit]:
                    if tile_cb != cb:
                        continue
                    assert q_lo <= SUBLANES * g < q_hi
                    piece = pair[SUBLANES * g - q_lo:SUBLANES * (g + 1) - q_lo, :]
                    if owns_tile:
                        a_tiles[g, cb] = piece
                    else:
                        a_tiles[g, cb] = jnp.where(code_of(g, cb) == bit, piece,
                                                   a_tiles.get((g, cb), 0.0))
        zero_tile = jnp.zeros((SUBLANES, LANES), F32)
        a_mat = jnp.concatenate(
            [jnp.concatenate([a_tiles.get((g, cb), zero_tile) for g in range(n_grp)], axis=0)
             for cb in range(c_len // LANES)], axis=1)
        intra = jnp.dot(a_mat.astype(BF16), v.astype(BF16), preferred_element_type=F32)

        state_t = state_ref[...]
        cross = lax.dot_general((q * jnp.exp2(b)).astype(BF16), state_t.astype(BF16), NT_DIMS,
                                preferred_element_type=F32)
        b_last = b_ref[r0 + c_len - 1:r0 + c_len, :]
        k_dec = k * jnp.exp2(b_last - b)
        state_ref[...] = state_t * jnp.exp2(b_last) + lax.dot_general(
            v.astype(BF16), k_dec.astype(BF16), TN_DIMS, preferred_element_type=F32)

        out = intra + cross
        normed = out * lax.rsqrt(jnp.mean(out * out, axis=-1, keepdims=True) + NORM_EPS)
        gated = (normed * nrm_ref[...]) * _silu(g_ref[rows, :])
        o_ref[rows, :] = gated.astype(BF16)


HGRN_ROWS_PER_STEP = 2048


def hgrn_grid_steps(batch, seq):
    return batch * HGRN_HEADS * (seq // min(seq, HGRN_ROWS_PER_STEP))


def hgrn_mixer(proj, lower_bound_params, hgrn_norm, layer, batch, seq, side_casts=()):
    rows_per_step = min(seq, HGRN_ROWS_PER_STEP)
    n = proj.shape[0]
    d = HGRN_HEAD_DIM
    depth = lower_bound_params.shape[0]
    blk = HGRN_BLOCK
    assert seq % rows_per_step == 0 and rows_per_step % blk == 0
    steps = seq // rows_per_step
    codes = _hgrn_pair_codes(blk)
    tril = jnp.asarray(codes >= 0, BF16)

    def col_spec(off):
        return pl.BlockSpec((rows_per_step, d), lambda b, h, s: (b * steps + s, off // d + h))

    def step_of(b, h, s):
        return (b * HGRN_HEADS + h) * steps + s

    out, *casts = pl.pallas_call(
        functools.partial(_hgrn_kernel, layer=layer, rows_per_step=rows_per_step,
                          n_side=len(side_casts)),
        grid=(batch, HGRN_HEADS, steps),
        in_specs=[pl.BlockSpec((depth, d), lambda b, h, s: (0, h)),
                  pl.BlockSpec((blk, blk), lambda b, h, s: (0, 0)),
                  pl.BlockSpec((blk, blk), lambda b, h, s: (0, 0)),
                  col_spec(OFF_HQ), col_spec(OFF_HF), col_spec(OFF_HI), col_spec(OFF_HG),
                  pl.BlockSpec((1, d), lambda b, h, s: (0, h)),
                  *[c.in_spec(step_of) for c in side_casts]],
        out_specs=[pl.BlockSpec((rows_per_step, d), lambda b, h, s: (b * steps + s, h)),
                   *[c.out_spec(step_of) for c in side_casts]],
        out_shape=[jax.ShapeDtypeStruct((n, HGRN_WIDTH), BF16),
                   *[c.out_shape() for c in side_casts]],
        scratch_shapes=[pltpu.VMEM((d, d), F32),
                        pltpu.VMEM((rows_per_step, d), F32),
                        pltpu.VMEM((rows_per_step, d), F32)],
        compiler_params=_compiler_params(3, 40),
        name="hgrn2",
    )(lower_bound_params.astype(F32), tril, jnp.asarray(codes), proj, proj, proj, proj,
      hgrn_norm.reshape(1, HGRN_WIDTH), *[c.operand for c in side_casts])
    return out, [c.finish(x) for c, x in zip(side_casts, casts)]


def kernel(x, hgrn_lower_bounds, norm_mix, w_in, ret_norm, hgrn_norm, w_branch_ret, w_branch_moba,
           w_branch_hgrn, w_out, norm_ffn, w_ffn_gate, w_ffn_up, w_ffn_down, final_norm):
    batch, seq, d_model = x.shape
    assert d_model == D_MODEL
    depth = w_in.shape[0]
    w_in_layers = [cast_layer(w_in, 0)]
    moba_steps = moba_grid_steps(batch, seq)
    hgrn_steps = hgrn_grid_steps(batch, seq)
    moba_casts = [SideCast(w, moba_steps) for w in (w_ffn_gate, w_ffn_up, w_ffn_down)]
    hgrn_casts = [SideCast(w, hgrn_steps)
                  for w in (w_branch_ret, w_branch_moba, w_branch_hgrn, w_out)]
    hgrn_casts += [SideCast(w_in, hgrn_steps, part=l, n_parts=depth) for l in range(1, depth)]
    h = x.reshape(batch * seq, d_model)
    for layer in range(depth):
        proj = norm_matmul(h, norm_mix[layer], w_in_layers[layer], 0)
        ret = retention_mixer(proj, ret_norm[layer], batch, seq)
        moba, cast = moba_mixer(proj, batch, seq, moba_casts if layer == 0 else ())
        if layer == 0:
            w_ffn_gate, w_ffn_up, w_ffn_down = cast
        hgrn, cast = hgrn_mixer(proj, hgrn_lower_bounds, hgrn_norm[layer], layer, batch, seq,
                                hgrn_casts if layer == 0 else ())
        if layer == 0:
            w_branch_ret, w_branch_moba, w_branch_hgrn, w_out = cast[:4]
            w_in_layers += cast[4:]
        h = merge_out_proj(ret, moba, hgrn, proj, w_branch_ret, w_branch_moba, w_branch_hgrn,
                           w_out, h, layer)
        act = ffn_up(h, norm_ffn[layer], w_ffn_gate, w_ffn_up, layer)
        h = ffn_down(act, w_ffn_down, h, layer, final_norm, normalize=layer == depth - 1)
    return h.reshape(batch, seq, d_model)
```

```python
import functools
import math

import jax
import jax.numpy as jnp
import numpy as np
from jax import lax
from jax.experimental import pallas as pl
from jax.experimental.pallas import tpu as pltpu

F32 = jnp.float32
BF16 = jnp.bfloat16

D_MODEL = 2048
RET_HEADS = 4
RET_HEAD_DIM = 256
RET_WIDTH = RET_HEADS * RET_HEAD_DIM
RET_CHUNK = 128
RET_ROPE_BASE = 10000.0
MOBA_HEADS = 8
MOBA_HEAD_DIM = 128
MOBA_WIDTH = MOBA_HEADS * MOBA_HEAD_DIM
MOBA_BLOCK = 256
MOBA_TOPK = 3
ROPE_THETA = 500000.0
ROPE_DIM = MOBA_HEAD_DIM // 4
HGRN_HEADS = 8
HGRN_HEAD_DIM = 128
HGRN_WIDTH = HGRN_HEADS * HGRN_HEAD_DIM
HGRN_BLOCK = 256
NORM_EPS = 1e-6
IN_SIZES = (RET_WIDTH,) * 4 + (MOBA_WIDTH,) * 3 + (HGRN_WIDTH,) * 4 + (D_MODEL,) * 3
IN_COLS = sum(IN_SIZES)
IN_OFFS = tuple(sum(IN_SIZES[:i]) for i in range(len(IN_SIZES)))
(OFF_RQ, OFF_RK, OFF_RV, OFF_RG, OFF_MQ, OFF_MK, OFF_MV,
 OFF_HQ, OFF_HF, OFF_HI, OFF_HG, OFF_GR, OFF_GM, OFF_GH) = IN_OFFS

V7X_VMEM_BYTES = 64 * 1024 * 1024
LANES = 128
SUBLANES = 8
MASK_VALUE = -1e30
LOG2_E = math.log2(math.e)

NT_DIMS = (((1,), (1,)), ((), ()))
TN_DIMS = (((0,), (0,)), ((), ()))


def _compiler_params(n_grid_dims, vmem_mib):
    assert vmem_mib * 1024 * 1024 < V7X_VMEM_BYTES
    return pltpu.CompilerParams(
        dimension_semantics=("arbitrary",) * n_grid_dims,
        vmem_limit_bytes=vmem_mib * 1024 * 1024)


def _silu(x):
    return x * jax.nn.sigmoid(x)


BF16_SUBLANES = 16


class SideCast:
    def __init__(self, array, n_steps, part=0, n_parts=1):
        layers, rows, self.cols = array.shape
        part_rows = layers // n_parts * rows
        self.steps_per_slab = 1
        while part_rows * self.steps_per_slab % (n_steps * BF16_SUBLANES):
            self.steps_per_slab *= 2
        assert n_steps % self.steps_per_slab == 0
        self.n_slabs = n_steps // self.steps_per_slab
        self.slab_rows = part_rows // self.n_slabs
        self.first_slab = part * self.n_slabs
        self.operand = array.reshape(n_parts * self.n_slabs, self.slab_rows, self.cols)
        self.part_shape = (layers // n_parts, rows, self.cols)

    def in_spec(self, step_of):
        return pl.BlockSpec(
            (None, self.slab_rows, self.cols),
            lambda *g: (self.first_slab + step_of(*g) // self.steps_per_slab, 0, 0))

    def out_spec(self, step_of):
        return pl.BlockSpec((None, self.slab_rows, self.cols),
                            lambda *g: (step_of(*g) // self.steps_per_slab, 0, 0))

    def out_shape(self):
        return jax.ShapeDtypeStruct((self.n_slabs, self.slab_rows, self.cols), BF16)

    def finish(self, cast):
        return cast.reshape(self.part_shape)


def _run_side_casts(in_refs, out_refs):
    for src, dst in zip(in_refs, out_refs):
        dst[...] = src[...].astype(BF16)


def _cast_kernel(src_ref, dst_ref):
    dst_ref[...] = src_ref[...].astype(BF16)


def cast_layer(w, layer, *, tn=1024):
    _, rows, cols = w.shape
    assert cols % tn == 0
    return pl.pallas_call(
        _cast_kernel,
        grid=(cols // tn,),
        in_specs=[pl.BlockSpec((None, rows, tn), lambda j: (layer, 0, j))],
        out_specs=pl.BlockSpec((None, rows, tn), lambda j: (0, 0, j)),
        out_shape=jax.ShapeDtypeStruct((1, rows, cols), BF16),
        compiler_params=_compiler_params(1, 40),
        name="cast_layer",
    )(w)


def _rms_normalize(x, g):
    r = lax.rsqrt(jnp.mean(x * x, axis=-1, keepdims=True) + NORM_EPS)
    return x * r * g


def _norm_matmul_step(indices, x_ref, g_ref, w_ref, o_ref, h_ref):
    @pl.when(indices[1] == 0)
    def _():
        h_ref[...] = _rms_normalize(x_ref[...], g_ref[...]).astype(BF16)

    o_ref[...] = jnp.dot(h_ref[...], w_ref[...], preferred_element_type=F32)


def norm_matmul(x, g, w, layer, *, tm=1024, tn=1024):
    n, d = x.shape
    cols = w.shape[2]
    assert n % tm == 0 and cols % tn == 0
    pipeline = pltpu.emit_pipeline(
        _norm_matmul_step,
        grid=(n // tm, cols // tn),
        in_specs=[pl.BlockSpec((tm, d), lambda i, j: (i, 0)),
                  pl.BlockSpec((1, d), lambda i, j: (0, 0)),
                  pl.BlockSpec((d, tn), lambda i, j: (0, j))],
        out_specs=[pl.BlockSpec((tm, tn), lambda i, j: (i, j))],
        _explicit_indices=True)

    def call(x_hbm, g_hbm, w_hbm, o_hbm, h_ref):
        pipeline(x_hbm, g_hbm, w_hbm.at[layer], o_hbm, scratches=(h_ref,))

    any_spec = pl.BlockSpec(memory_space=pl.ANY)
    return pl.pallas_call(
        call,
        in_specs=[any_spec, any_spec, any_spec],
        out_specs=any_spec,
        out_shape=jax.ShapeDtypeStruct((n, cols), F32),
        scratch_shapes=[pltpu.VMEM((tm, d), BF16)],
        compiler_params=pltpu.CompilerParams(vmem_limit_bytes=48 * 1024 * 1024),
        name="norm_inproj",
    )(x, g.reshape(1, d), w)


def _ffn_up_kernel(x_ref, g_ref, wg_ref, wu_ref, o_ref, h_ref):
    @pl.when(pl.program_id(1) == 0)
    def _():
        h_ref[...] = _rms_normalize(x_ref[...], g_ref[...]).astype(BF16)

    h = h_ref[...]
    gate = jnp.dot(h, wg_ref[...], preferred_element_type=F32)
    up = jnp.dot(h, wu_ref[...], preferred_element_type=F32)
    o_ref[...] = (_silu(gate) * up).astype(BF16)


def ffn_up(x, g, wg, wu, layer, *, tm=1024, tn=512):
    n, d = x.shape
    hidden = wg.shape[2]
    assert n % tm == 0 and hidden % tn == 0
    w_spec = pl.BlockSpec((None, d, tn), lambda i, j: (layer, 0, j))
    return pl.pallas_call(
        _ffn_up_kernel,
        grid=(n // tm, hidden // tn),
        in_specs=[pl.BlockSpec((tm, d), lambda i, j: (i, 0)),
                  pl.BlockSpec((1, d), lambda i, j: (0, 0)),
                  w_spec, w_spec],
        out_specs=pl.BlockSpec((tm, tn), lambda i, j: (i, j)),
        out_shape=jax.ShapeDtypeStruct((n, hidden), BF16),
        scratch_shapes=[pltpu.VMEM((tm, d), BF16)],
        compiler_params=_compiler_params(2, 48),
        name="ffn_up",
    )(x, g.reshape(1, d), wg, wu)


def _ffn_down_kernel(a_ref, w_ref, r_ref, g_ref, o_ref, *, normalize):
    out = r_ref[...] + jnp.dot(a_ref[...], w_ref[...], preferred_element_type=F32)
    o_ref[...] = _rms_normalize(out, g_ref[...]) if normalize else out


def ffn_down(a, w, res, layer, gain, *, normalize, tm=512):
    n, k = a.shape
    cols = w.shape[2]
    assert n % tm == 0
    row_spec = pl.BlockSpec((tm, cols), lambda i: (i, 0))
    return pl.pallas_call(
        functools.partial(_ffn_down_kernel, normalize=normalize),
        grid=(n // tm,),
        in_specs=[pl.BlockSpec((tm, k), lambda i: (i, 0)),
                  pl.BlockSpec((None, k, cols), lambda i: (layer, 0, 0),
                               pipeline_mode=pl.Buffered(1)),
                  row_spec,
                  pl.BlockSpec((1, cols), lambda i: (0, 0))],
        out_specs=row_spec,
        out_shape=jax.ShapeDtypeStruct((n, cols), F32),
        compiler_params=_compiler_params(1, 54),
        name="ffn_down",
    )(a, w, res, gain.reshape(1, cols))


MERGE_GATE_TILE = 1024


def _merge_out_kernel(*refs):
    n_gate = D_MODEL // MERGE_GATE_TILE
    branch_refs = refs[:3]
    gate_refs = refs[3:3 + 3 * n_gate]
    wb_refs = refs[3 + 3 * n_gate:6 + 3 * n_gate]
    wo_ref, res_ref, o_ref = refs[6 + 3 * n_gate:]
    parts = []
    for c in range(n_gate):
        cols = slice(c * MERGE_GATE_TILE, (c + 1) * MERGE_GATE_TILE)
        mixed = None
        for b in range(3):
            term = jax.nn.sigmoid(gate_refs[b * n_gate + c][...]) * jnp.dot(
                branch_refs[b][...], wb_refs[b][:, cols], preferred_element_type=F32)
            mixed = term if mixed is None else mixed + term
        parts.append(mixed.astype(BF16))
    mixed = jnp.concatenate(parts, axis=1)
    o_ref[...] = res_ref[...] + jnp.dot(mixed, wo_ref[...], preferred_element_type=F32)


def merge_out_proj(ret, moba, hgrn, proj, wr, wm, wh, wo, res, layer, *, tm=256):
    n = ret.shape[0]
    assert n % tm == 0 and D_MODEL % MERGE_GATE_TILE == 0
    assert all(off % MERGE_GATE_TILE == 0 for off in (OFF_GR, OFF_GM, OFF_GH))
    n_gate = D_MODEL // MERGE_GATE_TILE

    def branch_spec(width):
        return pl.BlockSpec((tm, width), lambda i: (i, 0))

    def gate_specs(off):
        return [pl.BlockSpec((tm, MERGE_GATE_TILE),
                             functools.partial(lambda i, col: (i, col), col=off // MERGE_GATE_TILE + c))
                for c in range(n_gate)]

    def weight_spec(rows):
        return pl.BlockSpec((None, rows, D_MODEL), lambda i: (layer, 0, 0),
                            pipeline_mode=pl.Buffered(1))

    row_spec = pl.BlockSpec((tm, D_MODEL), lambda i: (i, 0))
    return pl.pallas_call(
        _merge_out_kernel,
        grid=(n // tm,),
        in_specs=[branch_spec(RET_WIDTH), branch_spec(MOBA_WIDTH), branch_spec(HGRN_WIDTH),
                  *gate_specs(OFF_GR), *gate_specs(OFF_GM), *gate_specs(OFF_GH),
                  weight_spec(RET_WIDTH), weight_spec(MOBA_WIDTH), weight_spec(HGRN_WIDTH),
                  weight_spec(D_MODEL), row_spec],
        out_specs=row_spec,
        out_shape=jax.ShapeDtypeStruct((n, D_MODEL), F32),
        compiler_params=_compiler_params(1, 52),
        name="merge_out_proj",
    )(ret, moba, hgrn, *([proj] * (3 * n_gate)), wr, wm, wh, wo, res)


def _retention_kernel(lg_ref, q_ref, k_ref, v_ref, g_ref, cos_ref, sin_ref, nrm_ref,
                      o_ref, state_ref, *, chunks):
    c_len = RET_CHUNK
    d = RET_HEAD_DIM
    half = d // 2

    @pl.when(pl.program_id(1) == 0)
    def _():
        state_ref[...] = jnp.zeros_like(state_ref)

    t_col = lax.broadcasted_iota(jnp.int32, (c_len, 1), 0).astype(F32)
    rel = (lax.broadcasted_iota(jnp.int32, (c_len, c_len), 0)
           - lax.broadcasted_iota(jnp.int32, (c_len, c_len), 1)).astype(F32)

    def rotate(x, cos, sin):
        x1 = x[:, :half]
        x2 = x[:, half:]
        return jnp.concatenate([x1 * cos - x2 * sin, x2 * cos + x1 * sin], axis=1)

    for h in range(RET_HEADS):
        cols = slice(h * d, (h + 1) * d)
        lg = lg_ref[h]
        decay = jnp.where(rel >= 0, jnp.exp(lg * jnp.maximum(rel, 0.0)), 0.0)
        q_weight = jnp.exp(lg * (t_col + 1.0))
        k_weight = jnp.exp(lg * (c_len - 1.0 - t_col))
        chunk_decay = jnp.exp(jnp.full((1, d), lg * c_len, F32))
        for c in range(chunks):
            rows = slice(c * c_len, (c + 1) * c_len)
            cos = cos_ref[rows, :]
            sin = sin_ref[rows, :]
            q = rotate(q_ref[rows, cols], cos, sin)
            k = rotate(k_ref[rows, cols], cos, sin) * (d ** -0.5)
            v = v_ref[rows, cols].astype(BF16)
            state = state_ref[h]

            scores = lax.dot_general(q.astype(BF16), k.astype(BF16), NT_DIMS,
                                     preferred_element_type=F32) * decay
            intra = jnp.dot(scores.astype(BF16), v, preferred_element_type=F32)
            cross = jnp.dot((q * q_weight).astype(BF16), state.astype(BF16),
                            preferred_element_type=F32)
            kv = lax.dot_general((k * k_weight).astype(BF16), v, TN_DIMS,
                                 preferred_element_type=F32)
            state_ref[h] = chunk_decay * state + kv

            out = intra + cross
            centered = out - jnp.mean(out, axis=-1, keepdims=True)
            normed = centered * lax.rsqrt(
                jnp.mean(centered * centered, axis=-1, keepdims=True) + NORM_EPS)
            gated = (normed * nrm_ref[:, cols]) * _silu(g_ref[rows, cols])
            o_ref[rows, cols] = gated.astype(BF16)


def retention_mixer(proj, ret_norm, batch, seq, *, rows_per_step=1024):
    n = proj.shape[0]
    d = RET_HEAD_DIM
    assert seq % rows_per_step == 0 and rows_per_step % RET_CHUNK == 0
    steps = seq // rows_per_step
    log_gamma = jnp.log1p(-jnp.exp2(-5.0 - jnp.arange(RET_HEADS, dtype=F32)))
    half = d // 2
    inv_freq = RET_ROPE_BASE ** (-jnp.arange(half, dtype=F32) * 2.0 / d)
    ang = jnp.arange(seq).astype(F32)[:, None] * inv_freq[None, :]
    cos, sin = jnp.cos(ang), jnp.sin(ang)

    def col_spec(off):
        return pl.BlockSpec((rows_per_step, RET_WIDTH),
                            lambda b, s: (b * steps + s, off // RET_WIDTH))

    table_spec = pl.BlockSpec((rows_per_step, half), lambda b, s: (s, 0))
    return pl.pallas_call(
        functools.partial(_retention_kernel, chunks=rows_per_step // RET_CHUNK),
        grid=(batch, steps),
        in_specs=[pl.BlockSpec(memory_space=pltpu.SMEM),
                  col_spec(OFF_RQ), col_spec(OFF_RK), col_spec(OFF_RV), col_spec(OFF_RG),
                  table_spec, table_spec,
                  pl.BlockSpec((1, RET_WIDTH), lambda b, s: (0, 0))],
        out_specs=pl.BlockSpec((rows_per_step, RET_WIDTH), lambda b, s: (b * steps + s, 0)),
        out_shape=jax.ShapeDtypeStruct((n, RET_WIDTH), BF16),
        scratch_shapes=[pltpu.VMEM((RET_HEADS, d, d), F32)],
        compiler_params=_compiler_params(2, 52),
        name="retention",
    )(log_gamma, proj, proj, proj, proj, cos, sin, ret_norm.reshape(1, RET_WIDTH))


MOBA_PAIRS_PER_STEP = 2


def _moba_rotate(x, cos, sin_lo, sin_hi):
    half = ROPE_DIM // 2
    return (x * cos + pltpu.roll(x, MOBA_HEAD_DIM - half, 1) * sin_lo
            + pltpu.roll(x, half, 1) * sin_hi)


def _split_dot_nt(a, b):
    a_hi = a.astype(BF16)
    a_lo = (a - a_hi.astype(F32)).astype(BF16)
    b_hi = b.astype(BF16)
    b_lo = (b - b_hi.astype(F32)).astype(BF16)

    def dot_nt(x, y):
        return lax.dot_general(x, y, NT_DIMS, preferred_element_type=F32)

    return dot_nt(a_hi, b_hi) + dot_nt(a_hi, b_lo) + dot_nt(a_lo, b_hi)


def _moba_kernel(*refs, n_blocks, n_side):
    q_ref, k_ref, v_ref, cos_ref, slo_ref, shi_ref = refs[:6]
    o_ref = refs[6 + n_side]
    (kaug_ref, vt_ref, kmean_ref, qt_ref, s_ref, acc_ref,
     snap_ref) = refs[7 + 2 * n_side:]
    _run_side_casts(refs[6:6 + n_side], refs[7 + n_side:7 + 2 * n_side])
    blk = MOBA_BLOCK
    d = MOBA_HEAD_DIM
    step = pl.program_id(2)
    n_past = n_blocks - 1
    pairs = snap_ref.shape[0]
    gate_rows = kmean_ref.shape[0]
    groups = blk // SUBLANES
    q_scale = (d ** -0.5) * LOG2_E

    def rotate(ref, start):
        rows = pl.ds(start, blk)
        return _moba_rotate(ref[rows, :], cos_ref[rows, :], slo_ref[rows, :], shi_ref[rows, :])

    @pl.when(step == 0)
    def _():
        lane = lax.broadcasted_iota(jnp.int32, (blk, LANES), 1)
        eye = jnp.where(lax.broadcasted_iota(jnp.int32, (d, d), 0)
                        == lax.broadcasted_iota(jnp.int32, (d, d), 1), 1.0, 0.0).astype(BF16)

        def transposed(x_bf):
            return lax.dot_general(eye, x_bf, NT_DIMS, preferred_element_type=F32)

        kmean_ref[...] = jnp.zeros_like(kmean_ref)
        for j in range(n_blocks):
            kr = rotate(k_ref, j * blk)
            kaug_ref[j, :, :d] = kr.astype(BF16)
            kaug_ref[j, :, d:] = jnp.where(lane == j, 1.0, 0.0).astype(BF16)
            vt_ref[j] = transposed(v_ref[j * blk:(j + 1) * blk, :].astype(BF16)).astype(BF16)
            kmean_ref[j:j + 1, :] = jnp.mean(kr, axis=0, keepdims=True)
        row = lax.broadcasted_iota(jnp.int32, (gate_rows, blk), 0)
        row_f = row.astype(F32)
        pad = jnp.zeros((LANES - gate_rows, blk), F32)
        for i in range(n_blocks):
            q = rotate(q_ref, i * blk)
            gate = _split_dot_nt(kmean_ref[...], q)
            q_t = transposed((q * q_scale).astype(BF16))
            g = jnp.where(row < i, gate, -jnp.inf)
            bias = jnp.full((gate_rows, blk), MASK_VALUE, F32)
            for _ in range(min(MOBA_TOPK, n_blocks)):
                best = jnp.max(g, axis=0, keepdims=True)
                hit = (g == best) & (g > -jnp.inf)
                first = jnp.min(jnp.where(hit, row_f, float(LANES)), axis=0, keepdims=True)
                pick = row_f == first
                bias = jnp.where(pick, 0.0, bias)
                g = jnp.where(pick, -jnp.inf, g)
            bias = jnp.where(row == i, 0.0, bias)
            qt_ref[i] = jnp.concatenate([q_t, bias, pad], axis=0).astype(BF16)

    def group_reduce(x, op):
        return op(x.reshape(groups, SUBLANES, blk), axis=0)

    def score_own_block(i, own_slot):
        s_t = jnp.dot(kaug_ref[i], qt_ref[i], preferred_element_type=F32)
        key = lax.broadcasted_iota(jnp.int32, (blk, blk), 0)
        qry = lax.broadcasted_iota(jnp.int32, (blk, blk), 1)
        s_t = jnp.where(key <= qry, s_t, MASK_VALUE)
        s_ref[own_slot] = s_t
        return group_reduce(s_t, jnp.max)

    def tile_of(t, p):
        in_b = t >= p
        return in_b, jnp.where(in_b, t - p, t)

    def col_max(m8):
        return jnp.broadcast_to(jnp.max(m8, axis=0, keepdims=True), (SUBLANES, blk))

    def weights(slot, m8):
        p_t = jnp.exp2(s_ref[slot].reshape(groups, SUBLANES, blk) - m8[None])
        return jnp.sum(p_t, axis=0), p_t.reshape(blk, blk).astype(BF16)

    def values(j, p_bf):
        return jnp.dot(vt_ref[j], p_bf, preferred_element_type=F32)

    never = jnp.full((SUBLANES, blk), -jnp.inf, F32)
    scored = []
    for k in range(pairs):
        p = step * pairs + k
        base = k * (n_past + 2)
        i_a, i_b = p, n_past - p
        m_a = score_own_block(i_a, base + n_past)
        m_b = score_own_block(i_b, base + n_past + 1)
        for t in range(n_past):
            in_b, j = tile_of(t, p)
            s_t = jnp.dot(kaug_ref[j], qt_ref[jnp.where(in_b, i_b, i_a)],
                          preferred_element_type=F32)
            s_ref[base + t] = s_t
            gm = group_reduce(s_t, jnp.max)
            m_a = jnp.maximum(m_a, jnp.where(in_b, never, gm))
            m_b = jnp.maximum(m_b, jnp.where(in_b, gm, never))
        scored.append((p, base, i_a, i_b, col_max(m_a), col_max(m_b)))

    for k, (p, base, i_a, i_b, m_a, m_b) in enumerate(scored):
        l_a, p_own = weights(base + n_past, m_a)
        acc_ref[2 * k] = values(i_a, p_own)
        l_b, p_own = weights(base + n_past + 1, m_b)
        acc_ref[2 * k + 1] = values(i_b, p_own)
        acc = jnp.zeros((d, blk), F32)
        for t in range(n_past):
            in_b, j = tile_of(t, p)
            l_part, p_bf = weights(base + t, jnp.where(in_b, m_b, m_a))
            acc = acc * jnp.where(t == p, 0.0, 1.0) + values(j, p_bf)
            l_a = l_a + jnp.where(in_b, 0.0, l_part)
            l_b = l_b + jnp.where(in_b, l_part, 0.0)
            if t < snap_ref.shape[1]:
                snap_ref[k, t] = acc
        past_a = snap_ref[k, jnp.maximum(p - 1, 0)] * jnp.where(p > 0, 1.0, 0.0)
        for w, i, l8, past in ((2 * k, i_a, l_a, past_a), (2 * k + 1, i_b, l_b, acc)):
            l = jnp.sum(l8, axis=0, keepdims=True)
            out_t = (acc_ref[w] + past) / l
            o_ref[pl.ds(pl.multiple_of(i * blk, blk), blk), :] = out_t.T.astype(BF16)


def moba_grid_steps(batch, seq):
    return batch * MOBA_HEADS * (seq // MOBA_BLOCK // 2 // MOBA_PAIRS_PER_STEP)


def moba_mixer(proj, batch, seq, side_casts=()):
    n = proj.shape[0]
    d = MOBA_HEAD_DIM
    blk = MOBA_BLOCK
    assert seq % blk == 0
    n_blocks = seq // blk
    assert n_blocks <= LANES and n_blocks % 2 == 0
    half = ROPE_DIM // 2
    inv_freq = ROPE_THETA ** (-jnp.arange(half, dtype=F32) * 2.0 / ROPE_DIM)
    ang = jnp.arange(seq).astype(F32)[:, None] * inv_freq[None, :]
    cos, sin = jnp.cos(ang), jnp.sin(ang)
    zeros = jnp.zeros((seq, d - ROPE_DIM), F32)
    zero_half = jnp.zeros((seq, half), F32)
    cos_t = jnp.concatenate([cos, cos, jnp.ones_like(zeros)], axis=1)
    sin_lo = jnp.concatenate([-sin, zero_half, zeros], axis=1)
    sin_hi = jnp.concatenate([zero_half, sin, zeros], axis=1)

    def col_spec(off):
        return pl.BlockSpec((seq, d), lambda b, h, p: (b, off // d + h))

    table_spec = pl.BlockSpec((seq, d), lambda b, h, p: (0, 0))
    gate_rows = -(-n_blocks // SUBLANES) * SUBLANES
    pairs = MOBA_PAIRS_PER_STEP
    assert (n_blocks // 2) % pairs == 0
    n_snap = max(n_blocks // 2 - 1, 1)
    steps = n_blocks // 2 // pairs

    def step_of(b, h, p):
        return (b * MOBA_HEADS + h) * steps + p

    out, *casts = pl.pallas_call(
        functools.partial(_moba_kernel, n_blocks=n_blocks, n_side=len(side_casts)),
        grid=(batch, MOBA_HEADS, steps),
        in_specs=[col_spec(OFF_MQ), col_spec(OFF_MK), col_spec(OFF_MV),
                  table_spec, table_spec, table_spec,
                  *[c.in_spec(step_of) for c in side_casts]],
        out_specs=[pl.BlockSpec((seq, d), lambda b, h, p: (b, h)),
                   *[c.out_spec(step_of) for c in side_casts]],
        out_shape=[jax.ShapeDtypeStruct((n, MOBA_WIDTH), BF16),
                   *[c.out_shape() for c in side_casts]],
        scratch_shapes=[pltpu.VMEM((n_blocks, blk, 2 * d), BF16),
                        pltpu.VMEM((n_blocks, d, blk), BF16),
                        pltpu.VMEM((gate_rows, d), F32),
                        pltpu.VMEM((n_blocks, 2 * d, blk), BF16),
                        pltpu.VMEM((pairs * (n_blocks + 1), blk, blk), F32),
                        pltpu.VMEM((2 * pairs, d, blk), F32),
                        pltpu.VMEM((pairs, n_snap, d, blk), F32)],
        compiler_params=_compiler_params(3, 54),
        name="moba",
    )(proj, proj, proj, cos_t, sin_lo, sin_hi, *[c.operand for c in side_casts])
    return out, [c.finish(x) for c, x in zip(side_casts, casts)]


def _hgrn_pair_codes(block):
    i = np.arange(block)[:, None]
    j = np.arange(block)[None, :]
    top_bit = np.floor(np.log2(np.maximum(i ^ j, 1))).astype(np.int32)
    n_levels = block.bit_length() - 1
    return np.where(i > j, top_bit, np.where(i == j, n_levels, -1)).astype(np.int32)


def _hgrn_kernel(*refs, layer, rows_per_step, n_side):
    lbp_ref, tril_ref, code_ref, q_ref, f_ref, v_ref, g_ref, nrm_ref = refs[:8]
    o_ref = refs[8 + n_side]
    state_ref, b_ref, k_ref = refs[9 + 2 * n_side:]
    _run_side_casts(refs[8:8 + n_side], refs[9 + n_side:9 + 2 * n_side])
    c_len = HGRN_BLOCK
    d = HGRN_HEAD_DIM

    @pl.when(pl.program_id(2) == 0)
    def _():
        state_ref[...] = jnp.zeros_like(state_ref)

    params = lbp_ref[...]
    e = jnp.exp(params - jnp.max(params, axis=0, keepdims=True))
    soft = e / jnp.sum(e, axis=0, keepdims=True)
    lower = jnp.zeros((1, d), F32)
    for r in range(1, layer + 1):
        lower = lower + soft[r:r + 1, :]

    tril = tril_ref[...]
    for seg0 in range(0, rows_per_step, c_len):
        seg = slice(seg0, seg0 + c_len)
        forget = lower + (1.0 - lower) * jax.nn.sigmoid(f_ref[seg, :])
        k_ref[seg, :] = 1.0 - forget
        log_f = jnp.log2(forget)
        hi = log_f.astype(BF16)
        rest = log_f - hi.astype(F32)
        mid = rest.astype(BF16)
        lo = (rest - mid.astype(F32)).astype(BF16)
        b_ref[seg, :] = (jnp.dot(tril, hi, preferred_element_type=F32)
                         + jnp.dot(tril, mid, preferred_element_type=F32)
                         + jnp.dot(tril, lo, preferred_element_type=F32))

    assert c_len & (c_len - 1) == 0 and c_len % LANES == 0
    levels = [1 << s for s in range(c_len.bit_length() - 1)]
    n_grp = c_len // SUBLANES
    row8 = lax.broadcasted_iota(jnp.int32, (SUBLANES, d), 0)
    zero8 = jnp.zeros((SUBLANES, d), F32)
    codes = _hgrn_pair_codes(c_len)
    tiles = [(g, cb) for g in range(n_grp) for cb in range(c_len // LANES)]

    def code_tile(g, cb):
        return codes[SUBLANES * g:SUBLANES * (g + 1), LANES * cb:LANES * (cb + 1)]

    level_tiles = [[(g, cb, bool(np.all(code_tile(g, cb) == bit))) for g, cb in tiles
                    if np.any(code_tile(g, cb) == bit)] for bit in range(len(levels) + 1)]

    def bcast_row(ref, r):
        return jnp.broadcast_to(ref[r:r + 1, :], (SUBLANES, d))

    def level_operands(r0, h, bg, qg, kg):
        q_parts, k_parts = [], []
        for g in range(n_grp):
            base = SUBLANES * g
            if h >= SUBLANES:
                b_a = bcast_row(b_ref, r0 + base // (2 * h) * (2 * h) + h - 1)
                if base & h:
                    q_parts.append(qg[g] * jnp.exp2(bg[g] - b_a))
                    k_parts.append(zero8)
                else:
                    q_parts.append(zero8)
                    k_parts.append(kg[g] * jnp.exp2(b_a - bg[g]))
                continue
            upper = (row8 & h) != 0
            if h == 1:
                q_parts.append(jnp.where(upper, qg[g] * (1.0 - kg[g]), 0.0))
                k_parts.append(jnp.where(upper, 0.0, kg[g]))
                continue
            b_a = bcast_row(b_ref, r0 + base + h - 1)
            for s in range(2 * h, SUBLANES, 2 * h):
                b_a = jnp.where(row8 >= s, bcast_row(b_ref, r0 + base + s + h - 1), b_a)
            decay = jnp.exp2(-jnp.abs(bg[g] - b_a))
            q_parts.append(jnp.where(upper, qg[g] * decay, 0.0))
            k_parts.append(jnp.where(upper, 0.0, kg[g] * decay))
        return (jnp.concatenate(q_parts, axis=0).astype(BF16),
                jnp.concatenate(k_parts, axis=0).astype(BF16))

    for c in range(rows_per_step // c_len):
        r0 = c * c_len
        rows = slice(r0, r0 + c_len)
        b = b_ref[rows, :]
        q = _silu(q_ref[rows, :])
        k = k_ref[rows, :]
        v = v_ref[rows, :]

        groups = [slice(SUBLANES * g, SUBLANES * (g + 1)) for g in range(n_grp)]
        bg = [b[g, :] for g in groups]
        qg = [q[g, :] for g in groups]
        kg = [k[g, :] for g in groups]
        def code_of(g, cb):
            return code_ref[SUBLANES * g:SUBLANES * (g + 1), LANES * cb:LANES * (cb + 1)]

        a_tiles = {}
        self_weight = jnp.sum(q * k, axis=-1, keepdims=True)
        for g, cb, _ in level_tiles[len(levels)]:
            a_tiles[g, cb] = jnp.where(code_of(g, cb) == len(levels),
                                       self_weight[groups[g], :], 0.0)
        for bit, h in enumerate(levels):
            q_h, k_h = level_operands(r0, h, bg, qg, kg)
            for cb in range(c_len // LANES):
                k_lo = LANES * cb
                if 2 * h >= LANES:
                    start = k_lo // (2 * h) * (2 * h)
                    if k_lo >= start + h:
                        continue
                    q_lo, q_hi = start + h, start + 2 * h
                else:
                    q_lo, q_hi = k_lo, k_lo + LANES
                pair = lax.dot_general(q_h[q_lo:q_hi, :], k_h[k_lo:k_lo + LANES, :], NT_DIMS,
                                       preferred_element_type=F32)
                for g, tile_cb, owns_tile in level_tiles[bit]:
                    if tile_cb != cb:
                        continue
                    assert q_lo <= SUBLANES * g < q_hi
                    piece = pair[SUBLANES * g - q_lo:SUBLANES * (g + 1) - q_lo, :]
                    if owns_tile:
                        a_tiles[g, cb] = piece
                    else:
                        a_tiles[g, cb] = jnp.where(code_of(g, cb) == bit, piece,
                                                   a_tiles.get((g, cb), 0.0))
        zero_tile = jnp.zeros((SUBLANES, LANES), F32)
        a_mat = jnp.concatenate(
            [jnp.concatenate([a_tiles.get((g, cb), zero_tile) for g in range(n_grp)], axis=0)
             for cb in range(c_len // LANES)], axis=1)
        intra = jnp.dot(a_mat.astype(BF16), v.astype(BF16), preferred_element_type=F32)

        state_t = state_ref[...]
        cross = lax.dot_general((q * jnp.exp2(b)).astype(BF16), state_t.astype(BF16), NT_DIMS,
                                preferred_element_type=F32)
        b_last = b_ref[r0 + c_len - 1:r0 + c_len, :]
        k_dec = k * jnp.exp2(b_last - b)
        state_ref[...] = state_t * jnp.exp2(b_last) + lax.dot_general(
            v.astype(BF16), k_dec.astype(BF16), TN_DIMS, preferred_element_type=F32)

        out = intra + cross
        normed = out * lax.rsqrt(jnp.mean(out * out, axis=-1, keepdims=True) + NORM_EPS)
        gated = (normed * nrm_ref[...]) * _silu(g_ref[rows, :])
        o_ref[rows, :] = gated.astype(BF16)


HGRN_ROWS_PER_STEP = 2048


def hgrn_grid_steps(batch, seq):
    return batch * HGRN_HEADS * (seq // min(seq, HGRN_ROWS_PER_STEP))


def hgrn_mixer(proj, lower_bound_params, hgrn_norm, layer, batch, seq, side_casts=()):
    rows_per_step = min(seq, HGRN_ROWS_PER_STEP)
    n = proj.shape[0]
    d = HGRN_HEAD_DIM
    depth = lower_bound_params.shape[0]
    blk = HGRN_BLOCK
    assert seq % rows_per_step == 0 and rows_per_step % blk == 0
    steps = seq // rows_per_step
    codes = _hgrn_pair_codes(blk)
    tril = jnp.asarray(codes >= 0, BF16)

    def col_spec(off):
        return pl.BlockSpec((rows_per_step, d), lambda b, h, s: (b * steps + s, off // d + h))

    def step_of(b, h, s):
        return (b * HGRN_HEADS + h) * steps + s

    out, *casts = pl.pallas_call(
        functools.partial(_hgrn_kernel, layer=layer, rows_per_step=rows_per_step,
                          n_side=len(side_casts)),
        grid=(batch, HGRN_HEADS, steps),
        in_specs=[pl.BlockSpec((depth, d), lambda b, h, s: (0, h)),
                  pl.BlockSpec((blk, blk), lambda b, h, s: (0, 0)),
                  pl.BlockSpec((blk, blk), lambda b, h, s: (0, 0)),
                  col_spec(OFF_HQ), col_spec(OFF_HF), col_spec(OFF_HI), col_spec(OFF_HG),
                  pl.BlockSpec((1, d), lambda b, h, s: (0, h)),
                  *[c.in_spec(step_of) for c in side_casts]],
        out_specs=[pl.BlockSpec((rows_per_step, d), lambda b, h, s: (b * steps + s, h)),
                   *[c.out_spec(step_of) for c in side_casts]],
        out_shape=[jax.ShapeDtypeStruct((n, HGRN_WIDTH), BF16),
                   *[c.out_shape() for c in side_casts]],
        scratch_shapes=[pltpu.VMEM((d, d), F32),
                        pltpu.VMEM((rows_per_step, d), F32),
                        pltpu.VMEM((rows_per_step, d), F32)],
        compiler_params=_compiler_params(3, 40),
        name="hgrn2",
    )(lower_bound_params.astype(F32), tril, jnp.asarray(codes), proj, proj, proj, proj,
      hgrn_norm.reshape(1, HGRN_WIDTH), *[c.operand for c in side_casts])
    return out, [c.finish(x) for c, x in zip(side_casts, casts)]


def kernel(x, hgrn_lower_bounds, norm_mix, w_in, ret_norm, hgrn_norm, w_branch_ret, w_branch_moba,
           w_branch_hgrn, w_out, norm_ffn, w_ffn_gate, w_ffn_up, w_ffn_down, final_norm):
    batch, seq, d_model = x.shape
    assert d_model == D_MODEL
    depth = w_in.shape[0]
    w_in_layers = [cast_layer(w_in, 0)]
    moba_steps = moba_grid_steps(batch, seq)
    hgrn_steps = hgrn_grid_steps(batch, seq)
    moba_casts = [SideCast(w, moba_steps) for w in (w_ffn_gate, w_ffn_up, w_ffn_down)]
    hgrn_casts = [SideCast(w, hgrn_steps)
                  for w in (w_branch_ret, w_branch_moba, w_branch_hgrn, w_out)]
    hgrn_casts += [SideCast(w_in, hgrn_steps, part=l, n_parts=depth) for l in range(1, depth)]
    h = x.reshape(batch * seq, d_model)
    for layer in range(depth):
        proj = norm_matmul(h, norm_mix[layer], w_in_layers[layer], 0)
        ret = retention_mixer(proj, ret_norm[layer], batch, seq)
        moba, cast = moba_mixer(proj, batch, seq, moba_casts if layer == 0 else ())
        if layer == 0:
            w_ffn_gate, w_ffn_up, w_ffn_down = cast
        hgrn, cast = hgrn_mixer(proj, hgrn_lower_bounds, hgrn_norm[layer], layer, batch, seq,
                                hgrn_casts if layer == 0 else ())
        if layer == 0:
            w_branch_ret, w_branch_moba, w_branch_hgrn, w_out = cast[:4]
            w_in_layers += cast[4:]
        h = merge_out_proj(ret, moba, hgrn, proj, w_branch_ret, w_branch_moba, w_branch_hgrn,
                           w_out, h, layer)
        act = ffn_up(h, norm_ffn[layer], w_ffn_gate, w_ffn_up, layer)
        h = ffn_down(act, w_ffn_down, h, layer, final_norm, normalize=layer == depth - 1)
    return h.reshape(batch, seq, d_model)
```
